```python
import math
import jax
import jax.numpy as jnp
from jax import lax
import numpy as np

D_MODEL = 2048
BATCH = 8
SEQ = 2048
DEPTH = 1

HEAD_DIM = 64
N_HEADS_TOTAL = D_MODEL // HEAD_DIM
MIX_WIDTH = N_HEADS_TOTAL * HEAD_DIM
N_NSA_HEADS = N_HEADS_TOTAL // 2
N_FOX_HEADS = N_HEADS_TOTAL - N_NSA_HEADS
NSA_GQA = 4
N_NSA_KV = N_NSA_HEADS // NSA_GQA
CMP_BLOCK = 32
CMP_STRIDE = 16
CMP_HIDDEN = 2 * HEAD_DIM
SLC_BLOCK = 64
SLC_TOP_N = 16
WINDOW = 512
Q_BLOCK = 128
SLC_Q_BLOCK = 64
REL_BUCKETS = 32
REL_MAX_DIST = 128
N_GROUPS = 4
EXPERTS_PER_GROUP = 8
N_EXPERTS = N_GROUPS * EXPERTS_PER_GROUP
EXPERT_TOP_K = 2
EXPERT_FF = D_MODEL // 4
NORM_EPS = 1e-6
NEG_INF = -1e30
FORCE_BONUS = 1e4

NSA_Q_COLS = N_NSA_HEADS * HEAD_DIM
NSA_KV_COLS = N_NSA_KV * HEAD_DIM
NSA_GATE_COLS = N_NSA_HEADS * 3
FOX_COLS = N_FOX_HEADS * HEAD_DIM
IN_SPLIT_SIZES = (NSA_Q_COLS,) + (NSA_KV_COLS,) * 6 + (NSA_GATE_COLS, FOX_COLS, FOX_COLS, FOX_COLS, N_FOX_HEADS)
IN_COLS = sum(IN_SPLIT_SIZES)

kernel_name = 'hybrid_nsa_fox_hmoe'


def rms_norm(x, w):
    xf = x.astype(jnp.float32)
    y = xf * lax.rsqrt(jnp.mean(xf * xf, axis=-1, keepdims=True) + NORM_EPS)
    return (y * w.astype(jnp.float32)).astype(x.dtype)


def t5_bucket(dist):
    n = jnp.maximum(dist, 0)
    max_exact = REL_BUCKETS // 2
    rel = jnp.log(jnp.maximum(n, 1).astype(jnp.float32) / max_exact) / math.log(REL_MAX_DIST / max_exact)
    large = max_exact + (rel * (REL_BUCKETS - max_exact)).astype(jnp.int32)
    large = jnp.minimum(large, REL_BUCKETS - 1)
    return jnp.where(n < max_exact, n, large)


def masked_softmax(s, mask):
    s = jnp.where(mask, s, NEG_INF)
    m = jnp.max(s, axis=-1, keepdims=True)
    e = jnp.where(mask, jnp.exp(s - m), 0.0)
    return e / jnp.maximum(jnp.sum(e, axis=-1, keepdims=True), 1e-30)


def compress_blocks(kv, pe, w1, w2):
    b, t, g, d = kv.shape
    nc = (t - CMP_BLOCK) // CMP_STRIDE + 1
    idx = jnp.arange(nc)[:, None] * CMP_STRIDE + jnp.arange(CMP_BLOCK)[None, :]
    blocks = kv[:, idx] + pe[None, None, :, None, :]
    blocks = jnp.moveaxis(blocks, 3, 2).reshape(b, nc, g, CMP_BLOCK * d)
    return jax.nn.silu(blocks @ w1) @ w2


def nsa_mixer(q, k_cmp, v_cmp, k_slc, v_slc, k_win, v_win, gate_logits,
              pe_k, pe_v, ck_w1, ck_w2, cv_w1, cv_w2, rel_table):
    b, t, _, d = q.shape
    g, r = N_NSA_KV, NSA_GQA
    scale = d ** -0.5
    qg = q.reshape(b, t, g, r, d)
    t_pos = jnp.arange(t)

    kc = compress_blocks(k_cmp, pe_k, ck_w1, ck_w2)
    vc = compress_blocks(v_cmp, pe_v, cv_w1, cv_w2)
    nc = kc.shape[1]
    c_end = jnp.arange(nc) * CMP_STRIDE + CMP_BLOCK - 1
    dist_c = t_pos[:, None] - c_end[None, :]
    bias_c = rel_table[t5_bucket(dist_c)].reshape(t, nc, g, r).transpose(2, 3, 0, 1)
    s_c = jnp.einsum('btgrd,bcgd->bgrtc', qg, kc, preferred_element_type=jnp.float32) * scale + bias_c
    p_cmp = masked_softmax(s_c, dist_c >= 0)
    o_cmp = jnp.einsum('bgrtc,bcgd->btgrd', p_cmp.astype(vc.dtype), vc)

    ns = t // SLC_BLOCK
    ratio = SLC_BLOCK // CMP_STRIDE
    span = CMP_BLOCK // CMP_STRIDE
    offs = (jnp.arange(ratio)[:, None] - jnp.arange(span)[None, :]).reshape(-1)
    cidx = jnp.arange(ns)[:, None] * ratio + offs[None, :]
    cvalid = (cidx >= 0) & (cidx < nc)
    p_grp = jnp.sum(p_cmp, axis=2)
    imp = jnp.sum(jnp.where(cvalid, jnp.take(p_grp, jnp.clip(cidx, 0, nc - 1), axis=-1), 0.0), axis=-1)
    blk = jnp.arange(ns)
    cur = t_pos // SLC_BLOCK
    forced = (blk[None, :] == 0) | (blk[None, :] == cur[:, None]) | (blk[None, :] == cur[:, None] - 1)
    blk_valid = blk[None, :] * SLC_BLOCK <= t_pos[:, None]
    sel_score = jnp.where(blk_valid, imp + jnp.where(forced, FORCE_BONUS, 0.0), NEG_INF)
    n_sel = min(SLC_TOP_N, ns)
    _, sel = lax.top_k(sel_score, n_sel)
    nk = n_sel * SLC_BLOCK

    k_t = jnp.moveaxis(k_slc, 2, 1)
    v_t = jnp.moveaxis(v_slc, 2, 1)
    b_idx = jnp.arange(b)[:, None, None]
    g_idx = jnp.arange(g)[None, :, None]
    table_g = rel_table.reshape(REL_BUCKETS, g, r).transpose(1, 0, 2)

    def slc_chunk(ci):
        t0 = ci * SLC_Q_BLOCK
        qc = lax.dynamic_slice_in_dim(qg, t0, SLC_Q_BLOCK, axis=1)
        selc = lax.dynamic_slice_in_dim(sel, t0, SLC_Q_BLOCK, axis=2)
        pos = (selc[..., None] * SLC_BLOCK + jnp.arange(SLC_BLOCK)).reshape(b, g, SLC_Q_BLOCK * nk)
        kg = k_t[b_idx, g_idx, pos].reshape(b, g, SLC_Q_BLOCK, nk, d)
        vg = v_t[b_idx, g_idx, pos].reshape(b, g, SLC_Q_BLOCK, nk, d)
        dist = (t0 + jnp.arange(SLC_Q_BLOCK))[:, None] - pos.reshape(b, g, SLC_Q_BLOCK, nk)
        bias = table_g[jnp.arange(g)[None, :, None, None], t5_bucket(dist)]
        s = jnp.einsum('bqgrd,bgqkd->bgrqk', qc, kg, preferred_element_type=jnp.float32) * scale
        s = s + jnp.moveaxis(bias, 4, 2)
        p = masked_softmax(s, (dist >= 0)[:, :, None])
        return jnp.einsum('bgrqk,bgqkd->bqgrd', p.astype(vg.dtype), vg)

    o_slc = lax.map(slc_chunk, jnp.arange(t // SLC_Q_BLOCK))
    o_slc = jnp.moveaxis(o_slc, 0, 1).reshape(b, t, g, r, d)

    kp = jnp.pad(k_win, ((0, 0), (WINDOW, 0), (0, 0), (0, 0)))
    vp = jnp.pad(v_win, ((0, 0), (WINDOW, 0), (0, 0), (0, 0)))
    span_w = WINDOW + Q_BLOCK
    j_loc = jnp.arange(span_w)
    dist_w = jnp.arange(Q_BLOCK)[:, None] + WINDOW - j_loc[None, :]
    band = (dist_w >= 0) & (dist_w < WINDOW)
    bias_w = rel_table[t5_bucket(dist_w)].reshape(Q_BLOCK, span_w, g, r).transpose(2, 3, 0, 1)

    def win_block(bi):
        t0 = bi * Q_BLOCK
        qb = lax.dynamic_slice_in_dim(qg, t0, Q_BLOCK, axis=1)
        kb = lax.dynamic_slice_in_dim(kp, t0, span_w, axis=1)
        vb = lax.dynamic_slice_in_dim(vp, t0, span_w, axis=1)
        mask = band & (t0 - WINDOW + j_loc >= 0)[None, :]
        s = jnp.einsum('bqgrd,bkgd->bgrqk', qb, kb, preferred_element_type=jnp.float32) * scale + bias_w
        p = masked_softmax(s, mask)
        return jnp.einsum('bgrqk,bkgd->bqgrd', p.astype(vb.dtype), vb)

    o_win = lax.map(win_block, jnp.arange(t // Q_BLOCK))
    o_win = jnp.moveaxis(o_win, 0, 1).reshape(b, t, g, r, d)

    gates = jax.nn.sigmoid(gate_logits.astype(jnp.float32)).reshape(b, t, g, r, 3).astype(q.dtype)
    o = gates[..., 0:1] * o_cmp + gates[..., 1:2] * o_slc + gates[..., 2:3] * o_win
    return o.reshape(b, t, g * r * d)


def fox_mixer(q, k, v, f_logit, f_bias):
    b, t, h, d = q.shape
    scale = d ** -0.5
    log_f = jax.nn.log_sigmoid(f_logit.astype(jnp.float32) + f_bias.astype(jnp.float32))
    c = jnp.moveaxis(jnp.cumsum(log_f, axis=1), 1, 2)
    outs = []
    for bi in range(t // Q_BLOCK):
        t0, t1 = bi * Q_BLOCK, (bi + 1) * Q_BLOCK
        s = jnp.einsum('bqhd,bkhd->bhqk', q[:, t0:t1], k[:, :t1], preferred_element_type=jnp.float32) * scale
        s = s + c[:, :, t0:t1, None] - c[:, :, None, :t1]
        mask = jnp.arange(t0, t1)[:, None] >= jnp.arange(t1)[None, :]
        p = masked_softmax(s, mask)
        outs.append(jnp.einsum('bhqk,bkhd->bqhd', p.astype(v.dtype), v[:, :t1]))
    return jnp.concatenate(outs, axis=1).reshape(b, t, h * d)


def hier_moe(xn, wg, bg, we, be, w_gate, w_up, w_down):
    b, t, dm = xn.shape
    xf = xn.reshape(b * t, dm)
    glog = (xf @ wg + bg).astype(jnp.float32)
    pg = jax.nn.softmax(glog, axis=-1)
    gsel = jnp.argmax(glog, axis=-1)
    p_gsel = jnp.take_along_axis(pg, gsel[:, None], axis=-1)
    elog = (xf @ we + be).astype(jnp.float32).reshape(b * t, N_GROUPS, EXPERTS_PER_GROUP)
    elog_g = jnp.take_along_axis(elog, gsel[:, None, None], axis=1)[:, 0]
    top_v, top_i = lax.top_k(elog_g, EXPERT_TOP_K)
    w_top = jax.nn.softmax(top_v, axis=-1) * p_gsel
    eid = gsel[:, None] * EXPERTS_PER_GROUP + top_i
    combine = jnp.sum(jax.nn.one_hot(eid, N_EXPERTS, dtype=jnp.float32) * w_top[..., None], axis=1)
    combine = combine.astype(xf.dtype)
    y = jnp.zeros_like(xf)
    for e in range(N_EXPERTS):
        he = jax.nn.silu(xf @ w_gate[e]) * (xf @ w_up[e])
        y = y + combine[:, e:e + 1] * (he @ w_down[e])
    return y.reshape(b, t, dm)


def setup_inputs(seed: int = 0) -> dict:
    key = jax.random.key(seed)
    ks = jax.random.split(key, 24)

    def nrm(k, shape, scale):
        return jax.random.normal(k, shape, jnp.float32) * scale

    L = DEPTH
    return {
        'x': nrm(ks[0], (BATCH, SEQ, D_MODEL), 1.0),
        'attn_norm_w': 1.0 + nrm(ks[1], (L, D_MODEL), 0.01),
        'w_in': nrm(ks[2], (L, D_MODEL, IN_COLS), D_MODEL ** -0.5),
        'cmp_pe_k': nrm(ks[3], (L, CMP_BLOCK, HEAD_DIM), 0.1),
        'cmp_pe_v': nrm(ks[4], (L, CMP_BLOCK, HEAD_DIM), 0.1),
        'cmp_k_w1': nrm(ks[5], (L, CMP_BLOCK * HEAD_DIM, CMP_HIDDEN), (CMP_BLOCK * HEAD_DIM) ** -0.5),
        'cmp_k_w2': nrm(ks[6], (L, CMP_HIDDEN, HEAD_DIM), CMP_HIDDEN ** -0.5),
        'cmp_v_w1': nrm(ks[7], (L, CMP_BLOCK * HEAD_DIM, CMP_HIDDEN), (CMP_BLOCK * HEAD_DIM) ** -0.5),
        'cmp_v_w2': nrm(ks[8], (L, CMP_HIDDEN, HEAD_DIM), CMP_HIDDEN ** -0.5),
        'rel_bias_table': nrm(ks[9], (REL_BUCKETS, N_NSA_HEADS), 0.5),
        'fox_forget_b': 2.0 + nrm(ks[10], (L, N_FOX_HEADS), 0.5),
        'nsa_out_norm_w': 1.0 + nrm(ks[11], (L, N_NSA_HEADS * HEAD_DIM), 0.01),
        'fox_out_norm_w': 1.0 + nrm(ks[12], (L, N_FOX_HEADS * HEAD_DIM), 0.01),
        'w_out': nrm(ks[13], (L, MIX_WIDTH, D_MODEL), MIX_WIDTH ** -0.5),
        'ffn_norm_w': 1.0 + nrm(ks[14], (L, D_MODEL), 0.01),
        'router_group_w': nrm(ks[15], (L, D_MODEL, N_GROUPS), D_MODEL ** -0.5),
        'router_group_b': nrm(ks[16], (L, N_GROUPS), 0.01),
        'router_expert_w': nrm(ks[17], (L, D_MODEL, N_EXPERTS), D_MODEL ** -0.5),
        'router_expert_b': nrm(ks[18], (L, N_EXPERTS), 0.01),
        'expert_w_gate': nrm(ks[19], (L, N_EXPERTS, D_MODEL, EXPERT_FF), D_MODEL ** -0.5),
        'expert_w_up': nrm(ks[20], (L, N_EXPERTS, D_MODEL, EXPERT_FF), D_MODEL ** -0.5),
        'expert_w_down': nrm(ks[21], (L, N_EXPERTS, EXPERT_FF, D_MODEL), EXPERT_FF ** -0.5),
        'final_norm_w': 1.0 + nrm(ks[22], (D_MODEL,), 0.01),
    }


def reference(x, attn_norm_w, w_in, cmp_pe_k, cmp_pe_v, cmp_k_w1, cmp_k_w2, cmp_v_w1, cmp_v_w2,
              rel_bias_table, fox_forget_b, nsa_out_norm_w, fox_out_norm_w, w_out, ffn_norm_w,
              router_group_w, router_group_b, router_expert_w, router_expert_b,
              expert_w_gate, expert_w_up, expert_w_down, final_norm_w):
    b, t, _ = x.shape
    split_at = [int(v) for v in np.cumsum(IN_SPLIT_SIZES)[:-1]]
    h = x
    for layer in range(DEPTH):
        xn = rms_norm(h, attn_norm_w[layer])
        proj = xn @ w_in[layer]
        (nq, kcmp, vcmp, kslc, vslc, kwin, vwin, ngate, fq, fk, fv, ff) = jnp.split(proj, split_at, axis=-1)
        o_nsa = nsa_mixer(
            nq.reshape(b, t, N_NSA_HEADS, HEAD_DIM),
            kcmp.reshape(b, t, N_NSA_KV, HEAD_DIM), vcmp.reshape(b, t, N_NSA_KV, HEAD_DIM),
            kslc.reshape(b, t, N_NSA_KV, HEAD_DIM), vslc.reshape(b, t, N_NSA_KV, HEAD_DIM),
            kwin.reshape(b, t, N_NSA_KV, HEAD_DIM), vwin.reshape(b, t, N_NSA_KV, HEAD_DIM),
            ngate, cmp_pe_k[layer], cmp_pe_v[layer], cmp_k_w1[layer], cmp_k_w2[layer],
            cmp_v_w1[layer], cmp_v_w2[layer], rel_bias_table)
        o_fox = fox_mixer(
            fq.reshape(b, t, N_FOX_HEADS, HEAD_DIM), fk.reshape(b, t, N_FOX_HEADS, HEAD_DIM),
            fv.reshape(b, t, N_FOX_HEADS, HEAD_DIM), ff, fox_forget_b[layer])
        mixed = jnp.concatenate([rms_norm(o_nsa, nsa_out_norm_w[layer]),
                                 rms_norm(o_fox, fox_out_norm_w[layer])], axis=-1)
        h = h + mixed @ w_out[layer]
        hn = rms_norm(h, ffn_norm_w[layer])
        h = h + hier_moe(hn, router_group_w[layer], router_group_b[layer], router_expert_w[layer],
                         router_expert_b[layer], expert_w_gate[layer], expert_w_up[layer], expert_w_down[layer])
    return rms_norm(h, final_norm_w)
```

```python
import functools
import math

import numpy as np
import jax
import jax.numpy as jnp
from jax import lax
from jax.experimental import pallas as pl
from jax.experimental.pallas import tpu as pltpu

F32 = jnp.float32
BF16 = jnp.bfloat16

HEAD_DIM = 64
N_NSA_HEADS = 16
N_FOX_HEADS = 16
NSA_GQA = 4
N_NSA_KV = 4
CMP_BLOCK = 32
CMP_STRIDE = 16
CMP_HIDDEN = 128
SLC_BLOCK = 64
SLC_TOP_N = 16
WINDOW = 512
REL_BUCKETS = 32
REL_MAX_DIST = 128
N_GROUPS = 4
EXPERTS_PER_GROUP = 8
N_EXPERTS = 32
EXPERT_FF = 512
NORM_EPS = 1e-6
NEG_INF = -1e30
FORCE_BONUS = 1e4
SCALE = HEAD_DIM ** -0.5

LANES = 128
VMEM_LIMIT = 56 * 1024 * 1024

COL_NQ = 0
COL_KCMP = 1024
COL_VCMP = 1280
COL_SLC = 1536
COL_WIN = 2048
COL_FQ = 2560
COL_FKV = 3584
MAIN_COLS = 5632
MISC_GATE = 0
MISC_FF = 48

AT_TQ = 128
AT_TK = 128


def _nt(a, b):
    return lax.dot_general(a, b, (((1,), (1,)), ((), ())), preferred_element_type=F32)


def _dot(a, b):
    return jnp.dot(a, b, preferred_element_type=F32)


def _split3(x):
    hi = x.astype(BF16)
    r = x - hi.astype(F32)
    mid = r.astype(BF16)
    r = r - mid.astype(F32)
    return hi, mid, r.astype(BF16)


def _cparams(grid_rank):
    return pltpu.CompilerParams(dimension_semantics=("arbitrary",) * grid_rank, vmem_limit_bytes=VMEM_LIMIT)


def _bucket_table(n):
    d = np.arange(n, dtype=np.int64)
    max_exact = REL_BUCKETS // 2
    rel = np.log(np.maximum(d, 1).astype(np.float64) / max_exact) / math.log(REL_MAX_DIST / max_exact)
    scaled = rel * (REL_BUCKETS - max_exact)
    frac = scaled - np.floor(scaled)
    inner = (d > max_exact) & (d < REL_MAX_DIST)
    assert np.all((frac[inner] > 1e-3) & (frac[inner] < 1 - 1e-3))
    large = np.minimum(max_exact + np.floor(scaled + 1e-6).astype(np.int64), REL_BUCKETS - 1)
    return np.where(d < max_exact, d, large).astype(np.int32)


def _perm_columns():
    sizes = [1024] + [256] * 6 + [48, 1024, 1024, 1024, 16]
    offs = np.concatenate([[0], np.cumsum(sizes)])
    nq, kcmp, vcmp, kslc, vslc, kwin, vwin, ngate, fq, fk, fv, ff = [
        np.arange(offs[i], offs[i + 1]) for i in range(12)]

    def interleave(k, v, n):
        return np.stack([k.reshape(n, HEAD_DIM), v.reshape(n, HEAD_DIM)], axis=1).reshape(-1)

    main = np.concatenate([nq, kcmp, vcmp, interleave(kslc, vslc, N_NSA_KV),
                           interleave(kwin, vwin, N_NSA_KV), fq, interleave(fk, fv, N_FOX_HEADS)])
    assert main.shape[0] == MAIN_COLS
    misc = np.concatenate([ngate, ff])
    return main.astype(np.int32), misc.astype(np.int32)


def _proj_kernel(x_ref, nw_ref, w_ref, wm_ref, o_ref, om_ref, xn_sc):
    @pl.when(pl.program_id(1) == 0)
    def _():
        x = x_ref[...]
        y = x * lax.rsqrt(jnp.mean(x * x, axis=-1, keepdims=True) + NORM_EPS) * nw_ref[...]
        xn = y.astype(BF16)
        xn_sc[...] = xn
        om_ref[...] = _dot(xn, wm_ref[...])

    o_ref[...] = _dot(xn_sc[...], w_ref[...]).astype(BF16)


def _proj(x2, norm_w, w_main, w_misc, tm=512, tn=512):
    n, d = x2.shape
    return pl.pallas_call(
        _proj_kernel,
        grid=(n // tm, MAIN_COLS // tn),
        in_specs=[pl.BlockSpec((tm, d), lambda i, j: (i, 0)),
                  pl.BlockSpec((1, d), lambda i, j: (0, 0)),
                  pl.BlockSpec((d, tn), lambda i, j: (0, j)),
                  pl.BlockSpec((d, LANES), lambda i, j: (0, 0))],
        out_specs=[pl.BlockSpec((tm, tn), lambda i, j: (i, j)),
                   pl.BlockSpec((tm, LANES), lambda i, j: (i, 0))],
        out_shape=[jax.ShapeDtypeStruct((n, MAIN_COLS), BF16),
                   jax.ShapeDtypeStruct((n, LANES), F32)],
        scratch_shapes=[pltpu.VMEM((tm, d), BF16)],
        compiler_params=_cparams(2),
        name="proj",
    )(x2, norm_w, w_main, w_misc)


def _compress_kernel(x_ref, pea_ref, peb_ref, w1a_ref, w1b_ref, w2_ref, o_ref):
    x = x_ref[0, 0].astype(F32)
    xa = (x + pea_ref[0]).astype(BF16)
    xb = (x + peb_ref[0]).astype(BF16)
    a = _dot(xa, w1a_ref[0])
    b = _dot(xb, w1b_ref[0])
    rows = a.shape[0]
    pre = a + pltpu.roll(b, rows - 1, 0)
    hid = pre * jax.nn.sigmoid(pre)
    out = _dot(hid.astype(BF16), w2_ref[0])
    for g in range(N_NSA_KV):
        o_ref[0, 0, g] = out[:, g * HEAD_DIM:(g + 1) * HEAD_DIM]


def _compress(xkv, pea, peb, w1a, w1b, w2):
    _, b, rows, width = xkv.shape
    hid = N_NSA_KV * CMP_HIDDEN
    return pl.pallas_call(
        _compress_kernel,
        grid=(2, b),
        in_specs=[pl.BlockSpec((1, 1, rows, width), lambda s, i: (s, i, 0, 0)),
                  pl.BlockSpec((1, 1, width), lambda s, i: (s, 0, 0)),
                  pl.BlockSpec((1, 1, width), lambda s, i: (s, 0, 0)),
                  pl.BlockSpec((1, width, hid), lambda s, i: (s, 0, 0)),
                  pl.BlockSpec((1, width, hid), lambda s, i: (s, 0, 0)),
                  pl.BlockSpec((1, hid, N_NSA_KV * HEAD_DIM), lambda s, i: (s, 0, 0))],
        out_specs=pl.BlockSpec((1, 1, N_NSA_KV, rows, HEAD_DIM), lambda s, i: (s, i, 0, 0, 0)),
        out_shape=jax.ShapeDtypeStruct((2, b, N_NSA_KV, rows, HEAD_DIM), F32),
        compiler_params=_cparams(2),
        name="compress",
    )(xkv, pea, peb, w1a, w1b, w2)


def _foxprep_kernel(misc_ref, fb_ref, tri_ref, eye_ref, c_ref, ct_ref):
    z = misc_ref[0][:, MISC_FF:MISC_FF + N_FOX_HEADS] + fb_ref[...]
    logf = jnp.minimum(z, 0.0) - jnp.log(1.0 + jnp.exp(-jnp.abs(z)))
    tri = tri_ref[...]
    c = None
    for part in _split3(logf):
        term = _dot(tri, part)
        c = term if c is None else c + term
    ct = None
    for part in _split3(c):
        term = _nt(eye_ref[...], part)
        ct = term if ct is None else ct + term
    for p in range(N_FOX_HEADS // 2):
        c_ref[0, p] = c[:, 2 * p:2 * p + 2]
        ct_ref[0, p] = ct[2 * p:2 * p + 2, :]


def _foxprep(misc3, fb, tri, eye):
    b, t, _ = misc3.shape
    hp = N_FOX_HEADS // 2
    return pl.pallas_call(
        _foxprep_kernel,
        grid=(b,),
        in_specs=[pl.BlockSpec((1, t, LANES), lambda i: (i, 0, 0)),
                  pl.BlockSpec((1, N_FOX_HEADS), lambda i: (0, 0)),
                  pl.BlockSpec((t, t), lambda i: (0, 0)),
                  pl.BlockSpec((N_FOX_HEADS, N_FOX_HEADS), lambda i: (0, 0))],
        out_specs=[pl.BlockSpec((1, hp, t, 2), lambda i: (i, 0, 0, 0)),
                   pl.BlockSpec((1, hp, 2, t), lambda i: (i, 0, 0, 0))],
        out_shape=[jax.ShapeDtypeStruct((b, hp, t, 2), F32),
                   jax.ShapeDtypeStruct((b, hp, 2, t), F32)],
        compiler_params=_cparams(1),
        name="foxprep",
    )(misc3, fb, tri, eye)


def _cmpsel_kernel(q_ref, kvc_ref, bias_ref, gate_ref, impm_ref, eye_ref, o_ref, sel_ref):
    tq = q_ref.shape[1]
    ncp = kvc_ref.shape[3]
    ns = impm_ref.shape[0]
    t0 = pl.program_id(2) * tq
    kc = kvc_ref[0, 0, 0].astype(BF16)
    vc = kvc_ref[1, 0, 0].astype(BF16)
    t_col = t0 + lax.broadcasted_iota(jnp.int32, (tq, ncp), 0)
    c_row = lax.broadcasted_iota(jnp.int32, (tq, ncp), 1)
    valid = t_col >= c_row * CMP_STRIDE + (CMP_BLOCK - 1)
    q = q_ref[0]
    gate = jax.nn.sigmoid(gate_ref[0, 0])
    p_grp = jnp.zeros((tq, ncp), F32)
    for r in range(NSA_GQA):
        s = _nt(q[:, r * HEAD_DIM:(r + 1) * HEAD_DIM], kc) * SCALE + bias_ref[0, r]
        s = jnp.where(valid, s, NEG_INF)
        m = jnp.max(s, axis=-1, keepdims=True)
        e = jnp.where(valid, jnp.exp(s - m), 0.0)
        p = e / jnp.maximum(jnp.sum(e, axis=-1, keepdims=True), 1e-30)
        o = _dot(p.astype(BF16), vc)
        o_ref[0, :, r * HEAD_DIM:(r + 1) * HEAD_DIM] = o * gate[:, 3 * r:3 * r + 1]
        p_grp = p_grp + p

    imp = None
    for part in _split3(p_grp):
        term = _nt(impm_ref[...], part)
        imp = term if imp is None else imp + term
    blk = lax.broadcasted_iota(jnp.int32, (ns, tq), 0)
    t_row = t0 + lax.broadcasted_iota(jnp.int32, (ns, tq), 1)
    cur = t_row // SLC_BLOCK
    forced = (blk == 0) | (blk == cur) | (blk == cur - 1)
    score = jnp.where(blk * SLC_BLOCK <= t_row, imp + jnp.where(forced, FORCE_BONUS, 0.0), NEG_INF)
    rank = jnp.zeros((ns, tq), F32)
    for m_blk in range(ns):
        other = score[m_blk:m_blk + 1, :]
        ahead = (other > score) | ((other == score) & (blk > m_blk))
        rank = rank + jnp.where(ahead, 1.0, 0.0)
    sel_t = jnp.where(rank < float(min(SLC_TOP_N, ns)), 1.0, 0.0).astype(BF16)
    sel_ref[0, 0] = _nt(eye_ref[...], sel_t)


def _cmpsel(main3, kvc, bias_c, gates, impm, eye):
    b, t, _ = main3.shape
    ncp = kvc.shape[3]
    ns = impm.shape[0]
    tq = AT_TQ
    qw = NSA_GQA * HEAD_DIM
    return pl.pallas_call(
        _cmpsel_kernel,
        grid=(b, N_NSA_KV, t // tq),
        in_specs=[pl.BlockSpec((1, tq, qw), lambda i, g, j: (i, j, g)),
                  pl.BlockSpec((2, 1, 1, ncp, HEAD_DIM), lambda i, g, j: (0, i, g, 0, 0)),
                  pl.BlockSpec((1, NSA_GQA, tq, ncp), lambda i, g, j: (g, 0, j, 0)),
                  pl.BlockSpec((1, 1, tq, 3 * NSA_GQA), lambda i, g, j: (i, g, j, 0)),
                  pl.BlockSpec((ns, ncp), lambda i, g, j: (0, 0)),
                  pl.BlockSpec((tq, tq), lambda i, g, j: (0, 0))],
        out_specs=[pl.BlockSpec((1, tq, qw), lambda i, g, j: (i, j, g)),
                   pl.BlockSpec((1, 1, tq, ns), lambda i, g, j: (i, g, j, 0))],
        out_shape=[jax.ShapeDtypeStruct((b, t, N_NSA_HEADS * HEAD_DIM), F32),
                   jax.ShapeDtypeStruct((b, N_NSA_KV, t, ns), F32)],
        compiler_params=_cparams(3),
        name="cmpsel",
    )(main3, kvc, bias_c, gates, impm, eye)


def _online_update(s, v, m_ref, l_ref, acc_ref, idx):
    m_prev = m_ref[idx]
    m_new = jnp.maximum(m_prev, jnp.max(s, axis=-1, keepdims=True))
    alpha = jnp.exp(m_prev - m_new)
    p = jnp.exp(s - m_new[:, :1])
    l_ref[idx] = alpha * l_ref[idx] + jnp.sum(p, axis=-1, keepdims=True)
    acc_ref[idx] = alpha[:, :HEAD_DIM] * acc_ref[idx] + _dot(p.astype(BF16), v)
    m_ref[idx] = m_new


def _online_init(m_ref, l_ref, acc_ref):
    m_ref[...] = jnp.full(m_ref.shape, NEG_INF, F32)
    l_ref[...] = jnp.zeros(l_ref.shape, F32)
    acc_ref[...] = jnp.zeros(acc_ref.shape, F32)


def _slc_kernel(q_ref, kv_ref, sel_ref, bias_ref, gate_ref, o_ref, m_sc, l_sc, acc_sc):
    tq = q_ref.shape[1]
    tk = AT_TK
    ns = sel_ref.shape[3]
    qt = pl.program_id(2)
    q = q_ref[0]
    sel = sel_ref[0, 0].astype(BF16)
    blk = lax.broadcasted_iota(jnp.int32, (ns, tk), 0)
    key = lax.broadcasted_iota(jnp.int32, (ns, tk), 1)
    _online_init(m_sc, l_sc, acc_sc)

    def tile(j, carry):
        start = pl.multiple_of(j * tk, tk)
        kvt = kv_ref[0, pl.ds(start, tk), :]
        k = kvt[:, :HEAD_DIM]
        v = kvt[:, HEAD_DIM:]
        expand = jnp.where(blk == (start + key) // SLC_BLOCK, 1.0, 0.0).astype(BF16)
        allowed = _dot(sel, expand) > 0.5
        kind = jnp.minimum(qt - j, 2)
        for r in range(NSA_GQA):
            s = _nt(q[:, r * HEAD_DIM:(r + 1) * HEAD_DIM], k) * SCALE + bias_ref[0, r, kind]
            s = jnp.where(allowed, s, NEG_INF)
            _online_update(s, v, m_sc, l_sc, acc_sc, r)
        return carry

    lax.fori_loop(0, qt + 1, tile, 0)
    gate = jax.nn.sigmoid(gate_ref[0, 0])
    for r in range(NSA_GQA):
        o = acc_sc[r] / l_sc[r][:, :HEAD_DIM]
        o_ref[0, :, r * HEAD_DIM:(r + 1) * HEAD_DIM] = o * gate[:, 3 * r + 1:3 * r + 2]


def _slc(main3, sel, bias_t, gates):
    b, t, _ = main3.shape
    tq = AT_TQ
    qw = NSA_GQA * HEAD_DIM
    ns = sel.shape[3]
    kinds = bias_t.shape[2]
    return pl.pallas_call(
        _slc_kernel,
        grid=(b, N_NSA_KV, t // tq),
        in_specs=[pl.BlockSpec((1, tq, qw), lambda i, g, j: (i, j, g)),
                  pl.BlockSpec((1, t, LANES), lambda i, g, j: (i, 0, COL_SLC // LANES + g)),
                  pl.BlockSpec((1, 1, tq, ns), lambda i, g, j: (i, g, j, 0)),
                  pl.BlockSpec((1, NSA_GQA, kinds, tq, AT_TK), lambda i, g, j: (g, 0, 0, 0, 0)),
                  pl.BlockSpec((1, 1, tq, 3 * NSA_GQA), lambda i, g, j: (i, g, j, 0))],
        out_specs=pl.BlockSpec((1, tq, qw), lambda i, g, j: (i, j, g)),
        out_shape=jax.ShapeDtypeStruct((b, t, N_NSA_HEADS * HEAD_DIM), F32),
        scratch_shapes=[pltpu.VMEM((NSA_GQA, tq, LANES), F32),
                        pltpu.VMEM((NSA_GQA, tq, LANES), F32),
                        pltpu.VMEM((NSA_GQA, tq, HEAD_DIM), F32)],
        compiler_params=_cparams(3),
        name="slc",
    )(main3, main3, sel, bias_t, gates)


def _win_kernel(q_ref, kv_ref, bias_ref, gate_ref, o_ref, m_sc, l_sc, acc_sc):
    tq = q_ref.shape[1]
    tk = AT_TK
    kinds = bias_ref.shape[2]
    qt = pl.program_id(2)
    q = q_ref[0]
    _online_init(m_sc, l_sc, acc_sc)

    def tile(delta):
        start = pl.multiple_of((qt - delta) * tk, tk)
        kvt = kv_ref[0, pl.ds(start, tk), :]
        k = kvt[:, :HEAD_DIM]
        v = kvt[:, HEAD_DIM:]
        for r in range(NSA_GQA):
            s = _nt(q[:, r * HEAD_DIM:(r + 1) * HEAD_DIM], k) * SCALE + bias_ref[0, r, delta]
            _online_update(s, v, m_sc, l_sc, acc_sc, r)

    tile(0)
    for delta in range(1, kinds):
        pl.when(qt >= delta)(functools.partial(tile, delta))

    gate = jax.nn.sigmoid(gate_ref[0, 0])
    for r in range(NSA_GQA):
        o = acc_sc[r] / l_sc[r][:, :HEAD_DIM]
        o_ref[0, :, r * HEAD_DIM:(r + 1) * HEAD_DIM] = o * gate[:, 3 * r + 2:3 * r + 3]


def _win(main3, bias_t, gates):
    b, t, _ = main3.shape
    tq = AT_TQ
    qw = NSA_GQA * HEAD_DIM
    kinds = bias_t.shape[2]
    return pl.pallas_call(
        _win_kernel,
        grid=(b, N_NSA_KV, t // tq),
        in_specs=[pl.BlockSpec((1, tq, qw), lambda i, g, j: (i, j, g)),
                  pl.BlockSpec((1, t, LANES), lambda i, g, j: (i, 0, COL_WIN // LANES + g)),
                  pl.BlockSpec((1, NSA_GQA, kinds, tq, AT_TK), lambda i, g, j: (g, 0, 0, 0, 0)),
                  pl.BlockSpec((1, 1, tq, 3 * NSA_GQA), lambda i, g, j: (i, g, j, 0))],
        out_specs=pl.BlockSpec((1, tq, qw), lambda i, g, j: (i, j, g)),
        out_shape=jax.ShapeDtypeStruct((b, t, N_NSA_HEADS * HEAD_DIM), F32),
        scratch_shapes=[pltpu.VMEM((NSA_GQA, tq, LANES), F32),
                        pltpu.VMEM((NSA_GQA, tq, LANES), F32),
                        pltpu.VMEM((NSA_GQA, tq, HEAD_DIM), F32)],
        compiler_params=_cparams(3),
        name="win",
    )(main3, main3, bias_t, gates)


def _fox_kernel(q_ref, kv0_ref, kv1_ref, c_ref, ct_ref, o_ref, m_sc, l_sc, acc_sc, *, tk):
    tq = q_ref.shape[1]
    qt = pl.program_id(2)
    t0 = qt * tq
    q = q_ref[0]
    c_col = c_ref[0, 0]
    rel = (lax.broadcasted_iota(jnp.int32, (tq, tk), 0) - lax.broadcasted_iota(jnp.int32, (tq, tk), 1))
    _online_init(m_sc, l_sc, acc_sc)
    n_tiles = (t0 + tq + tk - 1) // tk

    def tile(j, carry):
        start = pl.multiple_of(j * tk, tk)
        causal = rel + (t0 - start) >= 0
        for h, kv_ref in enumerate((kv0_ref, kv1_ref)):
            kvt = kv_ref[0, pl.ds(start, tk), :]
            k = kvt[:, :HEAD_DIM]
            v = kvt[:, HEAD_DIM:]
            c_row = ct_ref[0, 0, h:h + 1, pl.ds(start, tk)]
            s = _nt(q[:, h * HEAD_DIM:(h + 1) * HEAD_DIM], k) * SCALE
            s = s + c_col[:, h:h + 1] - c_row
            s = jnp.where(causal, s, NEG_INF)
            _online_update(s, v, m_sc, l_sc, acc_sc, h)
        return carry

    lax.fori_loop(0, n_tiles, tile, 0)
    for h in range(2):
        o_ref[0, :, h * HEAD_DIM:(h + 1) * HEAD_DIM] = acc_sc[h] / l_sc[h][:, :HEAD_DIM]


def _fox(main3, c, ct, tq=256, tk=512):
    b, t, _ = main3.shape
    hp = N_FOX_HEADS // 2
    tk = min(tk, t)
    return pl.pallas_call(
        functools.partial(_fox_kernel, tk=tk),
        grid=(b, hp, t // tq),
        in_specs=[pl.BlockSpec((1, tq, LANES), lambda i, p, j: (i, j, COL_FQ // LANES + p)),
                  pl.BlockSpec((1, t, LANES), lambda i, p, j: (i, 0, COL_FKV // LANES + 2 * p)),
                  pl.BlockSpec((1, t, LANES), lambda i, p, j: (i, 0, COL_FKV // LANES + 2 * p + 1)),
                  pl.BlockSpec((1, 1, tq, 2), lambda i, p, j: (i, p, j, 0)),
                  pl.BlockSpec((1, 1, 2, t), lambda i, p, j: (i, p, 0, 0))],
        out_specs=pl.BlockSpec((1, tq, LANES), lambda i, p, j: (i, j, p)),
        out_shape=jax.ShapeDtypeStruct((b, t, N_FOX_HEADS * HEAD_DIM), F32),
        scratch_shapes=[pltpu.VMEM((2, tq, LANES), F32),
                        pltpu.VMEM((2, tq, LANES), F32),
                        pltpu.VMEM((2, tq, HEAD_DIM), F32)],
        compiler_params=_cparams(3),
        name="fox",
    )(main3, main3, main3, c, ct)


def _rms(x, w):
    return x * lax.rsqrt(jnp.mean(x * x, axis=-1, keepdims=True) + NORM_EPS) * w


def _outproj_kernel(oc_ref, os_ref, ow_ref, of_ref, x_ref, nnw_ref, fnw_ref, wo_ref, ffw_ref,
                    wr_ref, br_ref, h_ref, hn_ref, eid_ref, wt_ref):
    o_nsa = oc_ref[...] + os_ref[...] + ow_ref[...]
    mixed = jnp.concatenate([_rms(o_nsa, nnw_ref[...]), _rms(of_ref[...], fnw_ref[...])], axis=-1)
    h = x_ref[...] + _dot(mixed.astype(BF16), wo_ref[...])
    h_ref[...] = h
    hn = _rms(h, ffw_ref[...])
    hn_ref[...] = hn

    h_hi, h_mid, h_lo = _split3(hn)
    w_hi = wr_ref[0]
    w_lo = wr_ref[1]
    logits = (_dot(h_hi, w_hi) + _dot(h_mid, w_hi) + _dot(h_hi, w_lo) + _dot(h_lo, w_hi)
              + _dot(h_mid, w_lo)) + br_ref[...]
    tm = logits.shape[0]
    lane = lax.broadcasted_iota(jnp.int32, (tm, LANES), 1)
    big = jnp.int32(LANES)
    is_grp = (lane >= N_EXPERTS) & (lane < N_EXPERTS + N_GROUPS)
    glog = jnp.where(is_grp, logits, NEG_INF)
    gmax = jnp.max(glog, axis=-1, keepdims=True)
    gsel = jnp.min(jnp.where(glog == gmax, lane, big), axis=-1, keepdims=True) - N_EXPERTS
    p_gsel = 1.0 / jnp.sum(jnp.where(is_grp, jnp.exp(glog - gmax), 0.0), axis=-1, keepdims=True)
    in_grp = (lane < N_EXPERTS) & (lane // EXPERTS_PER_GROUP == gsel)
    e1 = jnp.where(in_grp, logits, NEG_INF)
    v1 = jnp.max(e1, axis=-1, keepdims=True)
    i1 = jnp.min(jnp.where(e1 == v1, lane, big), axis=-1, keepdims=True)
    e2 = jnp.where(lane == i1, NEG_INF, e1)
    v2 = jnp.max(e2, axis=-1, keepdims=True)
    i2 = jnp.min(jnp.where(e2 == v2, lane, big), axis=-1, keepdims=True)
    ex = jnp.exp(v2 - v1)
    w1 = p_gsel / (1.0 + ex)
    w2 = p_gsel * ex / (1.0 + ex)
    eid_ref[...] = jnp.where(lane == 0, i1, jnp.where(lane == 1, i2, 0))
    wt_ref[...] = jnp.where(lane == 0, w1, jnp.where(lane == 1, w2, 0.0))


def _outproj(oc, osl, ow, of, x2, nnw, fnw, wo, ffw, wr, br, tm=256):
    n, d = x2.shape
    half = oc.shape[1]
    row = lambda i: (i, 0)
    fixed = lambda i: (0, 0)
    return pl.pallas_call(
        _outproj_kernel,
        grid=(n // tm,),
        in_specs=[pl.BlockSpec((tm, half), row), pl.BlockSpec((tm, half), row),
                  pl.BlockSpec((tm, half), row), pl.BlockSpec((tm, half), row),
                  pl.BlockSpec((tm, d), row),
                  pl.BlockSpec((1, half), fixed), pl.BlockSpec((1, half), fixed),
                  pl.BlockSpec((d, d), fixed), pl.BlockSpec((1, d), fixed),
                  pl.BlockSpec((2, d, LANES), lambda i: (0, 0, 0)), pl.BlockSpec((1, LANES), fixed)],
        out_specs=[pl.BlockSpec((tm, d), row), pl.BlockSpec((tm, d), row),
                   pl.BlockSpec((tm, LANES), row), pl.BlockSpec((tm, LANES), row)],
        out_shape=[jax.ShapeDtypeStruct((n, d), F32), jax.ShapeDtypeStruct((n, d), F32),
                   jax.ShapeDtypeStruct((n, LANES), jnp.int32), jax.ShapeDtypeStruct((n, LANES), F32)],
        compiler_params=_cparams(1),
        name="outproj",
    )(oc, osl, ow, of, x2, nnw, fnw, wo, ffw, wr, br)


def _moe_kernel(te_ref, src_ref, nt_ref, hn_hbm, ws_ref, wg_ref, wu_ref, wd_ref, y_ref,
                xbuf, sem, wg_sc, wu_sc, wd_sc):
    tm = xbuf.shape[1]
    i = pl.program_id(0)
    n_used = nt_ref[0]
    slot = i % 2

    def gather(tile, slot_):
        def row(r, carry):
            tok = src_ref[tile * tm + r]
            pltpu.make_async_copy(hn_hbm.at[pl.ds(tok, 1)], xbuf.at[slot_, pl.ds(r, 1)],
                                  sem.at[slot_]).start()
            return carry
        lax.fori_loop(0, tm, row, 0)

    @pl.when(i == 0)
    def _():
        gather(0, 0)

    @pl.when(i + 1 < n_used)
    def _():
        gather(i + 1, 1 - slot)

    prev = te_ref[jnp.maximum(i - 1, 0)]

    @pl.when((i == 0) | (te_ref[i] != prev))
    def _():
        wg_sc[...] = wg_ref[0].astype(BF16)
        wu_sc[...] = wu_ref[0].astype(BF16)
        wd_sc[...] = wd_ref[0].astype(BF16)

    @pl.when(i < n_used)
    def _():
        pltpu.make_async_copy(hn_hbm.at[pl.ds(0, tm)], xbuf.at[slot], sem.at[slot]).wait()
        x = xbuf[slot].astype(BF16)
        gate = _dot(x, wg_sc[...])
        up = _dot(x, wu_sc[...])
        hid = gate * jax.nn.sigmoid(gate) * up
        y_ref[...] = ws_ref[...] * _dot(hid.astype(BF16), wd_sc[...])

    @pl.when(i >= n_used)
    def _():
        y_ref[...] = jnp.zeros(y_ref.shape, F32)


def _moe(tile_e, src_tok, n_used, hn, w_sorted, wg, wu, wd, tm):
    n_tiles = tile_e.shape[0]
    d = hn.shape[1]
    ff = wg.shape[2]
    grid_spec = pltpu.PrefetchScalarGridSpec(
        num_scalar_prefetch=3,
        grid=(n_tiles,),
        in_specs=[pl.BlockSpec(memory_space=pl.ANY),
                  pl.BlockSpec((tm, 1), lambda i, te, src, nt: (i, 0)),
                  pl.BlockSpec((1, d, ff), lambda i, te, src, nt: (te[i], 0, 0)),
                  pl.BlockSpec((1, d, ff), lambda i, te, src, nt: (te[i], 0, 0)),
                  pl.BlockSpec((1, ff, d), lambda i, te, src, nt: (te[i], 0, 0))],
        out_specs=pl.BlockSpec((tm, d), lambda i, te, src, nt: (i, 0)),
        scratch_shapes=[pltpu.VMEM((2, tm, d), F32),
                        pltpu.SemaphoreType.DMA((2,)),
                        pltpu.VMEM((d, ff), BF16), pltpu.VMEM((d, ff), BF16), pltpu.VMEM((ff, d), BF16)],
    )
    return pl.pallas_call(
        _moe_kernel,
        grid_spec=grid_spec,
        out_shape=jax.ShapeDtypeStruct((n_tiles * tm, d), F32),
        compiler_params=_cparams(1),
        name="moe",
    )(tile_e, src_tok, n_used, hn, w_sorted, wg, wu, wd)


def _combine_kernel(pos_ref, y_hbm, h_ref, fw_ref, o_ref, ybuf, sem):
    tm = h_ref.shape[0]
    i = pl.program_id(0)
    n = pl.num_programs(0)
    slot = i % 2

    def gather(tile, slot_):
        def row(r, carry):
            for k in range(2):
                p = pos_ref[2 * (tile * tm + r) + k]
                pltpu.make_async_copy(y_hbm.at[pl.ds(p, 1)], ybuf.at[slot_, k, pl.ds(r, 1)],
                                      sem.at[slot_]).start()
            return carry
        lax.fori_loop(0, tm, row, 0)

    @pl.when(i == 0)
    def _():
        gather(0, 0)

    @pl.when(i + 1 < n)
    def _():
        gather(i + 1, 1 - slot)

    for k in range(2):
        pltpu.make_async_copy(y_hbm.at[pl.ds(0, tm)], ybuf.at[slot, k], sem.at[slot]).wait()
    out = h_ref[...] + (ybuf[slot, 0] + ybuf[slot, 1])
    o_ref[...] = _rms(out, fw_ref[...])


def _combine(pos, y, h, fw, tm=256):
    n, d = h.shape
    grid_spec = pltpu.PrefetchScalarGridSpec(
        num_scalar_prefetch=1,
        grid=(n // tm,),
        in_specs=[pl.BlockSpec(memory_space=pl.ANY),
                  pl.BlockSpec((tm, d), lambda i, pos_: (i, 0)),
                  pl.BlockSpec((1, d), lambda i, pos_: (0, 0))],
        out_specs=pl.BlockSpec((tm, d), lambda i, pos_: (i, 0)),
        scratch_shapes=[pltpu.VMEM((2, 2, tm, d), F32), pltpu.SemaphoreType.DMA((2,))],
    )
    return pl.pallas_call(
        _combine_kernel,
        grid_spec=grid_spec,
        out_shape=jax.ShapeDtypeStruct((n, d), F32),
        compiler_params=_cparams(1),
        name="combine",
    )(pos, y, h, fw)


def _bias_tables(rel_table, t, ncp):
    buckets = _bucket_table(max(t, 1024))
    bd = jnp.take(rel_table, jnp.asarray(buckets), axis=0)
    dist_c = np.arange(t)[:, None] - (np.arange(ncp)[None, :] * CMP_STRIDE + CMP_BLOCK - 1)
    bias_c = jnp.take(bd, jnp.asarray(np.clip(dist_c, 0, None)), axis=0)
    bias_c = jnp.transpose(bias_c, (2, 0, 1)).reshape(N_NSA_KV, NSA_GQA, t, ncp)
    i = np.arange(AT_TQ)[:, None]
    j = np.arange(AT_TK)[None, :]
    kinds = WINDOW // AT_TK + 1
    dist = np.stack([d * AT_TK + i - j for d in range(kinds)])
    ok = (dist >= 0) & (dist < WINDOW)
    tiles = jnp.take(bd, jnp.asarray(np.clip(dist, 0, None)), axis=0)
    tiles = jnp.where(jnp.asarray(ok)[..., None], tiles, NEG_INF)
    tiles = jnp.transpose(tiles, (3, 0, 1, 2)).reshape(N_NSA_KV, NSA_GQA, kinds, AT_TQ, AT_TK)
    return bias_c, tiles


def _compress_weights(pe, w1, w2):
    half = CMP_STRIDE
    eye = jnp.eye(N_NSA_KV, dtype=F32)

    def expand_w1(w):
        w = w.reshape(half, HEAD_DIM, CMP_HIDDEN)
        return jnp.einsum("idn,gh->igdhn", w, eye).reshape(half * N_NSA_KV * HEAD_DIM, N_NSA_KV * CMP_HIDDEN)

    def expand_pe(p):
        return jnp.broadcast_to(p[:, None, :], (half, N_NSA_KV, HEAD_DIM)).reshape(1, -1)

    w1a = expand_w1(w1[:half * HEAD_DIM]).astype(BF16)
    w1b = expand_w1(w1[half * HEAD_DIM:]).astype(BF16)
    w2x = jnp.einsum("nd,gh->gnhd", w2, eye).reshape(N_NSA_KV * CMP_HIDDEN, N_NSA_KV * HEAD_DIM).astype(BF16)
    return expand_pe(pe[:half]), expand_pe(pe[half:]), w1a, w1b, w2x


def _routing_tables(eid, wts, tm):
    n = eid.shape[0]
    e_flat = eid.reshape(-1)
    onehot = (e_flat[:, None] == jnp.arange(N_EXPERTS, dtype=jnp.int32)[None, :]).astype(jnp.int32)
    csum = jnp.cumsum(onehot, axis=0)
    rank = jnp.take_along_axis(csum, e_flat[:, None], axis=1)[:, 0] - 1
    counts = csum[-1]
    padded = ((counts + tm - 1) // tm) * tm
    ends = jnp.cumsum(padded)
    starts = ends - padded
    pos = (starts[e_flat] + rank).astype(jnp.int32)
    n_tiles = (2 * n) // tm + N_EXPERTS
    rows = n_tiles * tm
    src_tok = jnp.zeros((rows,), jnp.int32).at[pos].set(jnp.arange(2 * n, dtype=jnp.int32) // 2)
    w_sorted = jnp.zeros((rows,), F32).at[pos].set(wts.reshape(-1))
    tile_start = jnp.arange(n_tiles, dtype=jnp.int32) * tm
    tile_e = jnp.minimum(jnp.searchsorted(ends, tile_start, side="right"), N_EXPERTS - 1).astype(jnp.int32)
    n_used = (ends[-1] // tm).astype(jnp.int32).reshape(1)
    last_e = tile_e[jnp.maximum(n_used[0] - 1, 0)]
    tile_e = jnp.where(tile_start < ends[-1], tile_e, last_e)
    return tile_e, src_tok, n_used, w_sorted.reshape(rows, 1), pos


def kernel(x, attn_norm_w, w_in, cmp_pe_k, cmp_pe_v, cmp_k_w1, cmp_k_w2, cmp_v_w1, cmp_v_w2,
           rel_bias_table, fox_forget_b, nsa_out_norm_w, fox_out_norm_w, w_out, ffn_norm_w,
           router_group_w, router_group_b, router_expert_w, router_expert_b,
           expert_w_gate, expert_w_up, expert_w_down, final_norm_w):
    b, t, d = x.shape
    n = b * t
    depth = w_in.shape[0]
    assert t % 512 == 0 and t >= WINDOW + AT_TQ and d == 2048
    rows = t // CMP_STRIDE
    main_cols, misc_cols = _perm_columns()
    bias_c, bias_t = _bias_tables(rel_bias_table, t, rows)
    ns = t // SLC_BLOCK
    ratio = SLC_BLOCK // CMP_STRIDE
    span = CMP_BLOCK // CMP_STRIDE
    nc = (t - CMP_BLOCK) // CMP_STRIDE + 1
    impm = np.zeros((ns, rows), np.float32)
    for blk in range(ns):
        for a in range(ratio):
            for s in range(span):
                c = blk * ratio + a - s
                if 0 <= c < nc:
                    impm[blk, c] += 1.0
    impm = jnp.asarray(impm, BF16)
    eye_q = jnp.eye(AT_TQ, dtype=BF16)
    eye_h = jnp.eye(N_FOX_HEADS, dtype=BF16)
    tri = jnp.asarray(np.tril(np.ones((t, t), np.float32)), BF16)
    moe_tm = 512

    h = x.reshape(n, d)
    for layer in range(depth):
        w_main = jnp.take(w_in[layer], jnp.asarray(main_cols), axis=1).astype(BF16)
        w_misc = jnp.take(w_in[layer], jnp.asarray(misc_cols), axis=1)
        w_misc = jnp.pad(w_misc, ((0, 0), (0, LANES - w_misc.shape[1]))).astype(BF16)
        main, misc = _proj(h, attn_norm_w[layer][None, :], w_main, w_misc)
        main3 = main.reshape(b, t, MAIN_COLS)
        misc3 = misc.reshape(b, t, LANES)

        xkv = jnp.stack([main3[:, :, COL_KCMP:COL_KCMP + 256], main3[:, :, COL_VCMP:COL_VCMP + 256]])
        xkv = xkv.reshape(2, b, rows, CMP_STRIDE * 256)
        pk = _compress_weights(cmp_pe_k[layer], cmp_k_w1[layer], cmp_k_w2[layer])
        pv = _compress_weights(cmp_pe_v[layer], cmp_v_w1[layer], cmp_v_w2[layer])
        kvc = _compress(xkv, *[jnp.stack([a, c]) for a, c in zip(pk, pv)])

        gates = misc3[:, :, MISC_GATE:MISC_GATE + 48].reshape(b, t, N_NSA_KV, 12).transpose(0, 2, 1, 3)
        o_cmp, sel = _cmpsel(main3, kvc, bias_c, gates, impm, eye_q)
        o_slc = _slc(main3, sel, bias_t[:, :, :3], gates)
        o_win = _win(main3, bias_t, gates)

        c, ct = _foxprep(misc3, fox_forget_b[layer][None, :], tri, eye_h)
        o_fox = _fox(main3, c, ct)

        wr = jnp.concatenate([router_expert_w[layer], router_group_w[layer]], axis=1)
        wr = jnp.pad(wr, ((0, 0), (0, LANES - wr.shape[1])))
        wr_hi = wr.astype(BF16)
        wr_lo = (wr - wr_hi.astype(F32)).astype(BF16)
        br = jnp.concatenate([router_expert_b[layer], router_group_b[layer]])
        br = jnp.pad(br, (0, LANES - br.shape[0]))[None, :]
        half = N_NSA_HEADS * HEAD_DIM
        h, hn, eid, wts = _outproj(
            o_cmp.reshape(n, half), o_slc.reshape(n, half), o_win.reshape(n, half), o_fox.reshape(n, half),
            h, nsa_out_norm_w[layer][None, :], fox_out_norm_w[layer][None, :],
            w_out[layer].astype(BF16), ffn_norm_w[layer][None, :], jnp.stack([wr_hi, wr_lo]), br)

        tile_e, src_tok, n_used, w_sorted, pos = _routing_tables(eid[:, :2], wts[:, :2], moe_tm)
        y = _moe(tile_e, src_tok, n_used, hn, w_sorted,
                 expert_w_gate[layer], expert_w_up[layer], expert_w_down[layer], moe_tm)
        last = layer == depth - 1
        fw = final_norm_w if last else jnp.ones((d,), F32)
        assert last, "the fused final norm assumes a single layer"
        h = _combine(pos, y, h, fw[None, :])
    return h.reshape(b, t, d)
```

```python
import functools
import math

import numpy as np
import jax
import jax.numpy as jnp
from jax import lax
from jax.experimental import pallas as pl
from jax.experimental.pallas import tpu as pltpu

F32 = jnp.float32
BF16 = jnp.bfloat16

HEAD_DIM = 64
N_NSA_HEADS = 16
N_FOX_HEADS = 16
NSA_GQA = 4
N_NSA_KV = 4
CMP_BLOCK = 32
CMP_STRIDE = 16
CMP_HIDDEN = 128
SLC_BLOCK = 64
SLC_TOP_N = 16
WINDOW = 512
REL_BUCKETS = 32
REL_MAX_DIST = 128
N_GROUPS = 4
EXPERTS_PER_GROUP = 8
N_EXPERTS = 32
EXPERT_FF = 512
NORM_EPS = 1e-6
NEG_INF = -1e30
FORCE_BONUS = 1e4
SCALE = HEAD_DIM ** -0.5

LANES = 128
VMEM_LIMIT = 56 * 1024 * 1024

COL_NQ = 0
COL_KCMP = 1024
COL_VCMP = 1280
COL_SLC = 1536
COL_WIN = 2048
COL_FQ = 2560
COL_FK = 3584
COL_FV = 4608
MAIN_COLS = 5632
MISC_GATE = 0
MISC_FF = 48

AT_TQ = 128
SLC_CHUNK = 512
SLC_SHIFT = 6
assert 1 << SLC_SHIFT == SLC_BLOCK


def _nt(a, b):
    return lax.dot_general(a, b, (((1,), (1,)), ((), ())), preferred_element_type=F32)


def _dot(a, b):
    return jnp.dot(a, b, preferred_element_type=F32)


def _split3(x):
    hi = x.astype(BF16)
    r = x - hi.astype(F32)
    mid = r.astype(BF16)
    r = r - mid.astype(F32)
    return hi, mid, r.astype(BF16)


def _cparams(grid_rank):
    return pltpu.CompilerParams(dimension_semantics=("arbitrary",) * grid_rank, vmem_limit_bytes=VMEM_LIMIT)


def _bucket_table(n):
    d = np.arange(n, dtype=np.int64)
    max_exact = REL_BUCKETS // 2
    rel = np.log(np.maximum(d, 1).astype(np.float64) / max_exact) / math.log(REL_MAX_DIST / max_exact)
    scaled = rel * (REL_BUCKETS - max_exact)
    frac = scaled - np.floor(scaled)
    inner = (d > max_exact) & (d < REL_MAX_DIST)
    assert np.all((frac[inner] > 1e-3) & (frac[inner] < 1 - 1e-3))
    large = np.minimum(max_exact + np.floor(scaled + 1e-6).astype(np.int64), REL_BUCKETS - 1)
    return np.where(d < max_exact, d, large).astype(np.int32)


def _project_weights(w):
    d = w.shape[0]
    sizes = [1024] + [256] * 6 + [48, 1024, 1024, 1024, 16]
    offs = np.concatenate([[0], np.cumsum(sizes)])
    nq, kcmp, vcmp, kslc, vslc, kwin, vwin, ngate, fq, fk, fv, ff = [
        w[:, int(offs[i]):int(offs[i + 1])] for i in range(12)]

    def interleave(k, v):
        k = k.reshape(d, N_NSA_KV, HEAD_DIM)
        v = v.reshape(d, N_NSA_KV, HEAD_DIM)
        return jnp.stack([k, v], axis=2).reshape(d, N_NSA_KV * 2 * HEAD_DIM)

    main = jnp.concatenate([nq, kcmp, vcmp, interleave(kslc, vslc), interleave(kwin, vwin), fq, fk, fv], axis=1)
    assert main.shape[1] == MAIN_COLS
    misc = jnp.concatenate([ngate, ff, jnp.zeros((d, LANES - 64), w.dtype)], axis=1)
    return main.astype(BF16), misc.astype(BF16)


def _proj_kernel(x_ref, nw_ref, w_ref, wm_ref, o_ref, om_ref, xn_sc):
    @pl.when(pl.program_id(1) == 0)
    def _():
        x = x_ref[...]
        y = x * lax.rsqrt(jnp.mean(x * x, axis=-1, keepdims=True) + NORM_EPS) * nw_ref[...]
        xn = y.astype(BF16)
        xn_sc[...] = xn
        om_ref[...] = _dot(xn, wm_ref[...])

    o_ref[...] = _dot(xn_sc[...], w_ref[...]).astype(BF16)


def _proj(x2, norm_w, w_main, w_misc, tm=512, tn=512):
    n, d = x2.shape
    return pl.pallas_call(
        _proj_kernel,
        grid=(n // tm, MAIN_COLS // tn),
        in_specs=[pl.BlockSpec((tm, d), lambda i, j: (i, 0)),
                  pl.BlockSpec((1, d), lambda i, j: (0, 0)),
                  pl.BlockSpec((d, tn), lambda i, j: (0, j)),
                  pl.BlockSpec((d, LANES), lambda i, j: (0, 0))],
        out_specs=[pl.BlockSpec((tm, tn), lambda i, j: (i, j)),
                   pl.BlockSpec((tm, LANES), lambda i, j: (i, 0))],
        out_shape=[jax.ShapeDtypeStruct((n, MAIN_COLS), BF16),
                   jax.ShapeDtypeStruct((n, LANES), F32)],
        scratch_shapes=[pltpu.VMEM((tm, d), BF16)],
        compiler_params=_cparams(2),
        name="proj",
    )(x2, norm_w, w_main, w_misc)


def _compress_kernel(x_ref, pea_ref, peb_ref, w1a_ref, w1b_ref, w2_ref, o_ref):
    x = x_ref[0, 0].astype(F32)
    xa = (x + pea_ref[0]).astype(BF16)
    xb = (x + peb_ref[0]).astype(BF16)
    a = _dot(xa, w1a_ref[0])
    b = _dot(xb, w1b_ref[0])
    rows = a.shape[0]
    pre = a + pltpu.roll(b, rows - 1, 0)
    hid = pre * jax.nn.sigmoid(pre)
    out = _dot(hid.astype(BF16), w2_ref[0])
    for g in range(N_NSA_KV):
        o_ref[0, 0, g] = out[:, g * HEAD_DIM:(g + 1) * HEAD_DIM]


def _compress(xkv, pea, peb, w1a, w1b, w2):
    _, b, rows, width = xkv.shape
    hid = N_NSA_KV * CMP_HIDDEN
    return pl.pallas_call(
        _compress_kernel,
        grid=(2, b),
        in_specs=[pl.BlockSpec((1, 1, rows, width), lambda s, i: (s, i, 0, 0)),
                  pl.BlockSpec((1, 1, width), lambda s, i: (s, 0, 0)),
                  pl.BlockSpec((1, 1, width), lambda s, i: (s, 0, 0)),
                  pl.BlockSpec((1, width, hid), lambda s, i: (s, 0, 0)),
                  pl.BlockSpec((1, width, hid), lambda s, i: (s, 0, 0)),
                  pl.BlockSpec((1, hid, N_NSA_KV * HEAD_DIM), lambda s, i: (s, 0, 0))],
        out_specs=pl.BlockSpec((1, 1, N_NSA_KV, rows, HEAD_DIM), lambda s, i: (s, i, 0, 0, 0)),
        out_shape=jax.ShapeDtypeStruct((2, b, N_NSA_KV, rows, HEAD_DIM), F32),
        compiler_params=_cparams(2),
        name="compress",
    )(xkv, pea, peb, w1a, w1b, w2)


def _foxprep_kernel(misc_ref, fb_ref, tri_ref, eye_ref, c_ref, ct_ref):
    z = misc_ref[0][:, MISC_FF:MISC_FF + N_FOX_HEADS] + fb_ref[...]
    logf = jnp.minimum(z, 0.0) - jnp.log(1.0 + jnp.exp(-jnp.abs(z)))
    tri = tri_ref[...]
    c = None
    for part in _split3(logf):
        term = _dot(tri, part)
        c = term if c is None else c + term
    ct = None
    for part in _split3(c):
        term = _nt(eye_ref[...], part)
        ct = term if ct is None else ct + term
    for p in range(N_FOX_HEADS // 2):
        c_ref[0, p] = c[:, 2 * p:2 * p + 2]
        ct_ref[0, p] = ct[2 * p:2 * p + 2, :]


def _foxprep(misc3, fb, tri, eye):
    b, t, _ = misc3.shape
    hp = N_FOX_HEADS // 2
    return pl.pallas_call(
        _foxprep_kernel,
        grid=(b,),
        in_specs=[pl.BlockSpec((1, t, LANES), lambda i: (i, 0, 0)),
                  pl.BlockSpec((1, N_FOX_HEADS), lambda i: (0, 0)),
                  pl.BlockSpec((t, t), lambda i: (0, 0)),
                  pl.BlockSpec((N_FOX_HEADS, N_FOX_HEADS), lambda i: (0, 0))],
        out_specs=[pl.BlockSpec((1, hp, t, 2), lambda i: (i, 0, 0, 0)),
                   pl.BlockSpec((1, hp, 2, t), lambda i: (i, 0, 0, 0))],
        out_shape=[jax.ShapeDtypeStruct((b, hp, t, 2), F32),
                   jax.ShapeDtypeStruct((b, hp, 2, t), F32)],
        compiler_params=_cparams(1),
        name="foxprep",
    )(misc3, fb, tri, eye)


def _cmpsel_kernel(q_ref, kvc_ref, bias_ref, gate_ref, impm_ref, eye_ref, o_ref, sel_ref):
    tq = q_ref.shape[1]
    ncp = kvc_ref.shape[3]
    ns = impm_ref.shape[0]
    t0 = pl.program_id(2) * tq
    kc = kvc_ref[0, 0, 0].astype(BF16)
    vc = kvc_ref[1, 0, 0].astype(BF16)
    t_col = t0 + lax.broadcasted_iota(jnp.int32, (tq, ncp), 0)
    c_row = lax.broadcasted_iota(jnp.int32, (tq, ncp), 1)
    valid = t_col >= c_row * CMP_STRIDE + (CMP_BLOCK - 1)
    q = q_ref[0]
    gate = jax.nn.sigmoid(gate_ref[0, 0])
    p_grp = jnp.zeros((tq, ncp), F32)
    for r in range(NSA_GQA):
        s = _nt(q[:, r * HEAD_DIM:(r + 1) * HEAD_DIM], kc) * SCALE + bias_ref[0, r]
        s = jnp.where(valid, s, NEG_INF)
        m = jnp.max(s, axis=-1, keepdims=True)
        e = jnp.where(valid, jnp.exp(s - m), 0.0)
        p = e / jnp.maximum(jnp.sum(e, axis=-1, keepdims=True), 1e-30)
        o = _dot(p.astype(BF16), vc)
        o_ref[0, :, r * HEAD_DIM:(r + 1) * HEAD_DIM] = o * gate[:, 3 * r:3 * r + 1]
        p_grp = p_grp + p

    imp = None
    for part in _split3(p_grp):
        term = _nt(impm_ref[...], part)
        imp = term if imp is None else imp + term
    blk = lax.broadcasted_iota(jnp.int32, (ns, tq), 0)
    t_row = t0 + lax.broadcasted_iota(jnp.int32, (ns, tq), 1)
    cur = t_row // SLC_BLOCK
    forced = (blk == 0) | (blk == cur) | (blk == cur - 1)
    score = jnp.where(blk * SLC_BLOCK <= t_row, imp + jnp.where(forced, FORCE_BONUS, 0.0), NEG_INF)
    rank = jnp.zeros((ns, tq), F32)
    for m_blk in range(ns):
        other = score[m_blk:m_blk + 1, :]
        ahead = (other > score) | ((other == score) & (blk > m_blk))
        rank = rank + jnp.where(ahead, 1.0, 0.0)
    sel_t = jnp.where(rank < float(min(SLC_TOP_N, ns)), 1.0, 0.0).astype(BF16)
    sel_ref[0, 0] = _nt(eye_ref[...], sel_t)


def _cmpsel(main3, kvc, bias_c, gates, impm, eye):
    b, t, _ = main3.shape
    ncp = kvc.shape[3]
    ns = impm.shape[0]
    tq = AT_TQ
    qw = NSA_GQA * HEAD_DIM
    return pl.pallas_call(
        _cmpsel_kernel,
        grid=(b, N_NSA_KV, t // tq),
        in_specs=[pl.BlockSpec((1, tq, qw), lambda i, g, j: (i, j, g)),
                  pl.BlockSpec((2, 1, 1, ncp, HEAD_DIM), lambda i, g, j: (0, i, g, 0, 0)),
                  pl.BlockSpec((1, NSA_GQA, tq, ncp), lambda i, g, j: (g, 0, j, 0)),
                  pl.BlockSpec((1, 1, tq, 3 * NSA_GQA), lambda i, g, j: (i, g, j, 0)),
                  pl.BlockSpec((ns, ncp), lambda i, g, j: (0, 0)),
                  pl.BlockSpec((tq, tq), lambda i, g, j: (0, 0))],
        out_specs=[pl.BlockSpec((1, tq, qw), lambda i, g, j: (i, j, g)),
                   pl.BlockSpec((1, 1, tq, ns), lambda i, g, j: (i, g, j, 0))],
        out_shape=[jax.ShapeDtypeStruct((b, t, N_NSA_HEADS * HEAD_DIM), F32),
                   jax.ShapeDtypeStruct((b, N_NSA_KV, t, ns), F32)],
        compiler_params=_cparams(3),
        name="cmpsel",
    )(main3, kvc, bias_c, gates, impm, eye)


def _softmax_first(s, v):
    m = jnp.max(s, axis=-1, keepdims=True)
    p = jnp.exp(s - m)
    return m, jnp.sum(p, axis=-1, keepdims=True), _dot(p.astype(BF16), v)


def _softmax_next(s, v, m, l, acc):
    m_new = jnp.maximum(m, jnp.max(s, axis=-1, keepdims=True))
    alpha = jnp.exp(m - m_new)
    p = jnp.exp(s - m_new)
    return m_new, alpha * l + jnp.sum(p, axis=-1, keepdims=True), alpha * acc + _dot(p.astype(BF16), v)


def _pad_heads(q, n_heads):
    zeros = jnp.zeros((q.shape[0], LANES - HEAD_DIM), q.dtype)
    q = q * jnp.asarray(SCALE, q.dtype)
    return [jnp.concatenate([q[:, r * HEAD_DIM:(r + 1) * HEAD_DIM], zeros], axis=1) for r in range(n_heads)]


def _slc_kernel(q_ref, kv_ref, sel_ref, nbias_ref, gate_ref, o_ref, kvp_sc):
    tq = q_ref.shape[1]
    ns = sel_ref.shape[3]
    chunk = SLC_CHUNK
    qt = pl.program_id(2)
    t0 = qt * tq

    @pl.when(qt == 0)
    def _():
        kvp_sc[0:tq, :] = jnp.zeros((tq, LANES), BF16)
        kvp_sc[tq:, :] = kv_ref[0]

    qs = _pad_heads(q_ref[0], NSA_GQA)
    sel = sel_ref[0, 0].astype(BF16)

    def allowed(first_key, width, limit):
        blk = lax.broadcasted_iota(jnp.int32, (ns, width), 0)
        key = first_key + lax.broadcasted_iota(jnp.int32, (ns, width), 1)
        hit = (blk == lax.shift_right_arithmetic(key, SLC_SHIFT)) & (key < limit)
        return _dot(sel, jnp.where(hit, 1.0, 0.0).astype(BF16)) > 0.5

    kvt = kvp_sc[pl.ds(pl.multiple_of(t0, tq), 2 * tq), :]
    ok = allowed(t0 - tq, 2 * tq, t0 + tq)
    state = tuple(_softmax_first(jnp.where(ok, _nt(qs[r], kvt) + nbias_ref[0, r], NEG_INF), kvt)
                  for r in range(NSA_GQA))

    def far(j, carry):
        first = j * chunk
        kvt = kvp_sc[pl.ds(pl.multiple_of(first + tq, tq), chunk), :]
        ok = allowed(first, chunk, t0 - tq)
        return tuple(_softmax_next(jnp.where(ok, _nt(qs[r], kvt), NEG_INF), kvt, *carry[r])
                     for r in range(NSA_GQA))

    state = lax.fori_loop(0, (t0 - tq + chunk - 1) // chunk, far, state)
    gate = jax.nn.sigmoid(gate_ref[0, 0])
    for r in range(NSA_GQA):
        _, l, acc = state[r]
        o_ref[0, :, r * HEAD_DIM:(r + 1) * HEAD_DIM] = acc[:, HEAD_DIM:] / l * gate[:, 3 * r + 1:3 * r + 2]


def _slc(main3, sel, nbias, gates):
    b, t, _ = main3.shape
    tq = AT_TQ
    qw = NSA_GQA * HEAD_DIM
    ns = sel.shape[3]
    assert t % SLC_CHUNK == 0
    return pl.pallas_call(
        _slc_kernel,
        grid=(b, N_NSA_KV, t // tq),
        in_specs=[pl.BlockSpec((1, tq, qw), lambda i, g, j: (i, j, g)),
                  pl.BlockSpec((1, t, LANES), lambda i, g, j: (i, 0, COL_SLC // LANES + g)),
                  pl.BlockSpec((1, 1, tq, ns), lambda i, g, j: (i, g, j, 0)),
                  pl.BlockSpec((1, NSA_GQA, tq, 2 * tq), lambda i, g, j: (g, 0, 0, 0)),
                  pl.BlockSpec((1, 1, tq, 3 * NSA_GQA), lambda i, g, j: (i, g, j, 0))],
        out_specs=pl.BlockSpec((1, tq, qw), lambda i, g, j: (i, j, g)),
        out_shape=jax.ShapeDtypeStruct((b, t, N_NSA_HEADS * HEAD_DIM), F32),
        scratch_shapes=[pltpu.VMEM((t + tq, LANES), BF16)],
        compiler_params=_cparams(3),
        name="slc",
    )(main3, main3, sel, nbias, gates)


def _win_kernel(q_ref, kv_ref, bias_ref, gate_ref, o_ref, kvp_sc):
    tq = q_ref.shape[1]
    span = bias_ref.shape[3]
    pad = span - tq
    qt = pl.program_id(2)
    t0 = qt * tq

    @pl.when(qt == 0)
    def _():
        kvp_sc[0:pad, :] = jnp.zeros((pad, LANES), BF16)
        kvp_sc[pad:, :] = kv_ref[0]

    qs = _pad_heads(q_ref[0], NSA_GQA)
    kvt = kvp_sc[pl.ds(pl.multiple_of(t0, tq), span), :]
    real = lax.broadcasted_iota(jnp.int32, (tq, span), 1) >= pad - t0
    gate = jax.nn.sigmoid(gate_ref[0, 0])
    for r in range(NSA_GQA):
        s = jnp.where(real, _nt(qs[r], kvt) + bias_ref[0, r], NEG_INF)
        _, l, acc = _softmax_first(s, kvt)
        o_ref[0, :, r * HEAD_DIM:(r + 1) * HEAD_DIM] = acc[:, HEAD_DIM:] / l * gate[:, 3 * r + 2:3 * r + 3]


def _win(main3, wbias, gates):
    b, t, _ = main3.shape
    tq = AT_TQ
    qw = NSA_GQA * HEAD_DIM
    span = wbias.shape[3]
    return pl.pallas_call(
        _win_kernel,
        grid=(b, N_NSA_KV, t // tq),
        in_specs=[pl.BlockSpec((1, tq, qw), lambda i, g, j: (i, j, g)),
                  pl.BlockSpec((1, t, LANES), lambda i, g, j: (i, 0, COL_WIN // LANES + g)),
                  pl.BlockSpec((1, NSA_GQA, tq, span), lambda i, g, j: (g, 0, 0, 0)),
                  pl.BlockSpec((1, 1, tq, 3 * NSA_GQA), lambda i, g, j: (i, g, j, 0))],
        out_specs=pl.BlockSpec((1, tq, qw), lambda i, g, j: (i, j, g)),
        out_shape=jax.ShapeDtypeStruct((b, t, N_NSA_HEADS * HEAD_DIM), F32),
        scratch_shapes=[pltpu.VMEM((t + span - tq, LANES), BF16)],
        compiler_params=_cparams(3),
        name="win",
    )(main3, main3, wbias, gates)


def _fox_kernel(q_ref, k_ref, v_ref, c_ref, ct_ref, o_ref, *, chunk):
    tq = q_ref.shape[1]
    t0 = pl.program_id(2) * tq
    q = q_ref[0] * jnp.asarray(SCALE, BF16)
    lane = lax.broadcasted_iota(jnp.int32, (tq, LANES), 1)
    low = lane < HEAD_DIM
    qs = (jnp.where(low, q, jnp.zeros_like(q)), jnp.where(low, jnp.zeros_like(q), q))
    c_col = c_ref[0, 0]

    def logits(h, start):
        kt = k_ref[0, pl.ds(start, chunk), :]
        c_row = ct_ref[0, 0, h:h + 1, pl.ds(start, chunk)]
        return _nt(qs[h], kt) + c_col[:, h:h + 1] - c_row

    d0 = pl.multiple_of((t0 // chunk) * chunk, chunk)
    rel = lax.broadcasted_iota(jnp.int32, (tq, chunk), 0) - lax.broadcasted_iota(jnp.int32, (tq, chunk), 1)
    causal = rel + (t0 - d0) >= 0
    vt = v_ref[0, pl.ds(d0, chunk), :]
    state = tuple(_softmax_first(jnp.where(causal, logits(h, d0), NEG_INF), vt) for h in range(2))

    def below(j, carry):
        start = pl.multiple_of(j * chunk, chunk)
        vt = v_ref[0, pl.ds(start, chunk), :]
        return tuple(_softmax_next(logits(h, start), vt, *carry[h]) for h in range(2))

    (_, l0, a0), (_, l1, a1) = lax.fori_loop(0, t0 // chunk, below, state)
    o_ref[0] = jnp.where(low, a0 / l0, a1 / l1)


def _fox(main3, c, ct, tq=256, chunk=512):
    b, t, _ = main3.shape
    hp = N_FOX_HEADS // 2
    assert chunk % tq == 0 and t % chunk == 0
    return pl.pallas_call(
        functools.partial(_fox_kernel, chunk=chunk),
        grid=(b, hp, t // tq),
        in_specs=[pl.BlockSpec((1, tq, LANES), lambda i, p, j: (i, j, COL_FQ // LANES + p)),
                  pl.BlockSpec((1, t, LANES), lambda i, p, j: (i, 0, COL_FK // LANES + p)),
                  pl.BlockSpec((1, t, LANES), lambda i, p, j: (i, 0, COL_FV // LANES + p)),
                  pl.BlockSpec((1, 1, tq, 2), lambda i, p, j: (i, p, j, 0)),
                  pl.BlockSpec((1, 1, 2, t), lambda i, p, j: (i, p, 0, 0))],
        out_specs=pl.BlockSpec((1, tq, LANES), lambda i, p, j: (i, j, p)),
        out_shape=jax.ShapeDtypeStruct((b, t, N_FOX_HEADS * HEAD_DIM), F32),
        compiler_params=_cparams(3),
        name="fox",
    )(main3, main3, main3, c, ct)


def _rms(x, w):
    return x * lax.rsqrt(jnp.mean(x * x, axis=-1, keepdims=True) + NORM_EPS) * w


def _outproj_kernel(oc_ref, os_ref, ow_ref, of_ref, x_ref, nnw_ref, fnw_ref, wo_ref, ffw_ref,
                    wr_ref, br_ref, h_ref, hn_ref, eid_ref, wt_ref):
    o_nsa = oc_ref[...] + os_ref[...] + ow_ref[...]
    mixed = jnp.concatenate([_rms(o_nsa, nnw_ref[...]), _rms(of_ref[...], fnw_ref[...])], axis=-1)
    h = x_ref[...] + _dot(mixed.astype(BF16), wo_ref[...])
    h_ref[...] = h
    hn = _rms(h, ffw_ref[...])
    hn_ref[...] = hn

    h_hi, h_mid, h_lo = _split3(hn)
    w_hi = wr_ref[0]
    w_lo = wr_ref[1]
    logits = (_dot(h_hi, w_hi) + _dot(h_mid, w_hi) + _dot(h_hi, w_lo) + _dot(h_lo, w_hi)
              + _dot(h_mid, w_lo)) + br_ref[...]
    tm = logits.shape[0]
    lane = lax.broadcasted_iota(jnp.int32, (tm, LANES), 1)
    big = jnp.int32(LANES)
    is_grp = (lane >= N_EXPERTS) & (lane < N_EXPERTS + N_GROUPS)
    glog = jnp.where(is_grp, logits, NEG_INF)
    gmax = jnp.max(glog, axis=-1, keepdims=True)
    gsel = jnp.min(jnp.where(glog == gmax, lane, big), axis=-1, keepdims=True) - N_EXPERTS
    p_gsel = 1.0 / jnp.sum(jnp.where(is_grp, jnp.exp(glog - gmax), 0.0), axis=-1, keepdims=True)
    in_grp = (lane < N_EXPERTS) & (lane // EXPERTS_PER_GROUP == gsel)
    e1 = jnp.where(in_grp, logits, NEG_INF)
    v1 = jnp.max(e1, axis=-1, keepdims=True)
    i1 = jnp.min(jnp.where(e1 == v1, lane, big), axis=-1, keepdims=True)
    e2 = jnp.where(lane == i1, NEG_INF, e1)
    v2 = jnp.max(e2, axis=-1, keepdims=True)
    i2 = jnp.min(jnp.where(e2 == v2, lane, big), axis=-1, keepdims=True)
    ex = jnp.exp(v2 - v1)
    w1 = p_gsel / (1.0 + ex)
    w2 = p_gsel * ex / (1.0 + ex)
    eid_ref[...] = jnp.where(lane == 0, i1, jnp.where(lane == 1, i2, 0))
    wt_ref[...] = jnp.where(lane == 0, w1, jnp.where(lane == 1, w2, 0.0))


def _outproj(oc, osl, ow, of, x2, nnw, fnw, wo, ffw, wr, br, tm=256):
    n, d = x2.shape
    half = oc.shape[1]
    row = lambda i: (i, 0)
    fixed = lambda i: (0, 0)
    return pl.pallas_call(
        _outproj_kernel,
        grid=(n // tm,),
        in_specs=[pl.BlockSpec((tm, half), row), pl.BlockSpec((tm, half), row),
                  pl.BlockSpec((tm, half), row), pl.BlockSpec((tm, half), row),
                  pl.BlockSpec((tm, d), row),
                  pl.BlockSpec((1, half), fixed), pl.BlockSpec((1, half), fixed),
                  pl.BlockSpec((d, d), fixed), pl.BlockSpec((1, d), fixed),
                  pl.BlockSpec((2, d, LANES), lambda i: (0, 0, 0)), pl.BlockSpec((1, LANES), fixed)],
        out_specs=[pl.BlockSpec((tm, d), row), pl.BlockSpec((tm, d), row),
                   pl.BlockSpec((tm, LANES), row), pl.BlockSpec((tm, LANES), row)],
        out_shape=[jax.ShapeDtypeStruct((n, d), F32), jax.ShapeDtypeStruct((n, d), F32),
                   jax.ShapeDtypeStruct((n, LANES), jnp.int32), jax.ShapeDtypeStruct((n, LANES), F32)],
        compiler_params=_cparams(1),
        name="outproj",
    )(oc, osl, ow, of, x2, nnw, fnw, wo, ffw, wr, br)


def _moe_kernel(te_ref, src_ref, nt_ref, hn_hbm, ws_ref, wg_ref, wu_ref, wd_ref, y_ref,
                xbuf, sem, wg_sc, wu_sc, wd_sc):
    tm = xbuf.shape[1]
    i = pl.program_id(0)
    n_used = nt_ref[0]
    slot = i % 2

    def gather(tile, slot_):
        def row(r, carry):
            tok = src_ref[tile * tm + r]
            pltpu.make_async_copy(hn_hbm.at[pl.ds(tok, 1)], xbuf.at[slot_, pl.ds(r, 1)],
                                  sem.at[slot_]).start()
            return carry
        lax.fori_loop(0, tm, row, 0)

    @pl.when(i == 0)
    def _():
        gather(0, 0)

    @pl.when(i + 1 < n_used)
    def _():
        gather(i + 1, 1 - slot)

    prev = te_ref[jnp.maximum(i - 1, 0)]

    @pl.when((i == 0) | (te_ref[i] != prev))
    def _():
        wg_sc[...] = wg_ref[0].astype(BF16)
        wu_sc[...] = wu_ref[0].astype(BF16)
        wd_sc[...] = wd_ref[0].astype(BF16)

    @pl.when(i < n_used)
    def _():
        pltpu.make_async_copy(hn_hbm.at[pl.ds(0, tm)], xbuf.at[slot], sem.at[slot]).wait()
        x = xbuf[slot].astype(BF16)
        gate = _dot(x, wg_sc[...])
        up = _dot(x, wu_sc[...])
        hid = gate * jax.nn.sigmoid(gate) * up
        y_ref[...] = ws_ref[...] * _dot(hid.astype(BF16), wd_sc[...])

    @pl.when(i >= n_used)
    def _():
        y_ref[...] = jnp.zeros(y_ref.shape, F32)


def _moe(tile_e, src_tok, n_used, hn, w_sorted, wg, wu, wd, tm):
    n_tiles = tile_e.shape[0]
    d = hn.shape[1]
    ff = wg.shape[2]
    grid_spec = pltpu.PrefetchScalarGridSpec(
        num_scalar_prefetch=3,
        grid=(n_tiles,),
        in_specs=[pl.BlockSpec(memory_space=pl.ANY),
                  pl.BlockSpec((tm, 1), lambda i, te, src, nt: (i, 0)),
                  pl.BlockSpec((1, d, ff), lambda i, te, src, nt: (te[i], 0, 0)),
                  pl.BlockSpec((1, d, ff), lambda i, te, src, nt: (te[i], 0, 0)),
                  pl.BlockSpec((1, ff, d), lambda i, te, src, nt: (te[i], 0, 0))],
        out_specs=pl.BlockSpec((tm, d), lambda i, te, src, nt: (i, 0)),
        scratch_shapes=[pltpu.VMEM((2, tm, d), F32),
                        pltpu.SemaphoreType.DMA((2,)),
                        pltpu.VMEM((d, ff), BF16), pltpu.VMEM((d, ff), BF16), pltpu.VMEM((ff, d), BF16)],
    )
    return pl.pallas_call(
        _moe_kernel,
        grid_spec=grid_spec,
        out_shape=jax.ShapeDtypeStruct((n_tiles * tm, d), F32),
        compiler_params=_cparams(1),
        name="moe",
    )(tile_e, src_tok, n_used, hn, w_sorted, wg, wu, wd)


def _combine_kernel(pos_ref, y_hbm, h_ref, fw_ref, o_ref, ybuf, sem):
    tm = h_ref.shape[0]
    i = pl.program_id(0)
    n = pl.num_programs(0)
    slot = i % 2

    def gather(tile, slot_):
        def row(r, carry):
            for k in range(2):
                p = pos_ref[2 * (tile * tm + r) + k]
                pltpu.make_async_copy(y_hbm.at[pl.ds(p, 1)], ybuf.at[slot_, k, pl.ds(r, 1)],
                                      sem.at[slot_]).start()
            return carry
        lax.fori_loop(0, tm, row, 0)

    @pl.when(i == 0)
    def _():
        gather(0, 0)

    @pl.when(i + 1 < n)
    def _():
        gather(i + 1, 1 - slot)

    for k in range(2):
        pltpu.make_async_copy(y_hbm.at[pl.ds(0, tm)], ybuf.at[slot, k], sem.at[slot]).wait()
    out = h_ref[...] + (ybuf[slot, 0] + ybuf[slot, 1])
    o_ref[...] = _rms(out, fw_ref[...])


def _combine(pos, y, h, fw, tm=256):
    n, d = h.shape
    grid_spec = pltpu.PrefetchScalarGridSpec(
        num_scalar_prefetch=1,
        grid=(n // tm,),
        in_specs=[pl.BlockSpec(memory_space=pl.ANY),
                  pl.BlockSpec((tm, d), lambda i, pos_: (i, 0)),
                  pl.BlockSpec((1, d), lambda i, pos_: (0, 0))],
        out_specs=pl.BlockSpec((tm, d), lambda i, pos_: (i, 0)),
        scratch_shapes=[pltpu.VMEM((2, 2, tm, d), F32), pltpu.SemaphoreType.DMA((2,))],
    )
    return pl.pallas_call(
        _combine_kernel,
        grid_spec=grid_spec,
        out_shape=jax.ShapeDtypeStruct((n, d), F32),
        compiler_params=_cparams(1),
        name="combine",
    )(pos, y, h, fw)


def _bias_tables(rel_table, t, ncp):
    tq = AT_TQ
    far = REL_MAX_DIST
    bd = jnp.take(rel_table.T, jnp.asarray(_bucket_table(far + 1)), axis=1)
    i = np.arange(tq)[:, None]

    def tile(dist, ok, shift):
        vals = jnp.take(bd, jnp.asarray(np.clip(dist, 0, far)), axis=1)
        if shift:
            vals = vals - bd[:, far][:, None, None]
        vals = jnp.where(jnp.asarray(ok)[None], vals, NEG_INF)
        return vals.reshape(N_NSA_KV, NSA_GQA, tq, dist.shape[1])

    step = tq // CMP_STRIDE
    u = np.arange(2 * ncp)[None, :]
    dist_m = i - CMP_STRIDE * (u - ncp) - (CMP_BLOCK - 1)
    master = tile(dist_m, np.ones_like(dist_m, bool), False)
    bias_c = jnp.stack([master[..., ncp - step * qt:2 * ncp - step * qt] for qt in range(t // tq)], axis=2)
    bias_c = bias_c.reshape(N_NSA_KV, NSA_GQA, t, ncp)

    j = np.arange(2 * tq)[None, :]
    dist_n = tq + i - j
    nbias = tile(dist_n, dist_n >= 0, True)
    j = np.arange(WINDOW + tq)[None, :]
    dist_w = WINDOW + i - j
    wbias = tile(dist_w, (dist_w >= 0) & (dist_w < WINDOW), False)
    return bias_c, nbias, wbias


def _compress_weights(pe, w1, w2):
    half = CMP_STRIDE
    eye = jnp.eye(N_NSA_KV, dtype=F32)

    def expand_w1(w):
        w = w.reshape(half, HEAD_DIM, CMP_HIDDEN)
        return jnp.einsum("idn,gh->igdhn", w, eye).reshape(half * N_NSA_KV * HEAD_DIM, N_NSA_KV * CMP_HIDDEN)

    def expand_pe(p):
        return jnp.broadcast_to(p[:, None, :], (half, N_NSA_KV, HEAD_DIM)).reshape(1, -1)

    w1a = expand_w1(w1[:half * HEAD_DIM]).astype(BF16)
    w1b = expand_w1(w1[half * HEAD_DIM:]).astype(BF16)
    w2x = jnp.einsum("nd,gh->gnhd", w2, eye).reshape(N_NSA_KV * CMP_HIDDEN, N_NSA_KV * HEAD_DIM).astype(BF16)
    return expand_pe(pe[:half]), expand_pe(pe[half:]), w1a, w1b, w2x


def _routing_tables(eid, wts, tm):
    n = eid.shape[0]
    e_flat = eid.reshape(-1)
    onehot = (e_flat[:, None] == jnp.arange(N_EXPERTS, dtype=jnp.int32)[None, :]).astype(jnp.int32)
    csum = jnp.cumsum(onehot, axis=0)
    rank = jnp.take_along_axis(csum, e_flat[:, None], axis=1)[:, 0] - 1
    counts = csum[-1]
    padded = ((counts + tm - 1) // tm) * tm
    ends = jnp.cumsum(padded)
    starts = ends - padded
    pos = (starts[e_flat] + rank).astype(jnp.int32)
    n_tiles = (2 * n) // tm + N_EXPERTS
    rows = n_tiles * tm
    src_pair = jnp.zeros((rows,), jnp.int32).at[pos].set(jnp.arange(2 * n, dtype=jnp.int32))
    src_tok = src_pair // 2
    w_sorted = wts.reshape(-1)[src_pair]
    tile_start = jnp.arange(n_tiles, dtype=jnp.int32) * tm
    tile_e = jnp.minimum(jnp.searchsorted(ends, tile_start, side="right"), N_EXPERTS - 1).astype(jnp.int32)
    n_used = (ends[-1] // tm).astype(jnp.int32).reshape(1)
    last_e = tile_e[jnp.maximum(n_used[0] - 1, 0)]
    tile_e = jnp.where(tile_start < ends[-1], tile_e, last_e)
    return tile_e, src_tok, n_used, w_sorted.reshape(rows, 1), pos


def kernel(x, attn_norm_w, w_in, cmp_pe_k, cmp_pe_v, cmp_k_w1, cmp_k_w2, cmp_v_w1, cmp_v_w2,
           rel_bias_table, fox_forget_b, nsa_out_norm_w, fox_out_norm_w, w_out, ffn_norm_w,
           router_group_w, router_group_b, router_expert_w, router_expert_b,
           expert_w_gate, expert_w_up, expert_w_down, final_norm_w):
    b, t, d = x.shape
    n = b * t
    depth = w_in.shape[0]
    assert t % 512 == 0 and t >= WINDOW + AT_TQ and d == 2048
    rows = t // CMP_STRIDE
    bias_c, nbias, wbias = _bias_tables(rel_bias_table, t, rows)
    ns = t // SLC_BLOCK
    ratio = SLC_BLOCK // CMP_STRIDE
    span = CMP_BLOCK // CMP_STRIDE
    nc = (t - CMP_BLOCK) // CMP_STRIDE + 1
    impm = np.zeros((ns, rows), np.float32)
    for blk in range(ns):
        for a in range(ratio):
            for s in range(span):
                c = blk * ratio + a - s
                if 0 <= c < nc:
                    impm[blk, c] += 1.0
    impm = jnp.asarray(impm, BF16)
    eye_q = jnp.eye(AT_TQ, dtype=BF16)
    eye_h = jnp.eye(N_FOX_HEADS, dtype=BF16)
    tri = jnp.asarray(np.tril(np.ones((t, t), np.float32)), BF16)
    moe_tm = 512

    h = x.reshape(n, d)
    for layer in range(depth):
        w_main, w_misc = _project_weights(w_in[layer])
        main, misc = _proj(h, attn_norm_w[layer][None, :], w_main, w_misc)
        main3 = main.reshape(b, t, MAIN_COLS)
        misc3 = misc.reshape(b, t, LANES)

        xkv = jnp.stack([main3[:, :, COL_KCMP:COL_KCMP + 256], main3[:, :, COL_VCMP:COL_VCMP + 256]])
        xkv = xkv.reshape(2, b, rows, CMP_STRIDE * 256)
        pk = _compress_weights(cmp_pe_k[layer], cmp_k_w1[layer], cmp_k_w2[layer])
        pv = _compress_weights(cmp_pe_v[layer], cmp_v_w1[layer], cmp_v_w2[layer])
        kvc = _compress(xkv, *[jnp.stack([a, c]) for a, c in zip(pk, pv)])

        gates = misc3[:, :, MISC_GATE:MISC_GATE + 48].reshape(b, t, N_NSA_KV, 12).transpose(0, 2, 1, 3)
        o_cmp, sel = _cmpsel(main3, kvc, bias_c, gates, impm, eye_q)
        o_slc = _slc(main3, sel, nbias, gates)
        o_win = _win(main3, wbias, gates)

        c, ct = _foxprep(misc3, fox_forget_b[layer][None, :], tri, eye_h)
        o_fox = _fox(main3, c, ct)

        wr = jnp.concatenate([router_expert_w[layer], router_group_w[layer]], axis=1)
        wr = jnp.pad(wr, ((0, 0), (0, LANES - wr.shape[1])))
        wr_hi = wr.astype(BF16)
        wr_lo = (wr - wr_hi.astype(F32)).astype(BF16)
        br = jnp.concatenate([router_expert_b[layer], router_group_b[layer]])
        br = jnp.pad(br, (0, LANES - br.shape[0]))[None, :]
        half = N_NSA_HEADS * HEAD_DIM
        h, hn, eid, wts = _outproj(
            o_cmp.reshape(n, half), o_slc.reshape(n, half), o_win.reshape(n, half), o_fox.reshape(n, half),
            h, nsa_out_norm_w[layer][None, :], fox_out_norm_w[layer][None, :],
            w_out[layer].astype(BF16), ffn_norm_w[layer][None, :], jnp.stack([wr_hi, wr_lo]), br)

        tile_e, src_tok, n_used, w_sorted, pos = _routing_tables(eid[:, :2], wts[:, :2], moe_tm)
        y = _moe(tile_e, src_tok, n_used, hn, w_sorted,
                 expert_w_gate[layer], expert_w_up[layer], expert_w_down[layer], moe_tm)
        last = layer == depth - 1
        fw = final_norm_w if last else jnp.ones((d,), F32)
        assert last, "the fused final norm assumes a single layer"
        h = _combine(pos, y, h, fw[None, :])
    return h.reshape(b, t, d)
```

```python
import functools
import math

import numpy as np
import jax
import jax.numpy as jnp
from jax import lax
from jax.experimental import pallas as pl
from jax.experimental.pallas import tpu as pltpu

F32 = jnp.float32
BF16 = jnp.bfloat16

HEAD_DIM = 64
N_NSA_HEADS = 16
N_FOX_HEADS = 16
NSA_GQA = 4
N_NSA_KV = 4
CMP_BLOCK = 32
CMP_STRIDE = 16
CMP_HIDDEN = 128
SLC_BLOCK = 64
SLC_TOP_N = 16
WINDOW = 512
REL_BUCKETS = 32
REL_MAX_DIST = 128
N_GROUPS = 4
EXPERTS_PER_GROUP = 8
N_EXPERTS = 32
EXPERT_FF = 512
NORM_EPS = 1e-6
NEG_INF = -1e30
FORCE_BONUS = 1e4
SCALE = HEAD_DIM ** -0.5

LANES = 128
VMEM_LIMIT = 56 * 1024 * 1024

COL_NQ = 0
COL_KCMP = 1024
COL_VCMP = 1280
COL_SLC = 1536
COL_WIN = 2048
COL_FQ = 2560
COL_FK = 3584
COL_FV = 4608
MAIN_COLS = 5632
MISC_GATE = 0
MISC_FF = 48

AT_TQ = 128
SLC_CHUNK = 512
SLC_SHIFT = 6
assert 1 << SLC_SHIFT == SLC_BLOCK
MASK_BLOCK_LANES = 32
PAD_LANE = HEAD_DIM + MASK_BLOCK_LANES
KEY_PAD = 512


def _nt(a, b):
    return lax.dot_general(a, b, (((1,), (1,)), ((), ())), preferred_element_type=F32)


def _dot(a, b):
    return jnp.dot(a, b, preferred_element_type=F32)


def _split3(x):
    hi = x.astype(BF16)
    r = x - hi.astype(F32)
    mid = r.astype(BF16)
    r = r - mid.astype(F32)
    return hi, mid, r.astype(BF16)


def _cparams(grid_rank):
    return pltpu.CompilerParams(dimension_semantics=("arbitrary",) * grid_rank, vmem_limit_bytes=VMEM_LIMIT)


def _bucket_table(n):
    d = np.arange(n, dtype=np.int64)
    max_exact = REL_BUCKETS // 2
    rel = np.log(np.maximum(d, 1).astype(np.float64) / max_exact) / math.log(REL_MAX_DIST / max_exact)
    scaled = rel * (REL_BUCKETS - max_exact)
    frac = scaled - np.floor(scaled)
    inner = (d > max_exact) & (d < REL_MAX_DIST)
    assert np.all((frac[inner] > 1e-3) & (frac[inner] < 1 - 1e-3))
    large = np.minimum(max_exact + np.floor(scaled + 1e-6).astype(np.int64), REL_BUCKETS - 1)
    return np.where(d < max_exact, d, large).astype(np.int32)


def _project_weights(w):
    d = w.shape[0]
    sizes = [1024] + [256] * 6 + [48, 1024, 1024, 1024, 16]
    offs = np.concatenate([[0], np.cumsum(sizes)])
    nq, kcmp, vcmp, kslc, vslc, kwin, vwin, ngate, fq, fk, fv, ff = [
        w[:, int(offs[i]):int(offs[i + 1])] for i in range(12)]

    def interleave(k, v):
        k = k.reshape(d, N_NSA_KV, HEAD_DIM)
        v = v.reshape(d, N_NSA_KV, HEAD_DIM)
        return jnp.stack([k, v], axis=2).reshape(d, N_NSA_KV * 2 * HEAD_DIM)

    main = jnp.concatenate([nq, kcmp, vcmp, interleave(kslc, vslc), interleave(kwin, vwin), fq, fk, fv], axis=1)
    assert main.shape[1] == MAIN_COLS
    misc = jnp.concatenate([ngate, ff, jnp.zeros((d, LANES - 64), w.dtype)], axis=1)
    return main.astype(BF16), misc.astype(BF16)


def _proj_kernel(x_ref, nw_ref, w_ref, wm_ref, o_ref, om_ref, xn_sc):
    @pl.when(pl.program_id(1) == 0)
    def _():
        x = x_ref[...]
        y = x * lax.rsqrt(jnp.mean(x * x, axis=-1, keepdims=True) + NORM_EPS) * nw_ref[...]
        xn = y.astype(BF16)
        xn_sc[...] = xn
        om_ref[...] = _dot(xn, wm_ref[...])

    o_ref[...] = _dot(xn_sc[...], w_ref[...]).astype(BF16)


def _proj(x2, norm_w, w_main, w_misc, tm=512, tn=512):
    n, d = x2.shape
    return pl.pallas_call(
        _proj_kernel,
        grid=(n // tm, MAIN_COLS // tn),
        in_specs=[pl.BlockSpec((tm, d), lambda i, j: (i, 0)),
                  pl.BlockSpec((1, d), lambda i, j: (0, 0)),
                  pl.BlockSpec((d, tn), lambda i, j: (0, j)),
                  pl.BlockSpec((d, LANES), lambda i, j: (0, 0))],
        out_specs=[pl.BlockSpec((tm, tn), lambda i, j: (i, j)),
                   pl.BlockSpec((tm, LANES), lambda i, j: (i, 0))],
        out_shape=[jax.ShapeDtypeStruct((n, MAIN_COLS), BF16),
                   jax.ShapeDtypeStruct((n, LANES), F32)],
        scratch_shapes=[pltpu.VMEM((tm, d), BF16)],
        compiler_params=_cparams(2),
        name="proj",
    )(x2, norm_w, w_main, w_misc)


def _compress_kernel(x_ref, pea_ref, peb_ref, w1a_ref, w1b_ref, w2_ref, o_ref):
    x = x_ref[0, 0].astype(F32)
    xa = (x + pea_ref[0]).astype(BF16)
    xb = (x + peb_ref[0]).astype(BF16)
    a = _dot(xa, w1a_ref[0])
    b = _dot(xb, w1b_ref[0])
    rows = a.shape[0]
    pre = a + pltpu.roll(b, rows - 1, 0)
    hid = pre * jax.nn.sigmoid(pre)
    out = _dot(hid.astype(BF16), w2_ref[0])
    for g in range(N_NSA_KV):
        o_ref[0, 0, g] = out[:, g * HEAD_DIM:(g + 1) * HEAD_DIM]


def _compress(xkv, pea, peb, w1a, w1b, w2):
    _, b, rows, width = xkv.shape
    hid = N_NSA_KV * CMP_HIDDEN
    return pl.pallas_call(
        _compress_kernel,
        grid=(2, b),
        in_specs=[pl.BlockSpec((1, 1, rows, width), lambda s, i: (s, i, 0, 0)),
                  pl.BlockSpec((1, 1, width), lambda s, i: (s, 0, 0)),
                  pl.BlockSpec((1, 1, width), lambda s, i: (s, 0, 0)),
                  pl.BlockSpec((1, width, hid), lambda s, i: (s, 0, 0)),
                  pl.BlockSpec((1, width, hid), lambda s, i: (s, 0, 0)),
                  pl.BlockSpec((1, hid, N_NSA_KV * HEAD_DIM), lambda s, i: (s, 0, 0))],
        out_specs=pl.BlockSpec((1, 1, N_NSA_KV, rows, HEAD_DIM), lambda s, i: (s, i, 0, 0, 0)),
        out_shape=jax.ShapeDtypeStruct((2, b, N_NSA_KV, rows, HEAD_DIM), F32),
        compiler_params=_cparams(2),
        name="compress",
    )(xkv, pea, peb, w1a, w1b, w2)


def _foxprep_kernel(misc_ref, fb_ref, tri_ref, eye_ref, c_ref, ct_ref):
    z = misc_ref[0][:, MISC_FF:MISC_FF + N_FOX_HEADS] + fb_ref[...]
    logf = jnp.minimum(z, 0.0) - jnp.log(1.0 + jnp.exp(-jnp.abs(z)))
    tri = tri_ref[...]
    c = None
    for part in _split3(logf):
        term = _dot(tri, part)
        c = term if c is None else c + term
    ct = None
    for part in _split3(c):
        term = _nt(eye_ref[...], part)
        ct = term if ct is None else ct + term
    for p in range(N_FOX_HEADS // 2):
        c_ref[0, p] = c[:, 2 * p:2 * p + 2]
        ct_ref[0, p] = ct[2 * p:2 * p + 2, :]


def _foxprep(misc3, fb, tri, eye):
    b, t, _ = misc3.shape
    hp = N_FOX_HEADS // 2
    return pl.pallas_call(
        _foxprep_kernel,
        grid=(b,),
        in_specs=[pl.BlockSpec((1, t, LANES), lambda i: (i, 0, 0)),
                  pl.BlockSpec((1, N_FOX_HEADS), lambda i: (0, 0)),
                  pl.BlockSpec((t, t), lambda i: (0, 0)),
                  pl.BlockSpec((N_FOX_HEADS, N_FOX_HEADS), lambda i: (0, 0))],
        out_specs=[pl.BlockSpec((1, hp, t, 2), lambda i: (i, 0, 0, 0)),
                   pl.BlockSpec((1, hp, 2, t), lambda i: (i, 0, 0, 0))],
        out_shape=[jax.ShapeDtypeStruct((b, hp, t, 2), F32),
                   jax.ShapeDtypeStruct((b, hp, 2, t), F32)],
        compiler_params=_cparams(1),
        name="foxprep",
    )(misc3, fb, tri, eye)


def _cmpsel_kernel(q_ref, kvc_ref, bias_ref, gate_ref, impm_ref, eye_ref, o_ref, sel_ref):
    tq = q_ref.shape[1]
    ncp = kvc_ref.shape[3]
    ns = impm_ref.shape[0]
    t0 = pl.program_id(2) * tq
    kc = kvc_ref[0, 0, 0].astype(BF16)
    vc = kvc_ref[1, 0, 0].astype(BF16)
    rows = NSA_GQA * tq
    t_col = t0 + (lax.broadcasted_iota(jnp.int32, (rows, ncp), 0) & (tq - 1))
    c_row = lax.broadcasted_iota(jnp.int32, (rows, ncp), 1)
    valid = t_col >= c_row * CMP_STRIDE + (CMP_BLOCK - 1)
    q = q_ref[0] * jnp.asarray(SCALE, BF16)
    qst = jnp.concatenate([q[:, r * HEAD_DIM:(r + 1) * HEAD_DIM] for r in range(NSA_GQA)], axis=0)
    s = _nt(qst, kc) + bias_ref[0].reshape(rows, ncp)
    s = jnp.where(valid, s, NEG_INF)
    m = jnp.max(s, axis=-1, keepdims=True)
    e = jnp.where(valid, jnp.exp(s - m), 0.0)
    p = e / jnp.maximum(jnp.sum(e, axis=-1, keepdims=True), 1e-30)
    o = _dot(p.astype(BF16), vc)
    gate = jax.nn.sigmoid(gate_ref[0, 0])
    p_grp = jnp.zeros((tq, ncp), F32)
    for r in range(NSA_GQA):
        o_ref[0, :, r * HEAD_DIM:(r + 1) * HEAD_DIM] = o[r * tq:(r + 1) * tq] * gate[:, 3 * r:3 * r + 1]
        p_grp = p_grp + p[r * tq:(r + 1) * tq]

    imp = None
    for part in _split3(p_grp):
        term = _nt(impm_ref[...], part)
        imp = term if imp is None else imp + term
    blk = lax.broadcasted_iota(jnp.int32, (ns, tq), 0)
    t_row = t0 + lax.broadcasted_iota(jnp.int32, (ns, tq), 1)
    cur = t_row // SLC_BLOCK
    forced = (blk == 0) | (blk == cur) | (blk == cur - 1)
    score = jnp.where(blk * SLC_BLOCK <= t_row, imp + jnp.where(forced, FORCE_BONUS, 0.0), NEG_INF)
    rank = jnp.zeros((ns, tq), F32)
    for m_blk in range(ns):
        other = score[m_blk:m_blk + 1, :]
        ahead = (other > score) | ((other == score) & (blk > m_blk))
        rank = rank + jnp.where(ahead, 1.0, 0.0)
    unsel = jnp.where(rank < float(min(SLC_TOP_N, ns)), 0.0, 1.0)
    pad_row = jnp.where(lax.broadcasted_iota(jnp.int32, (HEAD_DIM - MASK_BLOCK_LANES, tq), 0) == 0, 1.0, 0.0)
    parts = [unsel, pad_row]
    if ns < MASK_BLOCK_LANES:
        parts.insert(1, jnp.zeros((MASK_BLOCK_LANES - ns, tq), F32))
    flags = _nt(eye_ref[...], jnp.concatenate(parts, axis=0).astype(BF16))
    sel_ref[0, 0] = (flags * NEG_INF).astype(BF16)


def _cmpsel(main3, kvc, bias_c, gates, impm, eye):
    b, t, _ = main3.shape
    ncp = kvc.shape[3]
    ns = impm.shape[0]
    tq = AT_TQ
    qw = NSA_GQA * HEAD_DIM
    return pl.pallas_call(
        _cmpsel_kernel,
        grid=(b, N_NSA_KV, t // tq),
        in_specs=[pl.BlockSpec((1, tq, qw), lambda i, g, j: (i, j, g)),
                  pl.BlockSpec((2, 1, 1, ncp, HEAD_DIM), lambda i, g, j: (0, i, g, 0, 0)),
                  pl.BlockSpec((1, NSA_GQA, tq, ncp), lambda i, g, j: (g, 0, j, 0)),
                  pl.BlockSpec((1, 1, tq, 3 * NSA_GQA), lambda i, g, j: (i, g, j, 0)),
                  pl.BlockSpec((ns, ncp), lambda i, g, j: (0, 0)),
                  pl.BlockSpec((tq, tq), lambda i, g, j: (0, 0))],
        out_specs=[pl.BlockSpec((1, tq, qw), lambda i, g, j: (i, j, g)),
                   pl.BlockSpec((1, 1, tq, HEAD_DIM), lambda i, g, j: (i, g, j, 0))],
        out_shape=[jax.ShapeDtypeStruct((b, t, N_NSA_HEADS * HEAD_DIM), F32),
                   jax.ShapeDtypeStruct((b, N_NSA_KV, t, HEAD_DIM), BF16)],
        compiler_params=_cparams(3),
        name="cmpsel",
    )(main3, kvc, bias_c, gates, impm, eye)


def _softmax_first(s, v):
    m = jnp.max(s, axis=-1, keepdims=True)
    p = jnp.exp(s - m)
    return m, jnp.sum(p, axis=-1, keepdims=True), _dot(p.astype(BF16), v)


def _softmax_next(s, v, m, l, acc):
    m_new = jnp.maximum(m, jnp.max(s, axis=-1, keepdims=True))
    alpha = jnp.exp(m - m_new)
    p = jnp.exp(s - m_new)
    return m_new, alpha * l + jnp.sum(p, axis=-1, keepdims=True), alpha * acc + _dot(p.astype(BF16), v)


def _stack_heads(q, tail, n_heads):
    q = q * jnp.asarray(SCALE, q.dtype)
    return jnp.concatenate(
        [jnp.concatenate([q[:, r * HEAD_DIM:(r + 1) * HEAD_DIM], tail], axis=1) for r in range(n_heads)], axis=0)


def _fill_key_scratch(kp_sc, kvp_sc, kv, with_blocks):
    t = kv.shape[0]
    lane = lax.broadcasted_iota(jnp.int32, (t, LANES), 1)
    if with_blocks:
        blk = lax.shift_right_logical(lax.broadcasted_iota(jnp.int32, (t, LANES), 0), SLC_SHIFT)
        aug = jnp.where(lane - HEAD_DIM == blk, 1.0, 0.0).astype(BF16)
    else:
        aug = jnp.zeros((t, LANES), BF16)
    kp_sc[KEY_PAD:, :] = jnp.where(lane < HEAD_DIM, kv, aug)
    lane_p = lax.broadcasted_iota(jnp.int32, (KEY_PAD, LANES), 1)
    kp_sc[0:KEY_PAD, :] = jnp.where(lane_p == PAD_LANE, 1.0, 0.0).astype(BF16)
    kvp_sc[KEY_PAD:, :] = kv
    kvp_sc[0:KEY_PAD, :] = jnp.zeros((KEY_PAD, LANES), BF16)


def _store_gated(o_ref, acc, l, gate, branch):
    tq = o_ref.shape[1]
    o = acc[:, HEAD_DIM:] / l
    for r in range(NSA_GQA):
        col = 3 * r + branch
        o_ref[0, :, r * HEAD_DIM:(r + 1) * HEAD_DIM] = o[r * tq:(r + 1) * tq] * gate[:, col:col + 1]


def _slc_kernel(q_ref, kv_ref, sel_ref, nbias_ref, gate_ref, o_ref, kp_sc, kvp_sc):
    tq = q_ref.shape[1]
    chunk = SLC_CHUNK
    qt = pl.program_id(2)
    t0 = qt * tq

    @pl.when(qt == 0)
    def _():
        _fill_key_scratch(kp_sc, kvp_sc, kv_ref[0], True)

    qst = _stack_heads(q_ref[0], sel_ref[0, 0], NSA_GQA)

    near = pl.multiple_of(t0 - tq + KEY_PAD, tq)
    state = _softmax_first(_nt(qst, kp_sc[pl.ds(near, 2 * tq), :]) + nbias_ref[0], kvp_sc[pl.ds(near, 2 * tq), :])

    def far(j, carry):
        first = pl.multiple_of(t0 - tq + KEY_PAD - (j + 1) * chunk, tq)
        return _softmax_next(_nt(qst, kp_sc[pl.ds(first, chunk), :]), kvp_sc[pl.ds(first, chunk), :], *carry)

    _, l, acc = lax.fori_loop(0, (t0 - tq + chunk - 1) // chunk, far, state)
    _store_gated(o_ref, acc, l, jax.nn.sigmoid(gate_ref[0, 0]), 1)


def _slc(main3, sel, nbias, gates):
    b, t, _ = main3.shape
    tq = AT_TQ
    qw = NSA_GQA * HEAD_DIM
    assert KEY_PAD >= SLC_CHUNK and KEY_PAD % tq == 0
    return pl.pallas_call(
        _slc_kernel,
        grid=(b, N_NSA_KV, t // tq),
        in_specs=[pl.BlockSpec((1, tq, qw), lambda i, g, j: (i, j, g)),
                  pl.BlockSpec((1, t, LANES), lambda i, g, j: (i, 0, COL_SLC // LANES + g)),
                  pl.BlockSpec((1, 1, tq, HEAD_DIM), lambda i, g, j: (i, g, j, 0)),
                  pl.BlockSpec((1, NSA_GQA * tq, 2 * tq), lambda i, g, j: (g, 0, 0)),
                  pl.BlockSpec((1, 1, tq, 3 * NSA_GQA), lambda i, g, j: (i, g, j, 0))],
        out_specs=pl.BlockSpec((1, tq, qw), lambda i, g, j: (i, j, g)),
        out_shape=jax.ShapeDtypeStruct((b, t, N_NSA_HEADS * HEAD_DIM), F32),
        scratch_shapes=[pltpu.VMEM((t + KEY_PAD, LANES), BF16), pltpu.VMEM((t + KEY_PAD, LANES), BF16)],
        compiler_params=_cparams(3),
        name="slc",
    )(main3, main3, sel, nbias, gates)


def _win_kernel(q_ref, kv_ref, bias_ref, gate_ref, o_ref, kp_sc, kvp_sc):
    tq = q_ref.shape[1]
    span = bias_ref.shape[2]
    qt = pl.program_id(2)

    @pl.when(qt == 0)
    def _():
        _fill_key_scratch(kp_sc, kvp_sc, kv_ref[0], False)

    tail = jnp.where(lax.broadcasted_iota(jnp.int32, (tq, HEAD_DIM), 1) == PAD_LANE - HEAD_DIM, NEG_INF, 0.0)
    qst = _stack_heads(q_ref[0], tail.astype(BF16), NSA_GQA)
    first = pl.multiple_of(qt * tq, tq)
    s = _nt(qst, kp_sc[pl.ds(first, span), :]) + bias_ref[0]
    _, l, acc = _softmax_first(s, kvp_sc[pl.ds(first, span), :])
    _store_gated(o_ref, acc, l, jax.nn.sigmoid(gate_ref[0, 0]), 2)


def _win(main3, wbias, gates):
    b, t, _ = main3.shape
    tq = AT_TQ
    qw = NSA_GQA * HEAD_DIM
    span = wbias.shape[2]
    assert span - tq == KEY_PAD
    return pl.pallas_call(
        _win_kernel,
        grid=(b, N_NSA_KV, t // tq),
        in_specs=[pl.BlockSpec((1, tq, qw), lambda i, g, j: (i, j, g)),
                  pl.BlockSpec((1, t, LANES), lambda i, g, j: (i, 0, COL_WIN // LANES + g)),
                  pl.BlockSpec((1, NSA_GQA * tq, span), lambda i, g, j: (g, 0, 0)),
                  pl.BlockSpec((1, 1, tq, 3 * NSA_GQA), lambda i, g, j: (i, g, j, 0))],
        out_specs=pl.BlockSpec((1, tq, qw), lambda i, g, j: (i, j, g)),
        out_shape=jax.ShapeDtypeStruct((b, t, N_NSA_HEADS * HEAD_DIM), F32),
        scratch_shapes=[pltpu.VMEM((t + KEY_PAD, LANES), BF16), pltpu.VMEM((t + KEY_PAD, LANES), BF16)],
        compiler_params=_cparams(3),
        name="win",
    )(main3, main3, wbias, gates)


def _fox_kernel(q_ref, k_ref, v_ref, c_ref, ct_ref, o_ref, *, chunk):
    tq = q_ref.shape[1]
    t0 = pl.program_id(2) * tq
    q = q_ref[0] * jnp.asarray(SCALE, BF16)
    lane = lax.broadcasted_iota(jnp.int32, (tq, LANES), 1)
    low = lane < HEAD_DIM
    qst = jnp.concatenate([jnp.where(low, q, jnp.zeros_like(q)), jnp.where(low, jnp.zeros_like(q), q)], axis=0)
    c_col = c_ref[0, 0]

    def logits(start):
        s = _nt(qst, k_ref[0, pl.ds(start, chunk), :])
        c_row = ct_ref[0, 0, :, pl.ds(start, chunk)]
        return jnp.concatenate([s[:tq] + c_col[:, 0:1] - c_row[0:1], s[tq:] + c_col[:, 1:2] - c_row[1:2]], axis=0)

    d0 = pl.multiple_of((t0 // chunk) * chunk, chunk)
    row = lax.broadcasted_iota(jnp.int32, (2 * tq, chunk), 0) & (tq - 1)
    causal = row - lax.broadcasted_iota(jnp.int32, (2 * tq, chunk), 1) + (t0 - d0) >= 0
    state = _softmax_first(jnp.where(causal, logits(d0), NEG_INF), v_ref[0, pl.ds(d0, chunk), :])

    def below(j, carry):
        start = pl.multiple_of(j * chunk, chunk)
        return _softmax_next(logits(start), v_ref[0, pl.ds(start, chunk), :], *carry)

    _, l, acc = lax.fori_loop(0, t0 // chunk, below, state)
    o = acc / l
    o_ref[0] = jnp.where(low, o[:tq], o[tq:])


def _fox(main3, c, ct, tq=256, chunk=512):
    b, t, _ = main3.shape
    hp = N_FOX_HEADS // 2
    assert chunk % tq == 0 and t % chunk == 0
    return pl.pallas_call(
        functools.partial(_fox_kernel, chunk=chunk),
        grid=(b, hp, t // tq),
        in_specs=[pl.BlockSpec((1, tq, LANES), lambda i, p, j: (i, j, COL_FQ // LANES + p)),
                  pl.BlockSpec((1, t, LANES), lambda i, p, j: (i, 0, COL_FK // LANES + p)),
                  pl.BlockSpec((1, t, LANES), lambda i, p, j: (i, 0, COL_FV // LANES + p)),
                  pl.BlockSpec((1, 1, tq, 2), lambda i, p, j: (i, p, j, 0)),
                  pl.BlockSpec((1, 1, 2, t), lambda i, p, j: (i, p, 0, 0))],
        out_specs=pl.BlockSpec((1, tq, LANES), lambda i, p, j: (i, j, p)),
        out_shape=jax.ShapeDtypeStruct((b, t, N_FOX_HEADS * HEAD_DIM), F32),
        compiler_params=_cparams(3),
        name="fox",
    )(main3, main3, main3, c, ct)


def _rms(x, w):
    return x * lax.rsqrt(jnp.mean(x * x, axis=-1, keepdims=True) + NORM_EPS) * w


def _outproj_kernel(oc_ref, os_ref, ow_ref, of_ref, x_ref, nnw_ref, fnw_ref, wo_ref, ffw_ref,
                    wr_ref, br_ref, h_ref, hn_ref, eid_ref, wt_ref):
    o_nsa = oc_ref[...] + os_ref[...] + ow_ref[...]
    mixed = jnp.concatenate([_rms(o_nsa, nnw_ref[...]), _rms(of_ref[...], fnw_ref[...])], axis=-1)
    h = x_ref[...] + _dot(mixed.astype(BF16), wo_ref[...])
    h_ref[...] = h
    hn = _rms(h, ffw_ref[...])
    hn_ref[...] = hn

    h_hi, h_mid, h_lo = _split3(hn)
    w_hi = wr_ref[0]
    w_lo = wr_ref[1]
    logits = (_dot(h_hi, w_hi) + _dot(h_mid, w_hi) + _dot(h_hi, w_lo) + _dot(h_lo, w_hi)
              + _dot(h_mid, w_lo)) + br_ref[...]
    tm = logits.shape[0]
    lane = lax.broadcasted_iota(jnp.int32, (tm, LANES), 1)
    big = jnp.int32(LANES)
    is_grp = (lane >= N_EXPERTS) & (lane < N_EXPERTS + N_GROUPS)
    glog = jnp.where(is_grp, logits, NEG_INF)
    gmax = jnp.max(glog, axis=-1, keepdims=True)
    gsel = jnp.min(jnp.where(glog == gmax, lane, big), axis=-1, keepdims=True) - N_EXPERTS
    p_gsel = 1.0 / jnp.sum(jnp.where(is_grp, jnp.exp(glog - gmax), 0.0), axis=-1, keepdims=True)
    in_grp = (lane < N_EXPERTS) & (lane // EXPERTS_PER_GROUP == gsel)
    e1 = jnp.where(in_grp, logits, NEG_INF)
    v1 = jnp.max(e1, axis=-1, keepdims=True)
    i1 = jnp.min(jnp.where(e1 == v1, lane, big), axis=-1, keepdims=True)
    e2 = jnp.where(lane == i1, NEG_INF, e1)
    v2 = jnp.max(e2, axis=-1, keepdims=True)
    i2 = jnp.min(jnp.where(e2 == v2, lane, big), axis=-1, keepdims=True)
    ex = jnp.exp(v2 - v1)
    w1 = p_gsel / (1.0 + ex)
    w2 = p_gsel * ex / (1.0 + ex)
    eid_ref[...] = jnp.where(lane == 0, i1, jnp.where(lane == 1, i2, 0))
    wt_ref[...] = jnp.where(lane == 0, w1, jnp.where(lane == 1, w2, 0.0))


def _outproj(oc, osl, ow, of, x2, nnw, fnw, wo, ffw, wr, br, tm=256):
    n, d = x2.shape
    half = oc.shape[1]
    row = lambda i: (i, 0)
    fixed = lambda i: (0, 0)
    return pl.pallas_call(
        _outproj_kernel,
        grid=(n // tm,),
        in_specs=[pl.BlockSpec((tm, half), row), pl.BlockSpec((tm, half), row),
                  pl.BlockSpec((tm, half), row), pl.BlockSpec((tm, half), row),
                  pl.BlockSpec((tm, d), row),
                  pl.BlockSpec((1, half), fixed), pl.BlockSpec((1, half), fixed),
                  pl.BlockSpec((d, d), fixed), pl.BlockSpec((1, d), fixed),
                  pl.BlockSpec((2, d, LANES), lambda i: (0, 0, 0)), pl.BlockSpec((1, LANES), fixed)],
        out_specs=[pl.BlockSpec((tm, d), row), pl.BlockSpec((tm, d), row),
                   pl.BlockSpec((tm, LANES), row), pl.BlockSpec((tm, LANES), row)],
        out_shape=[jax.ShapeDtypeStruct((n, d), F32), jax.ShapeDtypeStruct((n, d), F32),
                   jax.ShapeDtypeStruct((n, LANES), jnp.int32), jax.ShapeDtypeStruct((n, LANES), F32)],
        compiler_params=_cparams(1),
        name="outproj",
    )(oc, osl, ow, of, x2, nnw, fnw, wo, ffw, wr, br)


def _moe_kernel(te_ref, src_ref, nt_ref, hn_hbm, ws_ref, wg_ref, wu_ref, wd_ref, y_ref,
                xbuf, sem, wg_sc, wu_sc, wd_sc):
    tm = xbuf.shape[1]
    i = pl.program_id(0)
    n_used = nt_ref[0]
    slot = i % 2

    def gather(tile, slot_):
        def row(r, carry):
            tok = src_ref[tile * tm + r]
            pltpu.make_async_copy(hn_hbm.at[pl.ds(tok, 1)], xbuf.at[slot_, pl.ds(r, 1)],
                                  sem.at[slot_]).start()
            return carry
        lax.fori_loop(0, tm, row, 0)

    @pl.when(i == 0)
    def _():
        gather(0, 0)

    @pl.when(i + 1 < n_used)
    def _():
        gather(i + 1, 1 - slot)

    prev = te_ref[jnp.maximum(i - 1, 0)]

    @pl.when((i == 0) | (te_ref[i] != prev))
    def _():
        wg_sc[...] = wg_ref[0].astype(BF16)
        wu_sc[...] = wu_ref[0].astype(BF16)
        wd_sc[...] = wd_ref[0].astype(BF16)

    @pl.when(i < n_used)
    def _():
        pltpu.make_async_copy(hn_hbm.at[pl.ds(0, tm)], xbuf.at[slot], sem.at[slot]).wait()
        x = xbuf[slot].astype(BF16)
        gate = _dot(x, wg_sc[...])
        up = _dot(x, wu_sc[...])
        hid = gate * jax.nn.sigmoid(gate) * up
        y_ref[...] = ws_ref[...] * _dot(hid.astype(BF16), wd_sc[...])

    @pl.when(i >= n_used)
    def _():
        y_ref[...] = jnp.zeros(y_ref.shape, F32)


def _moe(tile_e, src_tok, n_used, hn, w_sorted, wg, wu, wd, tm):
    n_tiles = tile_e.shape[0]
    d = hn.shape[1]
    ff = wg.shape[2]
    grid_spec = pltpu.PrefetchScalarGridSpec(
        num_scalar_prefetch=3,
        grid=(n_tiles,),
        in_specs=[pl.BlockSpec(memory_space=pl.ANY),
                  pl.BlockSpec((tm, 1), lambda i, te, src, nt: (i, 0)),
                  pl.BlockSpec((1, d, ff), lambda i, te, src, nt: (te[i], 0, 0)),
                  pl.BlockSpec((1, d, ff), lambda i, te, src, nt: (te[i], 0, 0)),
                  pl.BlockSpec((1, ff, d), lambda i, te, src, nt: (te[i], 0, 0))],
        out_specs=pl.BlockSpec((tm, d), lambda i, te, src, nt: (i, 0)),
        scratch_shapes=[pltpu.VMEM((2, tm, d), F32),
                        pltpu.SemaphoreType.DMA((2,)),
                        pltpu.VMEM((d, ff), BF16), pltpu.VMEM((d, ff), BF16), pltpu.VMEM((ff, d), BF16)],
    )
    return pl.pallas_call(
        _moe_kernel,
        grid_spec=grid_spec,
        out_shape=jax.ShapeDtypeStruct((n_tiles * tm, d), F32),
        compiler_params=_cparams(1),
        name="moe",
    )(tile_e, src_tok, n_used, hn, w_sorted, wg, wu, wd)


def _combine_kernel(pos_ref, y_hbm, h_ref, fw_ref, o_ref, ybuf, sem):
    tm = h_ref.shape[0]
    i = pl.program_id(0)
    n = pl.num_programs(0)
    slot = i % 2

    def gather(tile, slot_):
        def row(r, carry):
            for k in range(2):
                p = pos_ref[2 * (tile * tm + r) + k]
                pltpu.make_async_copy(y_hbm.at[pl.ds(p, 1)], ybuf.at[slot_, k, pl.ds(r, 1)],
                                      sem.at[slot_]).start()
            return carry
        lax.fori_loop(0, tm, row, 0)

    @pl.when(i == 0)
    def _():
        gather(0, 0)

    @pl.when(i + 1 < n)
    def _():
        gather(i + 1, 1 - slot)

    for k in range(2):
        pltpu.make_async_copy(y_hbm.at[pl.ds(0, tm)], ybuf.at[slot, k], sem.at[slot]).wait()
    out = h_ref[...] + (ybuf[slot, 0] + ybuf[slot, 1])
    o_ref[...] = _rms(out, fw_ref[...])


def _combine(pos, y, h, fw, tm=256):
    n, d = h.shape
    grid_spec = pltpu.PrefetchScalarGridSpec(
        num_scalar_prefetch=1,
        grid=(n // tm,),
        in_specs=[pl.BlockSpec(memory_space=pl.ANY),
                  pl.BlockSpec((tm, d), lambda i, pos_: (i, 0)),
                  pl.BlockSpec((1, d), lambda i, pos_: (0, 0))],
        out_specs=pl.BlockSpec((tm, d), lambda i, pos_: (i, 0)),
        scratch_shapes=[pltpu.VMEM((2, 2, tm, d), F32), pltpu.SemaphoreType.DMA((2,))],
    )
    return pl.pallas_call(
        _combine_kernel,
        grid_spec=grid_spec,
        out_shape=jax.ShapeDtypeStruct((n, d), F32),
        compiler_params=_cparams(1),
        name="combine",
    )(pos, y, h, fw)


def _biasgen_kernel(tab_ref, bm_ref, bn_ref, bw_ref, om_ref, on_ref, ow_ref):
    h = pl.program_id(0)
    far = tab_ref[REL_BUCKETS - 1, h]

    def build(b_ref, shift):
        idx = b_ref[...]
        out = jnp.full(idx.shape, NEG_INF, F32)
        for bucket in range(REL_BUCKETS):
            out = jnp.where(idx == bucket, tab_ref[bucket, h] - shift, out)
        return out

    om_ref[0] = build(bm_ref, 0.0)
    on_ref[0] = build(bn_ref, far)
    ow_ref[0] = build(bw_ref, 0.0)


def _biasgen(rel_table, bm, bn, bw):
    heads = rel_table.shape[1]
    full = lambda a: pl.BlockSpec(a.shape, lambda h: (0, 0))
    out = lambda a: pl.BlockSpec((1,) + a.shape, lambda h: (h, 0, 0))
    return pl.pallas_call(
        _biasgen_kernel,
        grid=(heads,),
        in_specs=[pl.BlockSpec(memory_space=pltpu.SMEM), full(bm), full(bn), full(bw)],
        out_specs=[out(bm), out(bn), out(bw)],
        out_shape=[jax.ShapeDtypeStruct((heads,) + a.shape, F32) for a in (bm, bn, bw)],
        compiler_params=_cparams(1),
        name="biasgen",
    )(rel_table, bm, bn, bw)


def _bias_tables(rel_table, t, ncp):
    tq = AT_TQ
    far = REL_MAX_DIST
    buckets = _bucket_table(far + 1)
    i = np.arange(tq)[:, None]

    def bucket_map(dist, ok):
        return jnp.asarray(np.where(ok, buckets[np.clip(dist, 0, far)], -1).astype(np.int32))

    step = tq // CMP_STRIDE
    u = np.arange(2 * ncp)[None, :]
    dist_m = i - CMP_STRIDE * (u - ncp) - (CMP_BLOCK - 1)
    j = np.arange(2 * tq)[None, :]
    dist_n = tq + i - j
    j = np.arange(WINDOW + tq)[None, :]
    dist_w = WINDOW + i - j
    master, nbias, wbias = _biasgen(rel_table,
                                    bucket_map(dist_m, np.ones_like(dist_m, bool)),
                                    bucket_map(dist_n, dist_n >= 0),
                                    bucket_map(dist_w, (dist_w >= 0) & (dist_w < WINDOW)))
    master = master.reshape(N_NSA_KV, NSA_GQA, tq, 2 * ncp)
    bias_c = jnp.stack([master[..., ncp - step * qt:2 * ncp - step * qt] for qt in range(t // tq)], axis=2)
    bias_c = bias_c.reshape(N_NSA_KV, NSA_GQA, t, ncp)
    nbias = nbias.reshape(N_NSA_KV, NSA_GQA * tq, 2 * tq)
    wbias = wbias.reshape(N_NSA_KV, NSA_GQA * tq, WINDOW + tq)
    return bias_c, nbias, wbias


def _compress_weights(pe, w1, w2):
    half = CMP_STRIDE
    eye = jnp.eye(N_NSA_KV, dtype=F32)

    def expand_w1(w):
        w = w.reshape(half, HEAD_DIM, CMP_HIDDEN)
        return jnp.einsum("idn,gh->igdhn", w, eye).reshape(half * N_NSA_KV * HEAD_DIM, N_NSA_KV * CMP_HIDDEN)

    def expand_pe(p):
        return jnp.broadcast_to(p[:, None, :], (half, N_NSA_KV, HEAD_DIM)).reshape(1, -1)

    w1a = expand_w1(w1[:half * HEAD_DIM]).astype(BF16)
    w1b = expand_w1(w1[half * HEAD_DIM:]).astype(BF16)
    w2x = jnp.einsum("nd,gh->gnhd", w2, eye).reshape(N_NSA_KV * CMP_HIDDEN, N_NSA_KV * HEAD_DIM).astype(BF16)
    return expand_pe(pe[:half]), expand_pe(pe[half:]), w1a, w1b, w2x


def _routing_tables(eid, wts, tm):
    n = eid.shape[0]
    e_flat = eid.reshape(-1)
    onehot = (e_flat[:, None] == jnp.arange(N_EXPERTS, dtype=jnp.int32)[None, :]).astype(jnp.int32)
    csum = jnp.cumsum(onehot, axis=0)
    rank = jnp.take_along_axis(csum, e_flat[:, None], axis=1)[:, 0] - 1
    counts = csum[-1]
    padded = ((counts + tm - 1) // tm) * tm
    ends = jnp.cumsum(padded)
    starts = ends - padded
    pos = (starts[e_flat] + rank).astype(jnp.int32)
    n_tiles = (2 * n) // tm + N_EXPERTS
    rows = n_tiles * tm
    src_pair = jnp.zeros((rows,), jnp.int32).at[pos].set(jnp.arange(2 * n, dtype=jnp.int32))
    src_tok = src_pair // 2
    w_sorted = wts.reshape(-1)[src_pair]
    tile_start = jnp.arange(n_tiles, dtype=jnp.int32) * tm
    tile_e = jnp.minimum(jnp.searchsorted(ends, tile_start, side="right"), N_EXPERTS - 1).astype(jnp.int32)
    n_used = (ends[-1] // tm).astype(jnp.int32).reshape(1)
    last_e = tile_e[jnp.maximum(n_used[0] - 1, 0)]
    tile_e = jnp.where(tile_start < ends[-1], tile_e, last_e)
    return tile_e, src_tok, n_used, w_sorted.reshape(rows, 1), pos


def kernel(x, attn_norm_w, w_in, cmp_pe_k, cmp_pe_v, cmp_k_w1, cmp_k_w2, cmp_v_w1, cmp_v_w2,
           rel_bias_table, fox_forget_b, nsa_out_norm_w, fox_out_norm_w, w_out, ffn_norm_w,
           router_group_w, router_group_b, router_expert_w, router_expert_b,
           expert_w_gate, expert_w_up, expert_w_down, final_norm_w):
    b, t, d = x.shape
    n = b * t
    depth = w_in.shape[0]
    assert t % 512 == 0 and t >= WINDOW + AT_TQ and d == 2048 and t // SLC_BLOCK <= MASK_BLOCK_LANES
    rows = t // CMP_STRIDE
    bias_c, nbias, wbias = _bias_tables(rel_bias_table, t, rows)
    ns = t // SLC_BLOCK
    ratio = SLC_BLOCK // CMP_STRIDE
    span = CMP_BLOCK // CMP_STRIDE
    nc = (t - CMP_BLOCK) // CMP_STRIDE + 1
    impm = np.zeros((ns, rows), np.float32)
    for blk in range(ns):
        for a in range(ratio):
            for s in range(span):
                c = blk * ratio + a - s
                if 0 <= c < nc:
                    impm[blk, c] += 1.0
    impm = jnp.asarray(impm, BF16)
    eye_q = jnp.eye(AT_TQ, dtype=BF16)
    eye_h = jnp.eye(N_FOX_HEADS, dtype=BF16)
    tri = jnp.asarray(np.tril(np.ones((t, t), np.float32)), BF16)
    moe_tm = 512

    h = x.reshape(n, d)
    for layer in range(depth):
        w_main, w_misc = _project_weights(w_in[layer])
        main, misc = _proj(h, attn_norm_w[layer][None, :], w_main, w_misc)
        main3 = main.reshape(b, t, MAIN_COLS)
        misc3 = misc.reshape(b, t, LANES)

        xkv = jnp.stack([main3[:, :, COL_KCMP:COL_KCMP + 256], main3[:, :, COL_VCMP:COL_VCMP + 256]])
        xkv = xkv.reshape(2, b, rows, CMP_STRIDE * 256)
        pk = _compress_weights(cmp_pe_k[layer], cmp_k_w1[layer], cmp_k_w2[layer])
        pv = _compress_weights(cmp_pe_v[layer], cmp_v_w1[layer], cmp_v_w2[layer])
        kvc = _compress(xkv, *[jnp.stack([a, c]) for a, c in zip(pk, pv)])

        gates = misc3[:, :, MISC_GATE:MISC_GATE + 48].reshape(b, t, N_NSA_KV, 12).transpose(0, 2, 1, 3)
        o_cmp, sel = _cmpsel(main3, kvc, bias_c, gates, impm, eye_q)
        o_slc = _slc(main3, sel, nbias, gates)
        o_win = _win(main3, wbias, gates)

        c, ct = _foxprep(misc3, fox_forget_b[layer][None, :], tri, eye_h)
        o_fox = _fox(main3, c, ct)

        wr = jnp.concatenate([router_expert_w[layer], router_group_w[layer]], axis=1)
        wr = jnp.pad(wr, ((0, 0), (0, LANES - wr.shape[1])))
        wr_hi = wr.astype(BF16)
        wr_lo = (wr - wr_hi.astype(F32)).astype(BF16)
        br = jnp.concatenate([router_expert_b[layer], router_group_b[layer]])
        br = jnp.pad(br, (0, LANES - br.shape[0]))[None, :]
        half = N_NSA_HEADS * HEAD_DIM
        h, hn, eid, wts = _outproj(
            o_cmp.reshape(n, half), o_slc.reshape(n, half), o_win.reshape(n, half), o_fox.reshape(n, half),
            h, nsa_out_norm_w[layer][None, :], fox_out_norm_w[layer][None, :],
            w_out[layer].astype(BF16), ffn_norm_w[layer][None, :], jnp.stack([wr_hi, wr_lo]), br)

        tile_e, src_tok, n_used, w_sorted, pos = _routing_tables(eid[:, :2], wts[:, :2], moe_tm)
        y = _moe(tile_e, src_tok, n_used, hn, w_sorted,
                 expert_w_gate[layer], expert_w_up[layer], expert_w_down[layer], moe_tm)
        last = layer == depth - 1
        fw = final_norm_w if last else jnp.ones((d,), F32)
        assert last, "the fused final norm assumes a single layer"
        h = _combine(pos, y, h, fw[None, :])
    return h.reshape(b, t, d)
```

```python
import functools
import math

import numpy as np
import jax
import jax.numpy as jnp
from jax import lax
from jax.experimental import pallas as pl
from jax.experimental.pallas import tpu as pltpu

F32 = jnp.float32
BF16 = jnp.bfloat16

HEAD_DIM = 64
N_NSA_HEADS = 16
N_FOX_HEADS = 16
NSA_GQA = 4
N_NSA_KV = 4
CMP_BLOCK = 32
CMP_STRIDE = 16
CMP_HIDDEN = 128
SLC_BLOCK = 64
SLC_TOP_N = 16
WINDOW = 512
REL_BUCKETS = 32
REL_MAX_DIST = 128
N_GROUPS = 4
EXPERTS_PER_GROUP = 8
N_EXPERTS = 32
EXPERT_FF = 512
NORM_EPS = 1e-6
NEG_INF = -1e30
FORCE_BONUS = 1e4
SCALE = HEAD_DIM ** -0.5
LOG2E = math.log2(math.e)
Q_SCALE = SCALE * LOG2E

LANES = 128
VMEM_LIMIT = 56 * 1024 * 1024

COL_NQ = 0
COL_KCMP = 1024
COL_VCMP = 1280
COL_SLC = 1536
COL_WIN = 2048
COL_FQ = 2560
COL_FK = 3584
COL_FV = 4608
MAIN_COLS = 5632
MISC_GATE = 0
MISC_FF = 48

AT_TQ = 128
NSA_GROUPS_PER_STEP = 2
FOX_PAIRS_PER_STEP = 2
SLC_CHUNK = 512
SLC_SHIFT = 6
assert 1 << SLC_SHIFT == SLC_BLOCK
MASK_BLOCK_LANES = 32
PAD_LANE = HEAD_DIM + MASK_BLOCK_LANES
KEY_PAD = 512


def _nt(a, b):
    return lax.dot_general(a, b, (((1,), (1,)), ((), ())), preferred_element_type=F32)


def _dot(a, b):
    return jnp.dot(a, b, preferred_element_type=F32)


def _split3(x):
    hi = x.astype(BF16)
    r = x - hi.astype(F32)
    mid = r.astype(BF16)
    r = r - mid.astype(F32)
    return hi, mid, r.astype(BF16)


def _cparams(grid_rank):
    return pltpu.CompilerParams(dimension_semantics=("arbitrary",) * grid_rank, vmem_limit_bytes=VMEM_LIMIT)


def _bucket_table(n):
    d = np.arange(n, dtype=np.int64)
    max_exact = REL_BUCKETS // 2
    rel = np.log(np.maximum(d, 1).astype(np.float64) / max_exact) / math.log(REL_MAX_DIST / max_exact)
    scaled = rel * (REL_BUCKETS - max_exact)
    frac = scaled - np.floor(scaled)
    inner = (d > max_exact) & (d < REL_MAX_DIST)
    assert np.all((frac[inner] > 1e-3) & (frac[inner] < 1 - 1e-3))
    large = np.minimum(max_exact + np.floor(scaled + 1e-6).astype(np.int64), REL_BUCKETS - 1)
    return np.where(d < max_exact, d, large).astype(np.int32)


def _project_weights(w):
    d = w.shape[0]
    sizes = [1024] + [256] * 6 + [48, 1024, 1024, 1024, 16]
    offs = np.concatenate([[0], np.cumsum(sizes)])
    nq, kcmp, vcmp, kslc, vslc, kwin, vwin, ngate, fq, fk, fv, ff = [
        w[:, int(offs[i]):int(offs[i + 1])] for i in range(12)]

    def interleave(k, v):
        k = k.reshape(d, N_NSA_KV, HEAD_DIM)
        v = v.reshape(d, N_NSA_KV, HEAD_DIM)
        return jnp.stack([k, v], axis=2).reshape(d, N_NSA_KV * 2 * HEAD_DIM)

    main = jnp.concatenate([nq * Q_SCALE, kcmp, vcmp, interleave(kslc, vslc), interleave(kwin, vwin),
                            fq * Q_SCALE, fk, fv], axis=1)
    assert main.shape[1] == MAIN_COLS
    misc = jnp.concatenate([ngate, ff, jnp.zeros((d, LANES - 64), w.dtype)], axis=1)
    return main.astype(BF16), misc.astype(BF16)


def _proj_kernel(x_ref, nw_ref, w_ref, wm_ref, o_ref, om_ref, *, tn):
    x = x_ref[...]
    y = x * lax.rsqrt(jnp.mean(x * x, axis=-1, keepdims=True) + NORM_EPS) * nw_ref[...]
    xn = y.astype(BF16)
    om_ref[...] = _dot(xn, wm_ref[...])
    for c in range(o_ref.shape[1] // tn):
        o_ref[:, c * tn:(c + 1) * tn] = _dot(xn, w_ref[:, c * tn:(c + 1) * tn]).astype(BF16)


def _proj(x2, norm_w, w_main, w_misc, tm=512, tn=512):
    n, d = x2.shape
    once = pl.Buffered(1)
    return pl.pallas_call(
        functools.partial(_proj_kernel, tn=tn),
        grid=(n // tm,),
        in_specs=[pl.BlockSpec((tm, d), lambda i: (i, 0)),
                  pl.BlockSpec((1, d), lambda i: (0, 0)),
                  pl.BlockSpec((d, MAIN_COLS), lambda i: (0, 0), pipeline_mode=once),
                  pl.BlockSpec((d, LANES), lambda i: (0, 0), pipeline_mode=once)],
        out_specs=[pl.BlockSpec((tm, MAIN_COLS), lambda i: (i, 0)),
                   pl.BlockSpec((tm, LANES), lambda i: (i, 0))],
        out_shape=[jax.ShapeDtypeStruct((n, MAIN_COLS), BF16),
                   jax.ShapeDtypeStruct((n, LANES), F32)],
        compiler_params=_cparams(1),
        name="proj",
    )(x2, norm_w, w_main, w_misc)


def _compress_kernel(x_ref, pea_ref, peb_ref, w1a_ref, w1b_ref, w2_ref, o_ref):
    x = x_ref[0, 0].astype(F32)
    xa = (x + pea_ref[0]).astype(BF16)
    xb = (x + peb_ref[0]).astype(BF16)
    a = _dot(xa, w1a_ref[0])
    b = _dot(xb, w1b_ref[0])
    rows = a.shape[0]
    pre = a + pltpu.roll(b, rows - 1, 0)
    hid = pre * jax.nn.sigmoid(pre)
    out = _dot(hid.astype(BF16), w2_ref[0])
    for g in range(N_NSA_KV):
        o_ref[0, 0, g] = out[:, g * HEAD_DIM:(g + 1) * HEAD_DIM]


def _compress(xkv, pea, peb, w1a, w1b, w2):
    _, b, rows, width = xkv.shape
    hid = N_NSA_KV * CMP_HIDDEN
    return pl.pallas_call(
        _compress_kernel,
        grid=(2, b),
        in_specs=[pl.BlockSpec((1, 1, rows, width), lambda s, i: (s, i, 0, 0)),
                  pl.BlockSpec((1, 1, width), lambda s, i: (s, 0, 0)),
                  pl.BlockSpec((1, 1, width), lambda s, i: (s, 0, 0)),
                  pl.BlockSpec((1, width, hid), lambda s, i: (s, 0, 0)),
                  pl.BlockSpec((1, width, hid), lambda s, i: (s, 0, 0)),
                  pl.BlockSpec((1, hid, N_NSA_KV * HEAD_DIM), lambda s, i: (s, 0, 0))],
        out_specs=pl.BlockSpec((1, 1, N_NSA_KV, rows, HEAD_DIM), lambda s, i: (s, i, 0, 0, 0)),
        out_shape=jax.ShapeDtypeStruct((2, b, N_NSA_KV, rows, HEAD_DIM), F32),
        compiler_params=_cparams(2),
        name="compress",
    )(xkv, pea, peb, w1a, w1b, w2)


def _foxprep_kernel(misc_ref, fb_ref, tri_ref, eye_ref, c_ref, ct_ref):
    z = misc_ref[0][:, MISC_FF:MISC_FF + N_FOX_HEADS] + fb_ref[...]
    logf = (jnp.minimum(z, 0.0) - jnp.log(1.0 + jnp.exp(-jnp.abs(z)))) * LOG2E
    tri = tri_ref[...]
    c = None
    for part in _split3(logf):
        term = _dot(tri, part)
        c = term if c is None else c + term
    ct = None
    for part in _split3(c):
        term = _nt(eye_ref[...], part)
        ct = term if ct is None else ct + term
    for p in range(N_FOX_HEADS // 2):
        c_ref[0, p] = c[:, 2 * p:2 * p + 2]
        ct_ref[0, p] = ct[2 * p:2 * p + 2, :]


def _foxprep(misc3, fb, tri, eye):
    b, t, _ = misc3.shape
    hp = N_FOX_HEADS // 2
    return pl.pallas_call(
        _foxprep_kernel,
        grid=(b,),
        in_specs=[pl.BlockSpec((1, t, LANES), lambda i: (i, 0, 0)),
                  pl.BlockSpec((1, N_FOX_HEADS), lambda i: (0, 0)),
                  pl.BlockSpec((t, t), lambda i: (0, 0)),
                  pl.BlockSpec((N_FOX_HEADS, N_FOX_HEADS), lambda i: (0, 0))],
        out_specs=[pl.BlockSpec((1, hp, t, 2), lambda i: (i, 0, 0, 0)),
                   pl.BlockSpec((1, hp, 2, t), lambda i: (i, 0, 0, 0))],
        out_shape=[jax.ShapeDtypeStruct((b, hp, t, 2), F32),
                   jax.ShapeDtypeStruct((b, hp, 2, t), F32)],
        compiler_params=_cparams(1),
        name="foxprep",
    )(misc3, fb, tri, eye)


def _cmpsel_kernel(q_ref, kvc_ref, bias_ref, gate_ref, impm_ref, eye_ref, o_ref, sel_ref):
    tq = q_ref.shape[1]
    ncp = kvc_ref.shape[3]
    ns = impm_ref.shape[0]
    qw = NSA_GQA * HEAD_DIM
    t0 = pl.program_id(2) * tq
    rows = NSA_GQA * tq
    t_col = t0 + (lax.broadcasted_iota(jnp.int32, (rows, ncp), 0) & (tq - 1))
    c_row = lax.broadcasted_iota(jnp.int32, (rows, ncp), 1)
    valid = t_col >= c_row * CMP_STRIDE + (CMP_BLOCK - 1)
    blk = lax.broadcasted_iota(jnp.int32, (ns, tq), 0)
    t_row = t0 + lax.broadcasted_iota(jnp.int32, (ns, tq), 1)
    cur = t_row // SLC_BLOCK
    bonus = jnp.where((blk == 0) | (blk == cur) | (blk == cur - 1), FORCE_BONUS, 0.0)
    blk_valid = blk * SLC_BLOCK <= t_row
    pad_row = jnp.where(lax.broadcasted_iota(jnp.int32, (HEAD_DIM - MASK_BLOCK_LANES, tq), 0) == 0, 1.0, 0.0)
    q_all = q_ref[0]

    for c in range(kvc_ref.shape[2]):
        kc = kvc_ref[0, 0, c].astype(BF16)
        vc = kvc_ref[1, 0, c].astype(BF16)
        q = q_all[:, c * qw:(c + 1) * qw]
        qst = jnp.concatenate([q[:, r * HEAD_DIM:(r + 1) * HEAD_DIM] for r in range(NSA_GQA)], axis=0)
        s = _nt(qst, kc) + bias_ref[c].reshape(rows, ncp)
        s = jnp.where(valid, s, NEG_INF)
        m = jnp.max(s, axis=-1, keepdims=True)
        e = jnp.where(valid, jnp.exp2(s - m), 0.0)
        p = e / jnp.maximum(jnp.sum(e, axis=-1, keepdims=True), 1e-30)
        o = _dot(p.astype(BF16), vc)
        gate = jax.nn.sigmoid(gate_ref[0, c])
        p_grp = jnp.zeros((tq, ncp), F32)
        for r in range(NSA_GQA):
            col = (c * NSA_GQA + r) * HEAD_DIM
            o_ref[0, :, col:col + HEAD_DIM] = o[r * tq:(r + 1) * tq] * gate[:, 3 * r:3 * r + 1]
            p_grp = p_grp + p[r * tq:(r + 1) * tq]

        imp = None
        for part in _split3(p_grp):
            term = _nt(impm_ref[...], part)
            imp = term if imp is None else imp + term
        score = jnp.where(blk_valid, imp + bonus, NEG_INF)
        rank = jnp.zeros((ns, tq), F32)
        for m_blk in range(ns):
            other = score[m_blk:m_blk + 1, :]
            ahead = (other > score) | ((other == score) & (blk > m_blk))
            rank = rank + jnp.where(ahead, 1.0, 0.0)
        unsel = jnp.where(rank < float(min(SLC_TOP_N, ns)), 0.0, 1.0)
        parts = [unsel, pad_row]
        if ns < MASK_BLOCK_LANES:
            parts.insert(1, jnp.zeros((MASK_BLOCK_LANES - ns, tq), F32))
        flags = _nt(eye_ref[...], jnp.concatenate(parts, axis=0).astype(BF16))
        sel_ref[0, c] = (flags * NEG_INF).astype(BF16)


def _cmpsel(main3, kvc, bias_c, gates, impm, eye):
    b, t, _ = main3.shape
    ncp = kvc.shape[3]
    ns = impm.shape[0]
    tq = AT_TQ
    qw = NSA_GQA * HEAD_DIM
    gs = NSA_GROUPS_PER_STEP
    return pl.pallas_call(
        _cmpsel_kernel,
        grid=(b, N_NSA_KV // gs, t // tq),
        in_specs=[pl.BlockSpec((1, tq, gs * qw), lambda i, g, j: (i, j, g)),
                  pl.BlockSpec((2, 1, gs, ncp, HEAD_DIM), lambda i, g, j: (0, i, g, 0, 0)),
                  pl.BlockSpec((gs, NSA_GQA, tq, ncp), lambda i, g, j: (g, 0, j, 0)),
                  pl.BlockSpec((1, gs, tq, 3 * NSA_GQA), lambda i, g, j: (i, g, j, 0)),
                  pl.BlockSpec((ns, ncp), lambda i, g, j: (0, 0)),
                  pl.BlockSpec((tq, tq), lambda i, g, j: (0, 0))],
        out_specs=[pl.BlockSpec((1, tq, gs * qw), lambda i, g, j: (i, j, g)),
                   pl.BlockSpec((1, gs, tq, HEAD_DIM), lambda i, g, j: (i, g, j, 0))],
        out_shape=[jax.ShapeDtypeStruct((b, t, N_NSA_HEADS * HEAD_DIM), F32),
                   jax.ShapeDtypeStruct((b, N_NSA_KV, t, HEAD_DIM), BF16)],
        compiler_params=_cparams(3),
        name="cmpsel",
    )(main3, kvc, bias_c, gates, impm, eye)


def _softmax_first(s, v):
    m = jnp.max(s, axis=-1, keepdims=True)
    p = jnp.exp2(s - m)
    return m, jnp.sum(p, axis=-1, keepdims=True), _dot(p.astype(BF16), v)


def _softmax_next(s, v, m, l, acc):
    m_new = jnp.maximum(m, jnp.max(s, axis=-1, keepdims=True))
    alpha = jnp.exp2(m - m_new)
    p = jnp.exp2(s - m_new)
    return m_new, alpha * l + jnp.sum(p, axis=-1, keepdims=True), alpha * acc + _dot(p.astype(BF16), v)


def _stack_heads(q, tail, n_heads):
    return jnp.concatenate(
        [jnp.concatenate([q[:, r * HEAD_DIM:(r + 1) * HEAD_DIM], tail], axis=1) for r in range(n_heads)], axis=0)


def _fill_key_scratch(kp_sc, kvp_sc, kv, with_blocks):
    t = kv.shape[0]
    lane = lax.broadcasted_iota(jnp.int32, (t, LANES), 1)
    if with_blocks:
        blk = lax.shift_right_logical(lax.broadcasted_iota(jnp.int32, (t, LANES), 0), SLC_SHIFT)
        aug = jnp.where(lane - HEAD_DIM == blk, 1.0, 0.0).astype(BF16)
    else:
        aug = jnp.zeros((t, LANES), BF16)
    kp_sc[KEY_PAD:, :] = jnp.where(lane < HEAD_DIM, kv, aug)
    lane_p = lax.broadcasted_iota(jnp.int32, (KEY_PAD, LANES), 1)
    kp_sc[0:KEY_PAD, :] = jnp.where(lane_p == PAD_LANE, 1.0, 0.0).astype(BF16)
    kvp_sc[KEY_PAD:, :] = jnp.where(lane < HEAD_DIM, jnp.where(lane == 0, 1.0, 0.0).astype(BF16), kv)
    kvp_sc[0:KEY_PAD, :] = jnp.zeros((KEY_PAD, LANES), BF16)


def _nsa_first(s, v):
    m = jnp.max(s, axis=-1, keepdims=True)
    return m, _dot(jnp.exp2(s - m).astype(BF16), v)


def _nsa_next(s, v, m, acc):
    m_new = jnp.maximum(m, jnp.max(s, axis=-1, keepdims=True))
    return m_new, jnp.exp2(m - m_new) * acc + _dot(jnp.exp2(s - m_new).astype(BF16), v)


def _store_gated(o_ref, group, acc, gate, branch):
    tq = o_ref.shape[1]
    o = acc[:, HEAD_DIM:] / acc[:, 0:1]
    for r in range(NSA_GQA):
        col = (group * NSA_GQA + r) * HEAD_DIM
        gcol = 3 * r + branch
        o_ref[0, :, col:col + HEAD_DIM] = o[r * tq:(r + 1) * tq] * gate[:, gcol:gcol + 1]


def _slc_kernel(q_ref, kv_ref, sel_ref, nbias_ref, gate_ref, o_ref, kp_sc, kvp_sc):
    tq = q_ref.shape[1]
    chunk = SLC_CHUNK
    qt = pl.program_id(2)
    t0 = qt * tq

    groups = range(kp_sc.shape[0])
    qw = NSA_GQA * HEAD_DIM

    @pl.when(qt == 0)
    def _():
        for c in groups:
            _fill_key_scratch(kp_sc.at[c], kvp_sc.at[c], kv_ref[0, :, c * LANES:(c + 1) * LANES], True)

    q = q_ref[0]
    qst = [_stack_heads(q[:, c * qw:(c + 1) * qw], sel_ref[0, c], NSA_GQA) for c in groups]

    near = pl.multiple_of(t0 - tq + KEY_PAD, tq)
    state = tuple(_nsa_first(_nt(qst[c], kp_sc[c, pl.ds(near, 2 * tq), :]) + nbias_ref[c],
                             kvp_sc[c, pl.ds(near, 2 * tq), :]) for c in groups)

    def far(j, carry):
        first = pl.multiple_of(t0 - tq + KEY_PAD - (j + 1) * chunk, tq)
        return tuple(_nsa_next(_nt(qst[c], kp_sc[c, pl.ds(first, chunk), :]),
                               kvp_sc[c, pl.ds(first, chunk), :], *carry[c]) for c in groups)

    state = lax.fori_loop(0, (t0 - tq + chunk - 1) // chunk, far, state)
    for c in groups:
        _store_gated(o_ref, c, state[c][1], jax.nn.sigmoid(gate_ref[0, c]), 1)


def _slc(main3, sel, nbias, gates):
    b, t, _ = main3.shape
    tq = AT_TQ
    qw = NSA_GQA * HEAD_DIM
    assert KEY_PAD >= SLC_CHUNK and KEY_PAD % tq == 0
    gs = NSA_GROUPS_PER_STEP
    kvw = gs * LANES
    return pl.pallas_call(
        _slc_kernel,
        grid=(b, N_NSA_KV // gs, t // tq),
        in_specs=[pl.BlockSpec((1, tq, gs * qw), lambda i, g, j: (i, j, g)),
                  pl.BlockSpec((1, t, kvw), lambda i, g, j: (i, 0, COL_SLC // kvw + g)),
                  pl.BlockSpec((1, gs, tq, HEAD_DIM), lambda i, g, j: (i, g, j, 0)),
                  pl.BlockSpec((gs, NSA_GQA * tq, 2 * tq), lambda i, g, j: (g, 0, 0)),
                  pl.BlockSpec((1, gs, tq, 3 * NSA_GQA), lambda i, g, j: (i, g, j, 0))],
        out_specs=pl.BlockSpec((1, tq, gs * qw), lambda i, g, j: (i, j, g)),
        out_shape=jax.ShapeDtypeStruct((b, t, N_NSA_HEADS * HEAD_DIM), F32),
        scratch_shapes=[pltpu.VMEM((gs, t + KEY_PAD, LANES), BF16), pltpu.VMEM((gs, t + KEY_PAD, LANES), BF16)],
        compiler_params=_cparams(3),
        name="slc",
    )(main3, main3, sel, nbias, gates)


def _win_kernel(q_ref, kv_ref, bias_ref, gate_ref, o_ref, kp_sc, kvp_sc):
    tq = q_ref.shape[1]
    span = bias_ref.shape[2]
    qt = pl.program_id(2)

    groups = range(kp_sc.shape[0])
    qw = NSA_GQA * HEAD_DIM

    @pl.when(qt == 0)
    def _():
        for c in groups:
            _fill_key_scratch(kp_sc.at[c], kvp_sc.at[c], kv_ref[0, :, c * LANES:(c + 1) * LANES], False)

    tail = jnp.where(lax.broadcasted_iota(jnp.int32, (tq, HEAD_DIM), 1) == PAD_LANE - HEAD_DIM, NEG_INF, 0.0)
    tail = tail.astype(BF16)
    q = q_ref[0]
    first = pl.multiple_of(qt * tq, tq)
    for c in groups:
        qst = _stack_heads(q[:, c * qw:(c + 1) * qw], tail, NSA_GQA)
        s = _nt(qst, kp_sc[c, pl.ds(first, span), :]) + bias_ref[c]
        _, acc = _nsa_first(s, kvp_sc[c, pl.ds(first, span), :])
        _store_gated(o_ref, c, acc, jax.nn.sigmoid(gate_ref[0, c]), 2)


def _win(main3, wbias, gates):
    b, t, _ = main3.shape
    tq = AT_TQ
    qw = NSA_GQA * HEAD_DIM
    span = wbias.shape[2]
    assert span - tq == KEY_PAD
    gs = NSA_GROUPS_PER_STEP
    kvw = gs * LANES
    return pl.pallas_call(
        _win_kernel,
        grid=(b, N_NSA_KV // gs, t // tq),
        in_specs=[pl.BlockSpec((1, tq, gs * qw), lambda i, g, j: (i, j, g)),
                  pl.BlockSpec((1, t, kvw), lambda i, g, j: (i, 0, COL_WIN // kvw + g)),
                  pl.BlockSpec((gs, NSA_GQA * tq, span), lambda i, g, j: (g, 0, 0)),
                  pl.BlockSpec((1, gs, tq, 3 * NSA_GQA), lambda i, g, j: (i, g, j, 0))],
        out_specs=pl.BlockSpec((1, tq, gs * qw), lambda i, g, j: (i, j, g)),
        out_shape=jax.ShapeDtypeStruct((b, t, N_NSA_HEADS * HEAD_DIM), F32),
        scratch_shapes=[pltpu.VMEM((gs, t + KEY_PAD, LANES), BF16), pltpu.VMEM((gs, t + KEY_PAD, LANES), BF16)],
        compiler_params=_cparams(3),
        name="win",
    )(main3, main3, wbias, gates)


def _fox_kernel(q_ref, k_ref, v_ref, c_ref, ct_ref, o_ref, *, chunk):
    tq = q_ref.shape[1]
    t0 = pl.program_id(2) * tq
    pairs = range(c_ref.shape[1])
    lane = lax.broadcasted_iota(jnp.int32, (tq, LANES), 1)
    low = lane < HEAD_DIM
    q_all = q_ref[0]
    zero = jnp.zeros((tq, LANES), BF16)
    qst, c_col = [], []
    for p in pairs:
        q = q_all[:, p * LANES:(p + 1) * LANES]
        qst.append(jnp.concatenate([jnp.where(low, q, zero), jnp.where(low, zero, q)], axis=0))
        c_col.append(c_ref[0, p])

    def logits(p, start):
        s = _nt(qst[p], k_ref[0, pl.ds(start, chunk), p * LANES:(p + 1) * LANES])
        c_row = ct_ref[0, p, :, pl.ds(start, chunk)]
        return jnp.concatenate([s[:tq] + c_col[p][:, 0:1] - c_row[0:1],
                                s[tq:] + c_col[p][:, 1:2] - c_row[1:2]], axis=0)

    def values(p, start):
        return v_ref[0, pl.ds(start, chunk), p * LANES:(p + 1) * LANES]

    d0 = pl.multiple_of((t0 // chunk) * chunk, chunk)
    row = lax.broadcasted_iota(jnp.int32, (2 * tq, chunk), 0) & (tq - 1)
    causal = row - lax.broadcasted_iota(jnp.int32, (2 * tq, chunk), 1) + (t0 - d0) >= 0
    state = tuple(_softmax_first(jnp.where(causal, logits(p, d0), NEG_INF), values(p, d0)) for p in pairs)

    def below(j, carry):
        start = pl.multiple_of(j * chunk, chunk)
        return tuple(_softmax_next(logits(p, start), values(p, start), *carry[p]) for p in pairs)

    state = lax.fori_loop(0, t0 // chunk, below, state)
    for p in pairs:
        _, l, acc = state[p]
        o = acc / l
        o_ref[0, :, p * LANES:(p + 1) * LANES] = jnp.where(low, o[:tq], o[tq:])


def _fox(main3, c, ct, tq=256, chunk=512):
    b, t, _ = main3.shape
    ps = FOX_PAIRS_PER_STEP
    hp = N_FOX_HEADS // 2
    w = ps * LANES
    assert chunk % tq == 0 and t % chunk == 0
    return pl.pallas_call(
        functools.partial(_fox_kernel, chunk=chunk),
        grid=(b, hp // ps, t // tq),
        in_specs=[pl.BlockSpec((1, tq, w), lambda i, p, j: (i, j, COL_FQ // w + p)),
                  pl.BlockSpec((1, t, w), lambda i, p, j: (i, 0, COL_FK // w + p)),
                  pl.BlockSpec((1, t, w), lambda i, p, j: (i, 0, COL_FV // w + p)),
                  pl.BlockSpec((1, ps, tq, 2), lambda i, p, j: (i, p, j, 0)),
                  pl.BlockSpec((1, ps, 2, t), lambda i, p, j: (i, p, 0, 0))],
        out_specs=pl.BlockSpec((1, tq, w), lambda i, p, j: (i, j, p)),
        out_shape=jax.ShapeDtypeStruct((b, t, N_FOX_HEADS * HEAD_DIM), F32),
        compiler_params=_cparams(3),
        name="fox",
    )(main3, main3, main3, c, ct)


def _rms(x, w):
    return x * lax.rsqrt(jnp.mean(x * x, axis=-1, keepdims=True) + NORM_EPS) * w


def _outproj_kernel(oc_ref, os_ref, ow_ref, of_ref, x_ref, nnw_ref, fnw_ref, wo_ref, ffw_ref,
                    wr_ref, br_ref, h_ref, hn_ref, eid_ref, wt_ref):
    o_nsa = oc_ref[...] + os_ref[...] + ow_ref[...]
    mixed = jnp.concatenate([_rms(o_nsa, nnw_ref[...]), _rms(of_ref[...], fnw_ref[...])], axis=-1)
    h = x_ref[...] + _dot(mixed.astype(BF16), wo_ref[...])
    h_ref[...] = h
    hn = _rms(h, ffw_ref[...])
    hn_ref[...] = hn

    h_hi, h_mid, _ = _split3(hn)
    w_hi = wr_ref[0]
    w_lo = wr_ref[1]
    logits = (_dot(h_hi, w_hi) + _dot(h_mid, w_hi) + _dot(h_hi, w_lo)) + br_ref[...]
    tm = logits.shape[0]
    lane = lax.broadcasted_iota(jnp.int32, (tm, LANES), 1)
    big = jnp.int32(LANES)
    is_grp = (lane >= N_EXPERTS) & (lane < N_EXPERTS + N_GROUPS)
    glog = jnp.where(is_grp, logits, NEG_INF)
    gmax = jnp.max(glog, axis=-1, keepdims=True)
    gsel = jnp.min(jnp.where(glog == gmax, lane, big), axis=-1, keepdims=True) - N_EXPERTS
    p_gsel = 1.0 / jnp.sum(jnp.where(is_grp, jnp.exp(glog - gmax), 0.0), axis=-1, keepdims=True)
    in_grp = (lane < N_EXPERTS) & (lane // EXPERTS_PER_GROUP == gsel)
    e1 = jnp.where(in_grp, logits, NEG_INF)
    v1 = jnp.max(e1, axis=-1, keepdims=True)
    i1 = jnp.min(jnp.where(e1 == v1, lane, big), axis=-1, keepdims=True)
    e2 = jnp.where(lane == i1, NEG_INF, e1)
    v2 = jnp.max(e2, axis=-1, keepdims=True)
    i2 = jnp.min(jnp.where(e2 == v2, lane, big), axis=-1, keepdims=True)
    ex = jnp.exp(v2 - v1)
    w1 = p_gsel / (1.0 + ex)
    w2 = p_gsel * ex / (1.0 + ex)
    eid_ref[...] = jnp.where(lane == 0, i1, jnp.where(lane == 1, i2, 0))
    wt_ref[...] = jnp.where(lane == 0, w1, jnp.where(lane == 1, w2, 0.0))


def _outproj(oc, osl, ow, of, x2, nnw, fnw, wo, ffw, wr, br, tm=256):
    n, d = x2.shape
    half = oc.shape[1]
    row = lambda i: (i, 0)
    fixed = lambda i: (0, 0)
    return pl.pallas_call(
        _outproj_kernel,
        grid=(n // tm,),
        in_specs=[pl.BlockSpec((tm, half), row), pl.BlockSpec((tm, half), row),
                  pl.BlockSpec((tm, half), row), pl.BlockSpec((tm, half), row),
                  pl.BlockSpec((tm, d), row),
                  pl.BlockSpec((1, half), fixed), pl.BlockSpec((1, half), fixed),
                  pl.BlockSpec((d, d), fixed), pl.BlockSpec((1, d), fixed),
                  pl.BlockSpec((2, d, LANES), lambda i: (0, 0, 0)), pl.BlockSpec((1, LANES), fixed)],
        out_specs=[pl.BlockSpec((tm, d), row), pl.BlockSpec((tm, d), row),
                   pl.BlockSpec((tm, LANES), row), pl.BlockSpec((tm, LANES), row)],
        out_shape=[jax.ShapeDtypeStruct((n, d), F32), jax.ShapeDtypeStruct((n, d), F32),
                   jax.ShapeDtypeStruct((n, LANES), jnp.int32), jax.ShapeDtypeStruct((n, LANES), F32)],
        compiler_params=_cparams(1),
        name="outproj",
    )(oc, osl, ow, of, x2, nnw, fnw, wo, ffw, wr, br)


def _moe_kernel(te_ref, src_ref, nt_ref, hn_hbm, ws_ref, wg_ref, wu_ref, wd_ref, y_ref,
                xbuf, sem, wg_sc, wu_sc, wd_sc):
    tm = xbuf.shape[1]
    i = pl.program_id(0)
    n_used = nt_ref[0]
    slot = i % 2

    def gather(tile, slot_):
        def row(r, carry):
            tok = src_ref[tile * tm + r]
            pltpu.make_async_copy(hn_hbm.at[pl.ds(tok, 1)], xbuf.at[slot_, pl.ds(r, 1)],
                                  sem.at[slot_]).start()
            return carry
        lax.fori_loop(0, tm, row, 0, unroll=8)

    @pl.when(i == 0)
    def _():
        gather(0, 0)

    @pl.when(i + 1 < n_used)
    def _():
        gather(i + 1, 1 - slot)

    prev = te_ref[jnp.maximum(i - 1, 0)]

    @pl.when((i == 0) | (te_ref[i] != prev))
    def _():
        wg_sc[...] = wg_ref[0].astype(BF16)
        wu_sc[...] = wu_ref[0].astype(BF16)
        wd_sc[...] = wd_ref[0].astype(BF16)

    @pl.when(i < n_used)
    def _():
        pltpu.make_async_copy(hn_hbm.at[pl.ds(0, tm)], xbuf.at[slot], sem.at[slot]).wait()
        x = xbuf[slot].astype(BF16)
        gate = _dot(x, wg_sc[...])
        up = _dot(x, wu_sc[...])
        hid = gate * jax.nn.sigmoid(gate) * up
        y_ref[...] = ws_ref[...] * _dot(hid.astype(BF16), wd_sc[...])

    @pl.when(i >= n_used)
    def _():
        y_ref[...] = jnp.zeros(y_ref.shape, F32)


def _moe(tile_e, src_tok, n_used, hn, w_sorted, wg, wu, wd, tm):
    n_tiles = tile_e.shape[0]
    d = hn.shape[1]
    ff = wg.shape[2]
    grid_spec = pltpu.PrefetchScalarGridSpec(
        num_scalar_prefetch=3,
        grid=(n_tiles,),
        in_specs=[pl.BlockSpec(memory_space=pl.ANY),
                  pl.BlockSpec((tm, 1), lambda i, te, src, nt: (i, 0)),
                  pl.BlockSpec((1, d, ff), lambda i, te, src, nt: (te[i], 0, 0)),
                  pl.BlockSpec((1, d, ff), lambda i, te, src, nt: (te[i], 0, 0)),
                  pl.BlockSpec((1, ff, d), lambda i, te, src, nt: (te[i], 0, 0))],
        out_specs=pl.BlockSpec((tm, d), lambda i, te, src, nt: (i, 0)),
        scratch_shapes=[pltpu.VMEM((2, tm, d), F32),
                        pltpu.SemaphoreType.DMA((2,)),
                        pltpu.VMEM((d, ff), BF16), pltpu.VMEM((d, ff), BF16), pltpu.VMEM((ff, d), BF16)],
    )
    return pl.pallas_call(
        _moe_kernel,
        grid_spec=grid_spec,
        out_shape=jax.ShapeDtypeStruct((n_tiles * tm, d), F32),
        compiler_params=_cparams(1),
        name="moe",
    )(tile_e, src_tok, n_used, hn, w_sorted, wg, wu, wd)


def _combine_kernel(pos_ref, y_hbm, h_ref, fw_ref, o_ref, ybuf, sem):
    tm = h_ref.shape[0]
    i = pl.program_id(0)
    n = pl.num_programs(0)
    slot = i % 2

    def gather(tile, slot_):
        def row(r, carry):
            for k in range(2):
                p = pos_ref[2 * (tile * tm + r) + k]
                pltpu.make_async_copy(y_hbm.at[pl.ds(p, 1)], ybuf.at[slot_, k, pl.ds(r, 1)],
                                      sem.at[slot_]).start()
            return carry
        lax.fori_loop(0, tm, row, 0, unroll=8)

    @pl.when(i == 0)
    def _():
        gather(0, 0)

    @pl.when(i + 1 < n)
    def _():
        gather(i + 1, 1 - slot)

    for k in range(2):
        pltpu.make_async_copy(y_hbm.at[pl.ds(0, tm)], ybuf.at[slot, k], sem.at[slot]).wait()
    out = h_ref[...] + (ybuf[slot, 0] + ybuf[slot, 1])
    o_ref[...] = _rms(out, fw_ref[...])


def _combine(pos, y, h, fw, tm=256):
    n, d = h.shape
    grid_spec = pltpu.PrefetchScalarGridSpec(
        num_scalar_prefetch=1,
        grid=(n // tm,),
        in_specs=[pl.BlockSpec(memory_space=pl.ANY),
                  pl.BlockSpec((tm, d), lambda i, pos_: (i, 0)),
                  pl.BlockSpec((1, d), lambda i, pos_: (0, 0))],
        out_specs=pl.BlockSpec((tm, d), lambda i, pos_: (i, 0)),
        scratch_shapes=[pltpu.VMEM((2, 2, tm, d), F32), pltpu.SemaphoreType.DMA((2,))],
    )
    return pl.pallas_call(
        _combine_kernel,
        grid_spec=grid_spec,
        out_shape=jax.ShapeDtypeStruct((n, d), F32),
        compiler_params=_cparams(1),
        name="combine",
    )(pos, y, h, fw)


def _biasgen_kernel(tab_ref, bm_ref, bn_ref, bw_ref, om_ref, on_ref, ow_ref):
    h = pl.program_id(0)
    far = tab_ref[REL_BUCKETS - 1, h]

    def build(b_ref, shift):
        idx = b_ref[...]
        out = jnp.full(idx.shape, NEG_INF, F32)
        for bucket in range(REL_BUCKETS):
            out = jnp.where(idx == bucket, (tab_ref[bucket, h] - shift) * LOG2E, out)
        return out

    om_ref[0] = build(bm_ref, 0.0)
    on_ref[0] = build(bn_ref, far)
    ow_ref[0] = build(bw_ref, 0.0)


def _biasgen(rel_table, bm, bn, bw):
    heads = rel_table.shape[1]
    full = lambda a: pl.BlockSpec(a.shape, lambda h: (0, 0))
    out = lambda a: pl.BlockSpec((1,) + a.shape, lambda h: (h, 0, 0))
    return pl.pallas_call(
        _biasgen_kernel,
        grid=(heads,),
        in_specs=[pl.BlockSpec(memory_space=pltpu.SMEM), full(bm), full(bn), full(bw)],
        out_specs=[out(bm), out(bn), out(bw)],
        out_shape=[jax.ShapeDtypeStruct((heads,) + a.shape, F32) for a in (bm, bn, bw)],
        compiler_params=_cparams(1),
        name="biasgen",
    )(rel_table, bm, bn, bw)


def _bias_tables(rel_table, t, ncp):
    tq = AT_TQ
    far = REL_MAX_DIST
    buckets = _bucket_table(far + 1)
    i = np.arange(tq)[:, None]

    def bucket_map(dist, ok):
        return jnp.asarray(np.where(ok, buckets[np.clip(dist, 0, far)], -1).astype(np.int32))

    step = tq // CMP_STRIDE
    u = np.arange(2 * ncp)[None, :]
    dist_m = i - CMP_STRIDE * (u - ncp) - (CMP_BLOCK - 1)
    j = np.arange(2 * tq)[None, :]
    dist_n = tq + i - j
    j = np.arange(WINDOW + tq)[None, :]
    dist_w = WINDOW + i - j
    master, nbias, wbias = _biasgen(rel_table,
                                    bucket_map(dist_m, np.ones_like(dist_m, bool)),
                                    bucket_map(dist_n, dist_n >= 0),
                                    bucket_map(dist_w, (dist_w >= 0) & (dist_w < WINDOW)))
    master = master.reshape(N_NSA_KV, NSA_GQA, tq, 2 * ncp)
    bias_c = jnp.stack([master[..., ncp - step * qt:2 * ncp - step * qt] for qt in range(t // tq)], axis=2)
    bias_c = bias_c.reshape(N_NSA_KV, NSA_GQA, t, ncp)
    nbias = nbias.reshape(N_NSA_KV, NSA_GQA * tq, 2 * tq)
    wbias = wbias.reshape(N_NSA_KV, NSA_GQA * tq, WINDOW + tq)
    return bias_c, nbias, wbias


def _compress_weights(pe, w1, w2):
    half = CMP_STRIDE
    eye = jnp.eye(N_NSA_KV, dtype=F32)

    def expand_w1(w):
        w = w.reshape(half, HEAD_DIM, CMP_HIDDEN)
        return jnp.einsum("idn,gh->igdhn", w, eye).reshape(half * N_NSA_KV * HEAD_DIM, N_NSA_KV * CMP_HIDDEN)

    def expand_pe(p):
        return jnp.broadcast_to(p[:, None, :], (half, N_NSA_KV, HEAD_DIM)).reshape(1, -1)

    w1a = expand_w1(w1[:half * HEAD_DIM]).astype(BF16)
    w1b = expand_w1(w1[half * HEAD_DIM:]).astype(BF16)
    w2x = jnp.einsum("nd,gh->gnhd", w2, eye).reshape(N_NSA_KV * CMP_HIDDEN, N_NSA_KV * HEAD_DIM).astype(BF16)
    return expand_pe(pe[:half]), expand_pe(pe[half:]), w1a, w1b, w2x


def _routing_tables(eid, wts, tm):
    n = eid.shape[0]
    e_flat = eid.reshape(-1)
    onehot = (e_flat[:, None] == jnp.arange(N_EXPERTS, dtype=jnp.int32)[None, :]).astype(jnp.int32)
    csum = jnp.cumsum(onehot, axis=0)
    rank = jnp.take_along_axis(csum, e_flat[:, None], axis=1)[:, 0] - 1
    counts = csum[-1]
    padded = ((counts + tm - 1) // tm) * tm
    ends = jnp.cumsum(padded)
    starts = ends - padded
    pos = (starts[e_flat] + rank).astype(jnp.int32)
    n_tiles = (2 * n) // tm + N_EXPERTS
    rows = n_tiles * tm
    src_pair = jnp.zeros((rows,), jnp.int32).at[pos].set(jnp.arange(2 * n, dtype=jnp.int32))
    src_tok = src_pair // 2
    w_sorted = wts.reshape(-1)[src_pair]
    tile_start = jnp.arange(n_tiles, dtype=jnp.int32) * tm
    tile_e = jnp.minimum(jnp.searchsorted(ends, tile_start, side="right"), N_EXPERTS - 1).astype(jnp.int32)
    n_used = (ends[-1] // tm).astype(jnp.int32).reshape(1)
    last_e = tile_e[jnp.maximum(n_used[0] - 1, 0)]
    tile_e = jnp.where(tile_start < ends[-1], tile_e, last_e)
    return tile_e, src_tok, n_used, w_sorted.reshape(rows, 1), pos


def kernel(x, attn_norm_w, w_in, cmp_pe_k, cmp_pe_v, cmp_k_w1, cmp_k_w2, cmp_v_w1, cmp_v_w2,
           rel_bias_table, fox_forget_b, nsa_out_norm_w, fox_out_norm_w, w_out, ffn_norm_w,
           router_group_w, router_group_b, router_expert_w, router_expert_b,
           expert_w_gate, expert_w_up, expert_w_down, final_norm_w):
    b, t, d = x.shape
    n = b * t
    depth = w_in.shape[0]
    assert t % 512 == 0 and t >= WINDOW + AT_TQ and d == 2048 and t // SLC_BLOCK <= MASK_BLOCK_LANES
    rows = t // CMP_STRIDE
    bias_c, nbias, wbias = _bias_tables(rel_bias_table, t, rows)
    ns = t // SLC_BLOCK
    ratio = SLC_BLOCK // CMP_STRIDE
    span = CMP_BLOCK // CMP_STRIDE
    nc = (t - CMP_BLOCK) // CMP_STRIDE + 1
    impm = np.zeros((ns, rows), np.float32)
    for blk in range(ns):
        for a in range(ratio):
            for s in range(span):
                c = blk * ratio + a - s
                if 0 <= c < nc:
                    impm[blk, c] += 1.0
    impm = jnp.asarray(impm, BF16)
    eye_q = jnp.eye(AT_TQ, dtype=BF16)
    eye_h = jnp.eye(N_FOX_HEADS, dtype=BF16)
    tri = jnp.asarray(np.tril(np.ones((t, t), np.float32)), BF16)
    moe_tm = 512

    h = x.reshape(n, d)
    for layer in range(depth):
        w_main, w_misc = _project_weights(w_in[layer])
        main, misc = _proj(h, attn_norm_w[layer][None, :], w_main, w_misc)
        main3 = main.reshape(b, t, MAIN_COLS)
        misc3 = misc.reshape(b, t, LANES)

        xkv = jnp.stack([main3[:, :, COL_KCMP:COL_KCMP + 256], main3[:, :, COL_VCMP:COL_VCMP + 256]])
        xkv = xkv.reshape(2, b, rows, CMP_STRIDE * 256)
        pk = _compress_weights(cmp_pe_k[layer], cmp_k_w1[layer], cmp_k_w2[layer])
        pv = _compress_weights(cmp_pe_v[layer], cmp_v_w1[layer], cmp_v_w2[layer])
        kvc = _compress(xkv, *[jnp.stack([a, c]) for a, c in zip(pk, pv)])

        gates = misc3[:, :, MISC_GATE:MISC_GATE + 48].reshape(b, t, N_NSA_KV, 12).transpose(0, 2, 1, 3)
        o_cmp, sel = _cmpsel(main3, kvc, bias_c, gates, impm, eye_q)
        o_slc = _slc(main3, sel, nbias, gates)
        o_win = _win(main3, wbias, gates)

        c, ct = _foxprep(misc3, fox_forget_b[layer][None, :], tri, eye_h)
        o_fox = _fox(main3, c, ct)

        wr = jnp.concatenate([router_expert_w[layer], router_group_w[layer]], axis=1)
        wr = jnp.pad(wr, ((0, 0), (0, LANES - wr.shape[1])))
        wr_hi = wr.astype(BF16)
        wr_lo = (wr - wr_hi.astype(F32)).astype(BF16)
        br = jnp.concatenate([router_expert_b[layer], router_group_b[layer]])
        br = jnp.pad(br, (0, LANES - br.shape[0]))[None, :]
        half = N_NSA_HEADS * HEAD_DIM
        h, hn, eid, wts = _outproj(
            o_cmp.reshape(n, half), o_slc.reshape(n, half), o_win.reshape(n, half), o_fox.reshape(n, half),
            h, nsa_out_norm_w[layer][None, :], fox_out_norm_w[layer][None, :],
            w_out[layer].astype(BF16), ffn_norm_w[layer][None, :], jnp.stack([wr_hi, wr_lo]), br)

        tile_e, src_tok, n_used, w_sorted, pos = _routing_tables(eid[:, :2], wts[:, :2], moe_tm)
        y = _moe(tile_e, src_tok, n_used, hn, w_sorted,
                 expert_w_gate[layer], expert_w_up[layer], expert_w_down[layer], moe_tm)
        last = layer == depth - 1
        fw = final_norm_w if last else jnp.ones((d,), F32)
        assert last, "the fused final norm assumes a single layer"
        h = _combine(pos, y, h, fw[None, :])
    return h.reshape(b, t, d)
```

```python
import functools
import math

import numpy as np
import jax
import jax.numpy as jnp
from jax import lax
from jax.experimental import pallas as pl
from jax.experimental.pallas import tpu as pltpu

F32 = jnp.float32
BF16 = jnp.bfloat16

HEAD_DIM = 64
N_NSA_HEADS = 16
N_FOX_HEADS = 16
NSA_GQA = 4
N_NSA_KV = 4
CMP_BLOCK = 32
CMP_STRIDE = 16
CMP_HIDDEN = 128
SLC_BLOCK = 64
SLC_TOP_N = 16
WINDOW = 512
REL_BUCKETS = 32
REL_MAX_DIST = 128
N_GROUPS = 4
EXPERTS_PER_GROUP = 8
N_EXPERTS = 32
EXPERT_FF = 512
NORM_EPS = 1e-6
NEG_INF = -1e30
FORCE_BONUS = 1e4
SCALE = HEAD_DIM ** -0.5
LOG2E = math.log2(math.e)
Q_SCALE = SCALE * LOG2E

LANES = 128
VMEM_LIMIT = 56 * 1024 * 1024

COL_NQ = 0
COL_KCMP = 1024
COL_VCMP = 1280
COL_SLC = 1536
COL_WIN = 2048
COL_FQ = 2560
COL_FK = 3584
COL_FV = 4608
MAIN_COLS = 5632
MISC_GATE = 0
MISC_FF = 48

AT_TQ = 128
NSA_GROUPS_PER_STEP = 2
FOX_PAIRS_PER_STEP = 2
SLC_CHUNK = 512
SLC_SHIFT = 6
assert 1 << SLC_SHIFT == SLC_BLOCK
MASK_BLOCK_LANES = 32
PAD_LANE = HEAD_DIM + MASK_BLOCK_LANES
KEY_PAD = 512


def _nt(a, b):
    return lax.dot_general(a, b, (((1,), (1,)), ((), ())), preferred_element_type=F32)


def _dot(a, b):
    return jnp.dot(a, b, preferred_element_type=F32)


def _split3(x):
    hi = x.astype(BF16)
    r = x - hi.astype(F32)
    mid = r.astype(BF16)
    r = r - mid.astype(F32)
    return hi, mid, r.astype(BF16)


def _cparams(grid_rank):
    return pltpu.CompilerParams(dimension_semantics=("arbitrary",) * grid_rank, vmem_limit_bytes=VMEM_LIMIT)


def _bucket_table(n):
    d = np.arange(n, dtype=np.int64)
    max_exact = REL_BUCKETS // 2
    rel = np.log(np.maximum(d, 1).astype(np.float64) / max_exact) / math.log(REL_MAX_DIST / max_exact)
    scaled = rel * (REL_BUCKETS - max_exact)
    frac = scaled - np.floor(scaled)
    inner = (d > max_exact) & (d < REL_MAX_DIST)
    assert np.all((frac[inner] > 1e-3) & (frac[inner] < 1 - 1e-3))
    large = np.minimum(max_exact + np.floor(scaled + 1e-6).astype(np.int64), REL_BUCKETS - 1)
    return np.where(d < max_exact, d, large).astype(np.int32)


def _project_weights(w):
    d = w.shape[0]
    sizes = [1024] + [256] * 6 + [48, 1024, 1024, 1024, 16]
    offs = np.concatenate([[0], np.cumsum(sizes)])
    nq, kcmp, vcmp, kslc, vslc, kwin, vwin, ngate, fq, fk, fv, ff = [
        w[:, int(offs[i]):int(offs[i + 1])] for i in range(12)]

    def interleave(k, v):
        k = k.reshape(d, N_NSA_KV, HEAD_DIM)
        v = v.reshape(d, N_NSA_KV, HEAD_DIM)
        return jnp.stack([k, v], axis=2).reshape(d, N_NSA_KV * 2 * HEAD_DIM)

    main = jnp.concatenate([nq * Q_SCALE, kcmp, vcmp, interleave(kslc, vslc), interleave(kwin, vwin),
                            fq * Q_SCALE, fk, fv], axis=1)
    assert main.shape[1] == MAIN_COLS
    misc = jnp.concatenate([ngate, ff, jnp.zeros((d, LANES - 64), w.dtype)], axis=1)
    return main.astype(BF16), misc.astype(BF16)


def _proj_kernel(x_ref, nw_ref, w_ref, wm_ref, o_ref, om_ref, *, tn):
    x = x_ref[...]
    y = x * lax.rsqrt(jnp.mean(x * x, axis=-1, keepdims=True) + NORM_EPS) * nw_ref[...]
    xn = y.astype(BF16)
    om_ref[...] = _dot(xn, wm_ref[...])
    for c in range(o_ref.shape[1] // tn):
        o_ref[:, c * tn:(c + 1) * tn] = _dot(xn, w_ref[:, c * tn:(c + 1) * tn]).astype(BF16)


def _proj(x2, norm_w, w_main, w_misc, tm=512, tn=512):
    n, d = x2.shape
    once = pl.Buffered(1)
    return pl.pallas_call(
        functools.partial(_proj_kernel, tn=tn),
        grid=(n // tm,),
        in_specs=[pl.BlockSpec((tm, d), lambda i: (i, 0)),
                  pl.BlockSpec((1, d), lambda i: (0, 0)),
                  pl.BlockSpec((d, MAIN_COLS), lambda i: (0, 0), pipeline_mode=once),
                  pl.BlockSpec((d, LANES), lambda i: (0, 0), pipeline_mode=once)],
        out_specs=[pl.BlockSpec((tm, MAIN_COLS), lambda i: (i, 0)),
                   pl.BlockSpec((tm, LANES), lambda i: (i, 0))],
        out_shape=[jax.ShapeDtypeStruct((n, MAIN_COLS), BF16),
                   jax.ShapeDtypeStruct((n, LANES), F32)],
        compiler_params=_cparams(1),
        name="proj",
    )(x2, norm_w, w_main, w_misc)


def _compress_kernel(x_ref, pea_ref, peb_ref, w1a_ref, w1b_ref, w2_ref, o_ref):
    x = x_ref[0, 0].astype(F32)
    xa = (x + pea_ref[0]).astype(BF16)
    xb = (x + peb_ref[0]).astype(BF16)
    a = _dot(xa, w1a_ref[0])
    b = _dot(xb, w1b_ref[0])
    rows = a.shape[0]
    pre = a + pltpu.roll(b, rows - 1, 0)
    hid = pre * jax.nn.sigmoid(pre)
    out = _dot(hid.astype(BF16), w2_ref[0])
    for g in range(N_NSA_KV):
        o_ref[0, 0, g] = out[:, g * HEAD_DIM:(g + 1) * HEAD_DIM]


def _compress(xkv, pea, peb, w1a, w1b, w2):
    _, b, rows, width = xkv.shape
    hid = N_NSA_KV * CMP_HIDDEN
    return pl.pallas_call(
        _compress_kernel,
        grid=(2, b),
        in_specs=[pl.BlockSpec((1, 1, rows, width), lambda s, i: (s, i, 0, 0)),
                  pl.BlockSpec((1, 1, width), lambda s, i: (s, 0, 0)),
                  pl.BlockSpec((1, 1, width), lambda s, i: (s, 0, 0)),
                  pl.BlockSpec((1, width, hid), lambda s, i: (s, 0, 0)),
                  pl.BlockSpec((1, width, hid), lambda s, i: (s, 0, 0)),
                  pl.BlockSpec((1, hid, N_NSA_KV * HEAD_DIM), lambda s, i: (s, 0, 0))],
        out_specs=pl.BlockSpec((1, 1, N_NSA_KV, rows, HEAD_DIM), lambda s, i: (s, i, 0, 0, 0)),
        out_shape=jax.ShapeDtypeStruct((2, b, N_NSA_KV, rows, HEAD_DIM), F32),
        compiler_params=_cparams(2),
        name="compress",
    )(xkv, pea, peb, w1a, w1b, w2)


def _foxprep_kernel(misc_ref, fb_ref, tri_ref, eye_ref, c_ref, ct_ref):
    z = misc_ref[0][:, MISC_FF:MISC_FF + N_FOX_HEADS] + fb_ref[...]
    logf = (jnp.minimum(z, 0.0) - jnp.log(1.0 + jnp.exp(-jnp.abs(z)))) * LOG2E
    tri = tri_ref[...]
    c = None
    for part in _split3(logf):
        term = _dot(tri, part)
        c = term if c is None else c + term
    ct = None
    for part in _split3(c):
        term = _nt(eye_ref[...], part)
        ct = term if ct is None else ct + term
    for p in range(N_FOX_HEADS // 2):
        c_ref[0, p] = c[:, 2 * p:2 * p + 2]
        ct_ref[0, p] = ct[2 * p:2 * p + 2, :]


def _foxprep(misc3, fb, tri, eye):
    b, t, _ = misc3.shape
    hp = N_FOX_HEADS // 2
    return pl.pallas_call(
        _foxprep_kernel,
        grid=(b,),
        in_specs=[pl.BlockSpec((1, t, LANES), lambda i: (i, 0, 0)),
                  pl.BlockSpec((1, N_FOX_HEADS), lambda i: (0, 0)),
                  pl.BlockSpec((t, t), lambda i: (0, 0)),
                  pl.BlockSpec((N_FOX_HEADS, N_FOX_HEADS), lambda i: (0, 0))],
        out_specs=[pl.BlockSpec((1, hp, t, 2), lambda i: (i, 0, 0, 0)),
                   pl.BlockSpec((1, hp, 2, t), lambda i: (i, 0, 0, 0))],
        out_shape=[jax.ShapeDtypeStruct((b, hp, t, 2), F32),
                   jax.ShapeDtypeStruct((b, hp, 2, t), F32)],
        compiler_params=_cparams(1),
        name="foxprep",
    )(misc3, fb, tri, eye)


def _cmpsel_kernel(q_ref, kvc_ref, bias_ref, gate_ref, impm_ref, eye_ref, o_ref, sel_ref):
    tq = q_ref.shape[1]
    ncp = kvc_ref.shape[3]
    ns = impm_ref.shape[0]
    qw = NSA_GQA * HEAD_DIM
    t0 = pl.program_id(2) * tq
    rows = NSA_GQA * tq
    t_col = t0 + (lax.broadcasted_iota(jnp.int32, (rows, ncp), 0) & (tq - 1))
    c_row = lax.broadcasted_iota(jnp.int32, (rows, ncp), 1)
    valid = t_col >= c_row * CMP_STRIDE + (CMP_BLOCK - 1)
    blk = lax.broadcasted_iota(jnp.int32, (ns, tq), 0)
    t_row = t0 + lax.broadcasted_iota(jnp.int32, (ns, tq), 1)
    cur = t_row // SLC_BLOCK
    bonus = jnp.where((blk == 0) | (blk == cur) | (blk == cur - 1), FORCE_BONUS, 0.0)
    blk_valid = blk * SLC_BLOCK <= t_row
    pad_row = jnp.where(lax.broadcasted_iota(jnp.int32, (HEAD_DIM - MASK_BLOCK_LANES, tq), 0) == 0, 1.0, 0.0)
    q_all = q_ref[0]

    for c in range(kvc_ref.shape[2]):
        kc = kvc_ref[0, 0, c].astype(BF16)
        vc = kvc_ref[1, 0, c].astype(BF16)
        q = q_all[:, c * qw:(c + 1) * qw]
        qst = jnp.concatenate([q[:, r * HEAD_DIM:(r + 1) * HEAD_DIM] for r in range(NSA_GQA)], axis=0)
        s = _nt(qst, kc) + bias_ref[c].reshape(rows, ncp)
        s = jnp.where(valid, s, NEG_INF)
        m = jnp.max(s, axis=-1, keepdims=True)
        e = jnp.where(valid, jnp.exp2(s - m), 0.0)
        p = e / jnp.maximum(jnp.sum(e, axis=-1, keepdims=True), 1e-30)
        o = _dot(p.astype(BF16), vc)
        gate = jax.nn.sigmoid(gate_ref[0, c])
        p_grp = jnp.zeros((tq, ncp), F32)
        for r in range(NSA_GQA):
            col = (c * NSA_GQA + r) * HEAD_DIM
            o_ref[0, :, col:col + HEAD_DIM] = o[r * tq:(r + 1) * tq] * gate[:, 3 * r:3 * r + 1]
            p_grp = p_grp + p[r * tq:(r + 1) * tq]

        imp = None
        for part in _split3(p_grp):
            term = _nt(impm_ref[...], part)
            imp = term if imp is None else imp + term
        score = jnp.where(blk_valid, imp + bonus, NEG_INF)
        rank = jnp.zeros((ns, tq), F32)
        for m_blk in range(ns):
            other = score[m_blk:m_blk + 1, :]
            ahead = (other > score) | ((other == score) & (blk > m_blk))
            rank = rank + jnp.where(ahead, 1.0, 0.0)
        unsel = jnp.where(rank < float(min(SLC_TOP_N, ns)), 0.0, 1.0)
        parts = [unsel, pad_row]
        if ns < MASK_BLOCK_LANES:
            parts.insert(1, jnp.zeros((MASK_BLOCK_LANES - ns, tq), F32))
        flags = _nt(eye_ref[...], jnp.concatenate(parts, axis=0).astype(BF16))
        sel_ref[0, c] = (flags * NEG_INF).astype(BF16)


def _cmpsel(main3, kvc, bias_c, gates, impm, eye):
    b, t, _ = main3.shape
    ncp = kvc.shape[3]
    ns = impm.shape[0]
    tq = AT_TQ
    qw = NSA_GQA * HEAD_DIM
    gs = NSA_GROUPS_PER_STEP
    return pl.pallas_call(
        _cmpsel_kernel,
        grid=(b, N_NSA_KV // gs, t // tq),
        in_specs=[pl.BlockSpec((1, tq, gs * qw), lambda i, g, j: (i, j, g)),
                  pl.BlockSpec((2, 1, gs, ncp, HEAD_DIM), lambda i, g, j: (0, i, g, 0, 0)),
                  pl.BlockSpec((gs, NSA_GQA, tq, ncp), lambda i, g, j: (g, 0, j, 0)),
                  pl.BlockSpec((1, gs, tq, 3 * NSA_GQA), lambda i, g, j: (i, g, j, 0)),
                  pl.BlockSpec((ns, ncp), lambda i, g, j: (0, 0)),
                  pl.BlockSpec((tq, tq), lambda i, g, j: (0, 0))],
        out_specs=[pl.BlockSpec((1, tq, gs * qw), lambda i, g, j: (i, j, g)),
                   pl.BlockSpec((1, gs, tq, HEAD_DIM), lambda i, g, j: (i, g, j, 0))],
        out_shape=[jax.ShapeDtypeStruct((b, t, N_NSA_HEADS * HEAD_DIM), F32),
                   jax.ShapeDtypeStruct((b, N_NSA_KV, t, HEAD_DIM), BF16)],
        compiler_params=_cparams(3),
        name="cmpsel",
    )(main3, kvc, bias_c, gates, impm, eye)


def _softmax_first(s, v):
    m = jnp.max(s, axis=-1, keepdims=True)
    p = jnp.exp2(s - m)
    return m, jnp.sum(p, axis=-1, keepdims=True), _dot(p.astype(BF16), v)


def _softmax_next(s, v, m, l, acc):
    m_new = jnp.maximum(m, jnp.max(s, axis=-1, keepdims=True))
    alpha = jnp.exp2(m - m_new)
    p = jnp.exp2(s - m_new)
    return m_new, alpha * l + jnp.sum(p, axis=-1, keepdims=True), alpha * acc + _dot(p.astype(BF16), v)


def _stack_heads(q, tail, n_heads):
    return jnp.concatenate(
        [jnp.concatenate([q[:, r * HEAD_DIM:(r + 1) * HEAD_DIM], tail], axis=1) for r in range(n_heads)], axis=0)


def _fill_key_scratch(kp_sc, kvp_sc, kv, with_blocks):
    t = kv.shape[0]
    lane = lax.broadcasted_iota(jnp.int32, (t, LANES), 1)
    if with_blocks:
        blk = lax.shift_right_logical(lax.broadcasted_iota(jnp.int32, (t, LANES), 0), SLC_SHIFT)
        aug = jnp.where(lane - HEAD_DIM == blk, 1.0, 0.0).astype(BF16)
    else:
        aug = jnp.zeros((t, LANES), BF16)
    kp_sc[KEY_PAD:, :] = jnp.where(lane < HEAD_DIM, kv, aug)
    lane_p = lax.broadcasted_iota(jnp.int32, (KEY_PAD, LANES), 1)
    kp_sc[0:KEY_PAD, :] = jnp.where(lane_p == PAD_LANE, 1.0, 0.0).astype(BF16)
    kvp_sc[KEY_PAD:, :] = jnp.where(lane < HEAD_DIM, jnp.where(lane == 0, 1.0, 0.0).astype(BF16), kv)
    kvp_sc[0:KEY_PAD, :] = jnp.zeros((KEY_PAD, LANES), BF16)


def _nsa_first(s, v):
    m = jnp.max(s, axis=-1, keepdims=True)
    return m, _dot(jnp.exp2(s - m).astype(BF16), v)


def _nsa_next(s, v, m, acc):
    m_new = jnp.maximum(m, jnp.max(s, axis=-1, keepdims=True))
    return m_new, jnp.exp2(m - m_new) * acc + _dot(jnp.exp2(s - m_new).astype(BF16), v)


def _store_gated(o_ref, group, acc, gate, branch):
    tq = o_ref.shape[1]
    o = acc[:, HEAD_DIM:] / acc[:, 0:1]
    for r in range(NSA_GQA):
        col = (group * NSA_GQA + r) * HEAD_DIM
        gcol = 3 * r + branch
        o_ref[0, :, col:col + HEAD_DIM] = o[r * tq:(r + 1) * tq] * gate[:, gcol:gcol + 1]


def _slc_kernel(q_ref, kv_ref, sel_ref, nbias_ref, gate_ref, o_ref, kp_sc, kvp_sc):
    tq = q_ref.shape[1]
    chunk = SLC_CHUNK
    qt = pl.program_id(2)
    t0 = qt * tq

    groups = range(kp_sc.shape[0])
    qw = NSA_GQA * HEAD_DIM

    @pl.when(qt == 0)
    def _():
        for c in groups:
            _fill_key_scratch(kp_sc.at[c], kvp_sc.at[c], kv_ref[0, :, c * LANES:(c + 1) * LANES], True)

    q = q_ref[0]
    qst = [_stack_heads(q[:, c * qw:(c + 1) * qw], sel_ref[0, c], NSA_GQA) for c in groups]

    near = pl.multiple_of(t0 + tq - chunk + KEY_PAD, tq)
    state = tuple(_nsa_first(_nt(qst[c], kp_sc[c, pl.ds(near, chunk), :]) + nbias_ref[c],
                             kvp_sc[c, pl.ds(near, chunk), :]) for c in groups)

    def far(j, carry):
        first = pl.multiple_of(near - (j + 1) * chunk, tq)
        return tuple(_nsa_next(_nt(qst[c], kp_sc[c, pl.ds(first, chunk), :]),
                               kvp_sc[c, pl.ds(first, chunk), :], *carry[c]) for c in groups)

    state = lax.fori_loop(0, (t0 + tq - 1) // chunk, far, state)
    for c in groups:
        _store_gated(o_ref, c, state[c][1], jax.nn.sigmoid(gate_ref[0, c]), 1)


def _slc(main3, sel, nbias, gates):
    b, t, _ = main3.shape
    tq = AT_TQ
    qw = NSA_GQA * HEAD_DIM
    assert KEY_PAD >= SLC_CHUNK and KEY_PAD % tq == 0
    gs = NSA_GROUPS_PER_STEP
    kvw = gs * LANES
    return pl.pallas_call(
        _slc_kernel,
        grid=(b, N_NSA_KV // gs, t // tq),
        in_specs=[pl.BlockSpec((1, tq, gs * qw), lambda i, g, j: (i, j, g)),
                  pl.BlockSpec((1, t, kvw), lambda i, g, j: (i, 0, COL_SLC // kvw + g)),
                  pl.BlockSpec((1, gs, tq, HEAD_DIM), lambda i, g, j: (i, g, j, 0)),
                  pl.BlockSpec((gs, NSA_GQA * tq, SLC_CHUNK), lambda i, g, j: (g, 0, 0)),
                  pl.BlockSpec((1, gs, tq, 3 * NSA_GQA), lambda i, g, j: (i, g, j, 0))],
        out_specs=pl.BlockSpec((1, tq, gs * qw), lambda i, g, j: (i, j, g)),
        out_shape=jax.ShapeDtypeStruct((b, t, N_NSA_HEADS * HEAD_DIM), F32),
        scratch_shapes=[pltpu.VMEM((gs, t + KEY_PAD, LANES), BF16), pltpu.VMEM((gs, t + KEY_PAD, LANES), BF16)],
        compiler_params=_cparams(3),
        name="slc",
    )(main3, main3, sel, nbias, gates)


def _win_kernel(q_ref, kv_ref, bias_ref, gate_ref, o_ref, kp_sc, kvp_sc):
    tq = q_ref.shape[1]
    span = bias_ref.shape[2]
    qt = pl.program_id(2)

    groups = range(kp_sc.shape[0])
    qw = NSA_GQA * HEAD_DIM

    @pl.when(qt == 0)
    def _():
        for c in groups:
            _fill_key_scratch(kp_sc.at[c], kvp_sc.at[c], kv_ref[0, :, c * LANES:(c + 1) * LANES], False)

    tail = jnp.where(lax.broadcasted_iota(jnp.int32, (tq, HEAD_DIM), 1) == PAD_LANE - HEAD_DIM, NEG_INF, 0.0)
    tail = tail.astype(BF16)
    q = q_ref[0]
    first = pl.multiple_of(qt * tq, tq)
    for c in groups:
        qst = _stack_heads(q[:, c * qw:(c + 1) * qw], tail, NSA_GQA)
        s = _nt(qst, kp_sc[c, pl.ds(first, span), :]) + bias_ref[c]
        _, acc = _nsa_first(s, kvp_sc[c, pl.ds(first, span), :])
        _store_gated(o_ref, c, acc, jax.nn.sigmoid(gate_ref[0, c]), 2)


def _win(main3, wbias, gates):
    b, t, _ = main3.shape
    tq = AT_TQ
    qw = NSA_GQA * HEAD_DIM
    span = wbias.shape[2]
    assert span - tq == KEY_PAD
    gs = NSA_GROUPS_PER_STEP
    kvw = gs * LANES
    return pl.pallas_call(
        _win_kernel,
        grid=(b, N_NSA_KV // gs, t // tq),
        in_specs=[pl.BlockSpec((1, tq, gs * qw), lambda i, g, j: (i, j, g)),
                  pl.BlockSpec((1, t, kvw), lambda i, g, j: (i, 0, COL_WIN // kvw + g)),
                  pl.BlockSpec((gs, NSA_GQA * tq, span), lambda i, g, j: (g, 0, 0)),
                  pl.BlockSpec((1, gs, tq, 3 * NSA_GQA), lambda i, g, j: (i, g, j, 0))],
        out_specs=pl.BlockSpec((1, tq, gs * qw), lambda i, g, j: (i, j, g)),
        out_shape=jax.ShapeDtypeStruct((b, t, N_NSA_HEADS * HEAD_DIM), F32),
        scratch_shapes=[pltpu.VMEM((gs, t + KEY_PAD, LANES), BF16), pltpu.VMEM((gs, t + KEY_PAD, LANES), BF16)],
        compiler_params=_cparams(3),
        name="win",
    )(main3, main3, wbias, gates)


def _fox_kernel(q_ref, k_ref, v_ref, c_ref, ct_ref, o_ref, *, chunk):
    tq = q_ref.shape[1]
    t0 = pl.program_id(2) * tq
    pairs = range(c_ref.shape[1])
    lane = lax.broadcasted_iota(jnp.int32, (tq, LANES), 1)
    low = lane < HEAD_DIM
    q_all = q_ref[0]
    zero = jnp.zeros((tq, LANES), BF16)
    qst, c_col = [], []
    for p in pairs:
        q = q_all[:, p * LANES:(p + 1) * LANES]
        qst.append(jnp.concatenate([jnp.where(low, q, zero), jnp.where(low, zero, q)], axis=0))
        c_col.append(c_ref[0, p])

    def logits(p, start):
        s = _nt(qst[p], k_ref[0, pl.ds(start, chunk), p * LANES:(p + 1) * LANES])
        c_row = ct_ref[0, p, :, pl.ds(start, chunk)]
        return jnp.concatenate([s[:tq] + c_col[p][:, 0:1] - c_row[0:1],
                                s[tq:] + c_col[p][:, 1:2] - c_row[1:2]], axis=0)

    def values(p, start):
        return v_ref[0, pl.ds(start, chunk), p * LANES:(p + 1) * LANES]

    d0 = pl.multiple_of((t0 // chunk) * chunk, chunk)
    row = lax.broadcasted_iota(jnp.int32, (2 * tq, chunk), 0) & (tq - 1)
    causal = row - lax.broadcasted_iota(jnp.int32, (2 * tq, chunk), 1) + (t0 - d0) >= 0
    state = tuple(_softmax_first(jnp.where(causal, logits(p, d0), NEG_INF), values(p, d0)) for p in pairs)

    def below(j, carry):
        start = pl.multiple_of(j * chunk, chunk)
        return tuple(_softmax_next(logits(p, start), values(p, start), *carry[p]) for p in pairs)

    state = lax.fori_loop(0, t0 // chunk, below, state)
    for p in pairs:
        _, l, acc = state[p]
        o = acc / l
        o_ref[0, :, p * LANES:(p + 1) * LANES] = jnp.where(low, o[:tq], o[tq:])


def _fox(main3, c, ct, tq=256, chunk=512):
    b, t, _ = main3.shape
    ps = FOX_PAIRS_PER_STEP
    hp = N_FOX_HEADS // 2
    w = ps * LANES
    assert chunk % tq == 0 and t % chunk == 0
    return pl.pallas_call(
        functools.partial(_fox_kernel, chunk=chunk),
        grid=(b, hp // ps, t // tq),
        in_specs=[pl.BlockSpec((1, tq, w), lambda i, p, j: (i, j, COL_FQ // w + p)),
                  pl.BlockSpec((1, t, w), lambda i, p, j: (i, 0, COL_FK // w + p)),
                  pl.BlockSpec((1, t, w), lambda i, p, j: (i, 0, COL_FV // w + p)),
                  pl.BlockSpec((1, ps, tq, 2), lambda i, p, j: (i, p, j, 0)),
                  pl.BlockSpec((1, ps, 2, t), lambda i, p, j: (i, p, 0, 0))],
        out_specs=pl.BlockSpec((1, tq, w), lambda i, p, j: (i, j, p)),
        out_shape=jax.ShapeDtypeStruct((b, t, N_FOX_HEADS * HEAD_DIM), F32),
        compiler_params=_cparams(3),
        name="fox",
    )(main3, main3, main3, c, ct)


def _rms(x, w):
    return x * lax.rsqrt(jnp.mean(x * x, axis=-1, keepdims=True) + NORM_EPS) * w


def _to_token_rows(ref, x):
    tm, d = x.shape
    parts = d // LANES
    for a in range(parts):
        ref[pl.ds(a, tm, stride=parts), :] = x[:, a * LANES:(a + 1) * LANES]


def _from_token_rows(ref, base, tm, d, pitch):
    return jnp.concatenate([ref[pl.ds(base + a, tm, stride=pitch), :] for a in range(d // LANES)], axis=1)


GATHER_PITCH = 20


def _outproj_kernel(oc_ref, os_ref, ow_ref, of_ref, x_ref, nnw_ref, fnw_ref, wo_ref, ffw_ref,
                    wr_ref, br_ref, h_ref, hn_ref, eid_ref, wt_ref):
    o_nsa = oc_ref[...] + os_ref[...] + ow_ref[...]
    mixed = jnp.concatenate([_rms(o_nsa, nnw_ref[...]), _rms(of_ref[...], fnw_ref[...])], axis=-1)
    h = x_ref[...] + _dot(mixed.astype(BF16), wo_ref[...])
    h_ref[...] = h
    hn = _rms(h, ffw_ref[...])
    _to_token_rows(hn_ref, hn)

    h_hi, h_mid, _ = _split3(hn)
    w_hi = wr_ref[0]
    w_lo = wr_ref[1]
    logits = (_dot(h_hi, w_hi) + _dot(h_mid, w_hi) + _dot(h_hi, w_lo)) + br_ref[...]
    tm = logits.shape[0]
    lane = lax.broadcasted_iota(jnp.int32, (tm, LANES), 1)
    big = jnp.int32(LANES)
    is_grp = (lane >= N_EXPERTS) & (lane < N_EXPERTS + N_GROUPS)
    glog = jnp.where(is_grp, logits, NEG_INF)
    gmax = jnp.max(glog, axis=-1, keepdims=True)
    gsel = jnp.min(jnp.where(glog == gmax, lane, big), axis=-1, keepdims=True) - N_EXPERTS
    p_gsel = 1.0 / jnp.sum(jnp.where(is_grp, jnp.exp(glog - gmax), 0.0), axis=-1, keepdims=True)
    in_grp = (lane < N_EXPERTS) & (lane // EXPERTS_PER_GROUP == gsel)
    e1 = jnp.where(in_grp, logits, NEG_INF)
    v1 = jnp.max(e1, axis=-1, keepdims=True)
    i1 = jnp.min(jnp.where(e1 == v1, lane, big), axis=-1, keepdims=True)
    e2 = jnp.where(lane == i1, NEG_INF, e1)
    v2 = jnp.max(e2, axis=-1, keepdims=True)
    i2 = jnp.min(jnp.where(e2 == v2, lane, big), axis=-1, keepdims=True)
    ex = jnp.exp(v2 - v1)
    w1 = p_gsel / (1.0 + ex)
    w2 = p_gsel * ex / (1.0 + ex)
    eid_ref[...] = jnp.where(lane == 0, i1, jnp.where(lane == 1, i2, 0))
    wt_ref[...] = jnp.where(lane == 0, w1, jnp.where(lane == 1, w2, 0.0))


def _outproj(oc, osl, ow, of, x2, nnw, fnw, wo, ffw, wr, br, tm=256):
    n, d = x2.shape
    half = oc.shape[1]
    row = lambda i: (i, 0)
    fixed = lambda i: (0, 0)
    return pl.pallas_call(
        _outproj_kernel,
        grid=(n // tm,),
        in_specs=[pl.BlockSpec((tm, half), row), pl.BlockSpec((tm, half), row),
                  pl.BlockSpec((tm, half), row), pl.BlockSpec((tm, half), row),
                  pl.BlockSpec((tm, d), row),
                  pl.BlockSpec((1, half), fixed), pl.BlockSpec((1, half), fixed),
                  pl.BlockSpec((d, d), fixed), pl.BlockSpec((1, d), fixed),
                  pl.BlockSpec((2, d, LANES), lambda i: (0, 0, 0)), pl.BlockSpec((1, LANES), fixed)],
        out_specs=[pl.BlockSpec((tm, d), row), pl.BlockSpec((tm * (d // LANES), LANES), row),
                   pl.BlockSpec((tm, LANES), row), pl.BlockSpec((tm, LANES), row)],
        out_shape=[jax.ShapeDtypeStruct((n, d), F32), jax.ShapeDtypeStruct((n * (d // LANES), LANES), F32),
                   jax.ShapeDtypeStruct((n, LANES), jnp.int32), jax.ShapeDtypeStruct((n, LANES), F32)],
        compiler_params=_cparams(1),
        name="outproj",
    )(oc, osl, ow, of, x2, nnw, fnw, wo, ffw, wr, br)


def _moe_kernel(te_ref, src_ref, nt_ref, hn_hbm, ws_ref, wg_ref, wu_ref, wd_ref, y_ref,
                xbuf, sem, wg_sc, wu_sc, wd_sc):
    tm = ws_ref.shape[0]
    d = wg_sc.shape[0]
    parts = d // LANES
    i = pl.program_id(0)
    n_used = nt_ref[0]
    slot = i % 2

    def gather(tile, slot_):
        def row(r, carry):
            src = pl.multiple_of(src_ref[tile * tm + r] * parts, parts)
            dst = (slot_ * tm + r) * GATHER_PITCH
            pltpu.make_async_copy(hn_hbm.at[pl.ds(src, parts)], xbuf.at[pl.ds(dst, parts)], sem.at[slot_]).start()
            return carry
        lax.fori_loop(0, tm, row, 0, unroll=8)

    @pl.when(i == 0)
    def _():
        gather(0, 0)

    @pl.when(i + 1 < n_used)
    def _():
        gather(i + 1, 1 - slot)

    prev = te_ref[jnp.maximum(i - 1, 0)]

    @pl.when((i == 0) | (te_ref[i] != prev))
    def _():
        wg_sc[...] = wg_ref[0].astype(BF16)
        wu_sc[...] = wu_ref[0].astype(BF16)
        wd_sc[...] = wd_ref[0].astype(BF16)

    @pl.when(i < n_used)
    def _():
        base = slot * (tm * GATHER_PITCH)
        pltpu.make_async_copy(hn_hbm.at[pl.ds(0, tm * parts)], xbuf.at[pl.ds(base, tm * parts)], sem.at[slot]).wait()
        x = _from_token_rows(xbuf, base, tm, d, GATHER_PITCH).astype(BF16)
        gate = _dot(x, wg_sc[...])
        up = _dot(x, wu_sc[...])
        hid = gate * jax.nn.sigmoid(gate) * up
        _to_token_rows(y_ref, ws_ref[...] * _dot(hid.astype(BF16), wd_sc[...]))

    @pl.when(i >= n_used)
    def _():
        y_ref[...] = jnp.zeros(y_ref.shape, F32)


def _moe(tile_e, src_tok, n_used, hn, w_sorted, wg, wu, wd, tm):
    n_tiles = tile_e.shape[0]
    d = wg.shape[1]
    ff = wg.shape[2]
    parts = d // LANES
    grid_spec = pltpu.PrefetchScalarGridSpec(
        num_scalar_prefetch=3,
        grid=(n_tiles,),
        in_specs=[pl.BlockSpec(memory_space=pl.ANY),
                  pl.BlockSpec((tm, 1), lambda i, te, src, nt: (i, 0)),
                  pl.BlockSpec((1, d, ff), lambda i, te, src, nt: (te[i], 0, 0)),
                  pl.BlockSpec((1, d, ff), lambda i, te, src, nt: (te[i], 0, 0)),
                  pl.BlockSpec((1, ff, d), lambda i, te, src, nt: (te[i], 0, 0))],
        out_specs=pl.BlockSpec((tm * parts, LANES), lambda i, te, src, nt: (i, 0)),
        scratch_shapes=[pltpu.VMEM((2 * tm * GATHER_PITCH, LANES), F32),
                        pltpu.SemaphoreType.DMA((2,)),
                        pltpu.VMEM((d, ff), BF16), pltpu.VMEM((d, ff), BF16), pltpu.VMEM((ff, d), BF16)],
    )
    return pl.pallas_call(
        _moe_kernel,
        grid_spec=grid_spec,
        out_shape=jax.ShapeDtypeStruct((n_tiles * tm * parts, LANES), F32),
        compiler_params=_cparams(1),
        name="moe",
    )(tile_e, src_tok, n_used, hn, w_sorted, wg, wu, wd)


def _combine_kernel(pos_ref, y_hbm, h_ref, fw_ref, o_ref, ybuf, sem):
    tm, d = h_ref.shape
    parts = d // LANES
    i = pl.program_id(0)
    n = pl.num_programs(0)
    slot = i % 2

    def gather(tile, slot_):
        def row(r, carry):
            for k in range(2):
                src = pl.multiple_of(pos_ref[2 * (tile * tm + r) + k] * parts, parts)
                dst = ((slot_ * 2 + k) * tm + r) * GATHER_PITCH
                pltpu.make_async_copy(y_hbm.at[pl.ds(src, parts)], ybuf.at[pl.ds(dst, parts)], sem.at[slot_]).start()
            return carry
        lax.fori_loop(0, tm, row, 0, unroll=8)

    @pl.when(i == 0)
    def _():
        gather(0, 0)

    @pl.when(i + 1 < n)
    def _():
        gather(i + 1, 1 - slot)

    rows = tm * GATHER_PITCH
    base = slot * (2 * rows)
    pltpu.make_async_copy(y_hbm.at[pl.ds(0, 2 * tm * parts)], ybuf.at[pl.ds(base, 2 * tm * parts)], sem.at[slot]).wait()
    out = h_ref[...] + (_from_token_rows(ybuf, base, tm, d, GATHER_PITCH)
                        + _from_token_rows(ybuf, base + rows, tm, d, GATHER_PITCH))
    o_ref[...] = _rms(out, fw_ref[...])


def _combine(pos, y, h, fw, tm=256):
    n, d = h.shape
    grid_spec = pltpu.PrefetchScalarGridSpec(
        num_scalar_prefetch=1,
        grid=(n // tm,),
        in_specs=[pl.BlockSpec(memory_space=pl.ANY),
                  pl.BlockSpec((tm, d), lambda i, pos_: (i, 0)),
                  pl.BlockSpec((1, d), lambda i, pos_: (0, 0))],
        out_specs=pl.BlockSpec((tm, d), lambda i, pos_: (i, 0)),
        scratch_shapes=[pltpu.VMEM((2 * 2 * tm * GATHER_PITCH, LANES), F32), pltpu.SemaphoreType.DMA((2,))],
    )
    return pl.pallas_call(
        _combine_kernel,
        grid_spec=grid_spec,
        out_shape=jax.ShapeDtypeStruct((n, d), F32),
        compiler_params=_cparams(1),
        name="combine",
    )(pos, y, h, fw)


def _biasgen_kernel(tab_ref, bm_ref, bn_ref, bw_ref, om_ref, on_ref, ow_ref):
    h = pl.program_id(0)
    far = tab_ref[REL_BUCKETS - 1, h]

    def build(b_ref, shift):
        idx = b_ref[...]
        out = jnp.full(idx.shape, NEG_INF, F32)
        for bucket in range(REL_BUCKETS):
            out = jnp.where(idx == bucket, (tab_ref[bucket, h] - shift) * LOG2E, out)
        return out

    om_ref[0] = build(bm_ref, 0.0)
    on_ref[0] = build(bn_ref, far)
    ow_ref[0] = build(bw_ref, 0.0)


def _biasgen(rel_table, bm, bn, bw):
    heads = rel_table.shape[1]
    full = lambda a: pl.BlockSpec(a.shape, lambda h: (0, 0))
    out = lambda a: pl.BlockSpec((1,) + a.shape, lambda h: (h, 0, 0))
    return pl.pallas_call(
        _biasgen_kernel,
        grid=(heads,),
        in_specs=[pl.BlockSpec(memory_space=pltpu.SMEM), full(bm), full(bn), full(bw)],
        out_specs=[out(bm), out(bn), out(bw)],
        out_shape=[jax.ShapeDtypeStruct((heads,) + a.shape, F32) for a in (bm, bn, bw)],
        compiler_params=_cparams(1),
        name="biasgen",
    )(rel_table, bm, bn, bw)


def _bias_tables(rel_table, t, ncp):
    tq = AT_TQ
    far = REL_MAX_DIST
    buckets = _bucket_table(far + 1)
    i = np.arange(tq)[:, None]

    def bucket_map(dist, ok):
        return jnp.asarray(np.where(ok, buckets[np.clip(dist, 0, far)], -1).astype(np.int32))

    step = tq // CMP_STRIDE
    u = np.arange(2 * ncp)[None, :]
    dist_m = i - CMP_STRIDE * (u - ncp) - (CMP_BLOCK - 1)
    j = np.arange(SLC_CHUNK)[None, :]
    dist_n = (SLC_CHUNK - tq) + i - j
    j = np.arange(WINDOW + tq)[None, :]
    dist_w = WINDOW + i - j
    master, nbias, wbias = _biasgen(rel_table,
                                    bucket_map(dist_m, np.ones_like(dist_m, bool)),
                                    bucket_map(dist_n, dist_n >= 0),
                                    bucket_map(dist_w, (dist_w >= 0) & (dist_w < WINDOW)))
    master = master.reshape(N_NSA_KV, NSA_GQA, tq, 2 * ncp)
    bias_c = jnp.stack([master[..., ncp - step * qt:2 * ncp - step * qt] for qt in range(t // tq)], axis=2)
    bias_c = bias_c.reshape(N_NSA_KV, NSA_GQA, t, ncp)
    nbias = nbias.reshape(N_NSA_KV, NSA_GQA * tq, SLC_CHUNK)
    wbias = wbias.reshape(N_NSA_KV, NSA_GQA * tq, WINDOW + tq)
    return bias_c, nbias, wbias


def _compress_weights(pe, w1, w2):
    half = CMP_STRIDE
    eye = jnp.eye(N_NSA_KV, dtype=F32)

    def expand_w1(w):
        w = w.reshape(half, HEAD_DIM, CMP_HIDDEN)
        return jnp.einsum("idn,gh->igdhn", w, eye).reshape(half * N_NSA_KV * HEAD_DIM, N_NSA_KV * CMP_HIDDEN)

    def expand_pe(p):
        return jnp.broadcast_to(p[:, None, :], (half, N_NSA_KV, HEAD_DIM)).reshape(1, -1)

    w1a = expand_w1(w1[:half * HEAD_DIM]).astype(BF16)
    w1b = expand_w1(w1[half * HEAD_DIM:]).astype(BF16)
    w2x = jnp.einsum("nd,gh->gnhd", w2, eye).reshape(N_NSA_KV * CMP_HIDDEN, N_NSA_KV * HEAD_DIM).astype(BF16)
    return expand_pe(pe[:half]), expand_pe(pe[half:]), w1a, w1b, w2x


def _routing_tables(eid, wts, tm):
    n = eid.shape[0]
    e_flat = eid.reshape(-1)
    onehot = (e_flat[:, None] == jnp.arange(N_EXPERTS, dtype=jnp.int32)[None, :]).astype(jnp.int32)
    csum = jnp.cumsum(onehot, axis=0)
    rank = jnp.take_along_axis(csum, e_flat[:, None], axis=1)[:, 0] - 1
    counts = csum[-1]
    padded = ((counts + tm - 1) // tm) * tm
    ends = jnp.cumsum(padded)
    starts = ends - padded
    pos = (starts[e_flat] + rank).astype(jnp.int32)
    n_tiles = (2 * n) // tm + N_EXPERTS
    rows = n_tiles * tm
    src_pair = jnp.zeros((rows,), jnp.int32).at[pos].set(jnp.arange(2 * n, dtype=jnp.int32))
    src_tok = src_pair // 2
    w_sorted = wts.reshape(-1)[src_pair]
    tile_start = jnp.arange(n_tiles, dtype=jnp.int32) * tm
    tile_e = jnp.minimum(jnp.searchsorted(ends, tile_start, side="right"), N_EXPERTS - 1).astype(jnp.int32)
    n_used = (ends[-1] // tm).astype(jnp.int32).reshape(1)
    last_e = tile_e[jnp.maximum(n_used[0] - 1, 0)]
    tile_e = jnp.where(tile_start < ends[-1], tile_e, last_e)
    return tile_e, src_tok, n_used, w_sorted.reshape(rows, 1), pos


def kernel(x, attn_norm_w, w_in, cmp_pe_k, cmp_pe_v, cmp_k_w1, cmp_k_w2, cmp_v_w1, cmp_v_w2,
           rel_bias_table, fox_forget_b, nsa_out_norm_w, fox_out_norm_w, w_out, ffn_norm_w,
           router_group_w, router_group_b, router_expert_w, router_expert_b,
           expert_w_gate, expert_w_up, expert_w_down, final_norm_w):
    b, t, d = x.shape
    n = b * t
    depth = w_in.shape[0]
    assert t % 512 == 0 and t >= WINDOW + AT_TQ and d == 2048 and t // SLC_BLOCK <= MASK_BLOCK_LANES
    rows = t // CMP_STRIDE
    bias_c, nbias, wbias = _bias_tables(rel_bias_table, t, rows)
    ns = t // SLC_BLOCK
    ratio = SLC_BLOCK // CMP_STRIDE
    span = CMP_BLOCK // CMP_STRIDE
    nc = (t - CMP_BLOCK) // CMP_STRIDE + 1
    impm = np.zeros((ns, rows), np.float32)
    for blk in range(ns):
        for a in range(ratio):
            for s in range(span):
                c = blk * ratio + a - s
                if 0 <= c < nc:
                    impm[blk, c] += 1.0
    impm = jnp.asarray(impm, BF16)
    eye_q = jnp.eye(AT_TQ, dtype=BF16)
    eye_h = jnp.eye(N_FOX_HEADS, dtype=BF16)
    tri = jnp.asarray(np.tril(np.ones((t, t), np.float32)), BF16)
    moe_tm = 512

    h = x.reshape(n, d)
    for layer in range(depth):
        w_main, w_misc = _project_weights(w_in[layer])
        main, misc = _proj(h, attn_norm_w[layer][None, :], w_main, w_misc)
        main3 = main.reshape(b, t, MAIN_COLS)
        misc3 = misc.reshape(b, t, LANES)

        xkv = jnp.stack([main3[:, :, COL_KCMP:COL_KCMP + 256], main3[:, :, COL_VCMP:COL_VCMP + 256]])
        xkv = xkv.reshape(2, b, rows, CMP_STRIDE * 256)
        pk = _compress_weights(cmp_pe_k[layer], cmp_k_w1[layer], cmp_k_w2[layer])
        pv = _compress_weights(cmp_pe_v[layer], cmp_v_w1[layer], cmp_v_w2[layer])
        kvc = _compress(xkv, *[jnp.stack([a, c]) for a, c in zip(pk, pv)])

        gates = misc3[:, :, MISC_GATE:MISC_GATE + 48].reshape(b, t, N_NSA_KV, 12).transpose(0, 2, 1, 3)
        o_cmp, sel = _cmpsel(main3, kvc, bias_c, gates, impm, eye_q)
        o_slc = _slc(main3, sel, nbias, gates)
        o_win = _win(main3, wbias, gates)

        c, ct = _foxprep(misc3, fox_forget_b[layer][None, :], tri, eye_h)
        o_fox = _fox(main3, c, ct)

        wr = jnp.concatenate([router_expert_w[layer], router_group_w[layer]], axis=1)
        wr = jnp.pad(wr, ((0, 0), (0, LANES - wr.shape[1])))
        wr_hi = wr.astype(BF16)
        wr_lo = (wr - wr_hi.astype(F32)).astype(BF16)
        br = jnp.concatenate([router_expert_b[layer], router_group_b[layer]])
        br = jnp.pad(br, (0, LANES - br.shape[0]))[None, :]
        half = N_NSA_HEADS * HEAD_DIM
        h, hn, eid, wts = _outproj(
            o_cmp.reshape(n, half), o_slc.reshape(n, half), o_win.reshape(n, half), o_fox.reshape(n, half),
            h, nsa_out_norm_w[layer][None, :], fox_out_norm_w[layer][None, :],
            w_out[layer].astype(BF16), ffn_norm_w[layer][None, :], jnp.stack([wr_hi, wr_lo]), br)

        tile_e, src_tok, n_used, w_sorted, pos = _routing_tables(eid[:, :2], wts[:, :2], moe_tm)
        y = _moe(tile_e, src_tok, n_used, hn, w_sorted,
                 expert_w_gate[layer], expert_w_up[layer], expert_w_down[layer], moe_tm)
        last = layer == depth - 1
        fw = final_norm_w if last else jnp.ones((d,), F32)
        assert last, "the fused final norm assumes a single layer"
        h = _combine(pos, y, h, fw[None, :])
    return h.reshape(b, t, d)
```

```python
import functools
import math

import numpy as np
import jax
import jax.numpy as jnp
from jax import lax
from jax.experimental import pallas as pl
from jax.experimental.pallas import tpu as pltpu

F32 = jnp.float32
BF16 = jnp.bfloat16

HEAD_DIM = 64
N_NSA_HEADS = 16
N_FOX_HEADS = 16
NSA_GQA = 4
N_NSA_KV = 4
CMP_BLOCK = 32
CMP_STRIDE = 16
CMP_HIDDEN = 128
SLC_BLOCK = 64
SLC_TOP_N = 16
WINDOW = 512
REL_BUCKETS = 32
REL_MAX_DIST = 128
N_GROUPS = 4
EXPERTS_PER_GROUP = 8
N_EXPERTS = 32
EXPERT_FF = 512
NORM_EPS = 1e-6
NEG_INF = -1e30
FORCE_BONUS = 1e4
SCALE = HEAD_DIM ** -0.5
LOG2E = math.log2(math.e)
Q_SCALE = SCALE * LOG2E

LANES = 128
VMEM_LIMIT = 56 * 1024 * 1024

COL_NQ = 0
COL_KCMP = 1024
COL_VCMP = 1280
COL_SLC = 1536
COL_WIN = 2048
COL_FQ = 2560
COL_FK = 3584
COL_FV = 4608
MAIN_COLS = 5632
MISC_GATE = 0
MISC_FF = 48

AT_TQ = 128
NSA_GROUPS_PER_STEP = 2
FOX_PAIRS_PER_STEP = 2
SLC_CHUNK = 512
SLC_SHIFT = 6
assert 1 << SLC_SHIFT == SLC_BLOCK
MASK_BLOCK_LANES = 32
PAD_LANE = HEAD_DIM + MASK_BLOCK_LANES
KEY_PAD = 512


def _nt(a, b):
    return lax.dot_general(a, b, (((1,), (1,)), ((), ())), preferred_element_type=F32)


def _dot(a, b):
    return jnp.dot(a, b, preferred_element_type=F32)


def _split3(x):
    hi = x.astype(BF16)
    r = x - hi.astype(F32)
    mid = r.astype(BF16)
    r = r - mid.astype(F32)
    return hi, mid, r.astype(BF16)


def _cparams(grid_rank):
    return pltpu.CompilerParams(dimension_semantics=("arbitrary",) * grid_rank, vmem_limit_bytes=VMEM_LIMIT)


def _bucket_table(n):
    d = np.arange(n, dtype=np.int64)
    max_exact = REL_BUCKETS // 2
    rel = np.log(np.maximum(d, 1).astype(np.float64) / max_exact) / math.log(REL_MAX_DIST / max_exact)
    scaled = rel * (REL_BUCKETS - max_exact)
    frac = scaled - np.floor(scaled)
    inner = (d > max_exact) & (d < REL_MAX_DIST)
    assert np.all((frac[inner] > 1e-3) & (frac[inner] < 1 - 1e-3))
    large = np.minimum(max_exact + np.floor(scaled + 1e-6).astype(np.int64), REL_BUCKETS - 1)
    return np.where(d < max_exact, d, large).astype(np.int32)


def _project_weights(w):
    d = w.shape[0]
    sizes = [1024] + [256] * 6 + [48, 1024, 1024, 1024, 16]
    offs = np.concatenate([[0], np.cumsum(sizes)])
    nq, kcmp, vcmp, kslc, vslc, kwin, vwin, ngate, fq, fk, fv, ff = [
        w[:, int(offs[i]):int(offs[i + 1])] for i in range(12)]

    def interleave(k, v):
        k = k.reshape(d, N_NSA_KV, HEAD_DIM)
        v = v.reshape(d, N_NSA_KV, HEAD_DIM)
        return jnp.stack([k, v], axis=2).reshape(d, N_NSA_KV * 2 * HEAD_DIM)

    main = jnp.concatenate([nq * Q_SCALE, kcmp, vcmp, interleave(kslc, vslc), interleave(kwin, vwin),
                            fq * Q_SCALE, fk, fv], axis=1)
    assert main.shape[1] == MAIN_COLS
    misc = jnp.concatenate([ngate, ff, jnp.zeros((d, LANES - 64), w.dtype)], axis=1)
    return main.astype(BF16), misc.astype(BF16)


def _proj_kernel(x_ref, nw_ref, w_ref, wm_ref, o_ref, om_ref, *, tn):
    x = x_ref[...]
    y = x * lax.rsqrt(jnp.mean(x * x, axis=-1, keepdims=True) + NORM_EPS) * nw_ref[...]
    xn = y.astype(BF16)
    om_ref[...] = _dot(xn, wm_ref[...])
    for c in range(o_ref.shape[1] // tn):
        o_ref[:, c * tn:(c + 1) * tn] = _dot(xn, w_ref[:, c * tn:(c + 1) * tn]).astype(BF16)


def _proj(x2, norm_w, w_main, w_misc, tm=512, tn=512):
    n, d = x2.shape
    once = pl.Buffered(1)
    return pl.pallas_call(
        functools.partial(_proj_kernel, tn=tn),
        grid=(n // tm,),
        in_specs=[pl.BlockSpec((tm, d), lambda i: (i, 0)),
                  pl.BlockSpec((1, d), lambda i: (0, 0)),
                  pl.BlockSpec((d, MAIN_COLS), lambda i: (0, 0), pipeline_mode=once),
                  pl.BlockSpec((d, LANES), lambda i: (0, 0), pipeline_mode=once)],
        out_specs=[pl.BlockSpec((tm, MAIN_COLS), lambda i: (i, 0)),
                   pl.BlockSpec((tm, LANES), lambda i: (i, 0))],
        out_shape=[jax.ShapeDtypeStruct((n, MAIN_COLS), BF16),
                   jax.ShapeDtypeStruct((n, LANES), F32)],
        compiler_params=_cparams(1),
        name="proj",
    )(x2, norm_w, w_main, w_misc)


def _compress_kernel(x_ref, pea_ref, peb_ref, w1a_ref, w1b_ref, w2_ref, o_ref):
    x = x_ref[0, 0].astype(F32)
    xa = (x + pea_ref[0]).astype(BF16)
    xb = (x + peb_ref[0]).astype(BF16)
    a = _dot(xa, w1a_ref[0])
    b = _dot(xb, w1b_ref[0])
    rows = a.shape[0]
    pre = a + pltpu.roll(b, rows - 1, 0)
    hid = pre * jax.nn.sigmoid(pre)
    out = _dot(hid.astype(BF16), w2_ref[0])
    for g in range(N_NSA_KV):
        o_ref[0, 0, g] = out[:, g * HEAD_DIM:(g + 1) * HEAD_DIM]


def _compress(xkv, pea, peb, w1a, w1b, w2):
    _, b, rows, width = xkv.shape
    hid = N_NSA_KV * CMP_HIDDEN
    return pl.pallas_call(
        _compress_kernel,
        grid=(2, b),
        in_specs=[pl.BlockSpec((1, 1, rows, width), lambda s, i: (s, i, 0, 0)),
                  pl.BlockSpec((1, 1, width), lambda s, i: (s, 0, 0)),
                  pl.BlockSpec((1, 1, width), lambda s, i: (s, 0, 0)),
                  pl.BlockSpec((1, width, hid), lambda s, i: (s, 0, 0)),
                  pl.BlockSpec((1, width, hid), lambda s, i: (s, 0, 0)),
                  pl.BlockSpec((1, hid, N_NSA_KV * HEAD_DIM), lambda s, i: (s, 0, 0))],
        out_specs=pl.BlockSpec((1, 1, N_NSA_KV, rows, HEAD_DIM), lambda s, i: (s, i, 0, 0, 0)),
        out_shape=jax.ShapeDtypeStruct((2, b, N_NSA_KV, rows, HEAD_DIM), F32),
        compiler_params=_cparams(2),
        name="compress",
    )(xkv, pea, peb, w1a, w1b, w2)


def _foxprep_kernel(misc_ref, fb_ref, tri_ref, eye_ref, c_ref, ct_ref):
    z = misc_ref[0][:, MISC_FF:MISC_FF + N_FOX_HEADS] + fb_ref[...]
    logf = (jnp.minimum(z, 0.0) - jnp.log(1.0 + jnp.exp(-jnp.abs(z)))) * LOG2E
    tri = tri_ref[...]
    c = None
    for part in _split3(logf):
        term = _dot(tri, part)
        c = term if c is None else c + term
    ct = None
    for part in _split3(c):
        term = _nt(eye_ref[...], part)
        ct = term if ct is None else ct + term
    for p in range(N_FOX_HEADS // 2):
        c_ref[0, p] = c[:, 2 * p:2 * p + 2]
        ct_ref[0, p] = ct[2 * p:2 * p + 2, :]


def _foxprep(misc3, fb, tri, eye):
    b, t, _ = misc3.shape
    hp = N_FOX_HEADS // 2
    return pl.pallas_call(
        _foxprep_kernel,
        grid=(b,),
        in_specs=[pl.BlockSpec((1, t, LANES), lambda i: (i, 0, 0)),
                  pl.BlockSpec((1, N_FOX_HEADS), lambda i: (0, 0)),
                  pl.BlockSpec((t, t), lambda i: (0, 0)),
                  pl.BlockSpec((N_FOX_HEADS, N_FOX_HEADS), lambda i: (0, 0))],
        out_specs=[pl.BlockSpec((1, hp, t, 2), lambda i: (i, 0, 0, 0)),
                   pl.BlockSpec((1, hp, 2, t), lambda i: (i, 0, 0, 0))],
        out_shape=[jax.ShapeDtypeStruct((b, hp, t, 2), F32),
                   jax.ShapeDtypeStruct((b, hp, 2, t), F32)],
        compiler_params=_cparams(1),
        name="foxprep",
    )(misc3, fb, tri, eye)


def _cmpsel_kernel(q_ref, kvc_ref, bias_ref, gate_ref, impm_ref, eye_ref, o_ref, sel_ref):
    tq = q_ref.shape[1]
    ncp = kvc_ref.shape[3]
    ns = impm_ref.shape[0]
    qw = NSA_GQA * HEAD_DIM
    t0 = pl.program_id(2) * tq
    rows = NSA_GQA * tq
    t_col = t0 + (lax.broadcasted_iota(jnp.int32, (rows, ncp), 0) & (tq - 1))
    c_row = lax.broadcasted_iota(jnp.int32, (rows, ncp), 1)
    valid = t_col >= c_row * CMP_STRIDE + (CMP_BLOCK - 1)
    blk = lax.broadcasted_iota(jnp.int32, (ns, tq), 0)
    t_row = t0 + lax.broadcasted_iota(jnp.int32, (ns, tq), 1)
    cur = t_row // SLC_BLOCK
    bonus = jnp.where((blk == 0) | (blk == cur) | (blk == cur - 1), FORCE_BONUS, 0.0)
    blk_valid = blk * SLC_BLOCK <= t_row
    pad_row = jnp.where(lax.broadcasted_iota(jnp.int32, (HEAD_DIM - MASK_BLOCK_LANES, tq), 0) == 0, 1.0, 0.0)
    q_all = q_ref[0]

    for c in range(kvc_ref.shape[2]):
        kc = kvc_ref[0, 0, c].astype(BF16)
        vc = kvc_ref[1, 0, c].astype(BF16)
        q = q_all[:, c * qw:(c + 1) * qw]
        qst = jnp.concatenate([q[:, r * HEAD_DIM:(r + 1) * HEAD_DIM] for r in range(NSA_GQA)], axis=0)
        s = _nt(qst, kc) + bias_ref[c].reshape(rows, ncp)
        s = jnp.where(valid, s, NEG_INF)
        m = jnp.max(s, axis=-1, keepdims=True)
        e = jnp.where(valid, jnp.exp2(s - m), 0.0)
        p = e / jnp.maximum(jnp.sum(e, axis=-1, keepdims=True), 1e-30)
        o = _dot(p.astype(BF16), vc)
        gate = jax.nn.sigmoid(gate_ref[0, c])
        p_grp = jnp.zeros((tq, ncp), F32)
        for r in range(NSA_GQA):
            col = (c * NSA_GQA + r) * HEAD_DIM
            o_ref[0, :, col:col + HEAD_DIM] = (o[r * tq:(r + 1) * tq] * gate[:, 3 * r:3 * r + 1]).astype(o_ref.dtype)
            p_grp = p_grp + p[r * tq:(r + 1) * tq]

        imp = None
        for part in _split3(p_grp):
            term = _nt(impm_ref[...], part)
            imp = term if imp is None else imp + term
        score = jnp.where(blk_valid, imp + bonus, NEG_INF)
        rank = jnp.zeros((ns, tq), F32)
        for m_blk in range(ns):
            other = score[m_blk:m_blk + 1, :]
            ahead = (other > score) | ((other == score) & (blk > m_blk))
            rank = rank + jnp.where(ahead, 1.0, 0.0)
        unsel = jnp.where(rank < float(min(SLC_TOP_N, ns)), 0.0, 1.0)
        parts = [unsel, pad_row]
        if ns < MASK_BLOCK_LANES:
            parts.insert(1, jnp.zeros((MASK_BLOCK_LANES - ns, tq), F32))
        flags = _nt(eye_ref[...], jnp.concatenate(parts, axis=0).astype(BF16))
        sel_ref[0, c] = (flags * NEG_INF).astype(BF16)


def _cmpsel(main3, kvc, bias_c, gates, impm, eye):
    b, t, _ = main3.shape
    ncp = kvc.shape[3]
    ns = impm.shape[0]
    tq = AT_TQ
    qw = NSA_GQA * HEAD_DIM
    gs = NSA_GROUPS_PER_STEP
    return pl.pallas_call(
        _cmpsel_kernel,
        grid=(b, N_NSA_KV // gs, t // tq),
        in_specs=[pl.BlockSpec((1, tq, gs * qw), lambda i, g, j: (i, j, g)),
                  pl.BlockSpec((2, 1, gs, ncp, HEAD_DIM), lambda i, g, j: (0, i, g, 0, 0)),
                  pl.BlockSpec((gs, NSA_GQA, tq, ncp), lambda i, g, j: (g, 0, j, 0)),
                  pl.BlockSpec((1, gs, tq, 3 * NSA_GQA), lambda i, g, j: (i, g, j, 0)),
                  pl.BlockSpec((ns, ncp), lambda i, g, j: (0, 0)),
                  pl.BlockSpec((tq, tq), lambda i, g, j: (0, 0))],
        out_specs=[pl.BlockSpec((1, tq, gs * qw), lambda i, g, j: (i, j, g)),
                   pl.BlockSpec((1, gs, tq, HEAD_DIM), lambda i, g, j: (i, g, j, 0))],
        out_shape=[jax.ShapeDtypeStruct((b, t, N_NSA_HEADS * HEAD_DIM), BF16),
                   jax.ShapeDtypeStruct((b, N_NSA_KV, t, HEAD_DIM), BF16)],
        compiler_params=_cparams(3),
        name="cmpsel",
    )(main3, kvc, bias_c, gates, impm, eye)


def _softmax_first(s, v):
    m = jnp.max(s, axis=-1, keepdims=True)
    p = jnp.exp2(s - m)
    return m, jnp.sum(p, axis=-1, keepdims=True), _dot(p.astype(BF16), v)


def _softmax_next(s, v, m, l, acc):
    m_new = jnp.maximum(m, jnp.max(s, axis=-1, keepdims=True))
    alpha = jnp.exp2(m - m_new)
    p = jnp.exp2(s - m_new)
    return m_new, alpha * l + jnp.sum(p, axis=-1, keepdims=True), alpha * acc + _dot(p.astype(BF16), v)


def _stack_heads(q, tail, n_heads):
    return jnp.concatenate(
        [jnp.concatenate([q[:, r * HEAD_DIM:(r + 1) * HEAD_DIM], tail], axis=1) for r in range(n_heads)], axis=0)


def _fill_key_scratch(kp_sc, kvp_sc, kv, with_blocks):
    t = kv.shape[0]
    lane = lax.broadcasted_iota(jnp.int32, (t, LANES), 1)
    if with_blocks:
        blk = lax.shift_right_logical(lax.broadcasted_iota(jnp.int32, (t, LANES), 0), SLC_SHIFT)
        aug = jnp.where(lane - HEAD_DIM == blk, 1.0, 0.0).astype(BF16)
    else:
        aug = jnp.zeros((t, LANES), BF16)
    kp_sc[KEY_PAD:, :] = jnp.where(lane < HEAD_DIM, kv, aug)
    lane_p = lax.broadcasted_iota(jnp.int32, (KEY_PAD, LANES), 1)
    kp_sc[0:KEY_PAD, :] = jnp.where(lane_p == PAD_LANE, 1.0, 0.0).astype(BF16)
    kvp_sc[KEY_PAD:, :] = jnp.where(lane < HEAD_DIM, jnp.where(lane == 0, 1.0, 0.0).astype(BF16), kv)
    kvp_sc[0:KEY_PAD, :] = jnp.zeros((KEY_PAD, LANES), BF16)


def _nsa_first(s, v):
    m = jnp.max(s, axis=-1, keepdims=True)
    return m, _dot(jnp.exp2(s - m).astype(BF16), v)


def _nsa_next(s, v, m, acc):
    m_new = jnp.maximum(m, jnp.max(s, axis=-1, keepdims=True))
    return m_new, jnp.exp2(m - m_new) * acc + _dot(jnp.exp2(s - m_new).astype(BF16), v)


def _store_gated(o_ref, group, acc, gate, branch):
    tq = o_ref.shape[1]
    o = acc[:, HEAD_DIM:] / acc[:, 0:1]
    for r in range(NSA_GQA):
        col = (group * NSA_GQA + r) * HEAD_DIM
        gcol = 3 * r + branch
        o_ref[0, :, col:col + HEAD_DIM] = (o[r * tq:(r + 1) * tq] * gate[:, gcol:gcol + 1]).astype(o_ref.dtype)


def _slc_kernel(q_ref, kv_ref, sel_ref, nbias_ref, gate_ref, o_ref, kp_sc, kvp_sc):
    tq = q_ref.shape[1]
    chunk = SLC_CHUNK
    qt = pl.program_id(2)
    t0 = qt * tq

    groups = range(kp_sc.shape[0])
    qw = NSA_GQA * HEAD_DIM

    @pl.when(qt == 0)
    def _():
        for c in groups:
            _fill_key_scratch(kp_sc.at[c], kvp_sc.at[c], kv_ref[0, :, c * LANES:(c + 1) * LANES], True)

    q = q_ref[0]
    qst = [_stack_heads(q[:, c * qw:(c + 1) * qw], sel_ref[0, c], NSA_GQA) for c in groups]

    near = pl.multiple_of(t0 + tq - chunk + KEY_PAD, tq)
    state = tuple(_nsa_first(_nt(qst[c], kp_sc[c, pl.ds(near, chunk), :]) + nbias_ref[c],
                             kvp_sc[c, pl.ds(near, chunk), :]) for c in groups)

    def far(j, carry):
        first = pl.multiple_of(near - (j + 1) * chunk, tq)
        return tuple(_nsa_next(_nt(qst[c], kp_sc[c, pl.ds(first, chunk), :]),
                               kvp_sc[c, pl.ds(first, chunk), :], *carry[c]) for c in groups)

    state = lax.fori_loop(0, (t0 + tq - 1) // chunk, far, state)
    for c in groups:
        _store_gated(o_ref, c, state[c][1], jax.nn.sigmoid(gate_ref[0, c]), 1)


def _slc(main3, sel, nbias, gates):
    b, t, _ = main3.shape
    tq = AT_TQ
    qw = NSA_GQA * HEAD_DIM
    assert KEY_PAD >= SLC_CHUNK and KEY_PAD % tq == 0
    gs = NSA_GROUPS_PER_STEP
    kvw = gs * LANES
    return pl.pallas_call(
        _slc_kernel,
        grid=(b, N_NSA_KV // gs, t // tq),
        in_specs=[pl.BlockSpec((1, tq, gs * qw), lambda i, g, j: (i, j, g)),
                  pl.BlockSpec((1, t, kvw), lambda i, g, j: (i, 0, COL_SLC // kvw + g)),
                  pl.BlockSpec((1, gs, tq, HEAD_DIM), lambda i, g, j: (i, g, j, 0)),
                  pl.BlockSpec((gs, NSA_GQA * tq, SLC_CHUNK), lambda i, g, j: (g, 0, 0)),
                  pl.BlockSpec((1, gs, tq, 3 * NSA_GQA), lambda i, g, j: (i, g, j, 0))],
        out_specs=pl.BlockSpec((1, tq, gs * qw), lambda i, g, j: (i, j, g)),
        out_shape=jax.ShapeDtypeStruct((b, t, N_NSA_HEADS * HEAD_DIM), BF16),
        scratch_shapes=[pltpu.VMEM((gs, t + KEY_PAD, LANES), BF16), pltpu.VMEM((gs, t + KEY_PAD, LANES), BF16)],
        compiler_params=_cparams(3),
        name="slc",
    )(main3, main3, sel, nbias, gates)


def _win_kernel(q_ref, kv_ref, bias_ref, gate_ref, o_ref, kp_sc, kvp_sc):
    tq = q_ref.shape[1]
    span = bias_ref.shape[2]
    qt = pl.program_id(2)

    groups = range(kp_sc.shape[0])
    qw = NSA_GQA * HEAD_DIM

    @pl.when(qt == 0)
    def _():
        for c in groups:
            _fill_key_scratch(kp_sc.at[c], kvp_sc.at[c], kv_ref[0, :, c * LANES:(c + 1) * LANES], False)

    tail = jnp.where(lax.broadcasted_iota(jnp.int32, (tq, HEAD_DIM), 1) == PAD_LANE - HEAD_DIM, NEG_INF, 0.0)
    tail = tail.astype(BF16)
    q = q_ref[0]
    first = pl.multiple_of(qt * tq, tq)
    for c in groups:
        qst = _stack_heads(q[:, c * qw:(c + 1) * qw], tail, NSA_GQA)
        s = _nt(qst, kp_sc[c, pl.ds(first, span), :]) + bias_ref[c]
        _, acc = _nsa_first(s, kvp_sc[c, pl.ds(first, span), :])
        _store_gated(o_ref, c, acc, jax.nn.sigmoid(gate_ref[0, c]), 2)


def _win(main3, wbias, gates):
    b, t, _ = main3.shape
    tq = AT_TQ
    qw = NSA_GQA * HEAD_DIM
    span = wbias.shape[2]
    assert span - tq == KEY_PAD
    gs = NSA_GROUPS_PER_STEP
    kvw = gs * LANES
    return pl.pallas_call(
        _win_kernel,
        grid=(b, N_NSA_KV // gs, t // tq),
        in_specs=[pl.BlockSpec((1, tq, gs * qw), lambda i, g, j: (i, j, g)),
                  pl.BlockSpec((1, t, kvw), lambda i, g, j: (i, 0, COL_WIN // kvw + g)),
                  pl.BlockSpec((gs, NSA_GQA * tq, span), lambda i, g, j: (g, 0, 0)),
                  pl.BlockSpec((1, gs, tq, 3 * NSA_GQA), lambda i, g, j: (i, g, j, 0))],
        out_specs=pl.BlockSpec((1, tq, gs * qw), lambda i, g, j: (i, j, g)),
        out_shape=jax.ShapeDtypeStruct((b, t, N_NSA_HEADS * HEAD_DIM), BF16),
        scratch_shapes=[pltpu.VMEM((gs, t + KEY_PAD, LANES), BF16), pltpu.VMEM((gs, t + KEY_PAD, LANES), BF16)],
        compiler_params=_cparams(3),
        name="win",
    )(main3, main3, wbias, gates)


def _fox_kernel(q_ref, k_ref, v_ref, c_ref, ct_ref, o_ref, *, chunk):
    tq = q_ref.shape[1]
    t0 = pl.program_id(2) * tq
    pairs = range(c_ref.shape[1])
    lane = lax.broadcasted_iota(jnp.int32, (tq, LANES), 1)
    low = lane < HEAD_DIM
    q_all = q_ref[0]
    zero = jnp.zeros((tq, LANES), BF16)
    qst, c_col = [], []
    for p in pairs:
        q = q_all[:, p * LANES:(p + 1) * LANES]
        qst.append(jnp.concatenate([jnp.where(low, q, zero), jnp.where(low, zero, q)], axis=0))
        c_col.append(c_ref[0, p])

    def logits(p, start):
        s = _nt(qst[p], k_ref[0, pl.ds(start, chunk), p * LANES:(p + 1) * LANES])
        c_row = ct_ref[0, p, :, pl.ds(start, chunk)]
        return jnp.concatenate([s[:tq] + c_col[p][:, 0:1] - c_row[0:1],
                                s[tq:] + c_col[p][:, 1:2] - c_row[1:2]], axis=0)

    def values(p, start):
        return v_ref[0, pl.ds(start, chunk), p * LANES:(p + 1) * LANES]

    d0 = pl.multiple_of((t0 // chunk) * chunk, chunk)
    row = lax.broadcasted_iota(jnp.int32, (2 * tq, chunk), 0) & (tq - 1)
    causal = row - lax.broadcasted_iota(jnp.int32, (2 * tq, chunk), 1) + (t0 - d0) >= 0
    state = tuple(_softmax_first(jnp.where(causal, logits(p, d0), NEG_INF), values(p, d0)) for p in pairs)

    def below(j, carry):
        start = pl.multiple_of(j * chunk, chunk)
        return tuple(_softmax_next(logits(p, start), values(p, start), *carry[p]) for p in pairs)

    state = lax.fori_loop(0, t0 // chunk, below, state)
    for p in pairs:
        _, l, acc = state[p]
        o = acc / l
        o_ref[0, :, p * LANES:(p + 1) * LANES] = jnp.where(low, o[:tq], o[tq:]).astype(o_ref.dtype)


def _fox(main3, c, ct, tq=256, chunk=512):
    b, t, _ = main3.shape
    ps = FOX_PAIRS_PER_STEP
    hp = N_FOX_HEADS // 2
    w = ps * LANES
    assert chunk % tq == 0 and t % chunk == 0
    return pl.pallas_call(
        functools.partial(_fox_kernel, chunk=chunk),
        grid=(b, hp // ps, t // tq),
        in_specs=[pl.BlockSpec((1, tq, w), lambda i, p, j: (i, j, COL_FQ // w + p)),
                  pl.BlockSpec((1, t, w), lambda i, p, j: (i, 0, COL_FK // w + p)),
                  pl.BlockSpec((1, t, w), lambda i, p, j: (i, 0, COL_FV // w + p)),
                  pl.BlockSpec((1, ps, tq, 2), lambda i, p, j: (i, p, j, 0)),
                  pl.BlockSpec((1, ps, 2, t), lambda i, p, j: (i, p, 0, 0))],
        out_specs=pl.BlockSpec((1, tq, w), lambda i, p, j: (i, j, p)),
        out_shape=jax.ShapeDtypeStruct((b, t, N_FOX_HEADS * HEAD_DIM), BF16),
        compiler_params=_cparams(3),
        name="fox",
    )(main3, main3, main3, c, ct)


def _rms(x, w):
    return x * lax.rsqrt(jnp.mean(x * x, axis=-1, keepdims=True) + NORM_EPS) * w


def _to_token_rows(ref, x):
    tm, d = x.shape
    parts = d // LANES
    for a in range(parts):
        ref[pl.ds(a, tm, stride=parts), :] = x[:, a * LANES:(a + 1) * LANES]


def _from_token_rows(ref, base, tm, d, pitch):
    return jnp.concatenate([ref[pl.ds(base + a, tm, stride=pitch), :] for a in range(d // LANES)], axis=1)


GATHER_PITCH = 20


def _outproj_kernel(oc_ref, os_ref, ow_ref, of_ref, x_ref, nnw_ref, fnw_ref, wo_ref, ffw_ref,
                    wr_ref, br_ref, h_ref, hn_ref, eid_ref, wt_ref):
    o_nsa = oc_ref[...].astype(F32) + os_ref[...].astype(F32) + ow_ref[...].astype(F32)
    mixed = jnp.concatenate([_rms(o_nsa, nnw_ref[...]), _rms(of_ref[...].astype(F32), fnw_ref[...])], axis=-1)
    h = x_ref[...] + _dot(mixed.astype(BF16), wo_ref[...])
    h_ref[...] = h
    hn = _rms(h, ffw_ref[...])
    _to_token_rows(hn_ref, hn)

    h_hi, h_mid, _ = _split3(hn)
    w_hi = wr_ref[0]
    w_lo = wr_ref[1]
    logits = (_dot(h_hi, w_hi) + _dot(h_mid, w_hi) + _dot(h_hi, w_lo)) + br_ref[...]
    tm = logits.shape[0]
    lane = lax.broadcasted_iota(jnp.int32, (tm, LANES), 1)
    big = jnp.int32(LANES)
    is_grp = (lane >= N_EXPERTS) & (lane < N_EXPERTS + N_GROUPS)
    glog = jnp.where(is_grp, logits, NEG_INF)
    gmax = jnp.max(glog, axis=-1, keepdims=True)
    gsel = jnp.min(jnp.where(glog == gmax, lane, big), axis=-1, keepdims=True) - N_EXPERTS
    p_gsel = 1.0 / jnp.sum(jnp.where(is_grp, jnp.exp(glog - gmax), 0.0), axis=-1, keepdims=True)
    in_grp = (lane < N_EXPERTS) & (lane // EXPERTS_PER_GROUP == gsel)
    e1 = jnp.where(in_grp, logits, NEG_INF)
    v1 = jnp.max(e1, axis=-1, keepdims=True)
    i1 = jnp.min(jnp.where(e1 == v1, lane, big), axis=-1, keepdims=True)
    e2 = jnp.where(lane == i1, NEG_INF, e1)
    v2 = jnp.max(e2, axis=-1, keepdims=True)
    i2 = jnp.min(jnp.where(e2 == v2, lane, big), axis=-1, keepdims=True)
    ex = jnp.exp(v2 - v1)
    w1 = p_gsel / (1.0 + ex)
    w2 = p_gsel * ex / (1.0 + ex)
    eid_ref[...] = jnp.where(lane == 0, i1, jnp.where(lane == 1, i2, 0))
    wt_ref[...] = jnp.where(lane == 0, w1, jnp.where(lane == 1, w2, 0.0))


def _outproj(oc, osl, ow, of, x2, nnw, fnw, wo, ffw, wr, br, tm=256):
    n, d = x2.shape
    half = oc.shape[1]
    row = lambda i: (i, 0)
    fixed = lambda i: (0, 0)
    return pl.pallas_call(
        _outproj_kernel,
        grid=(n // tm,),
        in_specs=[pl.BlockSpec((tm, half), row), pl.BlockSpec((tm, half), row),
                  pl.BlockSpec((tm, half), row), pl.BlockSpec((tm, half), row),
                  pl.BlockSpec((tm, d), row),
                  pl.BlockSpec((1, half), fixed), pl.BlockSpec((1, half), fixed),
                  pl.BlockSpec((d, d), fixed), pl.BlockSpec((1, d), fixed),
                  pl.BlockSpec((2, d, LANES), lambda i: (0, 0, 0)), pl.BlockSpec((1, LANES), fixed)],
        out_specs=[pl.BlockSpec((tm, d), row), pl.BlockSpec((tm * (d // LANES), LANES), row),
                   pl.BlockSpec((tm, LANES), row), pl.BlockSpec((tm, LANES), row)],
        out_shape=[jax.ShapeDtypeStruct((n, d), F32), jax.ShapeDtypeStruct((n * (d // LANES), LANES), F32),
                   jax.ShapeDtypeStruct((n, LANES), jnp.int32), jax.ShapeDtypeStruct((n, LANES), F32)],
        compiler_params=_cparams(1),
        name="outproj",
    )(oc, osl, ow, of, x2, nnw, fnw, wo, ffw, wr, br)


def _moe_kernel(te_ref, nt_ref, src0_ref, src1_ref, hn_hbm, ws_ref, wg_ref, wu_ref, wd_ref, y_ref,
                xbuf, sem, wg_sc, wu_sc, wd_sc):
    tm = ws_ref.shape[0]
    d = wg_sc.shape[0]
    parts = d // LANES
    i = pl.program_id(0)
    n_used = nt_ref[0]
    slot = i % 2

    def gather(src_ref, slot_):
        def row(r, carry):
            src = pl.multiple_of(src_ref[0, 0, r] * parts, parts)
            dst = (slot_ * tm + r) * GATHER_PITCH
            pltpu.make_async_copy(hn_hbm.at[pl.ds(src, parts)], xbuf.at[pl.ds(dst, parts)], sem.at[slot_]).start()
            return carry
        lax.fori_loop(0, tm, row, 0, unroll=8)

    @pl.when(i == 0)
    def _():
        gather(src0_ref, 0)

    @pl.when(i + 1 < n_used)
    def _():
        gather(src1_ref, 1 - slot)

    prev = te_ref[jnp.maximum(i - 1, 0)]

    @pl.when((i == 0) | (te_ref[i] != prev))
    def _():
        wg_sc[...] = wg_ref[0].astype(BF16)
        wu_sc[...] = wu_ref[0].astype(BF16)
        wd_sc[...] = wd_ref[0].astype(BF16)

    @pl.when(i < n_used)
    def _():
        base = slot * (tm * GATHER_PITCH)
        pltpu.make_async_copy(hn_hbm.at[pl.ds(0, tm * parts)], xbuf.at[pl.ds(base, tm * parts)], sem.at[slot]).wait()
        x = _from_token_rows(xbuf, base, tm, d, GATHER_PITCH).astype(BF16)
        gate = _dot(x, wg_sc[...])
        up = _dot(x, wu_sc[...])
        hid = gate * jax.nn.sigmoid(gate) * up
        _to_token_rows(y_ref, ws_ref[...] * _dot(hid.astype(BF16), wd_sc[...]))

    @pl.when(i >= n_used)
    def _():
        y_ref[...] = jnp.zeros(y_ref.shape, F32)


def _moe(tile_e, src_tok, n_used, hn, w_sorted, wg, wu, wd, tm):
    n_tiles = tile_e.shape[0]
    d = wg.shape[1]
    ff = wg.shape[2]
    parts = d // LANES
    src3 = src_tok.reshape(n_tiles, 1, tm)
    grid_spec = pltpu.PrefetchScalarGridSpec(
        num_scalar_prefetch=2,
        grid=(n_tiles,),
        in_specs=[pl.BlockSpec((1, 1, tm), lambda i, te, nt: (0, 0, 0), memory_space=pltpu.SMEM),
                  pl.BlockSpec((1, 1, tm), lambda i, te, nt: (jnp.minimum(i + 1, n_tiles - 1), 0, 0),
                               memory_space=pltpu.SMEM),
                  pl.BlockSpec(memory_space=pl.ANY),
                  pl.BlockSpec((tm, 1), lambda i, te, nt: (i, 0)),
                  pl.BlockSpec((1, d, ff), lambda i, te, nt: (te[i], 0, 0)),
                  pl.BlockSpec((1, d, ff), lambda i, te, nt: (te[i], 0, 0)),
                  pl.BlockSpec((1, ff, d), lambda i, te, nt: (te[i], 0, 0))],
        out_specs=pl.BlockSpec((tm * parts, LANES), lambda i, te, nt: (i, 0)),
        scratch_shapes=[pltpu.VMEM((2 * tm * GATHER_PITCH, LANES), F32),
                        pltpu.SemaphoreType.DMA((2,)),
                        pltpu.VMEM((d, ff), BF16), pltpu.VMEM((d, ff), BF16), pltpu.VMEM((ff, d), BF16)],
    )
    return pl.pallas_call(
        _moe_kernel,
        grid_spec=grid_spec,
        out_shape=jax.ShapeDtypeStruct((n_tiles * tm * parts, LANES), F32),
        compiler_params=_cparams(1),
        name="moe",
    )(tile_e, n_used, src3, src3, hn, w_sorted, wg, wu, wd)


def _combine_kernel(pos0_ref, pos1_ref, y_hbm, h_ref, fw_ref, o_ref, ybuf, sem):
    tm, d = h_ref.shape
    parts = d // LANES
    i = pl.program_id(0)
    n = pl.num_programs(0)
    slot = i % 2

    def gather(pos_ref, slot_):
        def row(r, carry):
            for k in range(2):
                src = pl.multiple_of(pos_ref[0, 0, 2 * r + k] * parts, parts)
                dst = ((slot_ * 2 + k) * tm + r) * GATHER_PITCH
                pltpu.make_async_copy(y_hbm.at[pl.ds(src, parts)], ybuf.at[pl.ds(dst, parts)], sem.at[slot_]).start()
            return carry
        lax.fori_loop(0, tm, row, 0, unroll=8)

    @pl.when(i == 0)
    def _():
        gather(pos0_ref, 0)

    @pl.when(i + 1 < n)
    def _():
        gather(pos1_ref, 1 - slot)

    rows = tm * GATHER_PITCH
    base = slot * (2 * rows)
    pltpu.make_async_copy(y_hbm.at[pl.ds(0, 2 * tm * parts)], ybuf.at[pl.ds(base, 2 * tm * parts)], sem.at[slot]).wait()
    out = h_ref[...] + (_from_token_rows(ybuf, base, tm, d, GATHER_PITCH)
                        + _from_token_rows(ybuf, base + rows, tm, d, GATHER_PITCH))
    o_ref[...] = _rms(out, fw_ref[...])


def _combine(pos, y, h, fw, tm=256):
    n, d = h.shape
    steps = n // tm
    pos3 = pos.reshape(steps, 1, 2 * tm)
    return pl.pallas_call(
        _combine_kernel,
        grid=(steps,),
        in_specs=[pl.BlockSpec((1, 1, 2 * tm), lambda i: (0, 0, 0), memory_space=pltpu.SMEM),
                  pl.BlockSpec((1, 1, 2 * tm), lambda i: (jnp.minimum(i + 1, steps - 1), 0, 0),
                               memory_space=pltpu.SMEM),
                  pl.BlockSpec(memory_space=pl.ANY),
                  pl.BlockSpec((tm, d), lambda i: (i, 0)),
                  pl.BlockSpec((1, d), lambda i: (0, 0))],
        out_specs=pl.BlockSpec((tm, d), lambda i: (i, 0)),
        out_shape=jax.ShapeDtypeStruct((n, d), F32),
        scratch_shapes=[pltpu.VMEM((2 * 2 * tm * GATHER_PITCH, LANES), F32), pltpu.SemaphoreType.DMA((2,))],
        compiler_params=_cparams(1),
        name="combine",
    )(pos3, pos3, y, h, fw)


def _biasgen_kernel(tab_ref, bm_ref, bn_ref, bw_ref, om_ref, on_ref, ow_ref):
    h = pl.program_id(0)
    far = tab_ref[REL_BUCKETS - 1, h]

    def build(b_ref, shift):
        idx = b_ref[...]
        out = jnp.full(idx.shape, NEG_INF, F32)
        for bucket in range(REL_BUCKETS):
            out = jnp.where(idx == bucket, (tab_ref[bucket, h] - shift) * LOG2E, out)
        return out

    om_ref[0] = build(bm_ref, 0.0)
    on_ref[0] = build(bn_ref, far)
    ow_ref[0] = build(bw_ref, 0.0)


def _biasgen(rel_table, bm, bn, bw):
    heads = rel_table.shape[1]
    full = lambda a: pl.BlockSpec(a.shape, lambda h: (0, 0))
    out = lambda a: pl.BlockSpec((1,) + a.shape, lambda h: (h, 0, 0))
    return pl.pallas_call(
        _biasgen_kernel,
        grid=(heads,),
        in_specs=[pl.BlockSpec(memory_space=pltpu.SMEM), full(bm), full(bn), full(bw)],
        out_specs=[out(bm), out(bn), out(bw)],
        out_shape=[jax.ShapeDtypeStruct((heads,) + a.shape, F32) for a in (bm, bn, bw)],
        compiler_params=_cparams(1),
        name="biasgen",
    )(rel_table, bm, bn, bw)


def _bias_tables(rel_table, t, ncp):
    tq = AT_TQ
    far = REL_MAX_DIST
    buckets = _bucket_table(far + 1)
    i = np.arange(tq)[:, None]

    def bucket_map(dist, ok):
        return jnp.asarray(np.where(ok, buckets[np.clip(dist, 0, far)], -1).astype(np.int32))

    step = tq // CMP_STRIDE
    u = np.arange(2 * ncp)[None, :]
    dist_m = i - CMP_STRIDE * (u - ncp) - (CMP_BLOCK - 1)
    j = np.arange(SLC_CHUNK)[None, :]
    dist_n = (SLC_CHUNK - tq) + i - j
    j = np.arange(WINDOW + tq)[None, :]
    dist_w = WINDOW + i - j
    master, nbias, wbias = _biasgen(rel_table,
                                    bucket_map(dist_m, np.ones_like(dist_m, bool)),
                                    bucket_map(dist_n, dist_n >= 0),
                                    bucket_map(dist_w, (dist_w >= 0) & (dist_w < WINDOW)))
    master = master.reshape(N_NSA_KV, NSA_GQA, tq, 2 * ncp)
    bias_c = jnp.stack([master[..., ncp - step * qt:2 * ncp - step * qt] for qt in range(t // tq)], axis=2)
    bias_c = bias_c.reshape(N_NSA_KV, NSA_GQA, t, ncp)
    nbias = nbias.reshape(N_NSA_KV, NSA_GQA * tq, SLC_CHUNK)
    wbias = wbias.reshape(N_NSA_KV, NSA_GQA * tq, WINDOW + tq)
    return bias_c, nbias, wbias


def _compress_weights(pe, w1, w2):
    half = CMP_STRIDE
    eye = jnp.eye(N_NSA_KV, dtype=F32)

    def expand_w1(w):
        w = w.reshape(half, HEAD_DIM, CMP_HIDDEN)
        return jnp.einsum("idn,gh->igdhn", w, eye).reshape(half * N_NSA_KV * HEAD_DIM, N_NSA_KV * CMP_HIDDEN)

    def expand_pe(p):
        return jnp.broadcast_to(p[:, None, :], (half, N_NSA_KV, HEAD_DIM)).reshape(1, -1)

    w1a = expand_w1(w1[:half * HEAD_DIM]).astype(BF16)
    w1b = expand_w1(w1[half * HEAD_DIM:]).astype(BF16)
    w2x = jnp.einsum("nd,gh->gnhd", w2, eye).reshape(N_NSA_KV * CMP_HIDDEN, N_NSA_KV * HEAD_DIM).astype(BF16)
    return expand_pe(pe[:half]), expand_pe(pe[half:]), w1a, w1b, w2x


def _routing_tables(eid, wts, tm):
    n = eid.shape[0]
    e_flat = eid.reshape(-1)
    onehot = (e_flat[:, None] == jnp.arange(N_EXPERTS, dtype=jnp.int32)[None, :]).astype(jnp.int32)
    csum = jnp.cumsum(onehot, axis=0)
    rank = jnp.take_along_axis(csum, e_flat[:, None], axis=1)[:, 0] - 1
    counts = csum[-1]
    padded = ((counts + tm - 1) // tm) * tm
    ends = jnp.cumsum(padded)
    starts = ends - padded
    pos = (starts[e_flat] + rank).astype(jnp.int32)
    n_tiles = (2 * n) // tm + N_EXPERTS
    rows = n_tiles * tm
    src_pair = jnp.zeros((rows,), jnp.int32).at[pos].set(jnp.arange(2 * n, dtype=jnp.int32))
    src_tok = src_pair // 2
    w_sorted = wts.reshape(-1)[src_pair]
    tile_start = jnp.arange(n_tiles, dtype=jnp.int32) * tm
    tile_e = jnp.minimum(jnp.searchsorted(ends, tile_start, side="right"), N_EXPERTS - 1).astype(jnp.int32)
    n_used = (ends[-1] // tm).astype(jnp.int32).reshape(1)
    last_e = tile_e[jnp.maximum(n_used[0] - 1, 0)]
    tile_e = jnp.where(tile_start < ends[-1], tile_e, last_e)
    return tile_e, src_tok, n_used, w_sorted.reshape(rows, 1), pos


def kernel(x, attn_norm_w, w_in, cmp_pe_k, cmp_pe_v, cmp_k_w1, cmp_k_w2, cmp_v_w1, cmp_v_w2,
           rel_bias_table, fox_forget_b, nsa_out_norm_w, fox_out_norm_w, w_out, ffn_norm_w,
           router_group_w, router_group_b, router_expert_w, router_expert_b,
           expert_w_gate, expert_w_up, expert_w_down, final_norm_w):
    b, t, d = x.shape
    n = b * t
    depth = w_in.shape[0]
    assert t % 512 == 0 and t >= WINDOW + AT_TQ and d == 2048 and t // SLC_BLOCK <= MASK_BLOCK_LANES
    rows = t // CMP_STRIDE
    bias_c, nbias, wbias = _bias_tables(rel_bias_table, t, rows)
    ns = t // SLC_BLOCK
    ratio = SLC_BLOCK // CMP_STRIDE
    span = CMP_BLOCK // CMP_STRIDE
    nc = (t - CMP_BLOCK) // CMP_STRIDE + 1
    impm = np.zeros((ns, rows), np.float32)
    for blk in range(ns):
        for a in range(ratio):
            for s in range(span):
                c = blk * ratio + a - s
                if 0 <= c < nc:
                    impm[blk, c] += 1.0
    impm = jnp.asarray(impm, BF16)
    eye_q = jnp.eye(AT_TQ, dtype=BF16)
    eye_h = jnp.eye(N_FOX_HEADS, dtype=BF16)
    tri = jnp.asarray(np.tril(np.ones((t, t), np.float32)), BF16)
    moe_tm = 512

    h = x.reshape(n, d)
    for layer in range(depth):
        w_main, w_misc = _project_weights(w_in[layer])
        main, misc = _proj(h, attn_norm_w[layer][None, :], w_main, w_misc)
        main3 = main.reshape(b, t, MAIN_COLS)
        misc3 = misc.reshape(b, t, LANES)

        xkv = jnp.stack([main3[:, :, COL_KCMP:COL_KCMP + 256], main3[:, :, COL_VCMP:COL_VCMP + 256]])
        xkv = xkv.reshape(2, b, rows, CMP_STRIDE * 256)
        pk = _compress_weights(cmp_pe_k[layer], cmp_k_w1[layer], cmp_k_w2[layer])
        pv = _compress_weights(cmp_pe_v[layer], cmp_v_w1[layer], cmp_v_w2[layer])
        kvc = _compress(xkv, *[jnp.stack([a, c]) for a, c in zip(pk, pv)])

        gates = misc3[:, :, MISC_GATE:MISC_GATE + 48].reshape(b, t, N_NSA_KV, 12).transpose(0, 2, 1, 3)
        o_cmp, sel = _cmpsel(main3, kvc, bias_c, gates, impm, eye_q)
        o_slc = _slc(main3, sel, nbias, gates)
        o_win = _win(main3, wbias, gates)

        c, ct = _foxprep(misc3, fox_forget_b[layer][None, :], tri, eye_h)
        o_fox = _fox(main3, c, ct)

        wr = jnp.concatenate([router_expert_w[layer], router_group_w[layer]], axis=1)
        wr = jnp.pad(wr, ((0, 0), (0, LANES - wr.shape[1])))
        wr_hi = wr.astype(BF16)
        wr_lo = (wr - wr_hi.astype(F32)).astype(BF16)
        br = jnp.concatenate([router_expert_b[layer], router_group_b[layer]])
        br = jnp.pad(br, (0, LANES - br.shape[0]))[None, :]
        half = N_NSA_HEADS * HEAD_DIM
        h, hn, eid, wts = _outproj(
            o_cmp.reshape(n, half), o_slc.reshape(n, half), o_win.reshape(n, half), o_fox.reshape(n, half),
            h, nsa_out_norm_w[layer][None, :], fox_out_norm_w[layer][None, :],
            w_out[layer].astype(BF16), ffn_norm_w[layer][None, :], jnp.stack([wr_hi, wr_lo]), br)

        tile_e, src_tok, n_used, w_sorted, pos = _routing_tables(eid[:, :2], wts[:, :2], moe_tm)
        y = _moe(tile_e, src_tok, n_used, hn, w_sorted,
                 expert_w_gate[layer], expert_w_up[layer], expert_w_down[layer], moe_tm)
        last = layer == depth - 1
        fw = final_norm_w if last else jnp.ones((d,), F32)
        assert last, "the fused final norm assumes a single layer"
        h = _combine(pos, y, h, fw[None, :])
    return h.reshape(b, t, d)
```

```python
import functools
import math

import numpy as np
import jax
import jax.numpy as jnp
from jax import lax
from jax.experimental import pallas as pl
from jax.experimental.pallas import tpu as pltpu

F32 = jnp.float32
BF16 = jnp.bfloat16

HEAD_DIM = 64
N_NSA_HEADS = 16
N_FOX_HEADS = 16
NSA_GQA = 4
N_NSA_KV = 4
CMP_BLOCK = 32
CMP_STRIDE = 16
CMP_HIDDEN = 128
SLC_BLOCK = 64
SLC_TOP_N = 16
WINDOW = 512
REL_BUCKETS = 32
REL_MAX_DIST = 128
N_GROUPS = 4
EXPERTS_PER_GROUP = 8
N_EXPERTS = 32
EXPERT_FF = 512
NORM_EPS = 1e-6
NEG_INF = -1e30
FORCE_BONUS = 1e4
SCALE = HEAD_DIM ** -0.5
LOG2E = math.log2(math.e)
Q_SCALE = SCALE * LOG2E

LANES = 128
VMEM_LIMIT = 56 * 1024 * 1024

COL_NQ = 0
COL_KCMP = 1024
COL_VCMP = 1280
COL_SLC = 1536
COL_WIN = 2048
COL_FQ = 2560
COL_FK = 3584
COL_FV = 4608
MAIN_COLS = 5632
MISC_GATE = 0
MISC_FF = 48

AT_TQ = 128
NSA_GROUPS_PER_STEP = 2
FOX_PAIRS_PER_STEP = 2
SLC_CHUNK = 512
SLC_SHIFT = 6
assert 1 << SLC_SHIFT == SLC_BLOCK
MASK_BLOCK_LANES = 32
PAD_LANE = HEAD_DIM + MASK_BLOCK_LANES
KEY_PAD = 512


def _nt(a, b):
    return lax.dot_general(a, b, (((1,), (1,)), ((), ())), preferred_element_type=F32)


def _dot(a, b):
    return jnp.dot(a, b, preferred_element_type=F32)


def _split3(x):
    hi = x.astype(BF16)
    r = x - hi.astype(F32)
    mid = r.astype(BF16)
    r = r - mid.astype(F32)
    return hi, mid, r.astype(BF16)


def _cparams(grid_rank):
    return pltpu.CompilerParams(dimension_semantics=("arbitrary",) * grid_rank, vmem_limit_bytes=VMEM_LIMIT)


def _bucket_table(n):
    d = np.arange(n, dtype=np.int64)
    max_exact = REL_BUCKETS // 2
    rel = np.log(np.maximum(d, 1).astype(np.float64) / max_exact) / math.log(REL_MAX_DIST / max_exact)
    scaled = rel * (REL_BUCKETS - max_exact)
    frac = scaled - np.floor(scaled)
    inner = (d > max_exact) & (d < REL_MAX_DIST)
    assert np.all((frac[inner] > 1e-3) & (frac[inner] < 1 - 1e-3))
    large = np.minimum(max_exact + np.floor(scaled + 1e-6).astype(np.int64), REL_BUCKETS - 1)
    return np.where(d < max_exact, d, large).astype(np.int32)


def _project_weights(w):
    d = w.shape[0]
    sizes = [1024] + [256] * 6 + [48, 1024, 1024, 1024, 16]
    offs = np.concatenate([[0], np.cumsum(sizes)])
    nq, kcmp, vcmp, kslc, vslc, kwin, vwin, ngate, fq, fk, fv, ff = [
        w[:, int(offs[i]):int(offs[i + 1])] for i in range(12)]

    def interleave(k, v):
        k = k.reshape(d, N_NSA_KV, HEAD_DIM)
        v = v.reshape(d, N_NSA_KV, HEAD_DIM)
        return jnp.stack([k, v], axis=2).reshape(d, N_NSA_KV * 2 * HEAD_DIM)

    main = jnp.concatenate([nq * Q_SCALE, kcmp, vcmp, interleave(kslc, vslc), interleave(kwin, vwin),
                            fq * Q_SCALE, fk, fv], axis=1)
    assert main.shape[1] == MAIN_COLS
    misc = jnp.concatenate([ngate, ff, jnp.zeros((d, LANES - 64), w.dtype)], axis=1)
    return main.astype(BF16), misc.astype(BF16)


def _proj_kernel(x_ref, nw_ref, w_ref, wm_ref, o_ref, om_ref, *, tn):
    x = x_ref[...]
    y = x * lax.rsqrt(jnp.mean(x * x, axis=-1, keepdims=True) + NORM_EPS) * nw_ref[...]
    xn = y.astype(BF16)
    om_ref[...] = _dot(xn, wm_ref[...])
    for c in range(o_ref.shape[1] // tn):
        o_ref[:, c * tn:(c + 1) * tn] = _dot(xn, w_ref[:, c * tn:(c + 1) * tn]).astype(BF16)


def _proj(x2, norm_w, w_main, w_misc, tm=512, tn=512):
    n, d = x2.shape
    once = pl.Buffered(1)
    return pl.pallas_call(
        functools.partial(_proj_kernel, tn=tn),
        grid=(n // tm,),
        in_specs=[pl.BlockSpec((tm, d), lambda i: (i, 0)),
                  pl.BlockSpec((1, d), lambda i: (0, 0)),
                  pl.BlockSpec((d, MAIN_COLS), lambda i: (0, 0), pipeline_mode=once),
                  pl.BlockSpec((d, LANES), lambda i: (0, 0), pipeline_mode=once)],
        out_specs=[pl.BlockSpec((tm, MAIN_COLS), lambda i: (i, 0)),
                   pl.BlockSpec((tm, LANES), lambda i: (i, 0))],
        out_shape=[jax.ShapeDtypeStruct((n, MAIN_COLS), BF16),
                   jax.ShapeDtypeStruct((n, LANES), F32)],
        compiler_params=_cparams(1),
        name="proj",
    )(x2, norm_w, w_main, w_misc)


def _compress_kernel(x_ref, pea_ref, peb_ref, w1a_ref, w1b_ref, w2_ref, o_ref):
    x = x_ref[0, 0].astype(F32)
    xa = (x + pea_ref[0]).astype(BF16)
    xb = (x + peb_ref[0]).astype(BF16)
    a = _dot(xa, w1a_ref[0])
    b = _dot(xb, w1b_ref[0])
    rows = a.shape[0]
    pre = a + pltpu.roll(b, rows - 1, 0)
    hid = pre * jax.nn.sigmoid(pre)
    out = _dot(hid.astype(BF16), w2_ref[0])
    for g in range(N_NSA_KV):
        o_ref[0, 0, g] = out[:, g * HEAD_DIM:(g + 1) * HEAD_DIM]


def _compress(xkv, pea, peb, w1a, w1b, w2):
    _, b, rows, width = xkv.shape
    hid = N_NSA_KV * CMP_HIDDEN
    return pl.pallas_call(
        _compress_kernel,
        grid=(2, b),
        in_specs=[pl.BlockSpec((1, 1, rows, width), lambda s, i: (s, i, 0, 0)),
                  pl.BlockSpec((1, 1, width), lambda s, i: (s, 0, 0)),
                  pl.BlockSpec((1, 1, width), lambda s, i: (s, 0, 0)),
                  pl.BlockSpec((1, width, hid), lambda s, i: (s, 0, 0)),
                  pl.BlockSpec((1, width, hid), lambda s, i: (s, 0, 0)),
                  pl.BlockSpec((1, hid, N_NSA_KV * HEAD_DIM), lambda s, i: (s, 0, 0))],
        out_specs=pl.BlockSpec((1, 1, N_NSA_KV, rows, HEAD_DIM), lambda s, i: (s, i, 0, 0, 0)),
        out_shape=jax.ShapeDtypeStruct((2, b, N_NSA_KV, rows, HEAD_DIM), F32),
        compiler_params=_cparams(2),
        name="compress",
    )(xkv, pea, peb, w1a, w1b, w2)


def _foxprep_kernel(misc_ref, fb_ref, tri_ref, eye_ref, c_ref, ct_ref):
    z = misc_ref[0][:, MISC_FF:MISC_FF + N_FOX_HEADS] + fb_ref[...]
    logf = (jnp.minimum(z, 0.0) - jnp.log(1.0 + jnp.exp(-jnp.abs(z)))) * LOG2E
    tri = tri_ref[...]
    c = None
    for part in _split3(logf):
        term = _dot(tri, part)
        c = term if c is None else c + term
    ct = None
    for part in _split3(c):
        term = _nt(eye_ref[...], part)
        ct = term if ct is None else ct + term
    for p in range(N_FOX_HEADS // 2):
        c_ref[0, p] = c[:, 2 * p:2 * p + 2]
        ct_ref[0, p] = ct[2 * p:2 * p + 2, :]


def _foxprep(misc3, fb, tri, eye):
    b, t, _ = misc3.shape
    hp = N_FOX_HEADS // 2
    return pl.pallas_call(
        _foxprep_kernel,
        grid=(b,),
        in_specs=[pl.BlockSpec((1, t, LANES), lambda i: (i, 0, 0)),
                  pl.BlockSpec((1, N_FOX_HEADS), lambda i: (0, 0)),
                  pl.BlockSpec((t, t), lambda i: (0, 0)),
                  pl.BlockSpec((N_FOX_HEADS, N_FOX_HEADS), lambda i: (0, 0))],
        out_specs=[pl.BlockSpec((1, hp, t, 2), lambda i: (i, 0, 0, 0)),
                   pl.BlockSpec((1, hp, 2, t), lambda i: (i, 0, 0, 0))],
        out_shape=[jax.ShapeDtypeStruct((b, hp, t, 2), F32),
                   jax.ShapeDtypeStruct((b, hp, 2, t), F32)],
        compiler_params=_cparams(1),
        name="foxprep",
    )(misc3, fb, tri, eye)


def _cmpsel_kernel(q_ref, kvc_ref, bias_ref, gate_ref, impm_ref, eye_ref, o_ref, sel_ref):
    tq = q_ref.shape[1]
    ncp = kvc_ref.shape[3]
    ns = impm_ref.shape[0]
    qw = NSA_GQA * HEAD_DIM
    t0 = pl.program_id(2) * tq
    rows = NSA_GQA * tq
    t_col = t0 + (lax.broadcasted_iota(jnp.int32, (rows, ncp), 0) & (tq - 1))
    c_row = lax.broadcasted_iota(jnp.int32, (rows, ncp), 1)
    valid = t_col >= c_row * CMP_STRIDE + (CMP_BLOCK - 1)
    blk = lax.broadcasted_iota(jnp.int32, (ns, tq), 0)
    t_row = t0 + lax.broadcasted_iota(jnp.int32, (ns, tq), 1)
    cur = t_row // SLC_BLOCK
    bonus = jnp.where((blk == 0) | (blk == cur) | (blk == cur - 1), FORCE_BONUS, 0.0)
    blk_valid = blk * SLC_BLOCK <= t_row
    pad_row = jnp.where(lax.broadcasted_iota(jnp.int32, (HEAD_DIM - MASK_BLOCK_LANES, tq), 0) == 0, 1.0, 0.0)
    q_all = q_ref[0]

    for c in range(kvc_ref.shape[2]):
        kc = kvc_ref[0, 0, c].astype(BF16)
        vc = kvc_ref[1, 0, c].astype(BF16)
        q = q_all[:, c * qw:(c + 1) * qw]
        qst = jnp.concatenate([q[:, r * HEAD_DIM:(r + 1) * HEAD_DIM] for r in range(NSA_GQA)], axis=0)
        s = _nt(qst, kc) + bias_ref[c].reshape(rows, ncp)
        s = jnp.where(valid, s, NEG_INF)
        m = jnp.max(s, axis=-1, keepdims=True)
        e = jnp.where(valid, jnp.exp2(s - m), 0.0)
        p = e / jnp.maximum(jnp.sum(e, axis=-1, keepdims=True), 1e-30)
        o = _dot(p.astype(BF16), vc)
        gate = jax.nn.sigmoid(gate_ref[0, c])
        p_grp = jnp.zeros((tq, ncp), F32)
        for r in range(NSA_GQA):
            col = (c * NSA_GQA + r) * HEAD_DIM
            o_ref[0, :, col:col + HEAD_DIM] = (o[r * tq:(r + 1) * tq] * gate[:, 3 * r:3 * r + 1]).astype(o_ref.dtype)
            p_grp = p_grp + p[r * tq:(r + 1) * tq]

        imp = None
        for part in _split3(p_grp):
            term = _nt(impm_ref[...], part)
            imp = term if imp is None else imp + term
        score = jnp.where(blk_valid, imp + bonus, NEG_INF)
        rank = jnp.zeros((ns, tq), F32)
        for m_blk in range(ns):
            other = score[m_blk:m_blk + 1, :]
            ahead = (other > score) | ((other == score) & (blk > m_blk))
            rank = rank + jnp.where(ahead, 1.0, 0.0)
        unsel = jnp.where(rank < float(min(SLC_TOP_N, ns)), 0.0, 1.0)
        parts = [unsel, pad_row]
        if ns < MASK_BLOCK_LANES:
            parts.insert(1, jnp.zeros((MASK_BLOCK_LANES - ns, tq), F32))
        flags = _nt(eye_ref[...], jnp.concatenate(parts, axis=0).astype(BF16))
        sel_ref[0, c] = (flags * NEG_INF).astype(BF16)


def _cmpsel(main3, kvc, bias_c, gates, impm, eye):
    b, t, _ = main3.shape
    ncp = kvc.shape[3]
    ns = impm.shape[0]
    tq = AT_TQ
    qw = NSA_GQA * HEAD_DIM
    gs = NSA_GROUPS_PER_STEP
    return pl.pallas_call(
        _cmpsel_kernel,
        grid=(b, N_NSA_KV // gs, t // tq),
        in_specs=[pl.BlockSpec((1, tq, gs * qw), lambda i, g, j: (i, j, g)),
                  pl.BlockSpec((2, 1, gs, ncp, HEAD_DIM), lambda i, g, j: (0, i, g, 0, 0)),
                  pl.BlockSpec((gs, NSA_GQA, tq, ncp), lambda i, g, j: (g, 0, j, 0)),
                  pl.BlockSpec((1, gs, tq, 3 * NSA_GQA), lambda i, g, j: (i, g, j, 0)),
                  pl.BlockSpec((ns, ncp), lambda i, g, j: (0, 0)),
                  pl.BlockSpec((tq, tq), lambda i, g, j: (0, 0))],
        out_specs=[pl.BlockSpec((1, tq, gs * qw), lambda i, g, j: (i, j, g)),
                   pl.BlockSpec((1, gs, tq, HEAD_DIM), lambda i, g, j: (i, g, j, 0))],
        out_shape=[jax.ShapeDtypeStruct((b, t, N_NSA_HEADS * HEAD_DIM), BF16),
                   jax.ShapeDtypeStruct((b, N_NSA_KV, t, HEAD_DIM), BF16)],
        compiler_params=_cparams(3),
        name="cmpsel",
    )(main3, kvc, bias_c, gates, impm, eye)


def _softmax_first(s, v):
    m = jnp.max(s, axis=-1, keepdims=True)
    p = jnp.exp2(s - m)
    return m, jnp.sum(p, axis=-1, keepdims=True), _dot(p.astype(BF16), v)


def _softmax_next(s, v, m, l, acc):
    m_new = jnp.maximum(m, jnp.max(s, axis=-1, keepdims=True))
    alpha = jnp.exp2(m - m_new)
    p = jnp.exp2(s - m_new)
    return m_new, alpha * l + jnp.sum(p, axis=-1, keepdims=True), alpha * acc + _dot(p.astype(BF16), v)


def _stack_heads(q, tail, n_heads):
    return jnp.concatenate(
        [jnp.concatenate([q[:, r * HEAD_DIM:(r + 1) * HEAD_DIM], tail], axis=1) for r in range(n_heads)], axis=0)


def _fill_key_scratch(kp_sc, kvp_sc, kv, with_blocks):
    t = kv.shape[0]
    lane = lax.broadcasted_iota(jnp.int32, (t, LANES), 1)
    if with_blocks:
        blk = lax.shift_right_logical(lax.broadcasted_iota(jnp.int32, (t, LANES), 0), SLC_SHIFT)
        aug = jnp.where(lane - HEAD_DIM == blk, 1.0, 0.0).astype(BF16)
    else:
        aug = jnp.zeros((t, LANES), BF16)
    kp_sc[KEY_PAD:, :] = jnp.where(lane < HEAD_DIM, kv, aug)
    lane_p = lax.broadcasted_iota(jnp.int32, (KEY_PAD, LANES), 1)
    kp_sc[0:KEY_PAD, :] = jnp.where(lane_p == PAD_LANE, 1.0, 0.0).astype(BF16)
    kvp_sc[KEY_PAD:, :] = jnp.where(lane < HEAD_DIM, jnp.where(lane == 0, 1.0, 0.0).astype(BF16), kv)
    kvp_sc[0:KEY_PAD, :] = jnp.zeros((KEY_PAD, LANES), BF16)


def _nsa_first(s, v):
    m = jnp.max(s, axis=-1, keepdims=True)
    return m, _dot(jnp.exp2(s - m).astype(BF16), v)


def _nsa_next(s, v, m, acc):
    m_new = jnp.maximum(m, jnp.max(s, axis=-1, keepdims=True))
    return m_new, jnp.exp2(m - m_new) * acc + _dot(jnp.exp2(s - m_new).astype(BF16), v)


def _store_gated(o_ref, group, acc, gate, branch):
    tq = o_ref.shape[1]
    o = acc[:, HEAD_DIM:] / acc[:, 0:1]
    for r in range(NSA_GQA):
        col = (group * NSA_GQA + r) * HEAD_DIM
        gcol = 3 * r + branch
        o_ref[0, :, col:col + HEAD_DIM] = (o[r * tq:(r + 1) * tq] * gate[:, gcol:gcol + 1]).astype(o_ref.dtype)


def _slc_kernel(q_ref, kv_ref, sel_ref, nbias_ref, gate_ref, o_ref, kp_sc, kvp_sc):
    tq = q_ref.shape[1]
    chunk = SLC_CHUNK
    qt = pl.program_id(2)
    t0 = qt * tq

    groups = range(kp_sc.shape[0])
    qw = NSA_GQA * HEAD_DIM

    @pl.when(qt == 0)
    def _():
        for c in groups:
            _fill_key_scratch(kp_sc.at[c], kvp_sc.at[c], kv_ref[0, :, c * LANES:(c + 1) * LANES], True)

    q = q_ref[0]
    qst = [_stack_heads(q[:, c * qw:(c + 1) * qw], sel_ref[0, c], NSA_GQA) for c in groups]

    near = pl.multiple_of(t0 + tq - chunk + KEY_PAD, tq)
    state = tuple(_nsa_first(_nt(qst[c], kp_sc[c, pl.ds(near, chunk), :]) + nbias_ref[c],
                             kvp_sc[c, pl.ds(near, chunk), :]) for c in groups)

    def far(j, carry):
        first = pl.multiple_of(near - (j + 1) * chunk, tq)
        return tuple(_nsa_next(_nt(qst[c], kp_sc[c, pl.ds(first, chunk), :]),
                               kvp_sc[c, pl.ds(first, chunk), :], *carry[c]) for c in groups)

    state = lax.fori_loop(0, (t0 + tq - 1) // chunk, far, state)
    for c in groups:
        _store_gated(o_ref, c, state[c][1], jax.nn.sigmoid(gate_ref[0, c]), 1)


def _slc(main3, sel, nbias, gates):
    b, t, _ = main3.shape
    tq = AT_TQ
    qw = NSA_GQA * HEAD_DIM
    assert KEY_PAD >= SLC_CHUNK and KEY_PAD % tq == 0
    gs = NSA_GROUPS_PER_STEP
    kvw = gs * LANES
    return pl.pallas_call(
        _slc_kernel,
        grid=(b, N_NSA_KV // gs, t // tq),
        in_specs=[pl.BlockSpec((1, tq, gs * qw), lambda i, g, j: (i, j, g)),
                  pl.BlockSpec((1, t, kvw), lambda i, g, j: (i, 0, COL_SLC // kvw + g)),
                  pl.BlockSpec((1, gs, tq, HEAD_DIM), lambda i, g, j: (i, g, j, 0)),
                  pl.BlockSpec((gs, NSA_GQA * tq, SLC_CHUNK), lambda i, g, j: (g, 0, 0)),
                  pl.BlockSpec((1, gs, tq, 3 * NSA_GQA), lambda i, g, j: (i, g, j, 0))],
        out_specs=pl.BlockSpec((1, tq, gs * qw), lambda i, g, j: (i, j, g)),
        out_shape=jax.ShapeDtypeStruct((b, t, N_NSA_HEADS * HEAD_DIM), BF16),
        scratch_shapes=[pltpu.VMEM((gs, t + KEY_PAD, LANES), BF16), pltpu.VMEM((gs, t + KEY_PAD, LANES), BF16)],
        compiler_params=_cparams(3),
        name="slc",
    )(main3, main3, sel, nbias, gates)


def _win_kernel(q_ref, kv_ref, bias_ref, gate_ref, o_ref, kp_sc, kvp_sc):
    tq = q_ref.shape[1]
    span = bias_ref.shape[2]
    qt = pl.program_id(2)

    groups = range(kp_sc.shape[0])
    qw = NSA_GQA * HEAD_DIM

    @pl.when(qt == 0)
    def _():
        for c in groups:
            _fill_key_scratch(kp_sc.at[c], kvp_sc.at[c], kv_ref[0, :, c * LANES:(c + 1) * LANES], False)

    tail = jnp.where(lax.broadcasted_iota(jnp.int32, (tq, HEAD_DIM), 1) == PAD_LANE - HEAD_DIM, NEG_INF, 0.0)
    tail = tail.astype(BF16)
    q = q_ref[0]
    first = pl.multiple_of(qt * tq, tq)
    for c in groups:
        qst = _stack_heads(q[:, c * qw:(c + 1) * qw], tail, NSA_GQA)
        s = _nt(qst, kp_sc[c, pl.ds(first, span), :]) + bias_ref[c]
        _, acc = _nsa_first(s, kvp_sc[c, pl.ds(first, span), :])
        _store_gated(o_ref, c, acc, jax.nn.sigmoid(gate_ref[0, c]), 2)


def _win(main3, wbias, gates):
    b, t, _ = main3.shape
    tq = AT_TQ
    qw = NSA_GQA * HEAD_DIM
    span = wbias.shape[2]
    assert span - tq == KEY_PAD
    gs = NSA_GROUPS_PER_STEP
    kvw = gs * LANES
    return pl.pallas_call(
        _win_kernel,
        grid=(b, N_NSA_KV // gs, t // tq),
        in_specs=[pl.BlockSpec((1, tq, gs * qw), lambda i, g, j: (i, j, g)),
                  pl.BlockSpec((1, t, kvw), lambda i, g, j: (i, 0, COL_WIN // kvw + g)),
                  pl.BlockSpec((gs, NSA_GQA * tq, span), lambda i, g, j: (g, 0, 0)),
                  pl.BlockSpec((1, gs, tq, 3 * NSA_GQA), lambda i, g, j: (i, g, j, 0))],
        out_specs=pl.BlockSpec((1, tq, gs * qw), lambda i, g, j: (i, j, g)),
        out_shape=jax.ShapeDtypeStruct((b, t, N_NSA_HEADS * HEAD_DIM), BF16),
        scratch_shapes=[pltpu.VMEM((gs, t + KEY_PAD, LANES), BF16), pltpu.VMEM((gs, t + KEY_PAD, LANES), BF16)],
        compiler_params=_cparams(3),
        name="win",
    )(main3, main3, wbias, gates)


def _fox_kernel(q_ref, k_ref, v_ref, c_ref, ct_ref, o_ref, *, chunk):
    tq = q_ref.shape[1]
    t0 = pl.program_id(2) * tq
    pairs = range(c_ref.shape[1])
    lane = lax.broadcasted_iota(jnp.int32, (tq, LANES), 1)
    low = lane < HEAD_DIM
    q_all = q_ref[0]
    zero = jnp.zeros((tq, LANES), BF16)
    qst, c_col = [], []
    for p in pairs:
        q = q_all[:, p * LANES:(p + 1) * LANES]
        qst.append(jnp.concatenate([jnp.where(low, q, zero), jnp.where(low, zero, q)], axis=0))
        c_col.append(c_ref[0, p])

    def logits(p, start):
        s = _nt(qst[p], k_ref[0, pl.ds(start, chunk), p * LANES:(p + 1) * LANES])
        c_row = ct_ref[0, p, :, pl.ds(start, chunk)]
        return jnp.concatenate([s[:tq] + c_col[p][:, 0:1] - c_row[0:1],
                                s[tq:] + c_col[p][:, 1:2] - c_row[1:2]], axis=0)

    def values(p, start):
        return v_ref[0, pl.ds(start, chunk), p * LANES:(p + 1) * LANES]

    d0 = pl.multiple_of((t0 // chunk) * chunk, chunk)
    row = lax.broadcasted_iota(jnp.int32, (2 * tq, chunk), 0) & (tq - 1)
    causal = row - lax.broadcasted_iota(jnp.int32, (2 * tq, chunk), 1) + (t0 - d0) >= 0
    state = tuple(_softmax_first(jnp.where(causal, logits(p, d0), NEG_INF), values(p, d0)) for p in pairs)

    def below(j, carry):
        start = pl.multiple_of(j * chunk, chunk)
        return tuple(_softmax_next(logits(p, start), values(p, start), *carry[p]) for p in pairs)

    state = lax.fori_loop(0, t0 // chunk, below, state)
    for p in pairs:
        _, l, acc = state[p]
        o = acc / l
        o_ref[0, :, p * LANES:(p + 1) * LANES] = jnp.where(low, o[:tq], o[tq:]).astype(o_ref.dtype)


def _fox(main3, c, ct, tq=256, chunk=512):
    b, t, _ = main3.shape
    ps = FOX_PAIRS_PER_STEP
    hp = N_FOX_HEADS // 2
    w = ps * LANES
    assert chunk % tq == 0 and t % chunk == 0
    return pl.pallas_call(
        functools.partial(_fox_kernel, chunk=chunk),
        grid=(b, hp // ps, t // tq),
        in_specs=[pl.BlockSpec((1, tq, w), lambda i, p, j: (i, j, COL_FQ // w + p)),
                  pl.BlockSpec((1, t, w), lambda i, p, j: (i, 0, COL_FK // w + p)),
                  pl.BlockSpec((1, t, w), lambda i, p, j: (i, 0, COL_FV // w + p)),
                  pl.BlockSpec((1, ps, tq, 2), lambda i, p, j: (i, p, j, 0)),
                  pl.BlockSpec((1, ps, 2, t), lambda i, p, j: (i, p, 0, 0))],
        out_specs=pl.BlockSpec((1, tq, w), lambda i, p, j: (i, j, p)),
        out_shape=jax.ShapeDtypeStruct((b, t, N_FOX_HEADS * HEAD_DIM), BF16),
        compiler_params=_cparams(3),
        name="fox",
    )(main3, main3, main3, c, ct)


def _rms(x, w):
    return x * lax.rsqrt(jnp.mean(x * x, axis=-1, keepdims=True) + NORM_EPS) * w


def _to_token_rows(ref, x):
    tm, d = x.shape
    parts = d // LANES
    for a in range(parts):
        ref[pl.ds(a, tm, stride=parts), :] = x[:, a * LANES:(a + 1) * LANES]


def _from_token_rows(ref, base, tm, d, pitch):
    return jnp.concatenate([ref[pl.ds(base + a, tm, stride=pitch), :] for a in range(d // LANES)], axis=1)


GATHER_PITCH = 20
GATHER_DMA_PRIORITY = 1


def _outproj_kernel(oc_ref, os_ref, ow_ref, of_ref, x_ref, nnw_ref, fnw_ref, wo_ref, ffw_ref,
                    wr_ref, br_ref, h_ref, hn_ref, eid_ref, wt_ref):
    o_nsa = oc_ref[...].astype(F32) + os_ref[...].astype(F32) + ow_ref[...].astype(F32)
    mixed = jnp.concatenate([_rms(o_nsa, nnw_ref[...]), _rms(of_ref[...].astype(F32), fnw_ref[...])], axis=-1)
    h = x_ref[...] + _dot(mixed.astype(BF16), wo_ref[...])
    h_ref[...] = h
    hn = _rms(h, ffw_ref[...])
    _to_token_rows(hn_ref, hn)

    h_hi, h_mid, _ = _split3(hn)
    w_hi = wr_ref[0]
    w_lo = wr_ref[1]
    logits = (_dot(h_hi, w_hi) + _dot(h_mid, w_hi) + _dot(h_hi, w_lo)) + br_ref[...]
    tm = logits.shape[0]
    lane = lax.broadcasted_iota(jnp.int32, (tm, LANES), 1)
    big = jnp.int32(LANES)
    is_grp = (lane >= N_EXPERTS) & (lane < N_EXPERTS + N_GROUPS)
    glog = jnp.where(is_grp, logits, NEG_INF)
    gmax = jnp.max(glog, axis=-1, keepdims=True)
    gsel = jnp.min(jnp.where(glog == gmax, lane, big), axis=-1, keepdims=True) - N_EXPERTS
    p_gsel = 1.0 / jnp.sum(jnp.where(is_grp, jnp.exp(glog - gmax), 0.0), axis=-1, keepdims=True)
    in_grp = (lane < N_EXPERTS) & (lane // EXPERTS_PER_GROUP == gsel)
    e1 = jnp.where(in_grp, logits, NEG_INF)
    v1 = jnp.max(e1, axis=-1, keepdims=True)
    i1 = jnp.min(jnp.where(e1 == v1, lane, big), axis=-1, keepdims=True)
    e2 = jnp.where(lane == i1, NEG_INF, e1)
    v2 = jnp.max(e2, axis=-1, keepdims=True)
    i2 = jnp.min(jnp.where(e2 == v2, lane, big), axis=-1, keepdims=True)
    ex = jnp.exp(v2 - v1)
    w1 = p_gsel / (1.0 + ex)
    w2 = p_gsel * ex / (1.0 + ex)
    eid_ref[...] = jnp.where(lane == 0, i1, jnp.where(lane == 1, i2, 0))
    wt_ref[...] = jnp.where(lane == 0, w1, jnp.where(lane == 1, w2, 0.0))


def _outproj(oc, osl, ow, of, x2, nnw, fnw, wo, ffw, wr, br, tm=512):
    n, d = x2.shape
    half = oc.shape[1]
    row = lambda i: (i, 0)
    fixed = lambda i: (0, 0)
    return pl.pallas_call(
        _outproj_kernel,
        grid=(n // tm,),
        in_specs=[pl.BlockSpec((tm, half), row), pl.BlockSpec((tm, half), row),
                  pl.BlockSpec((tm, half), row), pl.BlockSpec((tm, half), row),
                  pl.BlockSpec((tm, d), row),
                  pl.BlockSpec((1, half), fixed), pl.BlockSpec((1, half), fixed),
                  pl.BlockSpec((d, d), fixed), pl.BlockSpec((1, d), fixed),
                  pl.BlockSpec((2, d, LANES), lambda i: (0, 0, 0)), pl.BlockSpec((1, LANES), fixed)],
        out_specs=[pl.BlockSpec((tm, d), row), pl.BlockSpec((tm * (d // LANES), LANES), row),
                   pl.BlockSpec((tm, LANES), row), pl.BlockSpec((tm, LANES), row)],
        out_shape=[jax.ShapeDtypeStruct((n, d), F32), jax.ShapeDtypeStruct((n * (d // LANES), LANES), F32),
                   jax.ShapeDtypeStruct((n, LANES), jnp.int32), jax.ShapeDtypeStruct((n, LANES), F32)],
        compiler_params=_cparams(1),
        name="outproj",
    )(oc, osl, ow, of, x2, nnw, fnw, wo, ffw, wr, br)


def _moe_kernel(te_ref, nt_ref, src0_ref, src1_ref, hn_hbm, ws_ref, wg_ref, wu_ref, wd_ref, y_ref,
                xbuf, sem, wg_sc, wu_sc, wd_sc):
    tm = ws_ref.shape[0]
    d = wg_sc.shape[0]
    parts = d // LANES
    i = pl.program_id(0)
    n_used = nt_ref[0]
    slot = i % 2

    def gather(src_ref, slot_):
        def row(r, carry):
            src = pl.multiple_of(src_ref[0, 0, r] * parts, parts)
            dst = (slot_ * tm + r) * GATHER_PITCH
            pltpu.make_async_copy(hn_hbm.at[pl.ds(src, parts)], xbuf.at[pl.ds(dst, parts)],
                                  sem.at[slot_]).start(priority=GATHER_DMA_PRIORITY)
            return carry
        lax.fori_loop(0, tm, row, 0, unroll=8)

    @pl.when(i == 0)
    def _():
        gather(src0_ref, 0)

    @pl.when(i + 1 < n_used)
    def _():
        gather(src1_ref, 1 - slot)

    prev = te_ref[jnp.maximum(i - 1, 0)]

    @pl.when((i == 0) | (te_ref[i] != prev))
    def _():
        wg_sc[...] = wg_ref[0].astype(BF16)
        wu_sc[...] = wu_ref[0].astype(BF16)
        wd_sc[...] = wd_ref[0].astype(BF16)

    @pl.when(i < n_used)
    def _():
        base = slot * (tm * GATHER_PITCH)
        pltpu.make_async_copy(hn_hbm.at[pl.ds(0, tm * parts)], xbuf.at[pl.ds(base, tm * parts)], sem.at[slot]).wait()
        x = _from_token_rows(xbuf, base, tm, d, GATHER_PITCH).astype(BF16)
        gate = _dot(x, wg_sc[...])
        up = _dot(x, wu_sc[...])
        hid = gate * jax.nn.sigmoid(gate) * up
        _to_token_rows(y_ref, ws_ref[...] * _dot(hid.astype(BF16), wd_sc[...]))

    @pl.when(i >= n_used)
    def _():
        y_ref[...] = jnp.zeros(y_ref.shape, F32)


def _moe(tile_e, src_tok, n_used, hn, w_sorted, wg, wu, wd, tm):
    n_tiles = tile_e.shape[0]
    d = wg.shape[1]
    ff = wg.shape[2]
    parts = d // LANES
    src3 = src_tok.reshape(n_tiles, 1, tm)
    grid_spec = pltpu.PrefetchScalarGridSpec(
        num_scalar_prefetch=2,
        grid=(n_tiles,),
        in_specs=[pl.BlockSpec((1, 1, tm), lambda i, te, nt: (0, 0, 0), memory_space=pltpu.SMEM),
                  pl.BlockSpec((1, 1, tm), lambda i, te, nt: (jnp.minimum(i + 1, n_tiles - 1), 0, 0),
                               memory_space=pltpu.SMEM),
                  pl.BlockSpec(memory_space=pl.ANY),
                  pl.BlockSpec((tm, 1), lambda i, te, nt: (i, 0)),
                  pl.BlockSpec((1, d, ff), lambda i, te, nt: (te[i], 0, 0)),
                  pl.BlockSpec((1, d, ff), lambda i, te, nt: (te[i], 0, 0)),
                  pl.BlockSpec((1, ff, d), lambda i, te, nt: (te[i], 0, 0))],
        out_specs=pl.BlockSpec((tm * parts, LANES), lambda i, te, nt: (i, 0)),
        scratch_shapes=[pltpu.VMEM((2 * tm * GATHER_PITCH, LANES), F32),
                        pltpu.SemaphoreType.DMA((2,)),
                        pltpu.VMEM((d, ff), BF16), pltpu.VMEM((d, ff), BF16), pltpu.VMEM((ff, d), BF16)],
    )
    return pl.pallas_call(
        _moe_kernel,
        grid_spec=grid_spec,
        out_shape=jax.ShapeDtypeStruct((n_tiles * tm * parts, LANES), F32),
        compiler_params=_cparams(1),
        name="moe",
    )(tile_e, n_used, src3, src3, hn, w_sorted, wg, wu, wd)


def _combine_kernel(pos0_ref, pos1_ref, y_hbm, h_ref, fw_ref, o_ref, ybuf, sem):
    tm, d = h_ref.shape
    parts = d // LANES
    i = pl.program_id(0)
    n = pl.num_programs(0)
    slot = i % 2

    def gather(pos_ref, slot_):
        def row(r, carry):
            for k in range(2):
                src = pl.multiple_of(pos_ref[0, 0, 2 * r + k] * parts, parts)
                dst = ((slot_ * 2 + k) * tm + r) * GATHER_PITCH
                pltpu.make_async_copy(y_hbm.at[pl.ds(src, parts)], ybuf.at[pl.ds(dst, parts)],
                                      sem.at[slot_]).start(priority=GATHER_DMA_PRIORITY)
            return carry
        lax.fori_loop(0, tm, row, 0, unroll=8)

    @pl.when(i == 0)
    def _():
        gather(pos0_ref, 0)

    @pl.when(i + 1 < n)
    def _():
        gather(pos1_ref, 1 - slot)

    rows = tm * GATHER_PITCH
    base = slot * (2 * rows)
    pltpu.make_async_copy(y_hbm.at[pl.ds(0, 2 * tm * parts)], ybuf.at[pl.ds(base, 2 * tm * parts)], sem.at[slot]).wait()
    out = h_ref[...] + (_from_token_rows(ybuf, base, tm, d, GATHER_PITCH)
                        + _from_token_rows(ybuf, base + rows, tm, d, GATHER_PITCH))
    o_ref[...] = _rms(out, fw_ref[...])


def _combine(pos, y, h, fw, tm=256):
    n, d = h.shape
    steps = n // tm
    pos3 = pos.reshape(steps, 1, 2 * tm)
    return pl.pallas_call(
        _combine_kernel,
        grid=(steps,),
        in_specs=[pl.BlockSpec((1, 1, 2 * tm), lambda i: (0, 0, 0), memory_space=pltpu.SMEM),
                  pl.BlockSpec((1, 1, 2 * tm), lambda i: (jnp.minimum(i + 1, steps - 1), 0, 0),
                               memory_space=pltpu.SMEM),
                  pl.BlockSpec(memory_space=pl.ANY),
                  pl.BlockSpec((tm, d), lambda i: (i, 0)),
                  pl.BlockSpec((1, d), lambda i: (0, 0))],
        out_specs=pl.BlockSpec((tm, d), lambda i: (i, 0)),
        out_shape=jax.ShapeDtypeStruct((n, d), F32),
        scratch_shapes=[pltpu.VMEM((2 * 2 * tm * GATHER_PITCH, LANES), F32), pltpu.SemaphoreType.DMA((2,))],
        compiler_params=_cparams(1),
        name="combine",
    )(pos3, pos3, y, h, fw)


def _biasgen_kernel(tab_ref, bm_ref, bn_ref, bw_ref, om_ref, on_ref, ow_ref):
    h = pl.program_id(0)
    far = tab_ref[REL_BUCKETS - 1, h]

    def build(b_ref, shift):
        idx = b_ref[...]
        out = jnp.full(idx.shape, NEG_INF, F32)
        for bucket in range(REL_BUCKETS):
            out = jnp.where(idx == bucket, (tab_ref[bucket, h] - shift) * LOG2E, out)
        return out

    om_ref[0] = build(bm_ref, 0.0)
    on_ref[0] = build(bn_ref, far)
    ow_ref[0] = build(bw_ref, 0.0)


def _biasgen(rel_table, bm, bn, bw):
    heads = rel_table.shape[1]
    full = lambda a: pl.BlockSpec(a.shape, lambda h: (0, 0))
    out = lambda a: pl.BlockSpec((1,) + a.shape, lambda h: (h, 0, 0))
    return pl.pallas_call(
        _biasgen_kernel,
        grid=(heads,),
        in_specs=[pl.BlockSpec(memory_space=pltpu.SMEM), full(bm), full(bn), full(bw)],
        out_specs=[out(bm), out(bn), out(bw)],
        out_shape=[jax.ShapeDtypeStruct((heads,) + a.shape, F32) for a in (bm, bn, bw)],
        compiler_params=_cparams(1),
        name="biasgen",
    )(rel_table, bm, bn, bw)


def _bias_tables(rel_table, t, ncp):
    tq = AT_TQ
    far = REL_MAX_DIST
    buckets = _bucket_table(far + 1)
    i = np.arange(tq)[:, None]

    def bucket_map(dist, ok):
        return jnp.asarray(np.where(ok, buckets[np.clip(dist, 0, far)], -1).astype(np.int32))

    step = tq // CMP_STRIDE
    u = np.arange(2 * ncp)[None, :]
    dist_m = i - CMP_STRIDE * (u - ncp) - (CMP_BLOCK - 1)
    j = np.arange(SLC_CHUNK)[None, :]
    dist_n = (SLC_CHUNK - tq) + i - j
    j = np.arange(WINDOW + tq)[None, :]
    dist_w = WINDOW + i - j
    master, nbias, wbias = _biasgen(rel_table,
                                    bucket_map(dist_m, np.ones_like(dist_m, bool)),
                                    bucket_map(dist_n, dist_n >= 0),
                                    bucket_map(dist_w, (dist_w >= 0) & (dist_w < WINDOW)))
    master = master.reshape(N_NSA_KV, NSA_GQA, tq, 2 * ncp)
    bias_c = jnp.stack([master[..., ncp - step * qt:2 * ncp - step * qt] for qt in range(t // tq)], axis=2)
    bias_c = bias_c.reshape(N_NSA_KV, NSA_GQA, t, ncp)
    nbias = nbias.reshape(N_NSA_KV, NSA_GQA * tq, SLC_CHUNK)
    wbias = wbias.reshape(N_NSA_KV, NSA_GQA * tq, WINDOW + tq)
    return bias_c, nbias, wbias


def _compress_weights(pe, w1, w2):
    half = CMP_STRIDE
    eye = jnp.eye(N_NSA_KV, dtype=F32)

    def expand_w1(w):
        w = w.reshape(half, HEAD_DIM, CMP_HIDDEN)
        return jnp.einsum("idn,gh->igdhn", w, eye).reshape(half * N_NSA_KV * HEAD_DIM, N_NSA_KV * CMP_HIDDEN)

    def expand_pe(p):
        return jnp.broadcast_to(p[:, None, :], (half, N_NSA_KV, HEAD_DIM)).reshape(1, -1)

    w1a = expand_w1(w1[:half * HEAD_DIM]).astype(BF16)
    w1b = expand_w1(w1[half * HEAD_DIM:]).astype(BF16)
    w2x = jnp.einsum("nd,gh->gnhd", w2, eye).reshape(N_NSA_KV * CMP_HIDDEN, N_NSA_KV * HEAD_DIM).astype(BF16)
    return expand_pe(pe[:half]), expand_pe(pe[half:]), w1a, w1b, w2x


def _routing_tables(eid, wts, tm):
    n = eid.shape[0]
    e_flat = eid.reshape(-1)
    onehot = (e_flat[:, None] == jnp.arange(N_EXPERTS, dtype=jnp.int32)[None, :]).astype(jnp.int32)
    csum = jnp.cumsum(onehot, axis=0)
    rank = jnp.take_along_axis(csum, e_flat[:, None], axis=1)[:, 0] - 1
    counts = csum[-1]
    padded = ((counts + tm - 1) // tm) * tm
    ends = jnp.cumsum(padded)
    starts = ends - padded
    pos = (starts[e_flat] + rank).astype(jnp.int32)
    n_tiles = (2 * n) // tm + N_EXPERTS
    rows = n_tiles * tm
    src_pair = jnp.zeros((rows,), jnp.int32).at[pos].set(jnp.arange(2 * n, dtype=jnp.int32))
    src_tok = src_pair // 2
    w_sorted = wts.reshape(-1)[src_pair]
    tile_start = jnp.arange(n_tiles, dtype=jnp.int32) * tm
    tile_e = jnp.sum((ends[None, :] <= tile_start[:, None]).astype(jnp.int32), axis=1)
    tile_e = jnp.minimum(tile_e, N_EXPERTS - 1)
    n_used = (ends[-1] // tm).astype(jnp.int32).reshape(1)
    last_e = tile_e[jnp.maximum(n_used[0] - 1, 0)]
    tile_e = jnp.where(tile_start < ends[-1], tile_e, last_e)
    return tile_e, src_tok, n_used, w_sorted.reshape(rows, 1), pos


def kernel(x, attn_norm_w, w_in, cmp_pe_k, cmp_pe_v, cmp_k_w1, cmp_k_w2, cmp_v_w1, cmp_v_w2,
           rel_bias_table, fox_forget_b, nsa_out_norm_w, fox_out_norm_w, w_out, ffn_norm_w,
           router_group_w, router_group_b, router_expert_w, router_expert_b,
           expert_w_gate, expert_w_up, expert_w_down, final_norm_w):
    b, t, d = x.shape
    n = b * t
    depth = w_in.shape[0]
    assert t % 512 == 0 and t >= WINDOW + AT_TQ and d == 2048 and t // SLC_BLOCK <= MASK_BLOCK_LANES
    rows = t // CMP_STRIDE
    bias_c, nbias, wbias = _bias_tables(rel_bias_table, t, rows)
    ns = t // SLC_BLOCK
    ratio = SLC_BLOCK // CMP_STRIDE
    span = CMP_BLOCK // CMP_STRIDE
    nc = (t - CMP_BLOCK) // CMP_STRIDE + 1
    impm = np.zeros((ns, rows), np.float32)
    for blk in range(ns):
        for a in range(ratio):
            for s in range(span):
                c = blk * ratio + a - s
                if 0 <= c < nc:
                    impm[blk, c] += 1.0
    impm = jnp.asarray(impm, BF16)
    eye_q = jnp.eye(AT_TQ, dtype=BF16)
    eye_h = jnp.eye(N_FOX_HEADS, dtype=BF16)
    tri = jnp.asarray(np.tril(np.ones((t, t), np.float32)), BF16)
    moe_tm = 512

    h = x.reshape(n, d)
    for layer in range(depth):
        w_main, w_misc = _project_weights(w_in[layer])
        main, misc = _proj(h, attn_norm_w[layer][None, :], w_main, w_misc)
        main3 = main.reshape(b, t, MAIN_COLS)
        misc3 = misc.reshape(b, t, LANES)

        xkv = jnp.stack([main3[:, :, COL_KCMP:COL_KCMP + 256], main3[:, :, COL_VCMP:COL_VCMP + 256]])
        xkv = xkv.reshape(2, b, rows, CMP_STRIDE * 256)
        pk = _compress_weights(cmp_pe_k[layer], cmp_k_w1[layer], cmp_k_w2[layer])
        pv = _compress_weights(cmp_pe_v[layer], cmp_v_w1[layer], cmp_v_w2[layer])
        kvc = _compress(xkv, *[jnp.stack([a, c]) for a, c in zip(pk, pv)])

        gates = misc3[:, :, MISC_GATE:MISC_GATE + 48].reshape(b, t, N_NSA_KV, 12).transpose(0, 2, 1, 3)
        o_cmp, sel = _cmpsel(main3, kvc, bias_c, gates, impm, eye_q)
        o_slc = _slc(main3, sel, nbias, gates)
        o_win = _win(main3, wbias, gates)

        c, ct = _foxprep(misc3, fox_forget_b[layer][None, :], tri, eye_h)
        o_fox = _fox(main3, c, ct)

        wr = jnp.concatenate([router_expert_w[layer], router_group_w[layer]], axis=1)
        wr = jnp.pad(wr, ((0, 0), (0, LANES - wr.shape[1])))
        wr_hi = wr.astype(BF16)
        wr_lo = (wr - wr_hi.astype(F32)).astype(BF16)
        br = jnp.concatenate([router_expert_b[layer], router_group_b[layer]])
        br = jnp.pad(br, (0, LANES - br.shape[0]))[None, :]
        half = N_NSA_HEADS * HEAD_DIM
        h, hn, eid, wts = _outproj(
            o_cmp.reshape(n, half), o_slc.reshape(n, half), o_win.reshape(n, half), o_fox.reshape(n, half),
            h, nsa_out_norm_w[layer][None, :], fox_out_norm_w[layer][None, :],
            w_out[layer].astype(BF16), ffn_norm_w[layer][None, :], jnp.stack([wr_hi, wr_lo]), br)

        tile_e, src_tok, n_used, w_sorted, pos = _routing_tables(eid[:, :2], wts[:, :2], moe_tm)
        y = _moe(tile_e, src_tok, n_used, hn, w_sorted,
                 expert_w_gate[layer], expert_w_up[layer], expert_w_down[layer], moe_tm)
        last = layer == depth - 1
        fw = final_norm_w if last else jnp.ones((d,), F32)
        assert last, "the fused final norm assumes a single layer"
        h = _combine(pos, y, h, fw[None, :])
    return h.reshape(b, t, d)
```

```python
import functools
import math

import numpy as np
import jax
import jax.numpy as jnp
from jax import lax
from jax.experimental import pallas as pl
from jax.experimental.pallas import tpu as pltpu

F32 = jnp.float32
BF16 = jnp.bfloat16

HEAD_DIM = 64
N_NSA_HEADS = 16
N_FOX_HEADS = 16
NSA_GQA = 4
N_NSA_KV = 4
CMP_BLOCK = 32
CMP_STRIDE = 16
CMP_HIDDEN = 128
SLC_BLOCK = 64
SLC_TOP_N = 16
WINDOW = 512
REL_BUCKETS = 32
REL_MAX_DIST = 128
N_GROUPS = 4
EXPERTS_PER_GROUP = 8
N_EXPERTS = 32
EXPERT_FF = 512
NORM_EPS = 1e-6
NEG_INF = -1e30
FORCE_BONUS = 1e4
SCALE = HEAD_DIM ** -0.5
LOG2E = math.log2(math.e)
Q_SCALE = SCALE * LOG2E

LANES = 128
VMEM_LIMIT = 56 * 1024 * 1024

COL_NQ = 0
COL_KCMP = 1024
COL_VCMP = 1280
COL_SLC = 1536
COL_WIN = 2048
COL_FQ = 2560
COL_FK = 3584
COL_FV = 4608
MAIN_COLS = 5632
MISC_GATE = 0
MISC_FF = 48

AT_TQ = 128
NSA_GROUPS_PER_STEP = 2
FOX_PAIRS_PER_STEP = 2
SLC_CHUNK = 512
SLC_SHIFT = 6
assert 1 << SLC_SHIFT == SLC_BLOCK
MASK_BLOCK_LANES = 32
PAD_LANE = HEAD_DIM + MASK_BLOCK_LANES
KEY_PAD = 512


def _nt(a, b):
    return lax.dot_general(a, b, (((1,), (1,)), ((), ())), preferred_element_type=F32)


def _dot(a, b):
    return jnp.dot(a, b, preferred_element_type=F32)


def _split3(x):
    hi = x.astype(BF16)
    r = x - hi.astype(F32)
    mid = r.astype(BF16)
    r = r - mid.astype(F32)
    return hi, mid, r.astype(BF16)


def _cparams(grid_rank):
    return pltpu.CompilerParams(dimension_semantics=("arbitrary",) * grid_rank, vmem_limit_bytes=VMEM_LIMIT)


def _bucket_table(n):
    d = np.arange(n, dtype=np.int64)
    max_exact = REL_BUCKETS // 2
    rel = np.log(np.maximum(d, 1).astype(np.float64) / max_exact) / math.log(REL_MAX_DIST / max_exact)
    scaled = rel * (REL_BUCKETS - max_exact)
    frac = scaled - np.floor(scaled)
    inner = (d > max_exact) & (d < REL_MAX_DIST)
    assert np.all((frac[inner] > 1e-3) & (frac[inner] < 1 - 1e-3))
    large = np.minimum(max_exact + np.floor(scaled + 1e-6).astype(np.int64), REL_BUCKETS - 1)
    return np.where(d < max_exact, d, large).astype(np.int32)


def _project_weights(w):
    d = w.shape[0]
    sizes = [1024] + [256] * 6 + [48, 1024, 1024, 1024, 16]
    offs = np.concatenate([[0], np.cumsum(sizes)])
    nq, kcmp, vcmp, kslc, vslc, kwin, vwin, ngate, fq, fk, fv, ff = [
        w[:, int(offs[i]):int(offs[i + 1])] for i in range(12)]

    def interleave(k, v):
        k = k.reshape(d, N_NSA_KV, HEAD_DIM)
        v = v.reshape(d, N_NSA_KV, HEAD_DIM)
        return jnp.stack([k, v], axis=2).reshape(d, N_NSA_KV * 2 * HEAD_DIM)

    main = jnp.concatenate([nq * Q_SCALE, kcmp, vcmp, interleave(kslc, vslc), interleave(kwin, vwin),
                            fq * Q_SCALE, fk, fv], axis=1)
    assert main.shape[1] == MAIN_COLS
    misc = jnp.concatenate([ngate, ff, jnp.zeros((d, LANES - 64), w.dtype)], axis=1)
    return main.astype(BF16), misc.astype(BF16)


def _proj_kernel(x_ref, nw_ref, w_ref, wm_ref, o_ref, om_ref, *, tn):
    x = x_ref[...]
    y = x * lax.rsqrt(jnp.mean(x * x, axis=-1, keepdims=True) + NORM_EPS) * nw_ref[...]
    xn = y.astype(BF16)
    om_ref[...] = _dot(xn, wm_ref[...])
    for c in range(o_ref.shape[1] // tn):
        o_ref[:, c * tn:(c + 1) * tn] = _dot(xn, w_ref[:, c * tn:(c + 1) * tn]).astype(BF16)


def _proj(x2, norm_w, w_main, w_misc, tm=512, tn=512):
    n, d = x2.shape
    once = pl.Buffered(1)
    return pl.pallas_call(
        functools.partial(_proj_kernel, tn=tn),
        grid=(n // tm,),
        in_specs=[pl.BlockSpec((tm, d), lambda i: (i, 0)),
                  pl.BlockSpec((1, d), lambda i: (0, 0)),
                  pl.BlockSpec((d, MAIN_COLS), lambda i: (0, 0), pipeline_mode=once),
                  pl.BlockSpec((d, LANES), lambda i: (0, 0), pipeline_mode=once)],
        out_specs=[pl.BlockSpec((tm, MAIN_COLS), lambda i: (i, 0)),
                   pl.BlockSpec((tm, LANES), lambda i: (i, 0))],
        out_shape=[jax.ShapeDtypeStruct((n, MAIN_COLS), BF16),
                   jax.ShapeDtypeStruct((n, LANES), F32)],
        compiler_params=_cparams(1),
        name="proj",
    )(x2, norm_w, w_main, w_misc)


def _compress_kernel(x_ref, pea_ref, peb_ref, w1a_ref, w1b_ref, w2_ref, o_ref):
    x = x_ref[0, 0].astype(F32)
    xa = (x + pea_ref[0]).astype(BF16)
    xb = (x + peb_ref[0]).astype(BF16)
    a = _dot(xa, w1a_ref[0])
    b = _dot(xb, w1b_ref[0])
    rows = a.shape[0]
    pre = a + pltpu.roll(b, rows - 1, 0)
    hid = pre * jax.nn.sigmoid(pre)
    out = _dot(hid.astype(BF16), w2_ref[0])
    for g in range(N_NSA_KV):
        o_ref[0, 0, g] = out[:, g * HEAD_DIM:(g + 1) * HEAD_DIM]


def _compress(xkv, pea, peb, w1a, w1b, w2):
    _, b, rows, width = xkv.shape
    hid = N_NSA_KV * CMP_HIDDEN
    return pl.pallas_call(
        _compress_kernel,
        grid=(2, b),
        in_specs=[pl.BlockSpec((1, 1, rows, width), lambda s, i: (s, i, 0, 0)),
                  pl.BlockSpec((1, 1, width), lambda s, i: (s, 0, 0)),
                  pl.BlockSpec((1, 1, width), lambda s, i: (s, 0, 0)),
                  pl.BlockSpec((1, width, hid), lambda s, i: (s, 0, 0)),
                  pl.BlockSpec((1, width, hid), lambda s, i: (s, 0, 0)),
                  pl.BlockSpec((1, hid, N_NSA_KV * HEAD_DIM), lambda s, i: (s, 0, 0))],
        out_specs=pl.BlockSpec((1, 1, N_NSA_KV, rows, HEAD_DIM), lambda s, i: (s, i, 0, 0, 0)),
        out_shape=jax.ShapeDtypeStruct((2, b, N_NSA_KV, rows, HEAD_DIM), F32),
        compiler_params=_cparams(2),
        name="compress",
    )(xkv, pea, peb, w1a, w1b, w2)


def _foxprep_kernel(misc_ref, fb_ref, tri_ref, eye_ref, c_ref, ct_ref):
    z = misc_ref[0][:, MISC_FF:MISC_FF + N_FOX_HEADS] + fb_ref[...]
    logf = (jnp.minimum(z, 0.0) - jnp.log(1.0 + jnp.exp(-jnp.abs(z)))) * LOG2E
    tri = tri_ref[...]
    c = None
    for part in _split3(logf):
        term = _dot(tri, part)
        c = term if c is None else c + term
    ct = None
    for part in _split3(c):
        term = _nt(eye_ref[...], part)
        ct = term if ct is None else ct + term
    for p in range(N_FOX_HEADS // 2):
        c_ref[0, p] = c[:, 2 * p:2 * p + 2]
        ct_ref[0, p] = ct[2 * p:2 * p + 2, :]


def _foxprep(misc3, fb, tri, eye):
    b, t, _ = misc3.shape
    hp = N_FOX_HEADS // 2
    return pl.pallas_call(
        _foxprep_kernel,
        grid=(b,),
        in_specs=[pl.BlockSpec((1, t, LANES), lambda i: (i, 0, 0)),
                  pl.BlockSpec((1, N_FOX_HEADS), lambda i: (0, 0)),
                  pl.BlockSpec((t, t), lambda i: (0, 0)),
                  pl.BlockSpec((N_FOX_HEADS, N_FOX_HEADS), lambda i: (0, 0))],
        out_specs=[pl.BlockSpec((1, hp, t, 2), lambda i: (i, 0, 0, 0)),
                   pl.BlockSpec((1, hp, 2, t), lambda i: (i, 0, 0, 0))],
        out_shape=[jax.ShapeDtypeStruct((b, hp, t, 2), F32),
                   jax.ShapeDtypeStruct((b, hp, 2, t), F32)],
        compiler_params=_cparams(1),
        name="foxprep",
    )(misc3, fb, tri, eye)


def _cmpsel_kernel(q_ref, kvc_ref, bias_ref, gate_ref, impm_ref, eye_ref, o_ref, sel_ref):
    tq = q_ref.shape[1]
    ncp = kvc_ref.shape[3]
    ns = impm_ref.shape[0]
    qw = NSA_GQA * HEAD_DIM
    t0 = pl.program_id(2) * tq
    rows = NSA_GQA * tq
    t_col = t0 + (lax.broadcasted_iota(jnp.int32, (rows, ncp), 0) & (tq - 1))
    c_row = lax.broadcasted_iota(jnp.int32, (rows, ncp), 1)
    valid = t_col >= c_row * CMP_STRIDE + (CMP_BLOCK - 1)
    blk = lax.broadcasted_iota(jnp.int32, (ns, tq), 0)
    t_row = t0 + lax.broadcasted_iota(jnp.int32, (ns, tq), 1)
    cur = t_row // SLC_BLOCK
    bonus = jnp.where((blk == 0) | (blk == cur) | (blk == cur - 1), FORCE_BONUS, 0.0)
    blk_valid = blk * SLC_BLOCK <= t_row
    pad_row = jnp.where(lax.broadcasted_iota(jnp.int32, (HEAD_DIM - MASK_BLOCK_LANES, tq), 0) == 0, 1.0, 0.0)
    q_all = q_ref[0]

    for c in range(kvc_ref.shape[2]):
        kc = kvc_ref[0, 0, c].astype(BF16)
        vc = kvc_ref[1, 0, c].astype(BF16)
        q = q_all[:, c * qw:(c + 1) * qw]
        qst = jnp.concatenate([q[:, r * HEAD_DIM:(r + 1) * HEAD_DIM] for r in range(NSA_GQA)], axis=0)
        s = _nt(qst, kc) + bias_ref[c].reshape(rows, ncp)
        s = jnp.where(valid, s, NEG_INF)
        m = jnp.max(s, axis=-1, keepdims=True)
        e = jnp.where(valid, jnp.exp2(s - m), 0.0)
        p = e / jnp.maximum(jnp.sum(e, axis=-1, keepdims=True), 1e-30)
        o = _dot(p.astype(BF16), vc)
        gate = jax.nn.sigmoid(gate_ref[0, c])
        p_grp = jnp.zeros((tq, ncp), F32)
        for r in range(NSA_GQA):
            col = (c * NSA_GQA + r) * HEAD_DIM
            o_ref[0, :, col:col + HEAD_DIM] = (o[r * tq:(r + 1) * tq] * gate[:, 3 * r:3 * r + 1]).astype(o_ref.dtype)
            p_grp = p_grp + p[r * tq:(r + 1) * tq]

        imp = None
        for part in _split3(p_grp):
            term = _nt(impm_ref[...], part)
            imp = term if imp is None else imp + term
        score = jnp.where(blk_valid, imp + bonus, NEG_INF)
        rank = jnp.zeros((ns, tq), F32)
        for m_blk in range(ns):
            other = score[m_blk:m_blk + 1, :]
            ahead = (other > score) | ((other == score) & (blk > m_blk))
            rank = rank + jnp.where(ahead, 1.0, 0.0)
        unsel = jnp.where(rank < float(min(SLC_TOP_N, ns)), 0.0, 1.0)
        parts = [unsel, pad_row]
        if ns < MASK_BLOCK_LANES:
            parts.insert(1, jnp.zeros((MASK_BLOCK_LANES - ns, tq), F32))
        flags = _nt(eye_ref[...], jnp.concatenate(parts, axis=0).astype(BF16))
        sel_ref[0, c] = (flags * NEG_INF).astype(BF16)


def _cmpsel(main3, kvc, bias_c, gates, impm, eye):
    b, t, _ = main3.shape
    ncp = kvc.shape[3]
    ns = impm.shape[0]
    tq = AT_TQ
    qw = NSA_GQA * HEAD_DIM
    gs = NSA_GROUPS_PER_STEP
    return pl.pallas_call(
        _cmpsel_kernel,
        grid=(b, N_NSA_KV // gs, t // tq),
        in_specs=[pl.BlockSpec((1, tq, gs * qw), lambda i, g, j: (i, j, g)),
                  pl.BlockSpec((2, 1, gs, ncp, HEAD_DIM), lambda i, g, j: (0, i, g, 0, 0)),
                  pl.BlockSpec((gs, NSA_GQA, tq, ncp), lambda i, g, j: (g, 0, j, 0)),
                  pl.BlockSpec((1, gs, tq, 3 * NSA_GQA), lambda i, g, j: (i, g, j, 0)),
                  pl.BlockSpec((ns, ncp), lambda i, g, j: (0, 0)),
                  pl.BlockSpec((tq, tq), lambda i, g, j: (0, 0))],
        out_specs=[pl.BlockSpec((1, tq, gs * qw), lambda i, g, j: (i, j, g)),
                   pl.BlockSpec((1, gs, tq, HEAD_DIM), lambda i, g, j: (i, g, j, 0))],
        out_shape=[jax.ShapeDtypeStruct((b, t, N_NSA_HEADS * HEAD_DIM), BF16),
                   jax.ShapeDtypeStruct((b, N_NSA_KV, t, HEAD_DIM), BF16)],
        compiler_params=_cparams(3),
        name="cmpsel",
    )(main3, kvc, bias_c, gates, impm, eye)


def _softmax_first(s, v):
    m = jnp.max(s, axis=-1, keepdims=True)
    p = jnp.exp2(s - m)
    return m, jnp.sum(p, axis=-1, keepdims=True), _dot(p.astype(BF16), v)


def _softmax_next(s, v, m, l, acc):
    m_new = jnp.maximum(m, jnp.max(s, axis=-1, keepdims=True))
    alpha = jnp.exp2(m - m_new)
    p = jnp.exp2(s - m_new)
    return m_new, alpha * l + jnp.sum(p, axis=-1, keepdims=True), alpha * acc + _dot(p.astype(BF16), v)


def _stack_heads(q, tail, n_heads):
    return jnp.concatenate(
        [jnp.concatenate([q[:, r * HEAD_DIM:(r + 1) * HEAD_DIM], tail], axis=1) for r in range(n_heads)], axis=0)


def _fill_key_scratch(kp_sc, kvp_sc, kv, with_blocks):
    t = kv.shape[0]
    lane = lax.broadcasted_iota(jnp.int32, (t, LANES), 1)
    if with_blocks:
        blk = lax.shift_right_logical(lax.broadcasted_iota(jnp.int32, (t, LANES), 0), SLC_SHIFT)
        aug = jnp.where(lane - HEAD_DIM == blk, 1.0, 0.0).astype(BF16)
    else:
        aug = jnp.zeros((t, LANES), BF16)
    kp_sc[KEY_PAD:, :] = jnp.where(lane < HEAD_DIM, kv, aug)
    lane_p = lax.broadcasted_iota(jnp.int32, (KEY_PAD, LANES), 1)
    kp_sc[0:KEY_PAD, :] = jnp.where(lane_p == PAD_LANE, 1.0, 0.0).astype(BF16)
    kvp_sc[KEY_PAD:, :] = jnp.where(lane < HEAD_DIM, jnp.where(lane == 0, 1.0, 0.0).astype(BF16), kv)
    kvp_sc[0:KEY_PAD, :] = jnp.zeros((KEY_PAD, LANES), BF16)


def _nsa_first(s, v):
    m = jnp.max(s, axis=-1, keepdims=True)
    return m, _dot(jnp.exp2(s - m).astype(BF16), v)


def _nsa_next(s, v, m, acc):
    m_new = jnp.maximum(m, jnp.max(s, axis=-1, keepdims=True))
    return m_new, jnp.exp2(m - m_new) * acc + _dot(jnp.exp2(s - m_new).astype(BF16), v)


def _store_gated(o_ref, group, acc, gate, branch):
    tq = o_ref.shape[1]
    o = acc[:, HEAD_DIM:] / acc[:, 0:1]
    for r in range(NSA_GQA):
        col = (group * NSA_GQA + r) * HEAD_DIM
        gcol = 3 * r + branch
        o_ref[0, :, col:col + HEAD_DIM] = (o[r * tq:(r + 1) * tq] * gate[:, gcol:gcol + 1]).astype(o_ref.dtype)


def _slc_kernel(q_ref, kv_ref, sel_ref, nbias_ref, gate_ref, o_ref, kp_sc, kvp_sc):
    tq = q_ref.shape[1]
    chunk = SLC_CHUNK
    qt = pl.program_id(2)
    t0 = qt * tq

    groups = range(kp_sc.shape[0])
    qw = NSA_GQA * HEAD_DIM

    @pl.when(qt == 0)
    def _():
        for c in groups:
            _fill_key_scratch(kp_sc.at[c], kvp_sc.at[c], kv_ref[0, :, c * LANES:(c + 1) * LANES], True)

    q = q_ref[0]
    qst = [_stack_heads(q[:, c * qw:(c + 1) * qw], sel_ref[0, c], NSA_GQA) for c in groups]

    near = pl.multiple_of(t0 + tq - chunk + KEY_PAD, tq)
    state = tuple(_nsa_first(_nt(qst[c], kp_sc[c, pl.ds(near, chunk), :]) + nbias_ref[c],
                             kvp_sc[c, pl.ds(near, chunk), :]) for c in groups)

    def far(j, carry):
        first = pl.multiple_of(near - (j + 1) * chunk, tq)
        return tuple(_nsa_next(_nt(qst[c], kp_sc[c, pl.ds(first, chunk), :]),
                               kvp_sc[c, pl.ds(first, chunk), :], *carry[c]) for c in groups)

    state = lax.fori_loop(0, (t0 + tq - 1) // chunk, far, state)
    for c in groups:
        _store_gated(o_ref, c, state[c][1], jax.nn.sigmoid(gate_ref[0, c]), 1)


def _slc(main3, sel, nbias, gates):
    b, t, _ = main3.shape
    tq = AT_TQ
    qw = NSA_GQA * HEAD_DIM
    assert KEY_PAD >= SLC_CHUNK and KEY_PAD % tq == 0
    gs = NSA_GROUPS_PER_STEP
    kvw = gs * LANES
    return pl.pallas_call(
        _slc_kernel,
        grid=(b, N_NSA_KV // gs, t // tq),
        in_specs=[pl.BlockSpec((1, tq, gs * qw), lambda i, g, j: (i, j, g)),
                  pl.BlockSpec((1, t, kvw), lambda i, g, j: (i, 0, COL_SLC // kvw + g)),
                  pl.BlockSpec((1, gs, tq, HEAD_DIM), lambda i, g, j: (i, g, j, 0)),
                  pl.BlockSpec((gs, NSA_GQA * tq, SLC_CHUNK), lambda i, g, j: (g, 0, 0)),
                  pl.BlockSpec((1, gs, tq, 3 * NSA_GQA), lambda i, g, j: (i, g, j, 0))],
        out_specs=pl.BlockSpec((1, tq, gs * qw), lambda i, g, j: (i, j, g)),
        out_shape=jax.ShapeDtypeStruct((b, t, N_NSA_HEADS * HEAD_DIM), BF16),
        scratch_shapes=[pltpu.VMEM((gs, t + KEY_PAD, LANES), BF16), pltpu.VMEM((gs, t + KEY_PAD, LANES), BF16)],
        compiler_params=_cparams(3),
        name="slc",
    )(main3, main3, sel, nbias, gates)


def _win_kernel(q_ref, kv_ref, bias_ref, gate_ref, o_ref, kp_sc, kvp_sc):
    tq = q_ref.shape[1]
    span = bias_ref.shape[2]
    qt = pl.program_id(2)

    groups = range(kp_sc.shape[0])
    qw = NSA_GQA * HEAD_DIM

    @pl.when(qt == 0)
    def _():
        for c in groups:
            _fill_key_scratch(kp_sc.at[c], kvp_sc.at[c], kv_ref[0, :, c * LANES:(c + 1) * LANES], False)

    tail = jnp.where(lax.broadcasted_iota(jnp.int32, (tq, HEAD_DIM), 1) == PAD_LANE - HEAD_DIM, NEG_INF, 0.0)
    tail = tail.astype(BF16)
    q = q_ref[0]
    first = pl.multiple_of(qt * tq, tq)
    for c in groups:
        qst = _stack_heads(q[:, c * qw:(c + 1) * qw], tail, NSA_GQA)
        s = _nt(qst, kp_sc[c, pl.ds(first, span), :]) + bias_ref[c]
        _, acc = _nsa_first(s, kvp_sc[c, pl.ds(first, span), :])
        _store_gated(o_ref, c, acc, jax.nn.sigmoid(gate_ref[0, c]), 2)


def _win(main3, wbias, gates):
    b, t, _ = main3.shape
    tq = AT_TQ
    qw = NSA_GQA * HEAD_DIM
    span = wbias.shape[2]
    assert span - tq == KEY_PAD
    gs = NSA_GROUPS_PER_STEP
    kvw = gs * LANES
    return pl.pallas_call(
        _win_kernel,
        grid=(b, N_NSA_KV // gs, t // tq),
        in_specs=[pl.BlockSpec((1, tq, gs * qw), lambda i, g, j: (i, j, g)),
                  pl.BlockSpec((1, t, kvw), lambda i, g, j: (i, 0, COL_WIN // kvw + g)),
                  pl.BlockSpec((gs, NSA_GQA * tq, span), lambda i, g, j: (g, 0, 0)),
                  pl.BlockSpec((1, gs, tq, 3 * NSA_GQA), lambda i, g, j: (i, g, j, 0))],
        out_specs=pl.BlockSpec((1, tq, gs * qw), lambda i, g, j: (i, j, g)),
        out_shape=jax.ShapeDtypeStruct((b, t, N_NSA_HEADS * HEAD_DIM), BF16),
        scratch_shapes=[pltpu.VMEM((gs, t + KEY_PAD, LANES), BF16), pltpu.VMEM((gs, t + KEY_PAD, LANES), BF16)],
        compiler_params=_cparams(3),
        name="win",
    )(main3, main3, wbias, gates)


def _fox_kernel(q_ref, k_ref, v_ref, c_ref, ct_ref, o_ref, *, chunk):
    tq = q_ref.shape[1]
    t0 = pl.program_id(2) * tq
    pairs = range(c_ref.shape[1])
    lane = lax.broadcasted_iota(jnp.int32, (tq, LANES), 1)
    low = lane < HEAD_DIM
    q_all = q_ref[0]
    zero = jnp.zeros((tq, LANES), BF16)
    qst, c_col = [], []
    for p in pairs:
        q = q_all[:, p * LANES:(p + 1) * LANES]
        qst.append(jnp.concatenate([jnp.where(low, q, zero), jnp.where(low, zero, q)], axis=0))
        c_col.append(c_ref[0, p])

    def logits(p, start):
        s = _nt(qst[p], k_ref[0, pl.ds(start, chunk), p * LANES:(p + 1) * LANES])
        c_row = ct_ref[0, p, :, pl.ds(start, chunk)]
        return jnp.concatenate([s[:tq] + c_col[p][:, 0:1] - c_row[0:1],
                                s[tq:] + c_col[p][:, 1:2] - c_row[1:2]], axis=0)

    def values(p, start):
        return v_ref[0, pl.ds(start, chunk), p * LANES:(p + 1) * LANES]

    d0 = pl.multiple_of((t0 // chunk) * chunk, chunk)
    row = lax.broadcasted_iota(jnp.int32, (2 * tq, chunk), 0) & (tq - 1)
    causal = row - lax.broadcasted_iota(jnp.int32, (2 * tq, chunk), 1) + (t0 - d0) >= 0
    state = tuple(_softmax_first(jnp.where(causal, logits(p, d0), NEG_INF), values(p, d0)) for p in pairs)

    def below(j, carry):
        start = pl.multiple_of(j * chunk, chunk)
        return tuple(_softmax_next(logits(p, start), values(p, start), *carry[p]) for p in pairs)

    state = lax.fori_loop(0, t0 // chunk, below, state)
    for p in pairs:
        _, l, acc = state[p]
        o = acc / l
        o_ref[0, :, p * LANES:(p + 1) * LANES] = jnp.where(low, o[:tq], o[tq:]).astype(o_ref.dtype)


def _fox(main3, c, ct, tq=256, chunk=512):
    b, t, _ = main3.shape
    ps = FOX_PAIRS_PER_STEP
    hp = N_FOX_HEADS // 2
    w = ps * LANES
    assert chunk % tq == 0 and t % chunk == 0
    return pl.pallas_call(
        functools.partial(_fox_kernel, chunk=chunk),
        grid=(b, hp // ps, t // tq),
        in_specs=[pl.BlockSpec((1, tq, w), lambda i, p, j: (i, j, COL_FQ // w + p)),
                  pl.BlockSpec((1, t, w), lambda i, p, j: (i, 0, COL_FK // w + p)),
                  pl.BlockSpec((1, t, w), lambda i, p, j: (i, 0, COL_FV // w + p)),
                  pl.BlockSpec((1, ps, tq, 2), lambda i, p, j: (i, p, j, 0)),
                  pl.BlockSpec((1, ps, 2, t), lambda i, p, j: (i, p, 0, 0))],
        out_specs=pl.BlockSpec((1, tq, w), lambda i, p, j: (i, j, p)),
        out_shape=jax.ShapeDtypeStruct((b, t, N_FOX_HEADS * HEAD_DIM), BF16),
        compiler_params=_cparams(3),
        name="fox",
    )(main3, main3, main3, c, ct)


def _rms(x, w):
    return x * lax.rsqrt(jnp.mean(x * x, axis=-1, keepdims=True) + NORM_EPS) * w


def _to_token_rows(ref, x):
    tm, d = x.shape
    parts = d // LANES
    for a in range(parts):
        ref[pl.ds(a, tm, stride=parts), :] = x[:, a * LANES:(a + 1) * LANES]


def _from_token_rows(ref, base, tm, d, pitch):
    return jnp.concatenate([ref[pl.ds(base + a, tm, stride=pitch), :] for a in range(d // LANES)], axis=1)


GATHER_PITCH = 20


def _outproj_kernel(oc_ref, os_ref, ow_ref, of_ref, x_ref, nnw_ref, fnw_ref, wo_ref, ffw_ref,
                    wr_ref, br_ref, h_ref, hn_ref, eid_ref, wt_ref):
    o_nsa = oc_ref[...].astype(F32) + os_ref[...].astype(F32) + ow_ref[...].astype(F32)
    mixed = jnp.concatenate([_rms(o_nsa, nnw_ref[...]), _rms(of_ref[...].astype(F32), fnw_ref[...])], axis=-1)
    h = x_ref[...] + _dot(mixed.astype(BF16), wo_ref[...])
    h_ref[...] = h
    hn = _rms(h, ffw_ref[...])
    _to_token_rows(hn_ref, hn)

    h_hi, h_mid, _ = _split3(hn)
    w_hi = wr_ref[0]
    w_lo = wr_ref[1]
    logits = (_dot(h_hi, w_hi) + _dot(h_mid, w_hi) + _dot(h_hi, w_lo)) + br_ref[...]
    tm = logits.shape[0]
    lane = lax.broadcasted_iota(jnp.int32, (tm, LANES), 1)
    big = jnp.int32(LANES)
    is_grp = (lane >= N_EXPERTS) & (lane < N_EXPERTS + N_GROUPS)
    glog = jnp.where(is_grp, logits, NEG_INF)
    gmax = jnp.max(glog, axis=-1, keepdims=True)
    gsel = jnp.min(jnp.where(glog == gmax, lane, big), axis=-1, keepdims=True) - N_EXPERTS
    p_gsel = 1.0 / jnp.sum(jnp.where(is_grp, jnp.exp(glog - gmax), 0.0), axis=-1, keepdims=True)
    in_grp = (lane < N_EXPERTS) & (lane // EXPERTS_PER_GROUP == gsel)
    e1 = jnp.where(in_grp, logits, NEG_INF)
    v1 = jnp.max(e1, axis=-1, keepdims=True)
    i1 = jnp.min(jnp.where(e1 == v1, lane, big), axis=-1, keepdims=True)
    e2 = jnp.where(lane == i1, NEG_INF, e1)
    v2 = jnp.max(e2, axis=-1, keepdims=True)
    i2 = jnp.min(jnp.where(e2 == v2, lane, big), axis=-1, keepdims=True)
    ex = jnp.exp(v2 - v1)
    w1 = p_gsel / (1.0 + ex)
    w2 = p_gsel * ex / (1.0 + ex)
    eid_ref[...] = jnp.where(lane == 0, i1, jnp.where(lane == 1, i2, 0))
    wt_ref[...] = jnp.where(lane == 0, w1, jnp.where(lane == 1, w2, 0.0))


def _outproj(oc, osl, ow, of, x2, nnw, fnw, wo, ffw, wr, br, tm=512):
    n, d = x2.shape
    half = oc.shape[1]
    row = lambda i: (i, 0)
    fixed = lambda i: (0, 0)
    return pl.pallas_call(
        _outproj_kernel,
        grid=(n // tm,),
        in_specs=[pl.BlockSpec((tm, half), row), pl.BlockSpec((tm, half), row),
                  pl.BlockSpec((tm, half), row), pl.BlockSpec((tm, half), row),
                  pl.BlockSpec((tm, d), row),
                  pl.BlockSpec((1, half), fixed), pl.BlockSpec((1, half), fixed),
                  pl.BlockSpec((d, d), fixed), pl.BlockSpec((1, d), fixed),
                  pl.BlockSpec((2, d, LANES), lambda i: (0, 0, 0)), pl.BlockSpec((1, LANES), fixed)],
        out_specs=[pl.BlockSpec((tm, d), row), pl.BlockSpec((tm * (d // LANES), LANES), row),
                   pl.BlockSpec((tm, LANES), row), pl.BlockSpec((tm, LANES), row)],
        out_shape=[jax.ShapeDtypeStruct((n, d), F32), jax.ShapeDtypeStruct((n * (d // LANES), LANES), F32),
                   jax.ShapeDtypeStruct((n, LANES), jnp.int32), jax.ShapeDtypeStruct((n, LANES), F32)],
        compiler_params=_cparams(1),
        name="outproj",
    )(oc, osl, ow, of, x2, nnw, fnw, wo, ffw, wr, br)


def _moe_kernel(te_ref, nt_ref, src0_ref, src1_ref, hn_hbm, ws_ref, wg_ref, wu_ref, wd_ref, y_ref,
                xbuf, sem, wg_sc, wu_sc, wd_sc):
    tm = ws_ref.shape[0]
    d = wg_sc.shape[0]
    parts = d // LANES
    i = pl.program_id(0)
    n_used = nt_ref[0]
    slot = i % 2

    def gather(src_ref, slot_):
        def row(r, carry):
            src = pl.multiple_of(src_ref[0, 0, r] * parts, parts)
            dst = (slot_ * tm + r) * GATHER_PITCH
            pltpu.make_async_copy(hn_hbm.at[pl.ds(src, parts)], xbuf.at[pl.ds(dst, parts)], sem.at[slot_]).start()
            return carry
        lax.fori_loop(0, tm, row, 0, unroll=8)

    @pl.when(i == 0)
    def _():
        gather(src0_ref, 0)

    @pl.when(i + 1 < n_used)
    def _():
        gather(src1_ref, 1 - slot)

    prev = te_ref[jnp.maximum(i - 1, 0)]

    @pl.when((i == 0) | (te_ref[i] != prev))
    def _():
        wg_sc[...] = wg_ref[0].astype(BF16)
        wu_sc[...] = wu_ref[0].astype(BF16)
        wd_sc[...] = wd_ref[0].astype(BF16)

    @pl.when(i < n_used)
    def _():
        base = slot * (tm * GATHER_PITCH)
        pltpu.make_async_copy(hn_hbm.at[pl.ds(0, tm * parts)], xbuf.at[pl.ds(base, tm * parts)], sem.at[slot]).wait()
        x = _from_token_rows(xbuf, base, tm, d, GATHER_PITCH).astype(BF16)
        gate = _dot(x, wg_sc[...])
        up = _dot(x, wu_sc[...])
        hid = gate * jax.nn.sigmoid(gate) * up
        _to_token_rows(y_ref, ws_ref[...] * _dot(hid.astype(BF16), wd_sc[...]))

    @pl.when(i >= n_used)
    def _():
        y_ref[...] = jnp.zeros(y_ref.shape, F32)


def _moe(tile_e, src_tok, n_used, hn, w_sorted, wg, wu, wd, tm):
    n_tiles = tile_e.shape[0]
    d = wg.shape[1]
    ff = wg.shape[2]
    parts = d // LANES
    src3 = src_tok.reshape(n_tiles, 1, tm)
    grid_spec = pltpu.PrefetchScalarGridSpec(
        num_scalar_prefetch=2,
        grid=(n_tiles,),
        in_specs=[pl.BlockSpec((1, 1, tm), lambda i, te, nt: (0, 0, 0), memory_space=pltpu.SMEM),
                  pl.BlockSpec((1, 1, tm), lambda i, te, nt: (jnp.minimum(i + 1, n_tiles - 1), 0, 0),
                               memory_space=pltpu.SMEM),
                  pl.BlockSpec(memory_space=pl.ANY),
                  pl.BlockSpec((tm, 1), lambda i, te, nt: (i, 0)),
                  pl.BlockSpec((1, d, ff), lambda i, te, nt: (te[i], 0, 0)),
                  pl.BlockSpec((1, d, ff), lambda i, te, nt: (te[i], 0, 0)),
                  pl.BlockSpec((1, ff, d), lambda i, te, nt: (te[i], 0, 0))],
        out_specs=pl.BlockSpec((tm * parts, LANES), lambda i, te, nt: (i, 0)),
        scratch_shapes=[pltpu.VMEM((2 * tm * GATHER_PITCH, LANES), F32),
                        pltpu.SemaphoreType.DMA((2,)),
                        pltpu.VMEM((d, ff), BF16), pltpu.VMEM((d, ff), BF16), pltpu.VMEM((ff, d), BF16)],
    )
    return pl.pallas_call(
        _moe_kernel,
        grid_spec=grid_spec,
        out_shape=jax.ShapeDtypeStruct((n_tiles * tm * parts, LANES), F32),
        compiler_params=_cparams(1),
        name="moe",
    )(tile_e, n_used, src3, src3, hn, w_sorted, wg, wu, wd)


def _combine_kernel(pos0_ref, pos1_ref, y_hbm, h_ref, fw_ref, o_ref, ybuf, sem):
    tm, d = h_ref.shape
    parts = d // LANES
    i = pl.program_id(0)
    n = pl.num_programs(0)
    slot = i % 2

    def gather(pos_ref, slot_):
        def row(r, carry):
            for k in range(2):
                src = pl.multiple_of(pos_ref[0, 0, 2 * r + k] * parts, parts)
                dst = ((slot_ * 2 + k) * tm + r) * GATHER_PITCH
                pltpu.make_async_copy(y_hbm.at[pl.ds(src, parts)], ybuf.at[pl.ds(dst, parts)], sem.at[slot_]).start()
            return carry
        lax.fori_loop(0, tm, row, 0, unroll=8)

    @pl.when(i == 0)
    def _():
        gather(pos0_ref, 0)

    @pl.when(i + 1 < n)
    def _():
        gather(pos1_ref, 1 - slot)

    rows = tm * GATHER_PITCH
    base = slot * (2 * rows)
    pltpu.make_async_copy(y_hbm.at[pl.ds(0, 2 * tm * parts)], ybuf.at[pl.ds(base, 2 * tm * parts)], sem.at[slot]).wait()
    out = h_ref[...] + (_from_token_rows(ybuf, base, tm, d, GATHER_PITCH)
                        + _from_token_rows(ybuf, base + rows, tm, d, GATHER_PITCH))
    o_ref[...] = _rms(out, fw_ref[...])


def _combine(pos, y, h, fw, tm=256):
    n, d = h.shape
    steps = n // tm
    pos3 = pos.reshape(steps, 1, 2 * tm)
    return pl.pallas_call(
        _combine_kernel,
        grid=(steps,),
        in_specs=[pl.BlockSpec((1, 1, 2 * tm), lambda i: (0, 0, 0), memory_space=pltpu.SMEM),
                  pl.BlockSpec((1, 1, 2 * tm), lambda i: (jnp.minimum(i + 1, steps - 1), 0, 0),
                               memory_space=pltpu.SMEM),
                  pl.BlockSpec(memory_space=pl.ANY),
                  pl.BlockSpec((tm, d), lambda i: (i, 0)),
                  pl.BlockSpec((1, d), lambda i: (0, 0))],
        out_specs=pl.BlockSpec((tm, d), lambda i: (i, 0)),
        out_shape=jax.ShapeDtypeStruct((n, d), F32),
        scratch_shapes=[pltpu.VMEM((2 * 2 * tm * GATHER_PITCH, LANES), F32), pltpu.SemaphoreType.DMA((2,))],
        compiler_params=_cparams(1),
        name="combine",
    )(pos3, pos3, y, h, fw)


def _biasgen_kernel(tab_ref, bm_ref, bn_ref, bw_ref, om_ref, on_ref, ow_ref):
    h = pl.program_id(0)
    far = tab_ref[REL_BUCKETS - 1, h]

    def build(b_ref, shift):
        idx = b_ref[...]
        out = jnp.full(idx.shape, NEG_INF, F32)
        for bucket in range(REL_BUCKETS):
            out = jnp.where(idx == bucket, (tab_ref[bucket, h] - shift) * LOG2E, out)
        return out

    om_ref[0] = build(bm_ref, 0.0)
    on_ref[0] = build(bn_ref, far)
    ow_ref[0] = build(bw_ref, 0.0)


def _biasgen(rel_table, bm, bn, bw):
    heads = rel_table.shape[1]
    full = lambda a: pl.BlockSpec(a.shape, lambda h: (0, 0))
    out = lambda a: pl.BlockSpec((1,) + a.shape, lambda h: (h, 0, 0))
    return pl.pallas_call(
        _biasgen_kernel,
        grid=(heads,),
        in_specs=[pl.BlockSpec(memory_space=pltpu.SMEM), full(bm), full(bn), full(bw)],
        out_specs=[out(bm), out(bn), out(bw)],
        out_shape=[jax.ShapeDtypeStruct((heads,) + a.shape, F32) for a in (bm, bn, bw)],
        compiler_params=_cparams(1),
        name="biasgen",
    )(rel_table, bm, bn, bw)


def _bias_tables(rel_table, t, ncp):
    tq = AT_TQ
    far = REL_MAX_DIST
    buckets = _bucket_table(far + 1)
    i = np.arange(tq)[:, None]

    def bucket_map(dist, ok):
        return jnp.asarray(np.where(ok, buckets[np.clip(dist, 0, far)], -1).astype(np.int32))

    step = tq // CMP_STRIDE
    u = np.arange(2 * ncp)[None, :]
    dist_m = i - CMP_STRIDE * (u - ncp) - (CMP_BLOCK - 1)
    j = np.arange(SLC_CHUNK)[None, :]
    dist_n = (SLC_CHUNK - tq) + i - j
    j = np.arange(WINDOW + tq)[None, :]
    dist_w = WINDOW + i - j
    master, nbias, wbias = _biasgen(rel_table,
                                    bucket_map(dist_m, np.ones_like(dist_m, bool)),
                                    bucket_map(dist_n, dist_n >= 0),
                                    bucket_map(dist_w, (dist_w >= 0) & (dist_w < WINDOW)))
    master = master.reshape(N_NSA_KV, NSA_GQA, tq, 2 * ncp)
    bias_c = jnp.stack([master[..., ncp - step * qt:2 * ncp - step * qt] for qt in range(t // tq)], axis=2)
    bias_c = bias_c.reshape(N_NSA_KV, NSA_GQA, t, ncp)
    nbias = nbias.reshape(N_NSA_KV, NSA_GQA * tq, SLC_CHUNK)
    wbias = wbias.reshape(N_NSA_KV, NSA_GQA * tq, WINDOW + tq)
    return bias_c, nbias, wbias


def _compress_weights(pe, w1, w2):
    half = CMP_STRIDE
    eye = jnp.eye(N_NSA_KV, dtype=F32)

    def expand_w1(w):
        w = w.reshape(half, HEAD_DIM, CMP_HIDDEN)
        return jnp.einsum("idn,gh->igdhn", w, eye).reshape(half * N_NSA_KV * HEAD_DIM, N_NSA_KV * CMP_HIDDEN)

    def expand_pe(p):
        return jnp.broadcast_to(p[:, None, :], (half, N_NSA_KV, HEAD_DIM)).reshape(1, -1)

    w1a = expand_w1(w1[:half * HEAD_DIM]).astype(BF16)
    w1b = expand_w1(w1[half * HEAD_DIM:]).astype(BF16)
    w2x = jnp.einsum("nd,gh->gnhd", w2, eye).reshape(N_NSA_KV * CMP_HIDDEN, N_NSA_KV * HEAD_DIM).astype(BF16)
    return expand_pe(pe[:half]), expand_pe(pe[half:]), w1a, w1b, w2x


def _routing_tables(eid, wts, tm):
    n = eid.shape[0]
    e_flat = eid.reshape(-1)
    onehot = (e_flat[:, None] == jnp.arange(N_EXPERTS, dtype=jnp.int32)[None, :]).astype(jnp.int32)
    csum = jnp.cumsum(onehot, axis=0)
    rank = jnp.take_along_axis(csum, e_flat[:, None], axis=1)[:, 0] - 1
    counts = csum[-1]
    padded = ((counts + tm - 1) // tm) * tm
    ends = jnp.cumsum(padded)
    starts = ends - padded
    pos = (starts[e_flat] + rank).astype(jnp.int32)
    n_tiles = (2 * n) // tm + N_EXPERTS
    rows = n_tiles * tm
    src_pair = jnp.zeros((rows,), jnp.int32).at[pos].set(jnp.arange(2 * n, dtype=jnp.int32))
    src_tok = src_pair // 2
    w_sorted = wts.reshape(-1)[src_pair]
    tile_start = jnp.arange(n_tiles, dtype=jnp.int32) * tm
    tile_e = jnp.sum((ends[None, :] <= tile_start[:, None]).astype(jnp.int32), axis=1)
    tile_e = jnp.minimum(tile_e, N_EXPERTS - 1)
    n_used = (ends[-1] // tm).astype(jnp.int32).reshape(1)
    last_e = tile_e[jnp.maximum(n_used[0] - 1, 0)]
    tile_e = jnp.where(tile_start < ends[-1], tile_e, last_e)
    return tile_e, src_tok, n_used, w_sorted.reshape(rows, 1), pos


def kernel(x, attn_norm_w, w_in, cmp_pe_k, cmp_pe_v, cmp_k_w1, cmp_k_w2, cmp_v_w1, cmp_v_w2,
           rel_bias_table, fox_forget_b, nsa_out_norm_w, fox_out_norm_w, w_out, ffn_norm_w,
           router_group_w, router_group_b, router_expert_w, router_expert_b,
           expert_w_gate, expert_w_up, expert_w_down, final_norm_w):
    b, t, d = x.shape
    n = b * t
    depth = w_in.shape[0]
    assert t % 512 == 0 and t >= WINDOW + AT_TQ and d == 2048 and t // SLC_BLOCK <= MASK_BLOCK_LANES
    rows = t // CMP_STRIDE
    bias_c, nbias, wbias = _bias_tables(rel_bias_table, t, rows)
    ns = t // SLC_BLOCK
    ratio = SLC_BLOCK // CMP_STRIDE
    span = CMP_BLOCK // CMP_STRIDE
    nc = (t - CMP_BLOCK) // CMP_STRIDE + 1
    impm = np.zeros((ns, rows), np.float32)
    for blk in range(ns):
        for a in range(ratio):
            for s in range(span):
                c = blk * ratio + a - s
                if 0 <= c < nc:
                    impm[blk, c] += 1.0
    impm = jnp.asarray(impm, BF16)
    eye_q = jnp.eye(AT_TQ, dtype=BF16)
    eye_h = jnp.eye(N_FOX_HEADS, dtype=BF16)
    tri = jnp.asarray(np.tril(np.ones((t, t), np.float32)), BF16)
    moe_tm = 256

    h = x.reshape(n, d)
    for layer in range(depth):
        w_main, w_misc = _project_weights(w_in[layer])
        main, misc = _proj(h, attn_norm_w[layer][None, :], w_main, w_misc)
        main3 = main.reshape(b, t, MAIN_COLS)
        misc3 = misc.reshape(b, t, LANES)

        xkv = jnp.stack([main3[:, :, COL_KCMP:COL_KCMP + 256], main3[:, :, COL_VCMP:COL_VCMP + 256]])
        xkv = xkv.reshape(2, b, rows, CMP_STRIDE * 256)
        pk = _compress_weights(cmp_pe_k[layer], cmp_k_w1[layer], cmp_k_w2[layer])
        pv = _compress_weights(cmp_pe_v[layer], cmp_v_w1[layer], cmp_v_w2[layer])
        kvc = _compress(xkv, *[jnp.stack([a, c]) for a, c in zip(pk, pv)])

        gates = misc3[:, :, MISC_GATE:MISC_GATE + 48].reshape(b, t, N_NSA_KV, 12).transpose(0, 2, 1, 3)
        o_cmp, sel = _cmpsel(main3, kvc, bias_c, gates, impm, eye_q)
        o_slc = _slc(main3, sel, nbias, gates)
        o_win = _win(main3, wbias, gates)

        c, ct = _foxprep(misc3, fox_forget_b[layer][None, :], tri, eye_h)
        o_fox = _fox(main3, c, ct)

        wr = jnp.concatenate([router_expert_w[layer], router_group_w[layer]], axis=1)
        wr = jnp.pad(wr, ((0, 0), (0, LANES - wr.shape[1])))
        wr_hi = wr.astype(BF16)
        wr_lo = (wr - wr_hi.astype(F32)).astype(BF16)
        br = jnp.concatenate([router_expert_b[layer], router_group_b[layer]])
        br = jnp.pad(br, (0, LANES - br.shape[0]))[None, :]
        half = N_NSA_HEADS * HEAD_DIM
        h, hn, eid, wts = _outproj(
            o_cmp.reshape(n, half), o_slc.reshape(n, half), o_win.reshape(n, half), o_fox.reshape(n, half),
            h, nsa_out_norm_w[layer][None, :], fox_out_norm_w[layer][None, :],
            w_out[layer].astype(BF16), ffn_norm_w[layer][None, :], jnp.stack([wr_hi, wr_lo]), br)

        tile_e, src_tok, n_used, w_sorted, pos = _routing_tables(eid[:, :2], wts[:, :2], moe_tm)
        y = _moe(tile_e, src_tok, n_used, hn, w_sorted,
                 expert_w_gate[layer], expert_w_up[layer], expert_w_down[layer], moe_tm)
        last = layer == depth - 1
        fw = final_norm_w if last else jnp.ones((d,), F32)
        assert last, "the fused final norm assumes a single layer"
        h = _combine(pos, y, h, fw[None, :])
    return h.reshape(b, t, d)
```

```python
import functools
import math

import numpy as np
import jax
import jax.numpy as jnp
from jax import lax
from jax.experimental import pallas as pl
from jax.experimental.pallas import tpu as pltpu

F32 = jnp.float32
BF16 = jnp.bfloat16

HEAD_DIM = 64
N_NSA_HEADS = 16
N_FOX_HEADS = 16
NSA_GQA = 4
N_NSA_KV = 4
CMP_BLOCK = 32
CMP_STRIDE = 16
CMP_HIDDEN = 128
SLC_BLOCK = 64
SLC_TOP_N = 16
WINDOW = 512
REL_BUCKETS = 32
REL_MAX_DIST = 128
N_GROUPS = 4
EXPERTS_PER_GROUP = 8
N_EXPERTS = 32
EXPERT_FF = 512
NORM_EPS = 1e-6
NEG_INF = -1e30
FORCE_BONUS = 1e4
SCALE = HEAD_DIM ** -0.5
LOG2E = math.log2(math.e)
Q_SCALE = SCALE * LOG2E

LANES = 128
VMEM_LIMIT = 56 * 1024 * 1024

COL_NQ = 0
COL_KCMP = 1024
COL_VCMP = 1280
COL_SLC = 1536
COL_WIN = 2048
COL_FQ = 2560
COL_FK = 3584
COL_FV = 4608
MAIN_COLS = 5632
MISC_GATE = 0
MISC_FF = 48

AT_TQ = 128
NSA_GROUPS_PER_STEP = 2
FOX_PAIRS_PER_STEP = 2
SLC_CHUNK = 512
SLC_SHIFT = 6
assert 1 << SLC_SHIFT == SLC_BLOCK
MASK_BLOCK_LANES = 32
PAD_LANE = HEAD_DIM + MASK_BLOCK_LANES
KEY_PAD = 512


def _nt(a, b):
    return lax.dot_general(a, b, (((1,), (1,)), ((), ())), preferred_element_type=F32)


def _dot(a, b):
    return jnp.dot(a, b, preferred_element_type=F32)


def _split3(x):
    hi = x.astype(BF16)
    r = x - hi.astype(F32)
    mid = r.astype(BF16)
    r = r - mid.astype(F32)
    return hi, mid, r.astype(BF16)


def _cparams(grid_rank):
    return pltpu.CompilerParams(dimension_semantics=("arbitrary",) * grid_rank, vmem_limit_bytes=VMEM_LIMIT)


def _bucket_table(n):
    d = np.arange(n, dtype=np.int64)
    max_exact = REL_BUCKETS // 2
    rel = np.log(np.maximum(d, 1).astype(np.float64) / max_exact) / math.log(REL_MAX_DIST / max_exact)
    scaled = rel * (REL_BUCKETS - max_exact)
    frac = scaled - np.floor(scaled)
    inner = (d > max_exact) & (d < REL_MAX_DIST)
    assert np.all((frac[inner] > 1e-3) & (frac[inner] < 1 - 1e-3))
    large = np.minimum(max_exact + np.floor(scaled + 1e-6).astype(np.int64), REL_BUCKETS - 1)
    return np.where(d < max_exact, d, large).astype(np.int32)


def _project_weights(w):
    d = w.shape[0]
    sizes = [1024] + [256] * 6 + [48, 1024, 1024, 1024, 16]
    offs = np.concatenate([[0], np.cumsum(sizes)])
    nq, kcmp, vcmp, kslc, vslc, kwin, vwin, ngate, fq, fk, fv, ff = [
        w[:, int(offs[i]):int(offs[i + 1])] for i in range(12)]

    def interleave(k, v):
        k = k.reshape(d, N_NSA_KV, HEAD_DIM)
        v = v.reshape(d, N_NSA_KV, HEAD_DIM)
        return jnp.stack([k, v], axis=2).reshape(d, N_NSA_KV * 2 * HEAD_DIM)

    main = jnp.concatenate([nq * Q_SCALE, kcmp, vcmp, interleave(kslc, vslc), interleave(kwin, vwin),
                            fq * Q_SCALE, fk, fv], axis=1)
    assert main.shape[1] == MAIN_COLS
    misc = jnp.concatenate([ngate, ff, jnp.zeros((d, LANES - 64), w.dtype)], axis=1)
    return main.astype(BF16), misc.astype(BF16)


def _proj_kernel(x_ref, nw_ref, w_ref, wm_ref, o_ref, om_ref, *, tn):
    x = x_ref[...]
    y = x * lax.rsqrt(jnp.mean(x * x, axis=-1, keepdims=True) + NORM_EPS) * nw_ref[...]
    xn = y.astype(BF16)
    om_ref[...] = _dot(xn, wm_ref[...])
    for c in range(o_ref.shape[1] // tn):
        o_ref[:, c * tn:(c + 1) * tn] = _dot(xn, w_ref[:, c * tn:(c + 1) * tn]).astype(BF16)


def _proj(x2, norm_w, w_main, w_misc, tm=512, tn=512):
    n, d = x2.shape
    once = pl.Buffered(1)
    return pl.pallas_call(
        functools.partial(_proj_kernel, tn=tn),
        grid=(n // tm,),
        in_specs=[pl.BlockSpec((tm, d), lambda i: (i, 0)),
                  pl.BlockSpec((1, d), lambda i: (0, 0)),
                  pl.BlockSpec((d, MAIN_COLS), lambda i: (0, 0), pipeline_mode=once),
                  pl.BlockSpec((d, LANES), lambda i: (0, 0), pipeline_mode=once)],
        out_specs=[pl.BlockSpec((tm, MAIN_COLS), lambda i: (i, 0)),
                   pl.BlockSpec((tm, LANES), lambda i: (i, 0))],
        out_shape=[jax.ShapeDtypeStruct((n, MAIN_COLS), BF16),
                   jax.ShapeDtypeStruct((n, LANES), F32)],
        compiler_params=_cparams(1),
        name="proj",
    )(x2, norm_w, w_main, w_misc)


def _compress_kernel(x_ref, pea_ref, peb_ref, w1a_ref, w1b_ref, w2_ref, o_ref):
    x = x_ref[0, 0].astype(F32)
    xa = (x + pea_ref[0]).astype(BF16)
    xb = (x + peb_ref[0]).astype(BF16)
    a = _dot(xa, w1a_ref[0])
    b = _dot(xb, w1b_ref[0])
    rows = a.shape[0]
    pre = a + pltpu.roll(b, rows - 1, 0)
    hid = pre * jax.nn.sigmoid(pre)
    out = _dot(hid.astype(BF16), w2_ref[0])
    for g in range(N_NSA_KV):
        blk = out[:, g * HEAD_DIM:(g + 1) * HEAD_DIM]
        o_ref[0, 0, g] = jnp.concatenate([blk, blk], axis=1)


def _compress(xkv, pea, peb, w1a, w1b, w2):
    _, b, rows, width = xkv.shape
    hid = N_NSA_KV * CMP_HIDDEN
    return pl.pallas_call(
        _compress_kernel,
        grid=(2, b),
        in_specs=[pl.BlockSpec((1, 1, rows, width), lambda s, i: (s, i, 0, 0)),
                  pl.BlockSpec((1, 1, width), lambda s, i: (s, 0, 0)),
                  pl.BlockSpec((1, 1, width), lambda s, i: (s, 0, 0)),
                  pl.BlockSpec((1, width, hid), lambda s, i: (s, 0, 0)),
                  pl.BlockSpec((1, width, hid), lambda s, i: (s, 0, 0)),
                  pl.BlockSpec((1, hid, N_NSA_KV * HEAD_DIM), lambda s, i: (s, 0, 0))],
        out_specs=pl.BlockSpec((1, 1, N_NSA_KV, rows, LANES), lambda s, i: (s, i, 0, 0, 0)),
        out_shape=jax.ShapeDtypeStruct((2, b, N_NSA_KV, rows, LANES), F32),
        compiler_params=_cparams(2),
        name="compress",
    )(xkv, pea, peb, w1a, w1b, w2)


def _foxprep_kernel(misc_ref, fb_ref, tri_ref, eye_ref, c_ref, ct_ref):
    z = misc_ref[0][:, MISC_FF:MISC_FF + N_FOX_HEADS] + fb_ref[...]
    logf = (jnp.minimum(z, 0.0) - jnp.log(1.0 + jnp.exp(-jnp.abs(z)))) * LOG2E
    tri = tri_ref[...]
    c = None
    for part in _split3(logf):
        term = _dot(tri, part)
        c = term if c is None else c + term
    ct = None
    for part in _split3(c):
        term = _nt(eye_ref[...], part)
        ct = term if ct is None else ct + term
    for p in range(N_FOX_HEADS // 2):
        c_ref[0, p] = c[:, 2 * p:2 * p + 2]
        ct_ref[0, p] = ct[2 * p:2 * p + 2, :]


def _foxprep(misc3, fb, tri, eye):
    b, t, _ = misc3.shape
    hp = N_FOX_HEADS // 2
    return pl.pallas_call(
        _foxprep_kernel,
        grid=(b,),
        in_specs=[pl.BlockSpec((1, t, LANES), lambda i: (i, 0, 0)),
                  pl.BlockSpec((1, N_FOX_HEADS), lambda i: (0, 0)),
                  pl.BlockSpec((t, t), lambda i: (0, 0)),
                  pl.BlockSpec((N_FOX_HEADS, N_FOX_HEADS), lambda i: (0, 0))],
        out_specs=[pl.BlockSpec((1, hp, t, 2), lambda i: (i, 0, 0, 0)),
                   pl.BlockSpec((1, hp, 2, t), lambda i: (i, 0, 0, 0))],
        out_shape=[jax.ShapeDtypeStruct((b, hp, t, 2), F32),
                   jax.ShapeDtypeStruct((b, hp, 2, t), F32)],
        compiler_params=_cparams(1),
        name="foxprep",
    )(misc3, fb, tri, eye)


def _cmpsel_kernel(q_ref, kvc_ref, bias_ref, gate_ref, impm_ref, eye_ref, o_ref, sel_ref):
    tq = q_ref.shape[1]
    ncp = kvc_ref.shape[3]
    ns = impm_ref.shape[0]
    qw = NSA_GQA * HEAD_DIM
    t0 = pl.program_id(2) * tq
    rows = NSA_GQA * tq
    t_col = t0 + (lax.broadcasted_iota(jnp.int32, (rows, ncp), 0) & (tq - 1))
    c_row = lax.broadcasted_iota(jnp.int32, (rows, ncp), 1)
    valid = t_col >= c_row * CMP_STRIDE + (CMP_BLOCK - 1)
    blk = lax.broadcasted_iota(jnp.int32, (ns, tq), 0)
    t_row = t0 + lax.broadcasted_iota(jnp.int32, (ns, tq), 1)
    cur = t_row // SLC_BLOCK
    bonus = jnp.where((blk == 0) | (blk == cur) | (blk == cur - 1), FORCE_BONUS, 0.0)
    blk_valid = blk * SLC_BLOCK <= t_row
    pad_row = jnp.where(lax.broadcasted_iota(jnp.int32, (HEAD_DIM - MASK_BLOCK_LANES, tq), 0) == 0, 1.0, 0.0)
    q_all = q_ref[0]

    for c in range(kvc_ref.shape[2]):
        kc = kvc_ref[0, 0, c].astype(BF16)
        vc = kvc_ref[1, 0, c].astype(BF16)
        q = q_all[:, c * qw:(c + 1) * qw]
        qst = jnp.concatenate([q[:, r * HEAD_DIM:(r + 1) * HEAD_DIM] for r in range(NSA_GQA)], axis=0)
        s = _nt(qst, kc) + bias_ref[c].reshape(rows, ncp)
        s = jnp.where(valid, s, NEG_INF)
        m = jnp.max(s, axis=-1, keepdims=True)
        e = jnp.where(valid, jnp.exp2(s - m), 0.0)
        p = e / jnp.maximum(jnp.sum(e, axis=-1, keepdims=True), 1e-30)
        o = _dot(p.astype(BF16), vc)
        gate = jax.nn.sigmoid(gate_ref[0, c])
        p_grp = jnp.zeros((tq, ncp), F32)
        for r in range(NSA_GQA):
            col = (c * NSA_GQA + r) * HEAD_DIM
            o_ref[0, :, col:col + HEAD_DIM] = (o[r * tq:(r + 1) * tq] * gate[:, 3 * r:3 * r + 1]).astype(o_ref.dtype)
            p_grp = p_grp + p[r * tq:(r + 1) * tq]

        imp = None
        for part in _split3(p_grp):
            term = _nt(impm_ref[...], part)
            imp = term if imp is None else imp + term
        score = jnp.where(blk_valid, imp + bonus, NEG_INF)
        rank = jnp.zeros((ns, tq), F32)
        for m_blk in range(ns):
            other = score[m_blk:m_blk + 1, :]
            ahead = (other > score) | ((other == score) & (blk > m_blk))
            rank = rank + jnp.where(ahead, 1.0, 0.0)
        unsel = jnp.where(rank < float(min(SLC_TOP_N, ns)), 0.0, 1.0)
        parts = [unsel, pad_row]
        if ns < MASK_BLOCK_LANES:
            parts.insert(1, jnp.zeros((MASK_BLOCK_LANES - ns, tq), F32))
        flags = _nt(eye_ref[...], jnp.concatenate(parts, axis=0).astype(BF16))
        sel_ref[0, c] = (flags * NEG_INF).astype(BF16)


def _cmpsel(main3, kvc, bias_c, gates, impm, eye):
    b, t, _ = main3.shape
    ncp = kvc.shape[3]
    ns = impm.shape[0]
    tq = AT_TQ
    qw = NSA_GQA * HEAD_DIM
    gs = NSA_GROUPS_PER_STEP
    return pl.pallas_call(
        _cmpsel_kernel,
        grid=(b, N_NSA_KV // gs, t // tq),
        in_specs=[pl.BlockSpec((1, tq, gs * qw), lambda i, g, j: (i, j, g)),
                  pl.BlockSpec((2, 1, gs, ncp, HEAD_DIM), lambda i, g, j: (0, i, g, 0, 0)),
                  pl.BlockSpec((gs, NSA_GQA, tq, ncp), lambda i, g, j: (g, 0, j, 0)),
                  pl.BlockSpec((1, gs, tq, 3 * NSA_GQA), lambda i, g, j: (i, g, j, 0)),
                  pl.BlockSpec((ns, ncp), lambda i, g, j: (0, 0)),
                  pl.BlockSpec((tq, tq), lambda i, g, j: (0, 0))],
        out_specs=[pl.BlockSpec((1, tq, gs * qw), lambda i, g, j: (i, j, g)),
                   pl.BlockSpec((1, gs, tq, HEAD_DIM), lambda i, g, j: (i, g, j, 0))],
        out_shape=[jax.ShapeDtypeStruct((b, t, N_NSA_HEADS * HEAD_DIM), BF16),
                   jax.ShapeDtypeStruct((b, N_NSA_KV, t, HEAD_DIM), BF16)],
        compiler_params=_cparams(3),
        name="cmpsel",
    )(main3, kvc, bias_c, gates, impm, eye)


def _softmax_first(s, v):
    m = jnp.max(s, axis=-1, keepdims=True)
    p = jnp.exp2(s - m)
    return m, jnp.sum(p, axis=-1, keepdims=True), _dot(p.astype(BF16), v)


def _softmax_next(s, v, m, l, acc):
    m_new = jnp.maximum(m, jnp.max(s, axis=-1, keepdims=True))
    alpha = jnp.exp2(m - m_new)
    p = jnp.exp2(s - m_new)
    return m_new, alpha * l + jnp.sum(p, axis=-1, keepdims=True), alpha * acc + _dot(p.astype(BF16), v)


def _stack_heads(q, tail, n_heads):
    return jnp.concatenate(
        [jnp.concatenate([q[:, r * HEAD_DIM:(r + 1) * HEAD_DIM], tail], axis=1) for r in range(n_heads)], axis=0)


def _fill_key_scratch(kp_sc, kvp_sc, kv, with_blocks):
    t = kv.shape[0]
    lane = lax.broadcasted_iota(jnp.int32, (t, LANES), 1)
    if with_blocks:
        blk = lax.shift_right_logical(lax.broadcasted_iota(jnp.int32, (t, LANES), 0), SLC_SHIFT)
        aug = jnp.where(lane - HEAD_DIM == blk, 1.0, 0.0).astype(BF16)
    else:
        aug = jnp.zeros((t, LANES), BF16)
    kp_sc[KEY_PAD:, :] = jnp.where(lane < HEAD_DIM, kv, aug)
    lane_p = lax.broadcasted_iota(jnp.int32, (KEY_PAD, LANES), 1)
    kp_sc[0:KEY_PAD, :] = jnp.where(lane_p == PAD_LANE, 1.0, 0.0).astype(BF16)
    kvp_sc[KEY_PAD:, :] = jnp.where(lane < HEAD_DIM, jnp.where(lane == 0, 1.0, 0.0).astype(BF16), kv)
    kvp_sc[0:KEY_PAD, :] = jnp.zeros((KEY_PAD, LANES), BF16)


def _nsa_first(s, v):
    m = jnp.max(s, axis=-1, keepdims=True)
    return m, _dot(jnp.exp2(s - m).astype(BF16), v)


def _nsa_next(s, v, m, acc):
    m_new = jnp.maximum(m, jnp.max(s, axis=-1, keepdims=True))
    return m_new, jnp.exp2(m - m_new) * acc + _dot(jnp.exp2(s - m_new).astype(BF16), v)


def _store_gated(o_ref, group, acc, gate, branch):
    tq = o_ref.shape[1]
    o = acc[:, HEAD_DIM:] / acc[:, 0:1]
    for r in range(NSA_GQA):
        col = (group * NSA_GQA + r) * HEAD_DIM
        gcol = 3 * r + branch
        o_ref[0, :, col:col + HEAD_DIM] = (o[r * tq:(r + 1) * tq] * gate[:, gcol:gcol + 1]).astype(o_ref.dtype)


def _slc_kernel(q_ref, kv_ref, sel_ref, nbias_ref, gate_ref, o_ref, kp_sc, kvp_sc):
    tq = q_ref.shape[1]
    chunk = SLC_CHUNK
    qt = pl.program_id(2)
    t0 = qt * tq

    groups = range(kp_sc.shape[0])
    qw = NSA_GQA * HEAD_DIM

    @pl.when(qt == 0)
    def _():
        for c in groups:
            _fill_key_scratch(kp_sc.at[c], kvp_sc.at[c], kv_ref[0, :, c * LANES:(c + 1) * LANES], True)

    q = q_ref[0]
    qst = [_stack_heads(q[:, c * qw:(c + 1) * qw], sel_ref[0, c], NSA_GQA) for c in groups]

    near = pl.multiple_of(t0 + tq - chunk + KEY_PAD, tq)
    state = tuple(_nsa_first(_nt(qst[c], kp_sc[c, pl.ds(near, chunk), :]) + nbias_ref[c],
                             kvp_sc[c, pl.ds(near, chunk), :]) for c in groups)

    def far(j, carry):
        first = pl.multiple_of(near - (j + 1) * chunk, tq)
        return tuple(_nsa_next(_nt(qst[c], kp_sc[c, pl.ds(first, chunk), :]),
                               kvp_sc[c, pl.ds(first, chunk), :], *carry[c]) for c in groups)

    state = lax.fori_loop(0, (t0 + tq - 1) // chunk, far, state)
    for c in groups:
        _store_gated(o_ref, c, state[c][1], jax.nn.sigmoid(gate_ref[0, c]), 1)


def _slc(main3, sel, nbias, gates):
    b, t, _ = main3.shape
    tq = AT_TQ
    qw = NSA_GQA * HEAD_DIM
    assert KEY_PAD >= SLC_CHUNK and KEY_PAD % tq == 0
    gs = NSA_GROUPS_PER_STEP
    kvw = gs * LANES
    return pl.pallas_call(
        _slc_kernel,
        grid=(b, N_NSA_KV // gs, t // tq),
        in_specs=[pl.BlockSpec((1, tq, gs * qw), lambda i, g, j: (i, j, g)),
                  pl.BlockSpec((1, t, kvw), lambda i, g, j: (i, 0, COL_SLC // kvw + g)),
                  pl.BlockSpec((1, gs, tq, HEAD_DIM), lambda i, g, j: (i, g, j, 0)),
                  pl.BlockSpec((gs, NSA_GQA * tq, SLC_CHUNK), lambda i, g, j: (g, 0, 0)),
                  pl.BlockSpec((1, gs, tq, 3 * NSA_GQA), lambda i, g, j: (i, g, j, 0))],
        out_specs=pl.BlockSpec((1, tq, gs * qw), lambda i, g, j: (i, j, g)),
        out_shape=jax.ShapeDtypeStruct((b, t, N_NSA_HEADS * HEAD_DIM), BF16),
        scratch_shapes=[pltpu.VMEM((gs, t + KEY_PAD, LANES), BF16), pltpu.VMEM((gs, t + KEY_PAD, LANES), BF16)],
        compiler_params=_cparams(3),
        name="slc",
    )(main3, main3, sel, nbias, gates)


def _win_kernel(q_ref, kv_ref, bias_ref, gate_ref, o_ref, kp_sc, kvp_sc):
    tq = q_ref.shape[1]
    span = bias_ref.shape[2]
    qt = pl.program_id(2)

    groups = range(kp_sc.shape[0])
    qw = NSA_GQA * HEAD_DIM

    @pl.when(qt == 0)
    def _():
        for c in groups:
            _fill_key_scratch(kp_sc.at[c], kvp_sc.at[c], kv_ref[0, :, c * LANES:(c + 1) * LANES], False)

    tail = jnp.where(lax.broadcasted_iota(jnp.int32, (tq, HEAD_DIM), 1) == PAD_LANE - HEAD_DIM, NEG_INF, 0.0)
    tail = tail.astype(BF16)
    q = q_ref[0]
    first = pl.multiple_of(qt * tq, tq)
    for c in groups:
        qst = _stack_heads(q[:, c * qw:(c + 1) * qw], tail, NSA_GQA)
        s = _nt(qst, kp_sc[c, pl.ds(first, span), :]) + bias_ref[c]
        _, acc = _nsa_first(s, kvp_sc[c, pl.ds(first, span), :])
        _store_gated(o_ref, c, acc, jax.nn.sigmoid(gate_ref[0, c]), 2)


def _win(main3, wbias, gates):
    b, t, _ = main3.shape
    tq = AT_TQ
    qw = NSA_GQA * HEAD_DIM
    span = wbias.shape[2]
    assert span - tq == KEY_PAD
    gs = NSA_GROUPS_PER_STEP
    kvw = gs * LANES
    return pl.pallas_call(
        _win_kernel,
        grid=(b, N_NSA_KV // gs, t // tq),
        in_specs=[pl.BlockSpec((1, tq, gs * qw), lambda i, g, j: (i, j, g)),
                  pl.BlockSpec((1, t, kvw), lambda i, g, j: (i, 0, COL_WIN // kvw + g)),
                  pl.BlockSpec((gs, NSA_GQA * tq, span), lambda i, g, j: (g, 0, 0)),
                  pl.BlockSpec((1, gs, tq, 3 * NSA_GQA), lambda i, g, j: (i, g, j, 0))],
        out_specs=pl.BlockSpec((1, tq, gs * qw), lambda i, g, j: (i, j, g)),
        out_shape=jax.ShapeDtypeStruct((b, t, N_NSA_HEADS * HEAD_DIM), BF16),
        scratch_shapes=[pltpu.VMEM((gs, t + KEY_PAD, LANES), BF16), pltpu.VMEM((gs, t + KEY_PAD, LANES), BF16)],
        compiler_params=_cparams(3),
        name="win",
    )(main3, main3, wbias, gates)


def _nsa_kernel(q_ref, kvc_ref, cbias_ref, gate_ref, impm_ref, eye_ref, skv_ref, nbias_ref, wkv_ref, wbias_ref,
                o_ref, skp_sc, skvp_sc, wkp_sc, wkvp_sc):
    tq = q_ref.shape[1]
    ncp = kvc_ref.shape[3]
    ns = impm_ref.shape[0]
    chunk = SLC_CHUNK
    span = wbias_ref.shape[2]
    qw = NSA_GQA * HEAD_DIM
    rows = NSA_GQA * tq
    qt = pl.program_id(2)
    t0 = qt * tq
    groups = range(skp_sc.shape[0])

    @pl.when(qt == 0)
    def _():
        for c in groups:
            _fill_key_scratch(skp_sc.at[c], skvp_sc.at[c], skv_ref[0, :, c * LANES:(c + 1) * LANES], True)
            _fill_key_scratch(wkp_sc.at[c], wkvp_sc.at[c], wkv_ref[0, :, c * LANES:(c + 1) * LANES], False)

    t_col = t0 + (lax.broadcasted_iota(jnp.int32, (rows, ncp), 0) & (tq - 1))
    c_row = lax.broadcasted_iota(jnp.int32, (rows, ncp), 1)
    valid = t_col >= c_row * CMP_STRIDE + (CMP_BLOCK - 1)
    blk = lax.broadcasted_iota(jnp.int32, (ns, tq), 0)
    t_row = t0 + lax.broadcasted_iota(jnp.int32, (ns, tq), 1)
    cur = t_row // SLC_BLOCK
    bonus = jnp.where((blk == 0) | (blk == cur) | (blk == cur - 1), FORCE_BONUS, 0.0)
    blk_valid = blk * SLC_BLOCK <= t_row
    pad_row = jnp.where(lax.broadcasted_iota(jnp.int32, (HEAD_DIM - MASK_BLOCK_LANES, tq), 0) == 0, 1.0, 0.0)
    lane = lax.broadcasted_iota(jnp.int32, (rows, LANES), 1)
    upper = lane >= HEAD_DIM
    win_tail = jnp.where(lane == PAD_LANE, NEG_INF, 0.0).astype(BF16)
    zero_tail = jnp.zeros((tq, HEAD_DIM), BF16)
    q_all = q_ref[0]

    qst, sel_tail, o_cmp = [], [], []
    for c in groups:
        qs = _stack_heads(q_all[:, c * qw:(c + 1) * qw], zero_tail, NSA_GQA)
        kc = kvc_ref[0, 0, c].astype(BF16)
        vc = kvc_ref[1, 0, c].astype(BF16)
        s = jnp.where(valid, _nt(qs, kc) + cbias_ref[c].reshape(rows, ncp), NEG_INF)
        m = jnp.max(s, axis=-1, keepdims=True)
        e = jnp.where(valid, jnp.exp2(s - m), 0.0)
        p = e / jnp.maximum(jnp.sum(e, axis=-1, keepdims=True), 1e-30)
        o_cmp.append(_dot(p.astype(BF16), vc))
        p_grp = p[0:tq]
        for r in range(1, NSA_GQA):
            p_grp = p_grp + p[r * tq:(r + 1) * tq]

        imp = None
        for part in _split3(p_grp):
            term = _nt(impm_ref[...], part)
            imp = term if imp is None else imp + term
        score = jnp.where(blk_valid, imp + bonus, NEG_INF)
        rank = jnp.zeros((ns, tq), F32)
        for m_blk in range(ns):
            other = score[m_blk:m_blk + 1, :]
            ahead = (other > score) | ((other == score) & (blk > m_blk))
            rank = rank + jnp.where(ahead, 1.0, 0.0)
        parts = [jnp.where(rank < float(min(SLC_TOP_N, ns)), 0.0, 1.0), pad_row]
        if ns < MASK_BLOCK_LANES:
            parts.insert(1, jnp.zeros((MASK_BLOCK_LANES - ns, tq), F32))
        flags = _nt(eye_ref[...], jnp.concatenate(parts, axis=0).astype(BF16))
        tail = jnp.concatenate([zero_tail, (flags * NEG_INF).astype(BF16)], axis=1)
        qst.append(qs)
        sel_tail.append(jnp.concatenate([tail] * NSA_GQA, axis=0))

    qsel = [jnp.where(upper, sel_tail[c], qst[c]) for c in groups]
    near = pl.multiple_of(t0 + tq - chunk + KEY_PAD, tq)
    state = tuple(_nsa_first(_nt(qsel[c], skp_sc[c, pl.ds(near, chunk), :]) + nbias_ref[c],
                             skvp_sc[c, pl.ds(near, chunk), :]) for c in groups)

    def far(j, carry):
        first = pl.multiple_of(near - (j + 1) * chunk, tq)
        return tuple(_nsa_next(_nt(qsel[c], skp_sc[c, pl.ds(first, chunk), :]),
                               skvp_sc[c, pl.ds(first, chunk), :], *carry[c]) for c in groups)

    state = lax.fori_loop(0, (t0 + tq - 1) // chunk, far, state)

    wfirst = pl.multiple_of(t0, tq)
    for c in groups:
        qwin = jnp.where(upper, win_tail, qst[c])
        _, acc_w = _nsa_first(_nt(qwin, wkp_sc[c, pl.ds(wfirst, span), :]) + wbias_ref[c],
                              wkvp_sc[c, pl.ds(wfirst, span), :])
        acc_s = state[c][1]
        gate = jax.nn.sigmoid(gate_ref[0, c])
        g = [jnp.concatenate([gate[:, 3 * r + j:3 * r + j + 1] for r in range(NSA_GQA)], axis=0) for j in range(3)]
        tot = o_cmp[c] * g[0] + acc_s * (g[1] / acc_s[:, 0:1]) + acc_w * (g[2] / acc_w[:, 0:1])
        out = tot[:, HEAD_DIM:].astype(o_ref.dtype)
        for r in range(NSA_GQA):
            col = (c * NSA_GQA + r) * HEAD_DIM
            o_ref[0, :, col:col + HEAD_DIM] = out[r * tq:(r + 1) * tq]


def _nsa(main3, kvc, bias_c, gates, impm, eye, nbias, wbias):
    b, t, _ = main3.shape
    ncp = kvc.shape[3]
    ns = impm.shape[0]
    tq = AT_TQ
    qw = NSA_GQA * HEAD_DIM
    gs = NSA_GROUPS_PER_STEP
    kvw = gs * LANES
    span = wbias.shape[2]
    assert KEY_PAD >= SLC_CHUNK and KEY_PAD % tq == 0 and span - tq == KEY_PAD
    key_scratch = pltpu.VMEM((gs, t + KEY_PAD, LANES), BF16)
    return pl.pallas_call(
        _nsa_kernel,
        grid=(b, N_NSA_KV // gs, t // tq),
        in_specs=[pl.BlockSpec((1, tq, gs * qw), lambda i, g, j: (i, j, g)),
                  pl.BlockSpec((2, 1, gs, ncp, LANES), lambda i, g, j: (0, i, g, 0, 0)),
                  pl.BlockSpec((gs, NSA_GQA, tq, ncp), lambda i, g, j: (g, 0, j, 0)),
                  pl.BlockSpec((1, gs, tq, 3 * NSA_GQA), lambda i, g, j: (i, g, j, 0)),
                  pl.BlockSpec((ns, ncp), lambda i, g, j: (0, 0)),
                  pl.BlockSpec((tq, tq), lambda i, g, j: (0, 0)),
                  pl.BlockSpec((1, t, kvw), lambda i, g, j: (i, 0, COL_SLC // kvw + g)),
                  pl.BlockSpec((gs, NSA_GQA * tq, SLC_CHUNK), lambda i, g, j: (g, 0, 0)),
                  pl.BlockSpec((1, t, kvw), lambda i, g, j: (i, 0, COL_WIN // kvw + g)),
                  pl.BlockSpec((gs, NSA_GQA * tq, span), lambda i, g, j: (g, 0, 0))],
        out_specs=pl.BlockSpec((1, tq, gs * qw), lambda i, g, j: (i, j, g)),
        out_shape=jax.ShapeDtypeStruct((b, t, N_NSA_HEADS * HEAD_DIM), BF16),
        scratch_shapes=[key_scratch, key_scratch, key_scratch, key_scratch],
        compiler_params=_cparams(3),
        name="nsa",
    )(main3, kvc, bias_c, gates, impm, eye, main3, nbias, main3, wbias)


def _fox_kernel(q_ref, k_ref, v_ref, c_ref, ct_ref, o_ref, *, chunk):
    tq = q_ref.shape[1]
    t0 = pl.program_id(2) * tq
    pairs = range(c_ref.shape[1])
    lane = lax.broadcasted_iota(jnp.int32, (tq, LANES), 1)
    low = lane < HEAD_DIM
    q_all = q_ref[0]
    zero = jnp.zeros((tq, LANES), BF16)
    qst, c_col = [], []
    for p in pairs:
        q = q_all[:, p * LANES:(p + 1) * LANES]
        qst.append(jnp.concatenate([jnp.where(low, q, zero), jnp.where(low, zero, q)], axis=0))
        c_col.append(c_ref[0, p])

    def logits(p, start):
        s = _nt(qst[p], k_ref[0, pl.ds(start, chunk), p * LANES:(p + 1) * LANES])
        c_row = ct_ref[0, p, :, pl.ds(start, chunk)]
        return jnp.concatenate([s[:tq] + c_col[p][:, 0:1] - c_row[0:1],
                                s[tq:] + c_col[p][:, 1:2] - c_row[1:2]], axis=0)

    def values(p, start):
        return v_ref[0, pl.ds(start, chunk), p * LANES:(p + 1) * LANES]

    d0 = pl.multiple_of((t0 // chunk) * chunk, chunk)
    row = lax.broadcasted_iota(jnp.int32, (2 * tq, chunk), 0) & (tq - 1)
    causal = row - lax.broadcasted_iota(jnp.int32, (2 * tq, chunk), 1) + (t0 - d0) >= 0
    state = tuple(_softmax_first(jnp.where(causal, logits(p, d0), NEG_INF), values(p, d0)) for p in pairs)

    def below(j, carry):
        start = pl.multiple_of(j * chunk, chunk)
        return tuple(_softmax_next(logits(p, start), values(p, start), *carry[p]) for p in pairs)

    state = lax.fori_loop(0, t0 // chunk, below, state)
    for p in pairs:
        _, l, acc = state[p]
        o = acc / l
        o_ref[0, :, p * LANES:(p + 1) * LANES] = jnp.where(low, o[:tq], o[tq:]).astype(o_ref.dtype)


def _fox(main3, c, ct, tq=256, chunk=512):
    b, t, _ = main3.shape
    ps = FOX_PAIRS_PER_STEP
    hp = N_FOX_HEADS // 2
    w = ps * LANES
    assert chunk % tq == 0 and t % chunk == 0
    return pl.pallas_call(
        functools.partial(_fox_kernel, chunk=chunk),
        grid=(b, hp // ps, t // tq),
        in_specs=[pl.BlockSpec((1, tq, w), lambda i, p, j: (i, j, COL_FQ // w + p)),
                  pl.BlockSpec((1, t, w), lambda i, p, j: (i, 0, COL_FK // w + p)),
                  pl.BlockSpec((1, t, w), lambda i, p, j: (i, 0, COL_FV // w + p)),
                  pl.BlockSpec((1, ps, tq, 2), lambda i, p, j: (i, p, j, 0)),
                  pl.BlockSpec((1, ps, 2, t), lambda i, p, j: (i, p, 0, 0))],
        out_specs=pl.BlockSpec((1, tq, w), lambda i, p, j: (i, j, p)),
        out_shape=jax.ShapeDtypeStruct((b, t, N_FOX_HEADS * HEAD_DIM), BF16),
        compiler_params=_cparams(3),
        name="fox",
    )(main3, main3, main3, c, ct)


def _rms(x, w):
    return x * lax.rsqrt(jnp.mean(x * x, axis=-1, keepdims=True) + NORM_EPS) * w


def _to_token_rows(ref, x):
    tm, d = x.shape
    parts = d // LANES
    for a in range(parts):
        ref[pl.ds(a, tm, stride=parts), :] = x[:, a * LANES:(a + 1) * LANES]


def _from_token_rows(ref, base, tm, d, pitch):
    return jnp.concatenate([ref[pl.ds(base + a, tm, stride=pitch), :] for a in range(d // LANES)], axis=1)


GATHER_PITCH = 20


def _outproj_kernel(on_ref, of_ref, x_ref, nnw_ref, fnw_ref, wo_ref, ffw_ref,
                    wr_ref, br_ref, h_ref, hn_ref, eid_ref, wt_ref):
    mixed = jnp.concatenate([_rms(on_ref[...].astype(F32), nnw_ref[...]),
                             _rms(of_ref[...].astype(F32), fnw_ref[...])], axis=-1)
    h = x_ref[...] + _dot(mixed.astype(BF16), wo_ref[...])
    h_ref[...] = h
    hn = _rms(h, ffw_ref[...])
    _to_token_rows(hn_ref, hn)

    h_hi, h_mid, _ = _split3(hn)
    w_hi = wr_ref[0]
    w_lo = wr_ref[1]
    logits = (_dot(h_hi, w_hi) + _dot(h_mid, w_hi) + _dot(h_hi, w_lo)) + br_ref[...]
    tm = logits.shape[0]
    lane = lax.broadcasted_iota(jnp.int32, (tm, LANES), 1)
    big = jnp.int32(LANES)
    is_grp = (lane >= N_EXPERTS) & (lane < N_EXPERTS + N_GROUPS)
    glog = jnp.where(is_grp, logits, NEG_INF)
    gmax = jnp.max(glog, axis=-1, keepdims=True)
    gsel = jnp.min(jnp.where(glog == gmax, lane, big), axis=-1, keepdims=True) - N_EXPERTS
    p_gsel = 1.0 / jnp.sum(jnp.where(is_grp, jnp.exp(glog - gmax), 0.0), axis=-1, keepdims=True)
    in_grp = (lane < N_EXPERTS) & (lane // EXPERTS_PER_GROUP == gsel)
    e1 = jnp.where(in_grp, logits, NEG_INF)
    v1 = jnp.max(e1, axis=-1, keepdims=True)
    i1 = jnp.min(jnp.where(e1 == v1, lane, big), axis=-1, keepdims=True)
    e2 = jnp.where(lane == i1, NEG_INF, e1)
    v2 = jnp.max(e2, axis=-1, keepdims=True)
    i2 = jnp.min(jnp.where(e2 == v2, lane, big), axis=-1, keepdims=True)
    ex = jnp.exp(v2 - v1)
    w1 = p_gsel / (1.0 + ex)
    w2 = p_gsel * ex / (1.0 + ex)
    eid_ref[...] = jnp.where(lane == 0, i1, jnp.where(lane == 1, i2, 0))
    wt_ref[...] = jnp.where(lane == 0, w1, jnp.where(lane == 1, w2, 0.0))


def _outproj(on, of, x2, nnw, fnw, wo, ffw, wr, br, tm=512):
    n, d = x2.shape
    half = on.shape[1]
    row = lambda i: (i, 0)
    fixed = lambda i: (0, 0)
    return pl.pallas_call(
        _outproj_kernel,
        grid=(n // tm,),
        in_specs=[pl.BlockSpec((tm, half), row), pl.BlockSpec((tm, half), row),
                  pl.BlockSpec((tm, d), row),
                  pl.BlockSpec((1, half), fixed), pl.BlockSpec((1, half), fixed),
                  pl.BlockSpec((d, d), fixed), pl.BlockSpec((1, d), fixed),
                  pl.BlockSpec((2, d, LANES), lambda i: (0, 0, 0)), pl.BlockSpec((1, LANES), fixed)],
        out_specs=[pl.BlockSpec((tm, d), row), pl.BlockSpec((tm * (d // LANES), LANES), row),
                   pl.BlockSpec((tm, LANES), row), pl.BlockSpec((tm, LANES), row)],
        out_shape=[jax.ShapeDtypeStruct((n, d), F32), jax.ShapeDtypeStruct((n * (d // LANES), LANES), F32),
                   jax.ShapeDtypeStruct((n, LANES), jnp.int32), jax.ShapeDtypeStruct((n, LANES), F32)],
        compiler_params=_cparams(1),
        name="outproj",
    )(on, of, x2, nnw, fnw, wo, ffw, wr, br)


def _moe_kernel(te_ref, nt_ref, src0_ref, src1_ref, hn_hbm, ws_ref, wg_ref, wu_ref, wd_ref, y_ref,
                xbuf, sem, wg_sc, wu_sc, wd_sc):
    tm = ws_ref.shape[0]
    d = wg_sc.shape[0]
    parts = d // LANES
    i = pl.program_id(0)
    n_used = nt_ref[0]
    slot = i % 2

    def gather(src_ref, slot_):
        def row(r, carry):
            src = pl.multiple_of(src_ref[0, 0, r] * parts, parts)
            dst = (slot_ * tm + r) * GATHER_PITCH
            pltpu.make_async_copy(hn_hbm.at[pl.ds(src, parts)], xbuf.at[pl.ds(dst, parts)], sem.at[slot_]).start()
            return carry
        lax.fori_loop(0, tm, row, 0, unroll=8)

    @pl.when(i == 0)
    def _():
        gather(src0_ref, 0)

    @pl.when(i + 1 < n_used)
    def _():
        gather(src1_ref, 1 - slot)

    prev = te_ref[jnp.maximum(i - 1, 0)]

    @pl.when((i == 0) | (te_ref[i] != prev))
    def _():
        wg_sc[...] = wg_ref[0].astype(BF16)
        wu_sc[...] = wu_ref[0].astype(BF16)
        wd_sc[...] = wd_ref[0].astype(BF16)

    @pl.when(i < n_used)
    def _():
        base = slot * (tm * GATHER_PITCH)
        pltpu.make_async_copy(hn_hbm.at[pl.ds(0, tm * parts)], xbuf.at[pl.ds(base, tm * parts)], sem.at[slot]).wait()
        x = _from_token_rows(xbuf, base, tm, d, GATHER_PITCH).astype(BF16)
        gate = _dot(x, wg_sc[...])
        up = _dot(x, wu_sc[...])
        hid = gate * jax.nn.sigmoid(gate) * up
        _to_token_rows(y_ref, ws_ref[...] * _dot(hid.astype(BF16), wd_sc[...]))

    @pl.when(i >= n_used)
    def _():
        y_ref[...] = jnp.zeros(y_ref.shape, F32)


def _moe(tile_e, src_tok, n_used, hn, w_sorted, wg, wu, wd, tm):
    n_tiles = tile_e.shape[0]
    d = wg.shape[1]
    ff = wg.shape[2]
    parts = d // LANES
    src3 = src_tok.reshape(n_tiles, 1, tm)
    grid_spec = pltpu.PrefetchScalarGridSpec(
        num_scalar_prefetch=2,
        grid=(n_tiles,),
        in_specs=[pl.BlockSpec((1, 1, tm), lambda i, te, nt: (0, 0, 0), memory_space=pltpu.SMEM),
                  pl.BlockSpec((1, 1, tm), lambda i, te, nt: (jnp.minimum(i + 1, n_tiles - 1), 0, 0),
                               memory_space=pltpu.SMEM),
                  pl.BlockSpec(memory_space=pl.ANY),
                  pl.BlockSpec((tm, 1), lambda i, te, nt: (i, 0)),
                  pl.BlockSpec((1, d, ff), lambda i, te, nt: (te[i], 0, 0)),
                  pl.BlockSpec((1, d, ff), lambda i, te, nt: (te[i], 0, 0)),
                  pl.BlockSpec((1, ff, d), lambda i, te, nt: (te[i], 0, 0))],
        out_specs=pl.BlockSpec((tm * parts, LANES), lambda i, te, nt: (i, 0)),
        scratch_shapes=[pltpu.VMEM((2 * tm * GATHER_PITCH, LANES), F32),
                        pltpu.SemaphoreType.DMA((2,)),
                        pltpu.VMEM((d, ff), BF16), pltpu.VMEM((d, ff), BF16), pltpu.VMEM((ff, d), BF16)],
    )
    return pl.pallas_call(
        _moe_kernel,
        grid_spec=grid_spec,
        out_shape=jax.ShapeDtypeStruct((n_tiles * tm * parts, LANES), F32),
        compiler_params=_cparams(1),
        name="moe",
    )(tile_e, n_used, src3, src3, hn, w_sorted, wg, wu, wd)


def _combine_kernel(pos0_ref, pos1_ref, y_hbm, h_ref, fw_ref, o_ref, ybuf, sem):
    tm, d = h_ref.shape
    parts = d // LANES
    i = pl.program_id(0)
    n = pl.num_programs(0)
    slot = i % 2

    def gather(pos_ref, slot_):
        def row(r, carry):
            for k in range(2):
                src = pl.multiple_of(pos_ref[0, 0, 2 * r + k] * parts, parts)
                dst = ((slot_ * 2 + k) * tm + r) * GATHER_PITCH
                pltpu.make_async_copy(y_hbm.at[pl.ds(src, parts)], ybuf.at[pl.ds(dst, parts)], sem.at[slot_]).start()
            return carry
        lax.fori_loop(0, tm, row, 0, unroll=8)

    @pl.when(i == 0)
    def _():
        gather(pos0_ref, 0)

    @pl.when(i + 1 < n)
    def _():
        gather(pos1_ref, 1 - slot)

    rows = tm * GATHER_PITCH
    base = slot * (2 * rows)
    pltpu.make_async_copy(y_hbm.at[pl.ds(0, 2 * tm * parts)], ybuf.at[pl.ds(base, 2 * tm * parts)], sem.at[slot]).wait()
    out = h_ref[...] + (_from_token_rows(ybuf, base, tm, d, GATHER_PITCH)
                        + _from_token_rows(ybuf, base + rows, tm, d, GATHER_PITCH))
    o_ref[...] = _rms(out, fw_ref[...])


def _combine(pos, y, h, fw, tm=256):
    n, d = h.shape
    steps = n // tm
    pos3 = pos.reshape(steps, 1, 2 * tm)
    return pl.pallas_call(
        _combine_kernel,
        grid=(steps,),
        in_specs=[pl.BlockSpec((1, 1, 2 * tm), lambda i: (0, 0, 0), memory_space=pltpu.SMEM),
                  pl.BlockSpec((1, 1, 2 * tm), lambda i: (jnp.minimum(i + 1, steps - 1), 0, 0),
                               memory_space=pltpu.SMEM),
                  pl.BlockSpec(memory_space=pl.ANY),
                  pl.BlockSpec((tm, d), lambda i: (i, 0)),
                  pl.BlockSpec((1, d), lambda i: (0, 0))],
        out_specs=pl.BlockSpec((tm, d), lambda i: (i, 0)),
        out_shape=jax.ShapeDtypeStruct((n, d), F32),
        scratch_shapes=[pltpu.VMEM((2 * 2 * tm * GATHER_PITCH, LANES), F32), pltpu.SemaphoreType.DMA((2,))],
        compiler_params=_cparams(1),
        name="combine",
    )(pos3, pos3, y, h, fw)


def _biasgen_kernel(tab_ref, bm_ref, bn_ref, bw_ref, om_ref, on_ref, ow_ref):
    h = pl.program_id(0)
    far = tab_ref[REL_BUCKETS - 1, h]

    def build(b_ref, shift):
        idx = b_ref[...]
        out = jnp.full(idx.shape, NEG_INF, F32)
        for bucket in range(REL_BUCKETS):
            out = jnp.where(idx == bucket, (tab_ref[bucket, h] - shift) * LOG2E, out)
        return out

    om_ref[0] = build(bm_ref, 0.0)
    on_ref[0] = build(bn_ref, far)
    ow_ref[0] = build(bw_ref, 0.0)


def _biasgen(rel_table, bm, bn, bw):
    heads = rel_table.shape[1]
    full = lambda a: pl.BlockSpec(a.shape, lambda h: (0, 0))
    out = lambda a: pl.BlockSpec((1,) + a.shape, lambda h: (h, 0, 0))
    return pl.pallas_call(
        _biasgen_kernel,
        grid=(heads,),
        in_specs=[pl.BlockSpec(memory_space=pltpu.SMEM), full(bm), full(bn), full(bw)],
        out_specs=[out(bm), out(bn), out(bw)],
        out_shape=[jax.ShapeDtypeStruct((heads,) + a.shape, F32) for a in (bm, bn, bw)],
        compiler_params=_cparams(1),
        name="biasgen",
    )(rel_table, bm, bn, bw)


def _bias_tables(rel_table, t, ncp):
    tq = AT_TQ
    far = REL_MAX_DIST
    buckets = _bucket_table(far + 1)
    i = np.arange(tq)[:, None]

    def bucket_map(dist, ok):
        return jnp.asarray(np.where(ok, buckets[np.clip(dist, 0, far)], -1).astype(np.int32))

    step = tq // CMP_STRIDE
    u = np.arange(2 * ncp)[None, :]
    dist_m = i - CMP_STRIDE * (u - ncp) - (CMP_BLOCK - 1)
    j = np.arange(SLC_CHUNK)[None, :]
    dist_n = (SLC_CHUNK - tq) + i - j
    j = np.arange(WINDOW + tq)[None, :]
    dist_w = WINDOW + i - j
    master, nbias, wbias = _biasgen(rel_table,
                                    bucket_map(dist_m, np.ones_like(dist_m, bool)),
                                    bucket_map(dist_n, dist_n >= 0),
                                    bucket_map(dist_w, (dist_w >= 0) & (dist_w < WINDOW)))
    master = master.reshape(N_NSA_KV, NSA_GQA, tq, 2 * ncp)
    bias_c = jnp.stack([master[..., ncp - step * qt:2 * ncp - step * qt] for qt in range(t // tq)], axis=2)
    bias_c = bias_c.reshape(N_NSA_KV, NSA_GQA, t, ncp)
    nbias = nbias.reshape(N_NSA_KV, NSA_GQA * tq, SLC_CHUNK)
    wbias = wbias.reshape(N_NSA_KV, NSA_GQA * tq, WINDOW + tq)
    return bias_c, nbias, wbias


def _compress_weights(pe, w1, w2):
    half = CMP_STRIDE
    eye = jnp.eye(N_NSA_KV, dtype=F32)

    def expand_w1(w):
        w = w.reshape(half, HEAD_DIM, CMP_HIDDEN)
        return jnp.einsum("idn,gh->igdhn", w, eye).reshape(half * N_NSA_KV * HEAD_DIM, N_NSA_KV * CMP_HIDDEN)

    def expand_pe(p):
        return jnp.broadcast_to(p[:, None, :], (half, N_NSA_KV, HEAD_DIM)).reshape(1, -1)

    w1a = expand_w1(w1[:half * HEAD_DIM]).astype(BF16)
    w1b = expand_w1(w1[half * HEAD_DIM:]).astype(BF16)
    w2x = jnp.einsum("nd,gh->gnhd", w2, eye).reshape(N_NSA_KV * CMP_HIDDEN, N_NSA_KV * HEAD_DIM).astype(BF16)
    return expand_pe(pe[:half]), expand_pe(pe[half:]), w1a, w1b, w2x


def _routing_tables(eid, wts, tm):
    n = eid.shape[0]
    e_flat = eid.reshape(-1)
    onehot = (e_flat[:, None] == jnp.arange(N_EXPERTS, dtype=jnp.int32)[None, :]).astype(jnp.int32)
    csum = jnp.cumsum(onehot, axis=0)
    rank = jnp.take_along_axis(csum, e_flat[:, None], axis=1)[:, 0] - 1
    counts = csum[-1]
    padded = ((counts + tm - 1) // tm) * tm
    ends = jnp.cumsum(padded)
    starts = ends - padded
    pos = (starts[e_flat] + rank).astype(jnp.int32)
    n_tiles = (2 * n) // tm + N_EXPERTS
    rows = n_tiles * tm
    src_pair = jnp.zeros((rows,), jnp.int32).at[pos].set(jnp.arange(2 * n, dtype=jnp.int32))
    src_tok = src_pair // 2
    w_sorted = wts.reshape(-1)[src_pair]
    tile_start = jnp.arange(n_tiles, dtype=jnp.int32) * tm
    tile_e = jnp.sum((ends[None, :] <= tile_start[:, None]).astype(jnp.int32), axis=1)
    tile_e = jnp.minimum(tile_e, N_EXPERTS - 1)
    n_used = (ends[-1] // tm).astype(jnp.int32).reshape(1)
    last_e = tile_e[jnp.maximum(n_used[0] - 1, 0)]
    tile_e = jnp.where(tile_start < ends[-1], tile_e, last_e)
    return tile_e, src_tok, n_used, w_sorted.reshape(rows, 1), pos


def kernel(x, attn_norm_w, w_in, cmp_pe_k, cmp_pe_v, cmp_k_w1, cmp_k_w2, cmp_v_w1, cmp_v_w2,
           rel_bias_table, fox_forget_b, nsa_out_norm_w, fox_out_norm_w, w_out, ffn_norm_w,
           router_group_w, router_group_b, router_expert_w, router_expert_b,
           expert_w_gate, expert_w_up, expert_w_down, final_norm_w):
    b, t, d = x.shape
    n = b * t
    depth = w_in.shape[0]
    assert t % 512 == 0 and t >= WINDOW + AT_TQ and d == 2048 and t // SLC_BLOCK <= MASK_BLOCK_LANES
    rows = t // CMP_STRIDE
    bias_c, nbias, wbias = _bias_tables(rel_bias_table, t, rows)
    ns = t // SLC_BLOCK
    ratio = SLC_BLOCK // CMP_STRIDE
    span = CMP_BLOCK // CMP_STRIDE
    nc = (t - CMP_BLOCK) // CMP_STRIDE + 1
    impm = np.zeros((ns, rows), np.float32)
    for blk in range(ns):
        for a in range(ratio):
            for s in range(span):
                c = blk * ratio + a - s
                if 0 <= c < nc:
                    impm[blk, c] += 1.0
    impm = jnp.asarray(impm, BF16)
    eye_q = jnp.eye(AT_TQ, dtype=BF16)
    eye_h = jnp.eye(N_FOX_HEADS, dtype=BF16)
    tri = jnp.asarray(np.tril(np.ones((t, t), np.float32)), BF16)
    moe_tm = 256

    h = x.reshape(n, d)
    for layer in range(depth):
        w_main, w_misc = _project_weights(w_in[layer])
        main, misc = _proj(h, attn_norm_w[layer][None, :], w_main, w_misc)
        main3 = main.reshape(b, t, MAIN_COLS)
        misc3 = misc.reshape(b, t, LANES)

        xkv = jnp.stack([main3[:, :, COL_KCMP:COL_KCMP + 256], main3[:, :, COL_VCMP:COL_VCMP + 256]])
        xkv = xkv.reshape(2, b, rows, CMP_STRIDE * 256)
        pk = _compress_weights(cmp_pe_k[layer], cmp_k_w1[layer], cmp_k_w2[layer])
        pv = _compress_weights(cmp_pe_v[layer], cmp_v_w1[layer], cmp_v_w2[layer])
        kvc = _compress(xkv, *[jnp.stack([a, c]) for a, c in zip(pk, pv)])

        gates = misc3[:, :, MISC_GATE:MISC_GATE + 48].reshape(b, t, N_NSA_KV, 12).transpose(0, 2, 1, 3)
        o_nsa = _nsa(main3, kvc, bias_c, gates, impm, eye_q, nbias, wbias)

        c, ct = _foxprep(misc3, fox_forget_b[layer][None, :], tri, eye_h)
        o_fox = _fox(main3, c, ct)

        wr = jnp.concatenate([router_expert_w[layer], router_group_w[layer]], axis=1)
        wr = jnp.pad(wr, ((0, 0), (0, LANES - wr.shape[1])))
        wr_hi = wr.astype(BF16)
        wr_lo = (wr - wr_hi.astype(F32)).astype(BF16)
        br = jnp.concatenate([router_expert_b[layer], router_group_b[layer]])
        br = jnp.pad(br, (0, LANES - br.shape[0]))[None, :]
        half = N_NSA_HEADS * HEAD_DIM
        h, hn, eid, wts = _outproj(
            o_nsa.reshape(n, half), o_fox.reshape(n, half),
            h, nsa_out_norm_w[layer][None, :], fox_out_norm_w[layer][None, :],
            w_out[layer].astype(BF16), ffn_norm_w[layer][None, :], jnp.stack([wr_hi, wr_lo]), br)

        tile_e, src_tok, n_used, w_sorted, pos = _routing_tables(eid[:, :2], wts[:, :2], moe_tm)
        y = _moe(tile_e, src_tok, n_used, hn, w_sorted,
                 expert_w_gate[layer], expert_w_up[layer], expert_w_down[layer], moe_tm)
        last = layer == depth - 1
        fw = final_norm_w if last else jnp.ones((d,), F32)
        assert last, "the fused final norm assumes a single layer"
        h = _combine(pos, y, h, fw[None, :])
    return h.reshape(b, t, d)
```

```python
import functools
import math

import numpy as np
import jax
import jax.numpy as jnp
from jax import lax
from jax.experimental import pallas as pl
from jax.experimental.pallas import tpu as pltpu

F32 = jnp.float32
BF16 = jnp.bfloat16

HEAD_DIM = 64
N_NSA_HEADS = 16
N_FOX_HEADS = 16
NSA_GQA = 4
N_NSA_KV = 4
CMP_BLOCK = 32
CMP_STRIDE = 16
CMP_HIDDEN = 128
SLC_BLOCK = 64
SLC_TOP_N = 16
WINDOW = 512
REL_BUCKETS = 32
REL_MAX_DIST = 128
N_GROUPS = 4
EXPERTS_PER_GROUP = 8
N_EXPERTS = 32
EXPERT_FF = 512
NORM_EPS = 1e-6
NEG_INF = -1e30
FORCE_BONUS = 1e4
SCALE = HEAD_DIM ** -0.5
LOG2E = math.log2(math.e)
Q_SCALE = SCALE * LOG2E

LANES = 128
VMEM_LIMIT = 56 * 1024 * 1024

COL_NQ = 0
COL_KCMP = 1024
COL_VCMP = 1280
COL_SLC = 1536
COL_WIN = 2048
COL_FQ = 2560
COL_FK = 3584
COL_FV = 4608
MAIN_COLS = 5632
MISC_GATE = 0
MISC_FF = 48

AT_TQ = 128
NSA_GROUPS_PER_STEP = 4
FOX_PAIRS_PER_STEP = 4
SLC_CHUNK = 512
SLC_SHIFT = 6
assert 1 << SLC_SHIFT == SLC_BLOCK
MASK_BLOCK_LANES = 32
PAD_LANE = HEAD_DIM + MASK_BLOCK_LANES
KEY_PAD = 512


def _nt(a, b):
    return lax.dot_general(a, b, (((1,), (1,)), ((), ())), preferred_element_type=F32)


def _dot(a, b):
    return jnp.dot(a, b, preferred_element_type=F32)


def _split3(x):
    hi = x.astype(BF16)
    r = x - hi.astype(F32)
    mid = r.astype(BF16)
    r = r - mid.astype(F32)
    return hi, mid, r.astype(BF16)


def _cparams(grid_rank):
    return pltpu.CompilerParams(dimension_semantics=("arbitrary",) * grid_rank, vmem_limit_bytes=VMEM_LIMIT)


def _bucket_table(n):
    d = np.arange(n, dtype=np.int64)
    max_exact = REL_BUCKETS // 2
    rel = np.log(np.maximum(d, 1).astype(np.float64) / max_exact) / math.log(REL_MAX_DIST / max_exact)
    scaled = rel * (REL_BUCKETS - max_exact)
    frac = scaled - np.floor(scaled)
    inner = (d > max_exact) & (d < REL_MAX_DIST)
    assert np.all((frac[inner] > 1e-3) & (frac[inner] < 1 - 1e-3))
    large = np.minimum(max_exact + np.floor(scaled + 1e-6).astype(np.int64), REL_BUCKETS - 1)
    return np.where(d < max_exact, d, large).astype(np.int32)


def _project_weights(w):
    d = w.shape[0]
    sizes = [1024] + [256] * 6 + [48, 1024, 1024, 1024, 16]
    offs = np.concatenate([[0], np.cumsum(sizes)])
    nq, kcmp, vcmp, kslc, vslc, kwin, vwin, ngate, fq, fk, fv, ff = [
        w[:, int(offs[i]):int(offs[i + 1])] for i in range(12)]

    def interleave(k, v):
        k = k.reshape(d, N_NSA_KV, HEAD_DIM)
        v = v.reshape(d, N_NSA_KV, HEAD_DIM)
        return jnp.stack([k, v], axis=2).reshape(d, N_NSA_KV * 2 * HEAD_DIM)

    main = jnp.concatenate([nq * Q_SCALE, kcmp, vcmp, interleave(kslc, vslc), interleave(kwin, vwin),
                            fq * Q_SCALE, fk, fv], axis=1)
    assert main.shape[1] == MAIN_COLS
    misc = jnp.concatenate([ngate, ff, jnp.zeros((d, LANES - 64), w.dtype)], axis=1)
    return main.astype(BF16), misc.astype(BF16)


def _proj_kernel(x_ref, nw_ref, w_ref, wm_ref, o_ref, om_ref, *, tn):
    x = x_ref[...]
    y = x * lax.rsqrt(jnp.mean(x * x, axis=-1, keepdims=True) + NORM_EPS) * nw_ref[...]
    xn = y.astype(BF16)
    om_ref[...] = _dot(xn, wm_ref[...])
    for c in range(o_ref.shape[1] // tn):
        o_ref[:, c * tn:(c + 1) * tn] = _dot(xn, w_ref[:, c * tn:(c + 1) * tn]).astype(BF16)


def _proj(x2, norm_w, w_main, w_misc, tm=512, tn=512):
    n, d = x2.shape
    once = pl.Buffered(1)
    return pl.pallas_call(
        functools.partial(_proj_kernel, tn=tn),
        grid=(n // tm,),
        in_specs=[pl.BlockSpec((tm, d), lambda i: (i, 0)),
                  pl.BlockSpec((1, d), lambda i: (0, 0)),
                  pl.BlockSpec((d, MAIN_COLS), lambda i: (0, 0), pipeline_mode=once),
                  pl.BlockSpec((d, LANES), lambda i: (0, 0), pipeline_mode=once)],
        out_specs=[pl.BlockSpec((tm, MAIN_COLS), lambda i: (i, 0)),
                   pl.BlockSpec((tm, LANES), lambda i: (i, 0))],
        out_shape=[jax.ShapeDtypeStruct((n, MAIN_COLS), BF16),
                   jax.ShapeDtypeStruct((n, LANES), F32)],
        compiler_params=_cparams(1),
        name="proj",
    )(x2, norm_w, w_main, w_misc)


def _compress_kernel(x_ref, pea_ref, peb_ref, w1a_ref, w1b_ref, w2_ref, o_ref):
    x = x_ref[0, 0].astype(F32)
    xa = (x + pea_ref[0]).astype(BF16)
    xb = (x + peb_ref[0]).astype(BF16)
    a = _dot(xa, w1a_ref[0])
    b = _dot(xb, w1b_ref[0])
    rows = a.shape[0]
    pre = a + pltpu.roll(b, rows - 1, 0)
    hid = pre * jax.nn.sigmoid(pre)
    out = _dot(hid.astype(BF16), w2_ref[0])
    for g in range(N_NSA_KV):
        blk = out[:, g * HEAD_DIM:(g + 1) * HEAD_DIM]
        o_ref[0, 0, g] = jnp.concatenate([blk, blk], axis=1)


def _compress(xkv, pea, peb, w1a, w1b, w2):
    _, b, rows, width = xkv.shape
    hid = N_NSA_KV * CMP_HIDDEN
    return pl.pallas_call(
        _compress_kernel,
        grid=(2, b),
        in_specs=[pl.BlockSpec((1, 1, rows, width), lambda s, i: (s, i, 0, 0)),
                  pl.BlockSpec((1, 1, width), lambda s, i: (s, 0, 0)),
                  pl.BlockSpec((1, 1, width), lambda s, i: (s, 0, 0)),
                  pl.BlockSpec((1, width, hid), lambda s, i: (s, 0, 0)),
                  pl.BlockSpec((1, width, hid), lambda s, i: (s, 0, 0)),
                  pl.BlockSpec((1, hid, N_NSA_KV * HEAD_DIM), lambda s, i: (s, 0, 0))],
        out_specs=pl.BlockSpec((1, 1, N_NSA_KV, rows, LANES), lambda s, i: (s, i, 0, 0, 0)),
        out_shape=jax.ShapeDtypeStruct((2, b, N_NSA_KV, rows, LANES), F32),
        compiler_params=_cparams(2),
        name="compress",
    )(xkv, pea, peb, w1a, w1b, w2)


def _foxprep_kernel(misc_ref, fb_ref, tri_ref, eye_ref, c_ref, ct_ref):
    z = misc_ref[0][:, MISC_FF:MISC_FF + N_FOX_HEADS] + fb_ref[...]
    logf = (jnp.minimum(z, 0.0) - jnp.log(1.0 + jnp.exp(-jnp.abs(z)))) * LOG2E
    tri = tri_ref[...]
    c = None
    for part in _split3(logf):
        term = _dot(tri, part)
        c = term if c is None else c + term
    ct = None
    for part in _split3(c):
        term = _nt(eye_ref[...], part)
        ct = term if ct is None else ct + term
    for p in range(N_FOX_HEADS // 2):
        c_ref[0, p] = c[:, 2 * p:2 * p + 2]
        ct_ref[0, p] = ct[2 * p:2 * p + 2, :]


def _foxprep(misc3, fb, tri, eye):
    b, t, _ = misc3.shape
    hp = N_FOX_HEADS // 2
    return pl.pallas_call(
        _foxprep_kernel,
        grid=(b,),
        in_specs=[pl.BlockSpec((1, t, LANES), lambda i: (i, 0, 0)),
                  pl.BlockSpec((1, N_FOX_HEADS), lambda i: (0, 0)),
                  pl.BlockSpec((t, t), lambda i: (0, 0)),
                  pl.BlockSpec((N_FOX_HEADS, N_FOX_HEADS), lambda i: (0, 0))],
        out_specs=[pl.BlockSpec((1, hp, t, 2), lambda i: (i, 0, 0, 0)),
                   pl.BlockSpec((1, hp, 2, t), lambda i: (i, 0, 0, 0))],
        out_shape=[jax.ShapeDtypeStruct((b, hp, t, 2), F32),
                   jax.ShapeDtypeStruct((b, hp, 2, t), F32)],
        compiler_params=_cparams(1),
        name="foxprep",
    )(misc3, fb, tri, eye)


def _cmpsel_kernel(q_ref, kvc_ref, bias_ref, gate_ref, impm_ref, eye_ref, o_ref, sel_ref):
    tq = q_ref.shape[1]
    ncp = kvc_ref.shape[3]
    ns = impm_ref.shape[0]
    qw = NSA_GQA * HEAD_DIM
    t0 = pl.program_id(2) * tq
    rows = NSA_GQA * tq
    t_col = t0 + (lax.broadcasted_iota(jnp.int32, (rows, ncp), 0) & (tq - 1))
    c_row = lax.broadcasted_iota(jnp.int32, (rows, ncp), 1)
    valid = t_col >= c_row * CMP_STRIDE + (CMP_BLOCK - 1)
    blk = lax.broadcasted_iota(jnp.int32, (ns, tq), 0)
    t_row = t0 + lax.broadcasted_iota(jnp.int32, (ns, tq), 1)
    cur = t_row // SLC_BLOCK
    bonus = jnp.where((blk == 0) | (blk == cur) | (blk == cur - 1), FORCE_BONUS, 0.0)
    blk_valid = blk * SLC_BLOCK <= t_row
    pad_row = jnp.where(lax.broadcasted_iota(jnp.int32, (HEAD_DIM - MASK_BLOCK_LANES, tq), 0) == 0, 1.0, 0.0)
    q_all = q_ref[0]

    for c in range(kvc_ref.shape[2]):
        kc = kvc_ref[0, 0, c].astype(BF16)
        vc = kvc_ref[1, 0, c].astype(BF16)
        q = q_all[:, c * qw:(c + 1) * qw]
        qst = jnp.concatenate([q[:, r * HEAD_DIM:(r + 1) * HEAD_DIM] for r in range(NSA_GQA)], axis=0)
        s = _nt(qst, kc) + bias_ref[c].reshape(rows, ncp)
        s = jnp.where(valid, s, NEG_INF)
        m = jnp.max(s, axis=-1, keepdims=True)
        e = jnp.where(valid, jnp.exp2(s - m), 0.0)
        p = e / jnp.maximum(jnp.sum(e, axis=-1, keepdims=True), 1e-30)
        o = _dot(p.astype(BF16), vc)
        gate = jax.nn.sigmoid(gate_ref[0, c])
        p_grp = jnp.zeros((tq, ncp), F32)
        for r in range(NSA_GQA):
            col = (c * NSA_GQA + r) * HEAD_DIM
            o_ref[0, :, col:col + HEAD_DIM] = (o[r * tq:(r + 1) * tq] * gate[:, 3 * r:3 * r + 1]).astype(o_ref.dtype)
            p_grp = p_grp + p[r * tq:(r + 1) * tq]

        imp = None
        for part in _split3(p_grp):
            term = _nt(impm_ref[...], part)
            imp = term if imp is None else imp + term
        score = jnp.where(blk_valid, imp + bonus, NEG_INF)
        rank = jnp.zeros((ns, tq), F32)
        for m_blk in range(ns):
            other = score[m_blk:m_blk + 1, :]
            ahead = (other > score) | ((other == score) & (blk > m_blk))
            rank = rank + jnp.where(ahead, 1.0, 0.0)
        unsel = jnp.where(rank < float(min(SLC_TOP_N, ns)), 0.0, 1.0)
        parts = [unsel, pad_row]
        if ns < MASK_BLOCK_LANES:
            parts.insert(1, jnp.zeros((MASK_BLOCK_LANES - ns, tq), F32))
        flags = _nt(eye_ref[...], jnp.concatenate(parts, axis=0).astype(BF16))
        sel_ref[0, c] = (flags * NEG_INF).astype(BF16)


def _cmpsel(main3, kvc, bias_c, gates, impm, eye):
    b, t, _ = main3.shape
    ncp = kvc.shape[3]
    ns = impm.shape[0]
    tq = AT_TQ
    qw = NSA_GQA * HEAD_DIM
    gs = NSA_GROUPS_PER_STEP
    return pl.pallas_call(
        _cmpsel_kernel,
        grid=(b, N_NSA_KV // gs, t // tq),
        in_specs=[pl.BlockSpec((1, tq, gs * qw), lambda i, g, j: (i, j, g)),
                  pl.BlockSpec((2, 1, gs, ncp, HEAD_DIM), lambda i, g, j: (0, i, g, 0, 0)),
                  pl.BlockSpec((gs, NSA_GQA, tq, ncp), lambda i, g, j: (g, 0, j, 0)),
                  pl.BlockSpec((1, gs, tq, 3 * NSA_GQA), lambda i, g, j: (i, g, j, 0)),
                  pl.BlockSpec((ns, ncp), lambda i, g, j: (0, 0)),
                  pl.BlockSpec((tq, tq), lambda i, g, j: (0, 0))],
        out_specs=[pl.BlockSpec((1, tq, gs * qw), lambda i, g, j: (i, j, g)),
                   pl.BlockSpec((1, gs, tq, HEAD_DIM), lambda i, g, j: (i, g, j, 0))],
        out_shape=[jax.ShapeDtypeStruct((b, t, N_NSA_HEADS * HEAD_DIM), BF16),
                   jax.ShapeDtypeStruct((b, N_NSA_KV, t, HEAD_DIM), BF16)],
        compiler_params=_cparams(3),
        name="cmpsel",
    )(main3, kvc, bias_c, gates, impm, eye)


def _softmax_first(s, v):
    m = jnp.max(s, axis=-1, keepdims=True)
    p = jnp.exp2(s - m)
    return m, jnp.sum(p, axis=-1, keepdims=True), _dot(p.astype(BF16), v)


def _softmax_next(s, v, m, l, acc):
    m_new = jnp.maximum(m, jnp.max(s, axis=-1, keepdims=True))
    alpha = jnp.exp2(m - m_new)
    p = jnp.exp2(s - m_new)
    return m_new, alpha * l + jnp.sum(p, axis=-1, keepdims=True), alpha * acc + _dot(p.astype(BF16), v)


def _stack_heads(q, tail, n_heads):
    return jnp.concatenate(
        [jnp.concatenate([q[:, r * HEAD_DIM:(r + 1) * HEAD_DIM], tail], axis=1) for r in range(n_heads)], axis=0)


def _fill_key_scratch(kp_sc, kvp_sc, kv, with_blocks):
    t = kv.shape[0]
    lane = lax.broadcasted_iota(jnp.int32, (t, LANES), 1)
    if with_blocks:
        blk = lax.shift_right_logical(lax.broadcasted_iota(jnp.int32, (t, LANES), 0), SLC_SHIFT)
        aug = jnp.where(lane - HEAD_DIM == blk, 1.0, 0.0).astype(BF16)
    else:
        aug = jnp.zeros((t, LANES), BF16)
    kp_sc[KEY_PAD:, :] = jnp.where(lane < HEAD_DIM, kv, aug)
    lane_p = lax.broadcasted_iota(jnp.int32, (KEY_PAD, LANES), 1)
    kp_sc[0:KEY_PAD, :] = jnp.where(lane_p == PAD_LANE, 1.0, 0.0).astype(BF16)
    kvp_sc[KEY_PAD:, :] = jnp.where(lane < HEAD_DIM, jnp.where(lane == 0, 1.0, 0.0).astype(BF16), kv)
    kvp_sc[0:KEY_PAD, :] = jnp.zeros((KEY_PAD, LANES), BF16)


def _nsa_first(s, v):
    m = jnp.max(s, axis=-1, keepdims=True)
    return m, _dot(jnp.exp2(s - m).astype(BF16), v)


def _nsa_next(s, v, m, acc):
    m_new = jnp.maximum(m, jnp.max(s, axis=-1, keepdims=True))
    return m_new, jnp.exp2(m - m_new) * acc + _dot(jnp.exp2(s - m_new).astype(BF16), v)


def _store_gated(o_ref, group, acc, gate, branch):
    tq = o_ref.shape[1]
    o = acc[:, HEAD_DIM:] / acc[:, 0:1]
    for r in range(NSA_GQA):
        col = (group * NSA_GQA + r) * HEAD_DIM
        gcol = 3 * r + branch
        o_ref[0, :, col:col + HEAD_DIM] = (o[r * tq:(r + 1) * tq] * gate[:, gcol:gcol + 1]).astype(o_ref.dtype)


def _slc_kernel(q_ref, kv_ref, sel_ref, nbias_ref, gate_ref, o_ref, kp_sc, kvp_sc):
    tq = q_ref.shape[1]
    chunk = SLC_CHUNK
    qt = pl.program_id(2)
    t0 = qt * tq

    groups = range(kp_sc.shape[0])
    qw = NSA_GQA * HEAD_DIM

    @pl.when(qt == 0)
    def _():
        for c in groups:
            _fill_key_scratch(kp_sc.at[c], kvp_sc.at[c], kv_ref[0, :, c * LANES:(c + 1) * LANES], True)

    q = q_ref[0]
    qst = [_stack_heads(q[:, c * qw:(c + 1) * qw], sel_ref[0, c], NSA_GQA) for c in groups]

    near = pl.multiple_of(t0 + tq - chunk + KEY_PAD, tq)
    state = tuple(_nsa_first(_nt(qst[c], kp_sc[c, pl.ds(near, chunk), :]) + nbias_ref[c],
                             kvp_sc[c, pl.ds(near, chunk), :]) for c in groups)

    def far(j, carry):
        first = pl.multiple_of(near - (j + 1) * chunk, tq)
        return tuple(_nsa_next(_nt(qst[c], kp_sc[c, pl.ds(first, chunk), :]),
                               kvp_sc[c, pl.ds(first, chunk), :], *carry[c]) for c in groups)

    state = lax.fori_loop(0, (t0 + tq - 1) // chunk, far, state)
    for c in groups:
        _store_gated(o_ref, c, state[c][1], jax.nn.sigmoid(gate_ref[0, c]), 1)


def _slc(main3, sel, nbias, gates):
    b, t, _ = main3.shape
    tq = AT_TQ
    qw = NSA_GQA * HEAD_DIM
    assert KEY_PAD >= SLC_CHUNK and KEY_PAD % tq == 0
    gs = NSA_GROUPS_PER_STEP
    kvw = gs * LANES
    return pl.pallas_call(
        _slc_kernel,
        grid=(b, N_NSA_KV // gs, t // tq),
        in_specs=[pl.BlockSpec((1, tq, gs * qw), lambda i, g, j: (i, j, g)),
                  pl.BlockSpec((1, t, kvw), lambda i, g, j: (i, 0, COL_SLC // kvw + g)),
                  pl.BlockSpec((1, gs, tq, HEAD_DIM), lambda i, g, j: (i, g, j, 0)),
                  pl.BlockSpec((gs, NSA_GQA * tq, SLC_CHUNK), lambda i, g, j: (g, 0, 0)),
                  pl.BlockSpec((1, gs, tq, 3 * NSA_GQA), lambda i, g, j: (i, g, j, 0))],
        out_specs=pl.BlockSpec((1, tq, gs * qw), lambda i, g, j: (i, j, g)),
        out_shape=jax.ShapeDtypeStruct((b, t, N_NSA_HEADS * HEAD_DIM), BF16),
        scratch_shapes=[pltpu.VMEM((gs, t + KEY_PAD, LANES), BF16), pltpu.VMEM((gs, t + KEY_PAD, LANES), BF16)],
        compiler_params=_cparams(3),
        name="slc",
    )(main3, main3, sel, nbias, gates)


def _win_kernel(q_ref, kv_ref, bias_ref, gate_ref, o_ref, kp_sc, kvp_sc):
    tq = q_ref.shape[1]
    span = bias_ref.shape[2]
    qt = pl.program_id(2)

    groups = range(kp_sc.shape[0])
    qw = NSA_GQA * HEAD_DIM

    @pl.when(qt == 0)
    def _():
        for c in groups:
            _fill_key_scratch(kp_sc.at[c], kvp_sc.at[c], kv_ref[0, :, c * LANES:(c + 1) * LANES], False)

    tail = jnp.where(lax.broadcasted_iota(jnp.int32, (tq, HEAD_DIM), 1) == PAD_LANE - HEAD_DIM, NEG_INF, 0.0)
    tail = tail.astype(BF16)
    q = q_ref[0]
    first = pl.multiple_of(qt * tq, tq)
    for c in groups:
        qst = _stack_heads(q[:, c * qw:(c + 1) * qw], tail, NSA_GQA)
        s = _nt(qst, kp_sc[c, pl.ds(first, span), :]) + bias_ref[c]
        _, acc = _nsa_first(s, kvp_sc[c, pl.ds(first, span), :])
        _store_gated(o_ref, c, acc, jax.nn.sigmoid(gate_ref[0, c]), 2)


def _win(main3, wbias, gates):
    b, t, _ = main3.shape
    tq = AT_TQ
    qw = NSA_GQA * HEAD_DIM
    span = wbias.shape[2]
    assert span - tq == KEY_PAD
    gs = NSA_GROUPS_PER_STEP
    kvw = gs * LANES
    return pl.pallas_call(
        _win_kernel,
        grid=(b, N_NSA_KV // gs, t // tq),
        in_specs=[pl.BlockSpec((1, tq, gs * qw), lambda i, g, j: (i, j, g)),
                  pl.BlockSpec((1, t, kvw), lambda i, g, j: (i, 0, COL_WIN // kvw + g)),
                  pl.BlockSpec((gs, NSA_GQA * tq, span), lambda i, g, j: (g, 0, 0)),
                  pl.BlockSpec((1, gs, tq, 3 * NSA_GQA), lambda i, g, j: (i, g, j, 0))],
        out_specs=pl.BlockSpec((1, tq, gs * qw), lambda i, g, j: (i, j, g)),
        out_shape=jax.ShapeDtypeStruct((b, t, N_NSA_HEADS * HEAD_DIM), BF16),
        scratch_shapes=[pltpu.VMEM((gs, t + KEY_PAD, LANES), BF16), pltpu.VMEM((gs, t + KEY_PAD, LANES), BF16)],
        compiler_params=_cparams(3),
        name="win",
    )(main3, main3, wbias, gates)


def _nsa_kernel(q_ref, kvc_ref, cbias_ref, gate_ref, impm_ref, eye_ref, skv_ref, nbias_ref, wkv_ref, wbias_ref,
                o_ref, skp_sc, skvp_sc, wkp_sc, wkvp_sc):
    tq = q_ref.shape[1]
    ncp = kvc_ref.shape[3]
    ns = impm_ref.shape[0]
    chunk = SLC_CHUNK
    span = wbias_ref.shape[2]
    qw = NSA_GQA * HEAD_DIM
    rows = NSA_GQA * tq
    qt = pl.program_id(2)
    t0 = qt * tq
    groups = range(skp_sc.shape[0])

    @pl.when(qt == 0)
    def _():
        for c in groups:
            _fill_key_scratch(skp_sc.at[c], skvp_sc.at[c], skv_ref[0, :, c * LANES:(c + 1) * LANES], True)
            _fill_key_scratch(wkp_sc.at[c], wkvp_sc.at[c], wkv_ref[0, :, c * LANES:(c + 1) * LANES], False)

    t_col = t0 + (lax.broadcasted_iota(jnp.int32, (rows, ncp), 0) & (tq - 1))
    c_row = lax.broadcasted_iota(jnp.int32, (rows, ncp), 1)
    valid = t_col >= c_row * CMP_STRIDE + (CMP_BLOCK - 1)
    blk = lax.broadcasted_iota(jnp.int32, (ns, tq), 0)
    t_row = t0 + lax.broadcasted_iota(jnp.int32, (ns, tq), 1)
    cur = t_row // SLC_BLOCK
    bonus = jnp.where((blk == 0) | (blk == cur) | (blk == cur - 1), FORCE_BONUS, 0.0)
    blk_valid = blk * SLC_BLOCK <= t_row
    pad_row = jnp.where(lax.broadcasted_iota(jnp.int32, (HEAD_DIM - MASK_BLOCK_LANES, tq), 0) == 0, 1.0, 0.0)
    lane = lax.broadcasted_iota(jnp.int32, (rows, LANES), 1)
    upper = lane >= HEAD_DIM
    win_tail = jnp.where(lane == PAD_LANE, NEG_INF, 0.0).astype(BF16)
    zero_tail = jnp.zeros((tq, HEAD_DIM), BF16)
    q_all = q_ref[0]

    qst, sel_tail, o_cmp = [], [], []
    for c in groups:
        qs = _stack_heads(q_all[:, c * qw:(c + 1) * qw], zero_tail, NSA_GQA)
        kc = kvc_ref[0, 0, c].astype(BF16)
        vc = kvc_ref[1, 0, c].astype(BF16)
        s = jnp.where(valid, _nt(qs, kc) + cbias_ref[c].reshape(rows, ncp), NEG_INF)
        m = jnp.max(s, axis=-1, keepdims=True)
        e = jnp.where(valid, jnp.exp2(s - m), 0.0)
        p = e / jnp.maximum(jnp.sum(e, axis=-1, keepdims=True), 1e-30)
        o_cmp.append(_dot(p.astype(BF16), vc))
        p_grp = p[0:tq]
        for r in range(1, NSA_GQA):
            p_grp = p_grp + p[r * tq:(r + 1) * tq]

        imp = None
        for part in _split3(p_grp):
            term = _nt(impm_ref[...], part)
            imp = term if imp is None else imp + term
        score = jnp.where(blk_valid, imp + bonus, NEG_INF)
        rank = jnp.zeros((ns, tq), F32)
        for m_blk in range(ns):
            other = score[m_blk:m_blk + 1, :]
            ahead = (other > score) | ((other == score) & (blk > m_blk))
            rank = rank + jnp.where(ahead, 1.0, 0.0)
        parts = [jnp.where(rank < float(min(SLC_TOP_N, ns)), 0.0, 1.0), pad_row]
        if ns < MASK_BLOCK_LANES:
            parts.insert(1, jnp.zeros((MASK_BLOCK_LANES - ns, tq), F32))
        flags = _nt(eye_ref[...], jnp.concatenate(parts, axis=0).astype(BF16))
        tail = jnp.concatenate([zero_tail, (flags * NEG_INF).astype(BF16)], axis=1)
        qst.append(qs)
        sel_tail.append(jnp.concatenate([tail] * NSA_GQA, axis=0))

    qsel = [jnp.where(upper, sel_tail[c], qst[c]) for c in groups]
    near = pl.multiple_of(t0 + tq - chunk + KEY_PAD, tq)
    state = tuple(_nsa_first(_nt(qsel[c], skp_sc[c, pl.ds(near, chunk), :]) + nbias_ref[c],
                             skvp_sc[c, pl.ds(near, chunk), :]) for c in groups)

    def far(j, carry):
        first = pl.multiple_of(near - (j + 1) * chunk, tq)
        return tuple(_nsa_next(_nt(qsel[c], skp_sc[c, pl.ds(first, chunk), :]),
                               skvp_sc[c, pl.ds(first, chunk), :], *carry[c]) for c in groups)

    state = lax.fori_loop(0, (t0 + tq - 1) // chunk, far, state)

    wfirst = pl.multiple_of(t0, tq)
    for c in groups:
        qwin = jnp.where(upper, win_tail, qst[c])
        _, acc_w = _nsa_first(_nt(qwin, wkp_sc[c, pl.ds(wfirst, span), :]) + wbias_ref[c],
                              wkvp_sc[c, pl.ds(wfirst, span), :])
        acc_s = state[c][1]
        gate = jax.nn.sigmoid(gate_ref[0, c])
        g = [jnp.concatenate([gate[:, 3 * r + j:3 * r + j + 1] for r in range(NSA_GQA)], axis=0) for j in range(3)]
        tot = o_cmp[c] * g[0] + acc_s * (g[1] / acc_s[:, 0:1]) + acc_w * (g[2] / acc_w[:, 0:1])
        out = tot[:, HEAD_DIM:].astype(o_ref.dtype)
        for r in range(NSA_GQA):
            col = (c * NSA_GQA + r) * HEAD_DIM
            o_ref[0, :, col:col + HEAD_DIM] = out[r * tq:(r + 1) * tq]


def _nsa(main3, kvc, bias_c, gates, impm, eye, nbias, wbias):
    b, t, _ = main3.shape
    ncp = kvc.shape[3]
    ns = impm.shape[0]
    tq = AT_TQ
    qw = NSA_GQA * HEAD_DIM
    gs = NSA_GROUPS_PER_STEP
    kvw = gs * LANES
    span = wbias.shape[2]
    assert KEY_PAD >= SLC_CHUNK and KEY_PAD % tq == 0 and span - tq == KEY_PAD
    key_scratch = pltpu.VMEM((gs, t + KEY_PAD, LANES), BF16)
    return pl.pallas_call(
        _nsa_kernel,
        grid=(b, N_NSA_KV // gs, t // tq),
        in_specs=[pl.BlockSpec((1, tq, gs * qw), lambda i, g, j: (i, j, g)),
                  pl.BlockSpec((2, 1, gs, ncp, LANES), lambda i, g, j: (0, i, g, 0, 0)),
                  pl.BlockSpec((gs, NSA_GQA, tq, ncp), lambda i, g, j: (g, 0, j, 0)),
                  pl.BlockSpec((1, gs, tq, 3 * NSA_GQA), lambda i, g, j: (i, g, j, 0)),
                  pl.BlockSpec((ns, ncp), lambda i, g, j: (0, 0)),
                  pl.BlockSpec((tq, tq), lambda i, g, j: (0, 0)),
                  pl.BlockSpec((1, t, kvw), lambda i, g, j: (i, 0, COL_SLC // kvw + g)),
                  pl.BlockSpec((gs, NSA_GQA * tq, SLC_CHUNK), lambda i, g, j: (g, 0, 0)),
                  pl.BlockSpec((1, t, kvw), lambda i, g, j: (i, 0, COL_WIN // kvw + g)),
                  pl.BlockSpec((gs, NSA_GQA * tq, span), lambda i, g, j: (g, 0, 0))],
        out_specs=pl.BlockSpec((1, tq, gs * qw), lambda i, g, j: (i, j, g)),
        out_shape=jax.ShapeDtypeStruct((b, t, N_NSA_HEADS * HEAD_DIM), BF16),
        scratch_shapes=[key_scratch, key_scratch, key_scratch, key_scratch],
        compiler_params=_cparams(3),
        name="nsa",
    )(main3, kvc, bias_c, gates, impm, eye, main3, nbias, main3, wbias)


def _fox_kernel(q_ref, k_ref, v_ref, c_ref, ct_ref, o_ref, *, chunk):
    tq = q_ref.shape[1]
    t0 = pl.program_id(2) * tq
    pairs = range(c_ref.shape[1])
    lane = lax.broadcasted_iota(jnp.int32, (tq, LANES), 1)
    low = lane < HEAD_DIM
    q_all = q_ref[0]
    zero = jnp.zeros((tq, LANES), BF16)
    qst, c_col = [], []
    for p in pairs:
        q = q_all[:, p * LANES:(p + 1) * LANES]
        qst.append(jnp.concatenate([jnp.where(low, q, zero), jnp.where(low, zero, q)], axis=0))
        c_col.append(c_ref[0, p])

    def logits(p, start):
        s = _nt(qst[p], k_ref[0, pl.ds(start, chunk), p * LANES:(p + 1) * LANES])
        c_row = ct_ref[0, p, :, pl.ds(start, chunk)]
        return jnp.concatenate([s[:tq] + c_col[p][:, 0:1] - c_row[0:1],
                                s[tq:] + c_col[p][:, 1:2] - c_row[1:2]], axis=0)

    def values(p, start):
        return v_ref[0, pl.ds(start, chunk), p * LANES:(p + 1) * LANES]

    d0 = pl.multiple_of((t0 // chunk) * chunk, chunk)
    row = lax.broadcasted_iota(jnp.int32, (2 * tq, chunk), 0) & (tq - 1)
    causal = row - lax.broadcasted_iota(jnp.int32, (2 * tq, chunk), 1) + (t0 - d0) >= 0
    state = tuple(_softmax_first(jnp.where(causal, logits(p, d0), NEG_INF), values(p, d0)) for p in pairs)

    def below(j, carry):
        start = pl.multiple_of(j * chunk, chunk)
        return tuple(_softmax_next(logits(p, start), values(p, start), *carry[p]) for p in pairs)

    state = lax.fori_loop(0, t0 // chunk, below, state)
    for p in pairs:
        _, l, acc = state[p]
        o = acc / l
        o_ref[0, :, p * LANES:(p + 1) * LANES] = jnp.where(low, o[:tq], o[tq:]).astype(o_ref.dtype)


def _fox(main3, c, ct, tq=256, chunk=512):
    b, t, _ = main3.shape
    ps = FOX_PAIRS_PER_STEP
    hp = N_FOX_HEADS // 2
    w = ps * LANES
    assert chunk % tq == 0 and t % chunk == 0
    return pl.pallas_call(
        functools.partial(_fox_kernel, chunk=chunk),
        grid=(b, hp // ps, t // tq),
        in_specs=[pl.BlockSpec((1, tq, w), lambda i, p, j: (i, j, COL_FQ // w + p)),
                  pl.BlockSpec((1, t, w), lambda i, p, j: (i, 0, COL_FK // w + p)),
                  pl.BlockSpec((1, t, w), lambda i, p, j: (i, 0, COL_FV // w + p)),
                  pl.BlockSpec((1, ps, tq, 2), lambda i, p, j: (i, p, j, 0)),
                  pl.BlockSpec((1, ps, 2, t), lambda i, p, j: (i, p, 0, 0))],
        out_specs=pl.BlockSpec((1, tq, w), lambda i, p, j: (i, j, p)),
        out_shape=jax.ShapeDtypeStruct((b, t, N_FOX_HEADS * HEAD_DIM), BF16),
        compiler_params=_cparams(3),
        name="fox",
    )(main3, main3, main3, c, ct)


def _rms(x, w):
    return x * lax.rsqrt(jnp.mean(x * x, axis=-1, keepdims=True) + NORM_EPS) * w


def _to_token_rows(ref, x):
    tm, d = x.shape
    parts = d // LANES
    for a in range(parts):
        ref[pl.ds(a, tm, stride=parts), :] = x[:, a * LANES:(a + 1) * LANES]


def _from_token_rows(ref, base, tm, d, pitch):
    return jnp.concatenate([ref[pl.ds(base + a, tm, stride=pitch), :] for a in range(d // LANES)], axis=1)


GATHER_PITCH = 20


def _outproj_kernel(on_ref, of_ref, x_ref, nnw_ref, fnw_ref, wo_ref, ffw_ref,
                    wr_ref, br_ref, h_ref, hn_ref, eid_ref, wt_ref):
    mixed = jnp.concatenate([_rms(on_ref[...].astype(F32), nnw_ref[...]),
                             _rms(of_ref[...].astype(F32), fnw_ref[...])], axis=-1)
    h = x_ref[...] + _dot(mixed.astype(BF16), wo_ref[...])
    h_ref[...] = h
    hn = _rms(h, ffw_ref[...])
    _to_token_rows(hn_ref, hn)

    h_hi, h_mid, _ = _split3(hn)
    w_hi = wr_ref[0]
    w_lo = wr_ref[1]
    logits = (_dot(h_hi, w_hi) + _dot(h_mid, w_hi) + _dot(h_hi, w_lo)) + br_ref[...]
    tm = logits.shape[0]
    lane = lax.broadcasted_iota(jnp.int32, (tm, LANES), 1)
    big = jnp.int32(LANES)
    is_grp = (lane >= N_EXPERTS) & (lane < N_EXPERTS + N_GROUPS)
    glog = jnp.where(is_grp, logits, NEG_INF)
    gmax = jnp.max(glog, axis=-1, keepdims=True)
    gsel = jnp.min(jnp.where(glog == gmax, lane, big), axis=-1, keepdims=True) - N_EXPERTS
    p_gsel = 1.0 / jnp.sum(jnp.where(is_grp, jnp.exp(glog - gmax), 0.0), axis=-1, keepdims=True)
    in_grp = (lane < N_EXPERTS) & (lane // EXPERTS_PER_GROUP == gsel)
    e1 = jnp.where(in_grp, logits, NEG_INF)
    v1 = jnp.max(e1, axis=-1, keepdims=True)
    i1 = jnp.min(jnp.where(e1 == v1, lane, big), axis=-1, keepdims=True)
    e2 = jnp.where(lane == i1, NEG_INF, e1)
    v2 = jnp.max(e2, axis=-1, keepdims=True)
    i2 = jnp.min(jnp.where(e2 == v2, lane, big), axis=-1, keepdims=True)
    ex = jnp.exp(v2 - v1)
    w1 = p_gsel / (1.0 + ex)
    w2 = p_gsel * ex / (1.0 + ex)
    eid_ref[...] = jnp.where(lane == 0, i1, jnp.where(lane == 1, i2, 0))
    wt_ref[...] = jnp.where(lane == 0, w1, jnp.where(lane == 1, w2, 0.0))


def _outproj(on, of, x2, nnw, fnw, wo, ffw, wr, br, tm=512):
    n, d = x2.shape
    half = on.shape[1]
    row = lambda i: (i, 0)
    fixed = lambda i: (0, 0)
    return pl.pallas_call(
        _outproj_kernel,
        grid=(n // tm,),
        in_specs=[pl.BlockSpec((tm, half), row), pl.BlockSpec((tm, half), row),
                  pl.BlockSpec((tm, d), row),
                  pl.BlockSpec((1, half), fixed), pl.BlockSpec((1, half), fixed),
                  pl.BlockSpec((d, d), fixed), pl.BlockSpec((1, d), fixed),
                  pl.BlockSpec((2, d, LANES), lambda i: (0, 0, 0)), pl.BlockSpec((1, LANES), fixed)],
        out_specs=[pl.BlockSpec((tm, d), row), pl.BlockSpec((tm * (d // LANES), LANES), row),
                   pl.BlockSpec((tm, LANES), row), pl.BlockSpec((tm, LANES), row)],
        out_shape=[jax.ShapeDtypeStruct((n, d), F32), jax.ShapeDtypeStruct((n * (d // LANES), LANES), F32),
                   jax.ShapeDtypeStruct((n, LANES), jnp.int32), jax.ShapeDtypeStruct((n, LANES), F32)],
        compiler_params=_cparams(1),
        name="outproj",
    )(on, of, x2, nnw, fnw, wo, ffw, wr, br)


def _moe_kernel(te_ref, nt_ref, src0_ref, src1_ref, hn_hbm, ws_ref, wg_ref, wu_ref, wd_ref, y_ref,
                xbuf, sem, wg_sc, wu_sc, wd_sc):
    tm = ws_ref.shape[0]
    d = wg_sc.shape[0]
    parts = d // LANES
    i = pl.program_id(0)
    n_used = nt_ref[0]
    slot = i % 2

    def gather(src_ref, slot_):
        def row(r, carry):
            src = pl.multiple_of(src_ref[0, 0, r] * parts, parts)
            dst = (slot_ * tm + r) * GATHER_PITCH
            pltpu.make_async_copy(hn_hbm.at[pl.ds(src, parts)], xbuf.at[pl.ds(dst, parts)], sem.at[slot_]).start()
            return carry
        lax.fori_loop(0, tm, row, 0, unroll=8)

    @pl.when(i == 0)
    def _():
        gather(src0_ref, 0)

    @pl.when(i + 1 < n_used)
    def _():
        gather(src1_ref, 1 - slot)

    prev = te_ref[jnp.maximum(i - 1, 0)]

    @pl.when((i == 0) | (te_ref[i] != prev))
    def _():
        wg_sc[...] = wg_ref[0].astype(BF16)
        wu_sc[...] = wu_ref[0].astype(BF16)
        wd_sc[...] = wd_ref[0].astype(BF16)

    @pl.when(i < n_used)
    def _():
        base = slot * (tm * GATHER_PITCH)
        pltpu.make_async_copy(hn_hbm.at[pl.ds(0, tm * parts)], xbuf.at[pl.ds(base, tm * parts)], sem.at[slot]).wait()
        x = _from_token_rows(xbuf, base, tm, d, GATHER_PITCH).astype(BF16)
        gate = _dot(x, wg_sc[...])
        up = _dot(x, wu_sc[...])
        hid = gate * jax.nn.sigmoid(gate) * up
        _to_token_rows(y_ref, ws_ref[...] * _dot(hid.astype(BF16), wd_sc[...]))

    @pl.when(i >= n_used)
    def _():
        y_ref[...] = jnp.zeros(y_ref.shape, F32)


def _moe(tile_e, src_tok, n_used, hn, w_sorted, wg, wu, wd, tm):
    n_tiles = tile_e.shape[0]
    d = wg.shape[1]
    ff = wg.shape[2]
    parts = d // LANES
    src3 = src_tok.reshape(n_tiles, 1, tm)
    grid_spec = pltpu.PrefetchScalarGridSpec(
        num_scalar_prefetch=2,
        grid=(n_tiles,),
        in_specs=[pl.BlockSpec((1, 1, tm), lambda i, te, nt: (0, 0, 0), memory_space=pltpu.SMEM),
                  pl.BlockSpec((1, 1, tm), lambda i, te, nt: (jnp.minimum(i + 1, n_tiles - 1), 0, 0),
                               memory_space=pltpu.SMEM),
                  pl.BlockSpec(memory_space=pl.ANY),
                  pl.BlockSpec((tm, 1), lambda i, te, nt: (i, 0)),
                  pl.BlockSpec((1, d, ff), lambda i, te, nt: (te[i], 0, 0)),
                  pl.BlockSpec((1, d, ff), lambda i, te, nt: (te[i], 0, 0)),
                  pl.BlockSpec((1, ff, d), lambda i, te, nt: (te[i], 0, 0))],
        out_specs=pl.BlockSpec((tm * parts, LANES), lambda i, te, nt: (i, 0)),
        scratch_shapes=[pltpu.VMEM((2 * tm * GATHER_PITCH, LANES), F32),
                        pltpu.SemaphoreType.DMA((2,)),
                        pltpu.VMEM((d, ff), BF16), pltpu.VMEM((d, ff), BF16), pltpu.VMEM((ff, d), BF16)],
    )
    return pl.pallas_call(
        _moe_kernel,
        grid_spec=grid_spec,
        out_shape=jax.ShapeDtypeStruct((n_tiles * tm * parts, LANES), F32),
        compiler_params=_cparams(1),
        name="moe",
    )(tile_e, n_used, src3, src3, hn, w_sorted, wg, wu, wd)


def _combine_kernel(pos0_ref, pos1_ref, y_hbm, h_ref, fw_ref, o_ref, ybuf, sem):
    tm, d = h_ref.shape
    parts = d // LANES
    i = pl.program_id(0)
    n = pl.num_programs(0)
    slot = i % 2

    def gather(pos_ref, slot_):
        def row(r, carry):
            for k in range(2):
                src = pl.multiple_of(pos_ref[0, 0, 2 * r + k] * parts, parts)
                dst = ((slot_ * 2 + k) * tm + r) * GATHER_PITCH
                pltpu.make_async_copy(y_hbm.at[pl.ds(src, parts)], ybuf.at[pl.ds(dst, parts)], sem.at[slot_]).start()
            return carry
        lax.fori_loop(0, tm, row, 0, unroll=8)

    @pl.when(i == 0)
    def _():
        gather(pos0_ref, 0)

    @pl.when(i + 1 < n)
    def _():
        gather(pos1_ref, 1 - slot)

    rows = tm * GATHER_PITCH
    base = slot * (2 * rows)
    pltpu.make_async_copy(y_hbm.at[pl.ds(0, 2 * tm * parts)], ybuf.at[pl.ds(base, 2 * tm * parts)], sem.at[slot]).wait()
    out = h_ref[...] + (_from_token_rows(ybuf, base, tm, d, GATHER_PITCH)
                        + _from_token_rows(ybuf, base + rows, tm, d, GATHER_PITCH))
    o_ref[...] = _rms(out, fw_ref[...])


def _combine(pos, y, h, fw, tm=256):
    n, d = h.shape
    steps = n // tm
    pos3 = pos.reshape(steps, 1, 2 * tm)
    return pl.pallas_call(
        _combine_kernel,
        grid=(steps,),
        in_specs=[pl.BlockSpec((1, 1, 2 * tm), lambda i: (0, 0, 0), memory_space=pltpu.SMEM),
                  pl.BlockSpec((1, 1, 2 * tm), lambda i: (jnp.minimum(i + 1, steps - 1), 0, 0),
                               memory_space=pltpu.SMEM),
                  pl.BlockSpec(memory_space=pl.ANY),
                  pl.BlockSpec((tm, d), lambda i: (i, 0)),
                  pl.BlockSpec((1, d), lambda i: (0, 0))],
        out_specs=pl.BlockSpec((tm, d), lambda i: (i, 0)),
        out_shape=jax.ShapeDtypeStruct((n, d), F32),
        scratch_shapes=[pltpu.VMEM((2 * 2 * tm * GATHER_PITCH, LANES), F32), pltpu.SemaphoreType.DMA((2,))],
        compiler_params=_cparams(1),
        name="combine",
    )(pos3, pos3, y, h, fw)


def _biasgen_kernel(tab_ref, bm_ref, bn_ref, bw_ref, om_ref, on_ref, ow_ref):
    h = pl.program_id(0)
    far = tab_ref[REL_BUCKETS - 1, h]

    def build(b_ref, shift):
        idx = b_ref[...]
        out = jnp.full(idx.shape, NEG_INF, F32)
        for bucket in range(REL_BUCKETS):
            out = jnp.where(idx == bucket, (tab_ref[bucket, h] - shift) * LOG2E, out)
        return out

    om_ref[0] = build(bm_ref, 0.0)
    on_ref[0] = build(bn_ref, far)
    ow_ref[0] = build(bw_ref, 0.0)


def _biasgen(rel_table, bm, bn, bw):
    heads = rel_table.shape[1]
    full = lambda a: pl.BlockSpec(a.shape, lambda h: (0, 0))
    out = lambda a: pl.BlockSpec((1,) + a.shape, lambda h: (h, 0, 0))
    return pl.pallas_call(
        _biasgen_kernel,
        grid=(heads,),
        in_specs=[pl.BlockSpec(memory_space=pltpu.SMEM), full(bm), full(bn), full(bw)],
        out_specs=[out(bm), out(bn), out(bw)],
        out_shape=[jax.ShapeDtypeStruct((heads,) + a.shape, F32) for a in (bm, bn, bw)],
        compiler_params=_cparams(1),
        name="biasgen",
    )(rel_table, bm, bn, bw)


def _bias_tables(rel_table, t, ncp):
    tq = AT_TQ
    far = REL_MAX_DIST
    buckets = _bucket_table(far + 1)
    i = np.arange(tq)[:, None]

    def bucket_map(dist, ok):
        return jnp.asarray(np.where(ok, buckets[np.clip(dist, 0, far)], -1).astype(np.int32))

    step = tq // CMP_STRIDE
    u = np.arange(2 * ncp)[None, :]
    dist_m = i - CMP_STRIDE * (u - ncp) - (CMP_BLOCK - 1)
    j = np.arange(SLC_CHUNK)[None, :]
    dist_n = (SLC_CHUNK - tq) + i - j
    j = np.arange(WINDOW + tq)[None, :]
    dist_w = WINDOW + i - j
    master, nbias, wbias = _biasgen(rel_table,
                                    bucket_map(dist_m, np.ones_like(dist_m, bool)),
                                    bucket_map(dist_n, dist_n >= 0),
                                    bucket_map(dist_w, (dist_w >= 0) & (dist_w < WINDOW)))
    master = master.reshape(N_NSA_KV, NSA_GQA, tq, 2 * ncp)
    bias_c = jnp.stack([master[..., ncp - step * qt:2 * ncp - step * qt] for qt in range(t // tq)], axis=2)
    bias_c = bias_c.reshape(N_NSA_KV, NSA_GQA, t, ncp)
    nbias = nbias.reshape(N_NSA_KV, NSA_GQA * tq, SLC_CHUNK)
    wbias = wbias.reshape(N_NSA_KV, NSA_GQA * tq, WINDOW + tq)
    return bias_c, nbias, wbias


def _compress_weights(pe, w1, w2):
    half = CMP_STRIDE
    eye = jnp.eye(N_NSA_KV, dtype=F32)

    def expand_w1(w):
        w = w.reshape(half, HEAD_DIM, CMP_HIDDEN)
        return jnp.einsum("idn,gh->igdhn", w, eye).reshape(half * N_NSA_KV * HEAD_DIM, N_NSA_KV * CMP_HIDDEN)

    def expand_pe(p):
        return jnp.broadcast_to(p[:, None, :], (half, N_NSA_KV, HEAD_DIM)).reshape(1, -1)

    w1a = expand_w1(w1[:half * HEAD_DIM]).astype(BF16)
    w1b = expand_w1(w1[half * HEAD_DIM:]).astype(BF16)
    w2x = jnp.einsum("nd,gh->gnhd", w2, eye).reshape(N_NSA_KV * CMP_HIDDEN, N_NSA_KV * HEAD_DIM).astype(BF16)
    return expand_pe(pe[:half]), expand_pe(pe[half:]), w1a, w1b, w2x


def _routing_tables(eid, wts, tm):
    n = eid.shape[0]
    e_flat = eid.reshape(-1)
    onehot = (e_flat[:, None] == jnp.arange(N_EXPERTS, dtype=jnp.int32)[None, :]).astype(jnp.int32)
    csum = jnp.cumsum(onehot, axis=0)
    rank = jnp.take_along_axis(csum, e_flat[:, None], axis=1)[:, 0] - 1
    counts = csum[-1]
    padded = ((counts + tm - 1) // tm) * tm
    ends = jnp.cumsum(padded)
    starts = ends - padded
    pos = (starts[e_flat] + rank).astype(jnp.int32)
    n_tiles = (2 * n) // tm + N_EXPERTS
    rows = n_tiles * tm
    src_pair = jnp.zeros((rows,), jnp.int32).at[pos].set(jnp.arange(2 * n, dtype=jnp.int32))
    src_tok = src_pair // 2
    w_sorted = wts.reshape(-1)[src_pair]
    tile_start = jnp.arange(n_tiles, dtype=jnp.int32) * tm
    tile_e = jnp.sum((ends[None, :] <= tile_start[:, None]).astype(jnp.int32), axis=1)
    tile_e = jnp.minimum(tile_e, N_EXPERTS - 1)
    n_used = (ends[-1] // tm).astype(jnp.int32).reshape(1)
    last_e = tile_e[jnp.maximum(n_used[0] - 1, 0)]
    tile_e = jnp.where(tile_start < ends[-1], tile_e, last_e)
    return tile_e, src_tok, n_used, w_sorted.reshape(rows, 1), pos


def kernel(x, attn_norm_w, w_in, cmp_pe_k, cmp_pe_v, cmp_k_w1, cmp_k_w2, cmp_v_w1, cmp_v_w2,
           rel_bias_table, fox_forget_b, nsa_out_norm_w, fox_out_norm_w, w_out, ffn_norm_w,
           router_group_w, router_group_b, router_expert_w, router_expert_b,
           expert_w_gate, expert_w_up, expert_w_down, final_norm_w):
    b, t, d = x.shape
    n = b * t
    depth = w_in.shape[0]
    assert t % 512 == 0 and t >= WINDOW + AT_TQ and d == 2048 and t // SLC_BLOCK <= MASK_BLOCK_LANES
    rows = t // CMP_STRIDE
    bias_c, nbias, wbias = _bias_tables(rel_bias_table, t, rows)
    ns = t // SLC_BLOCK
    ratio = SLC_BLOCK // CMP_STRIDE
    span = CMP_BLOCK // CMP_STRIDE
    nc = (t - CMP_BLOCK) // CMP_STRIDE + 1
    impm = np.zeros((ns, rows), np.float32)
    for blk in range(ns):
        for a in range(ratio):
            for s in range(span):
                c = blk * ratio + a - s
                if 0 <= c < nc:
                    impm[blk, c] += 1.0
    impm = jnp.asarray(impm, BF16)
    eye_q = jnp.eye(AT_TQ, dtype=BF16)
    eye_h = jnp.eye(N_FOX_HEADS, dtype=BF16)
    tri = jnp.asarray(np.tril(np.ones((t, t), np.float32)), BF16)
    moe_tm = 256

    h = x.reshape(n, d)
    for layer in range(depth):
        w_main, w_misc = _project_weights(w_in[layer])
        main, misc = _proj(h, attn_norm_w[layer][None, :], w_main, w_misc)
        main3 = main.reshape(b, t, MAIN_COLS)
        misc3 = misc.reshape(b, t, LANES)

        xkv = jnp.stack([main3[:, :, COL_KCMP:COL_KCMP + 256], main3[:, :, COL_VCMP:COL_VCMP + 256]])
        xkv = xkv.reshape(2, b, rows, CMP_STRIDE * 256)
        pk = _compress_weights(cmp_pe_k[layer], cmp_k_w1[layer], cmp_k_w2[layer])
        pv = _compress_weights(cmp_pe_v[layer], cmp_v_w1[layer], cmp_v_w2[layer])
        kvc = _compress(xkv, *[jnp.stack([a, c]) for a, c in zip(pk, pv)])

        gates = misc3[:, :, MISC_GATE:MISC_GATE + 48].reshape(b, t, N_NSA_KV, 12).transpose(0, 2, 1, 3)
        o_nsa = _nsa(main3, kvc, bias_c, gates, impm, eye_q, nbias, wbias)

        c, ct = _foxprep(misc3, fox_forget_b[layer][None, :], tri, eye_h)
        o_fox = _fox(main3, c, ct)

        wr = jnp.concatenate([router_expert_w[layer], router_group_w[layer]], axis=1)
        wr = jnp.pad(wr, ((0, 0), (0, LANES - wr.shape[1])))
        wr_hi = wr.astype(BF16)
        wr_lo = (wr - wr_hi.astype(F32)).astype(BF16)
        br = jnp.concatenate([router_expert_b[layer], router_group_b[layer]])
        br = jnp.pad(br, (0, LANES - br.shape[0]))[None, :]
        half = N_NSA_HEADS * HEAD_DIM
        h, hn, eid, wts = _outproj(
            o_nsa.reshape(n, half), o_fox.reshape(n, half),
            h, nsa_out_norm_w[layer][None, :], fox_out_norm_w[layer][None, :],
            w_out[layer].astype(BF16), ffn_norm_w[layer][None, :], jnp.stack([wr_hi, wr_lo]), br)

        tile_e, src_tok, n_used, w_sorted, pos = _routing_tables(eid[:, :2], wts[:, :2], moe_tm)
        y = _moe(tile_e, src_tok, n_used, hn, w_sorted,
                 expert_w_gate[layer], expert_w_up[layer], expert_w_down[layer], moe_tm)
        last = layer == depth - 1
        fw = final_norm_w if last else jnp.ones((d,), F32)
        assert last, "the fused final norm assumes a single layer"
        h = _combine(pos, y, h, fw[None, :])
    return h.reshape(b, t, d)
```

```python
import functools
import math

import numpy as np
import jax
import jax.numpy as jnp
from jax import lax
from jax.experimental import pallas as pl
from jax.experimental.pallas import tpu as pltpu

F32 = jnp.float32
BF16 = jnp.bfloat16

HEAD_DIM = 64
N_NSA_HEADS = 16
N_FOX_HEADS = 16
NSA_GQA = 4
N_NSA_KV = 4
CMP_BLOCK = 32
CMP_STRIDE = 16
CMP_HIDDEN = 128
SLC_BLOCK = 64
SLC_TOP_N = 16
WINDOW = 512
REL_BUCKETS = 32
REL_MAX_DIST = 128
N_GROUPS = 4
EXPERTS_PER_GROUP = 8
N_EXPERTS = 32
EXPERT_FF = 512
NORM_EPS = 1e-6
NEG_INF = -1e30
FORCE_BONUS = 1e4
SCALE = HEAD_DIM ** -0.5
LOG2E = math.log2(math.e)
Q_SCALE = SCALE * LOG2E

LANES = 128
VMEM_LIMIT = 56 * 1024 * 1024

COL_NQ = 0
COL_KCMP = 1024
COL_VCMP = 1280
COL_SLC = 1536
COL_WIN = 2048
COL_FQ = 2560
COL_FK = 3584
COL_FV = 4608
MAIN_COLS = 5632
MISC_GATE = 0
MISC_FF = 48

AT_TQ = 128
NSA_GROUPS_PER_STEP = 4
FOX_PAIRS_PER_STEP = 4
SLC_CHUNK = 512
SLC_SHIFT = 6
assert 1 << SLC_SHIFT == SLC_BLOCK
MASK_BLOCK_LANES = 32
PAD_LANE = HEAD_DIM + MASK_BLOCK_LANES
KEY_PAD = 512


def _nt(a, b):
    return lax.dot_general(a, b, (((1,), (1,)), ((), ())), preferred_element_type=F32)


def _dot(a, b):
    return jnp.dot(a, b, preferred_element_type=F32)


def _split3(x):
    hi = x.astype(BF16)
    r = x - hi.astype(F32)
    mid = r.astype(BF16)
    r = r - mid.astype(F32)
    return hi, mid, r.astype(BF16)


def _cparams(grid_rank):
    return pltpu.CompilerParams(dimension_semantics=("arbitrary",) * grid_rank, vmem_limit_bytes=VMEM_LIMIT)


def _bucket_table(n):
    d = np.arange(n, dtype=np.int64)
    max_exact = REL_BUCKETS // 2
    rel = np.log(np.maximum(d, 1).astype(np.float64) / max_exact) / math.log(REL_MAX_DIST / max_exact)
    scaled = rel * (REL_BUCKETS - max_exact)
    frac = scaled - np.floor(scaled)
    inner = (d > max_exact) & (d < REL_MAX_DIST)
    assert np.all((frac[inner] > 1e-3) & (frac[inner] < 1 - 1e-3))
    large = np.minimum(max_exact + np.floor(scaled + 1e-6).astype(np.int64), REL_BUCKETS - 1)
    return np.where(d < max_exact, d, large).astype(np.int32)


def _project_weights(w):
    d = w.shape[0]
    sizes = [1024] + [256] * 6 + [48, 1024, 1024, 1024, 16]
    offs = np.concatenate([[0], np.cumsum(sizes)])
    nq, kcmp, vcmp, kslc, vslc, kwin, vwin, ngate, fq, fk, fv, ff = [
        w[:, int(offs[i]):int(offs[i + 1])] for i in range(12)]

    def interleave(k, v):
        k = k.reshape(d, N_NSA_KV, HEAD_DIM)
        v = v.reshape(d, N_NSA_KV, HEAD_DIM)
        return jnp.stack([k, v], axis=2).reshape(d, N_NSA_KV * 2 * HEAD_DIM)

    main = jnp.concatenate([nq * Q_SCALE, kcmp, vcmp, interleave(kslc, vslc), interleave(kwin, vwin),
                            fq * Q_SCALE, fk, fv], axis=1)
    assert main.shape[1] == MAIN_COLS
    misc = jnp.concatenate([ngate, ff, jnp.zeros((d, LANES - 64), w.dtype)], axis=1)
    return main.astype(BF16), misc.astype(BF16)


def _proj_kernel(x_ref, nw_ref, w_ref, wm_ref, o_ref, om_ref, *, tn):
    x = x_ref[...]
    y = x * lax.rsqrt(jnp.mean(x * x, axis=-1, keepdims=True) + NORM_EPS) * nw_ref[...]
    xn = y.astype(BF16)
    om_ref[...] = _dot(xn, wm_ref[...])
    for c in range(o_ref.shape[1] // tn):
        o_ref[:, c * tn:(c + 1) * tn] = _dot(xn, w_ref[:, c * tn:(c + 1) * tn]).astype(BF16)


def _proj(x2, norm_w, w_main, w_misc, tm=512, tn=512):
    n, d = x2.shape
    once = pl.Buffered(1)
    return pl.pallas_call(
        functools.partial(_proj_kernel, tn=tn),
        grid=(n // tm,),
        in_specs=[pl.BlockSpec((tm, d), lambda i: (i, 0)),
                  pl.BlockSpec((1, d), lambda i: (0, 0)),
                  pl.BlockSpec((d, MAIN_COLS), lambda i: (0, 0), pipeline_mode=once),
                  pl.BlockSpec((d, LANES), lambda i: (0, 0), pipeline_mode=once)],
        out_specs=[pl.BlockSpec((tm, MAIN_COLS), lambda i: (i, 0)),
                   pl.BlockSpec((tm, LANES), lambda i: (i, 0))],
        out_shape=[jax.ShapeDtypeStruct((n, MAIN_COLS), BF16),
                   jax.ShapeDtypeStruct((n, LANES), F32)],
        compiler_params=_cparams(1),
        name="proj",
    )(x2, norm_w, w_main, w_misc)


def _compress_kernel(x_ref, pea_ref, peb_ref, w1a_ref, w1b_ref, w2_ref, o_ref):
    x = x_ref[0, 0].astype(F32)
    xa = (x + pea_ref[0]).astype(BF16)
    xb = (x + peb_ref[0]).astype(BF16)
    a = _dot(xa, w1a_ref[0])
    b = _dot(xb, w1b_ref[0])
    rows = a.shape[0]
    pre = a + pltpu.roll(b, rows - 1, 0)
    hid = pre * jax.nn.sigmoid(pre)
    out = _dot(hid.astype(BF16), w2_ref[0])
    for g in range(N_NSA_KV):
        blk = out[:, g * HEAD_DIM:(g + 1) * HEAD_DIM]
        o_ref[0, 0, g] = jnp.concatenate([blk, blk], axis=1)


def _compress(xkv, pea, peb, w1a, w1b, w2):
    _, b, rows, width = xkv.shape
    hid = N_NSA_KV * CMP_HIDDEN
    return pl.pallas_call(
        _compress_kernel,
        grid=(2, b),
        in_specs=[pl.BlockSpec((1, 1, rows, width), lambda s, i: (s, i, 0, 0)),
                  pl.BlockSpec((1, 1, width), lambda s, i: (s, 0, 0)),
                  pl.BlockSpec((1, 1, width), lambda s, i: (s, 0, 0)),
                  pl.BlockSpec((1, width, hid), lambda s, i: (s, 0, 0)),
                  pl.BlockSpec((1, width, hid), lambda s, i: (s, 0, 0)),
                  pl.BlockSpec((1, hid, N_NSA_KV * HEAD_DIM), lambda s, i: (s, 0, 0))],
        out_specs=pl.BlockSpec((1, 1, N_NSA_KV, rows, LANES), lambda s, i: (s, i, 0, 0, 0)),
        out_shape=jax.ShapeDtypeStruct((2, b, N_NSA_KV, rows, LANES), F32),
        compiler_params=_cparams(2),
        name="compress",
    )(xkv, pea, peb, w1a, w1b, w2)


def _foxprep_kernel(misc_ref, fb_ref, tri_ref, eye_ref, c_ref, ct_ref):
    z = misc_ref[0][:, MISC_FF:MISC_FF + N_FOX_HEADS] + fb_ref[...]
    logf = (jnp.minimum(z, 0.0) - jnp.log(1.0 + jnp.exp(-jnp.abs(z)))) * LOG2E
    tri = tri_ref[...]
    c = None
    for part in _split3(logf):
        term = _dot(tri, part)
        c = term if c is None else c + term
    ct = None
    for part in _split3(c):
        term = _nt(eye_ref[...], part)
        ct = term if ct is None else ct + term
    for p in range(N_FOX_HEADS // 2):
        c_ref[0, p] = c[:, 2 * p:2 * p + 2]
        ct_ref[0, p] = ct[2 * p:2 * p + 2, :]


def _foxprep(misc3, fb, tri, eye):
    b, t, _ = misc3.shape
    hp = N_FOX_HEADS // 2
    return pl.pallas_call(
        _foxprep_kernel,
        grid=(b,),
        in_specs=[pl.BlockSpec((1, t, LANES), lambda i: (i, 0, 0)),
                  pl.BlockSpec((1, N_FOX_HEADS), lambda i: (0, 0)),
                  pl.BlockSpec((t, t), lambda i: (0, 0)),
                  pl.BlockSpec((N_FOX_HEADS, N_FOX_HEADS), lambda i: (0, 0))],
        out_specs=[pl.BlockSpec((1, hp, t, 2), lambda i: (i, 0, 0, 0)),
                   pl.BlockSpec((1, hp, 2, t), lambda i: (i, 0, 0, 0))],
        out_shape=[jax.ShapeDtypeStruct((b, hp, t, 2), F32),
                   jax.ShapeDtypeStruct((b, hp, 2, t), F32)],
        compiler_params=_cparams(1),
        name="foxprep",
    )(misc3, fb, tri, eye)


def _softmax_first(s, v):
    m = jnp.max(s, axis=-1, keepdims=True)
    p = jnp.exp2(s - m)
    return m, jnp.sum(p, axis=-1, keepdims=True), _dot(p.astype(BF16), v)


def _softmax_next(s, v, m, l, acc):
    m_new = jnp.maximum(m, jnp.max(s, axis=-1, keepdims=True))
    alpha = jnp.exp2(m - m_new)
    p = jnp.exp2(s - m_new)
    return m_new, alpha * l + jnp.sum(p, axis=-1, keepdims=True), alpha * acc + _dot(p.astype(BF16), v)


def _stack_heads(q, tail, n_heads):
    return jnp.concatenate(
        [jnp.concatenate([q[:, r * HEAD_DIM:(r + 1) * HEAD_DIM], tail], axis=1) for r in range(n_heads)], axis=0)


def _fill_key_scratch(kp_sc, kvp_sc, kv, with_blocks):
    t = kv.shape[0]
    lane = lax.broadcasted_iota(jnp.int32, (t, LANES), 1)
    if with_blocks:
        blk = lax.shift_right_logical(lax.broadcasted_iota(jnp.int32, (t, LANES), 0), SLC_SHIFT)
        aug = jnp.where(lane - HEAD_DIM == blk, 1.0, 0.0).astype(BF16)
    else:
        aug = jnp.zeros((t, LANES), BF16)
    kp_sc[KEY_PAD:, :] = jnp.where(lane < HEAD_DIM, kv, aug)
    lane_p = lax.broadcasted_iota(jnp.int32, (KEY_PAD, LANES), 1)
    kp_sc[0:KEY_PAD, :] = jnp.where(lane_p == PAD_LANE, 1.0, 0.0).astype(BF16)
    kvp_sc[KEY_PAD:, :] = jnp.where(lane < HEAD_DIM, jnp.where(lane == 0, 1.0, 0.0).astype(BF16), kv)
    kvp_sc[0:KEY_PAD, :] = jnp.zeros((KEY_PAD, LANES), BF16)


def _nsa_first(s, v):
    m = jnp.max(s, axis=-1, keepdims=True)
    return m, _dot(jnp.exp2(s - m).astype(BF16), v)


def _nsa_next(s, v, m, acc):
    m_new = jnp.maximum(m, jnp.max(s, axis=-1, keepdims=True))
    return m_new, jnp.exp2(m - m_new) * acc + _dot(jnp.exp2(s - m_new).astype(BF16), v)


def _nsa_kernel(q_ref, kvc_ref, cbias_ref, gate_ref, impm_ref, eye_ref, skv_ref, nbias_ref, wkv_ref, wbias_ref,
                o_ref, skp_sc, skvp_sc, wkp_sc, wkvp_sc):
    tq = q_ref.shape[1]
    ncp = kvc_ref.shape[3]
    ns = impm_ref.shape[0]
    chunk = SLC_CHUNK
    span = wbias_ref.shape[2]
    qw = NSA_GQA * HEAD_DIM
    rows = NSA_GQA * tq
    qt = pl.program_id(2)
    t0 = qt * tq
    groups = range(skp_sc.shape[0])

    @pl.when(qt == 0)
    def _():
        for c in groups:
            _fill_key_scratch(skp_sc.at[c], skvp_sc.at[c], skv_ref[0, :, c * LANES:(c + 1) * LANES], True)
            _fill_key_scratch(wkp_sc.at[c], wkvp_sc.at[c], wkv_ref[0, :, c * LANES:(c + 1) * LANES], False)

    t_col = t0 + (lax.broadcasted_iota(jnp.int32, (rows, ncp), 0) & (tq - 1))
    c_row = lax.broadcasted_iota(jnp.int32, (rows, ncp), 1)
    valid = t_col >= c_row * CMP_STRIDE + (CMP_BLOCK - 1)
    blk = lax.broadcasted_iota(jnp.int32, (ns, tq), 0)
    t_row = t0 + lax.broadcasted_iota(jnp.int32, (ns, tq), 1)
    cur = t_row // SLC_BLOCK
    bonus = jnp.where((blk == 0) | (blk == cur) | (blk == cur - 1), FORCE_BONUS, 0.0)
    blk_valid = blk * SLC_BLOCK <= t_row
    pad_row = jnp.where(lax.broadcasted_iota(jnp.int32, (HEAD_DIM - MASK_BLOCK_LANES, tq), 0) == 0, 1.0, 0.0)
    lane = lax.broadcasted_iota(jnp.int32, (rows, LANES), 1)
    upper = lane >= HEAD_DIM
    win_tail = jnp.where(lane == PAD_LANE, NEG_INF, 0.0).astype(BF16)
    zero_tail = jnp.zeros((tq, HEAD_DIM), BF16)
    q_all = q_ref[0]

    qst, sel_tail, o_cmp = [], [], []
    for c in groups:
        qs = _stack_heads(q_all[:, c * qw:(c + 1) * qw], zero_tail, NSA_GQA)
        kc = kvc_ref[0, 0, c].astype(BF16)
        vc = kvc_ref[1, 0, c].astype(BF16)
        s = jnp.where(valid, _nt(qs, kc) + cbias_ref[c].reshape(rows, ncp), NEG_INF)
        m = jnp.max(s, axis=-1, keepdims=True)
        e = jnp.where(valid, jnp.exp2(s - m), 0.0)
        p = e / jnp.maximum(jnp.sum(e, axis=-1, keepdims=True), 1e-30)
        o_cmp.append(_dot(p.astype(BF16), vc))
        p_grp = p[0:tq]
        for r in range(1, NSA_GQA):
            p_grp = p_grp + p[r * tq:(r + 1) * tq]

        imp = None
        for part in _split3(p_grp):
            term = _nt(impm_ref[...], part)
            imp = term if imp is None else imp + term
        score = jnp.where(blk_valid, imp + bonus, NEG_INF)
        rank = jnp.zeros((ns, tq), F32)
        for m_blk in range(ns):
            other = score[m_blk:m_blk + 1, :]
            ahead = (other > score) | ((other == score) & (blk > m_blk))
            rank = rank + jnp.where(ahead, 1.0, 0.0)
        parts = [jnp.where(rank < float(min(SLC_TOP_N, ns)), 0.0, 1.0), pad_row]
        if ns < MASK_BLOCK_LANES:
            parts.insert(1, jnp.zeros((MASK_BLOCK_LANES - ns, tq), F32))
        flags = _nt(eye_ref[...], jnp.concatenate(parts, axis=0).astype(BF16))
        tail = jnp.concatenate([zero_tail, (flags * NEG_INF).astype(BF16)], axis=1)
        qst.append(qs)
        sel_tail.append(jnp.concatenate([tail] * NSA_GQA, axis=0))

    qsel = [jnp.where(upper, sel_tail[c], qst[c]) for c in groups]
    near = pl.multiple_of(t0 + tq - chunk + KEY_PAD, tq)
    state = tuple(_nsa_first(_nt(qsel[c], skp_sc[c, pl.ds(near, chunk), :]) + nbias_ref[c],
                             skvp_sc[c, pl.ds(near, chunk), :]) for c in groups)

    def far(j, carry):
        first = pl.multiple_of(near - (j + 1) * chunk, tq)
        return tuple(_nsa_next(_nt(qsel[c], skp_sc[c, pl.ds(first, chunk), :]),
                               skvp_sc[c, pl.ds(first, chunk), :], *carry[c]) for c in groups)

    state = lax.fori_loop(0, (t0 + tq - 1) // chunk, far, state)

    wfirst = pl.multiple_of(t0, tq)
    for c in groups:
        qwin = jnp.where(upper, win_tail, qst[c])
        _, acc_w = _nsa_first(_nt(qwin, wkp_sc[c, pl.ds(wfirst, span), :]) + wbias_ref[c],
                              wkvp_sc[c, pl.ds(wfirst, span), :])
        acc_s = state[c][1]
        gate = jax.nn.sigmoid(gate_ref[0, c])
        g = [jnp.concatenate([gate[:, 3 * r + j:3 * r + j + 1] for r in range(NSA_GQA)], axis=0) for j in range(3)]
        tot = o_cmp[c] * g[0] + acc_s * (g[1] / acc_s[:, 0:1]) + acc_w * (g[2] / acc_w[:, 0:1])
        out = tot[:, HEAD_DIM:].astype(o_ref.dtype)
        for r in range(NSA_GQA):
            col = (c * NSA_GQA + r) * HEAD_DIM
            o_ref[0, :, col:col + HEAD_DIM] = out[r * tq:(r + 1) * tq]


def _nsa(main3, kvc, bias_c, gates, impm, eye, nbias, wbias):
    b, t, _ = main3.shape
    ncp = kvc.shape[3]
    ns = impm.shape[0]
    tq = AT_TQ
    qw = NSA_GQA * HEAD_DIM
    gs = NSA_GROUPS_PER_STEP
    kvw = gs * LANES
    span = wbias.shape[2]
    assert KEY_PAD >= SLC_CHUNK and KEY_PAD % tq == 0 and span - tq == KEY_PAD
    assert COL_SLC % kvw == 0 and COL_WIN % kvw == 0
    key_scratch = pltpu.VMEM((gs, t + KEY_PAD, LANES), BF16)
    return pl.pallas_call(
        _nsa_kernel,
        grid=(b, N_NSA_KV // gs, t // tq),
        in_specs=[pl.BlockSpec((1, tq, gs * qw), lambda i, g, j: (i, j, g)),
                  pl.BlockSpec((2, 1, gs, ncp, LANES), lambda i, g, j: (0, i, g, 0, 0)),
                  pl.BlockSpec((gs, NSA_GQA, tq, ncp), lambda i, g, j: (g, 0, j, 0)),
                  pl.BlockSpec((1, gs, tq, 3 * NSA_GQA), lambda i, g, j: (i, g, j, 0)),
                  pl.BlockSpec((ns, ncp), lambda i, g, j: (0, 0)),
                  pl.BlockSpec((tq, tq), lambda i, g, j: (0, 0)),
                  pl.BlockSpec((1, t, kvw), lambda i, g, j: (i, 0, COL_SLC // kvw + g)),
                  pl.BlockSpec((gs, NSA_GQA * tq, SLC_CHUNK), lambda i, g, j: (g, 0, 0)),
                  pl.BlockSpec((1, t, kvw), lambda i, g, j: (i, 0, COL_WIN // kvw + g)),
                  pl.BlockSpec((gs, NSA_GQA * tq, span), lambda i, g, j: (g, 0, 0))],
        out_specs=pl.BlockSpec((1, tq, gs * qw), lambda i, g, j: (i, j, g)),
        out_shape=jax.ShapeDtypeStruct((b, t, N_NSA_HEADS * HEAD_DIM), BF16),
        scratch_shapes=[key_scratch, key_scratch, key_scratch, key_scratch],
        compiler_params=_cparams(3),
        name="nsa",
    )(main3, kvc, bias_c, gates, impm, eye, main3, nbias, main3, wbias)


def _fox_kernel(q_ref, k_ref, v_ref, c_ref, ct_ref, o_ref, *, chunk):
    tq = q_ref.shape[1]
    t0 = pl.program_id(2) * tq
    pairs = range(c_ref.shape[1])
    lane = lax.broadcasted_iota(jnp.int32, (tq, LANES), 1)
    low = lane < HEAD_DIM
    q_all = q_ref[0]
    zero = jnp.zeros((tq, LANES), BF16)
    qst, c_col = [], []
    for p in pairs:
        q = q_all[:, p * LANES:(p + 1) * LANES]
        qst.append(jnp.concatenate([jnp.where(low, q, zero), jnp.where(low, zero, q)], axis=0))
        c_col.append(c_ref[0, p])

    def logits(p, start):
        s = _nt(qst[p], k_ref[0, pl.ds(start, chunk), p * LANES:(p + 1) * LANES])
        c_row = ct_ref[0, p, :, pl.ds(start, chunk)]
        return jnp.concatenate([s[:tq] + c_col[p][:, 0:1] - c_row[0:1],
                                s[tq:] + c_col[p][:, 1:2] - c_row[1:2]], axis=0)

    def values(p, start):
        return v_ref[0, pl.ds(start, chunk), p * LANES:(p + 1) * LANES]

    d0 = pl.multiple_of((t0 // chunk) * chunk, chunk)
    row = lax.broadcasted_iota(jnp.int32, (2 * tq, chunk), 0) & (tq - 1)
    causal = row - lax.broadcasted_iota(jnp.int32, (2 * tq, chunk), 1) + (t0 - d0) >= 0
    state = tuple(_softmax_first(jnp.where(causal, logits(p, d0), NEG_INF), values(p, d0)) for p in pairs)

    def below(j, carry):
        start = pl.multiple_of(j * chunk, chunk)
        return tuple(_softmax_next(logits(p, start), values(p, start), *carry[p]) for p in pairs)

    state = lax.fori_loop(0, t0 // chunk, below, state)
    for p in pairs:
        _, l, acc = state[p]
        o = acc / l
        o_ref[0, :, p * LANES:(p + 1) * LANES] = jnp.where(low, o[:tq], o[tq:]).astype(o_ref.dtype)


def _fox(main3, c, ct, tq=256, chunk=512):
    b, t, _ = main3.shape
    ps = FOX_PAIRS_PER_STEP
    hp = N_FOX_HEADS // 2
    w = ps * LANES
    assert chunk % tq == 0 and t % chunk == 0
    assert COL_FQ % w == 0 and COL_FK % w == 0 and COL_FV % w == 0
    return pl.pallas_call(
        functools.partial(_fox_kernel, chunk=chunk),
        grid=(b, hp // ps, t // tq),
        in_specs=[pl.BlockSpec((1, tq, w), lambda i, p, j: (i, j, COL_FQ // w + p)),
                  pl.BlockSpec((1, t, w), lambda i, p, j: (i, 0, COL_FK // w + p)),
                  pl.BlockSpec((1, t, w), lambda i, p, j: (i, 0, COL_FV // w + p)),
                  pl.BlockSpec((1, ps, tq, 2), lambda i, p, j: (i, p, j, 0)),
                  pl.BlockSpec((1, ps, 2, t), lambda i, p, j: (i, p, 0, 0))],
        out_specs=pl.BlockSpec((1, tq, w), lambda i, p, j: (i, j, p)),
        out_shape=jax.ShapeDtypeStruct((b, t, N_FOX_HEADS * HEAD_DIM), BF16),
        compiler_params=_cparams(3),
        name="fox",
    )(main3, main3, main3, c, ct)


def _rms(x, w):
    return x * lax.rsqrt(jnp.mean(x * x, axis=-1, keepdims=True) + NORM_EPS) * w


def _to_token_rows(ref, x):
    tm, d = x.shape
    parts = d // LANES
    for a in range(parts):
        ref[pl.ds(a, tm, stride=parts), :] = x[:, a * LANES:(a + 1) * LANES]


def _from_token_rows(ref, base, tm, d, pitch):
    return jnp.concatenate([ref[pl.ds(base + a, tm, stride=pitch), :] for a in range(d // LANES)], axis=1)


GATHER_PITCH = 20


def _outproj_kernel(on_ref, of_ref, x_ref, nnw_ref, fnw_ref, wo_ref, ffw_ref,
                    wr_ref, br_ref, h_ref, hn_ref, eid_ref, wt_ref):
    mixed = jnp.concatenate([_rms(on_ref[...].astype(F32), nnw_ref[...]),
                             _rms(of_ref[...].astype(F32), fnw_ref[...])], axis=-1)
    h = x_ref[...] + _dot(mixed.astype(BF16), wo_ref[...])
    h_ref[...] = h
    hn = _rms(h, ffw_ref[...])
    _to_token_rows(hn_ref, hn)

    h_hi, h_mid, _ = _split3(hn)
    both = _dot(h_hi, wr_ref[...])
    logits = (both[:, :LANES] + _dot(h_mid, wr_ref[:, :LANES]) + both[:, LANES:]) + br_ref[...]
    tm = logits.shape[0]
    lane = lax.broadcasted_iota(jnp.int32, (tm, LANES), 1)
    big = jnp.int32(LANES)
    is_grp = (lane >= N_EXPERTS) & (lane < N_EXPERTS + N_GROUPS)
    glog = jnp.where(is_grp, logits, NEG_INF)
    gmax = jnp.max(glog, axis=-1, keepdims=True)
    gsel = jnp.min(jnp.where(glog == gmax, lane, big), axis=-1, keepdims=True) - N_EXPERTS
    p_gsel = 1.0 / jnp.sum(jnp.where(is_grp, jnp.exp(glog - gmax), 0.0), axis=-1, keepdims=True)
    in_grp = (lane < N_EXPERTS) & (lane // EXPERTS_PER_GROUP == gsel)
    e1 = jnp.where(in_grp, logits, NEG_INF)
    v1 = jnp.max(e1, axis=-1, keepdims=True)
    i1 = jnp.min(jnp.where(e1 == v1, lane, big), axis=-1, keepdims=True)
    e2 = jnp.where(lane == i1, NEG_INF, e1)
    v2 = jnp.max(e2, axis=-1, keepdims=True)
    i2 = jnp.min(jnp.where(e2 == v2, lane, big), axis=-1, keepdims=True)
    ex = jnp.exp(v2 - v1)
    w1 = p_gsel / (1.0 + ex)
    w2 = p_gsel * ex / (1.0 + ex)
    eid_ref[...] = jnp.where(lane == 0, i1, jnp.where(lane == 1, i2, 0))
    wt_ref[...] = jnp.where(lane == 0, w1, jnp.where(lane == 1, w2, 0.0))


def _outproj(on, of, x2, nnw, fnw, wo, ffw, wr, br, tm=512):
    n, d = x2.shape
    half = on.shape[1]
    row = lambda i: (i, 0)
    fixed = lambda i: (0, 0)
    return pl.pallas_call(
        _outproj_kernel,
        grid=(n // tm,),
        in_specs=[pl.BlockSpec((tm, half), row), pl.BlockSpec((tm, half), row),
                  pl.BlockSpec((tm, d), row),
                  pl.BlockSpec((1, half), fixed), pl.BlockSpec((1, half), fixed),
                  pl.BlockSpec((d, d), fixed), pl.BlockSpec((1, d), fixed),
                  pl.BlockSpec((d, 2 * LANES), fixed), pl.BlockSpec((1, LANES), fixed)],
        out_specs=[pl.BlockSpec((tm, d), row), pl.BlockSpec((tm * (d // LANES), LANES), row),
                   pl.BlockSpec((tm, LANES), row), pl.BlockSpec((tm, LANES), row)],
        out_shape=[jax.ShapeDtypeStruct((n, d), F32), jax.ShapeDtypeStruct((n * (d // LANES), LANES), F32),
                   jax.ShapeDtypeStruct((n, LANES), jnp.int32), jax.ShapeDtypeStruct((n, LANES), F32)],
        compiler_params=_cparams(1),
        name="outproj",
    )(on, of, x2, nnw, fnw, wo, ffw, wr, br)


def _moe_kernel(te_ref, nt_ref, src0_ref, src1_ref, hn_hbm, ws_ref, wg_ref, wu_ref, wd_ref, y_ref,
                xbuf, sem, wg_sc, wu_sc, wd_sc):
    tm = ws_ref.shape[0]
    d = wg_sc.shape[0]
    parts = d // LANES
    i = pl.program_id(0)
    n_used = nt_ref[0]
    slot = i % 2

    def gather(src_ref, slot_):
        def row(r, carry):
            src = pl.multiple_of(src_ref[0, 0, r] * parts, parts)
            dst = (slot_ * tm + r) * GATHER_PITCH
            pltpu.make_async_copy(hn_hbm.at[pl.ds(src, parts)], xbuf.at[pl.ds(dst, parts)], sem.at[slot_]).start()
            return carry
        lax.fori_loop(0, tm, row, 0, unroll=8)

    @pl.when(i == 0)
    def _():
        gather(src0_ref, 0)

    @pl.when(i + 1 < n_used)
    def _():
        gather(src1_ref, 1 - slot)

    prev = te_ref[jnp.maximum(i - 1, 0)]

    @pl.when((i == 0) | (te_ref[i] != prev))
    def _():
        wg_sc[...] = wg_ref[0].astype(BF16)
        wu_sc[...] = wu_ref[0].astype(BF16)
        wd_sc[...] = wd_ref[0].astype(BF16)

    @pl.when(i < n_used)
    def _():
        base = slot * (tm * GATHER_PITCH)
        pltpu.make_async_copy(hn_hbm.at[pl.ds(0, tm * parts)], xbuf.at[pl.ds(base, tm * parts)], sem.at[slot]).wait()
        x = _from_token_rows(xbuf, base, tm, d, GATHER_PITCH).astype(BF16)
        gate = _dot(x, wg_sc[...])
        up = _dot(x, wu_sc[...])
        hid = gate * jax.nn.sigmoid(gate) * up
        _to_token_rows(y_ref, ws_ref[...] * _dot(hid.astype(BF16), wd_sc[...]))

    @pl.when(i >= n_used)
    def _():
        y_ref[...] = jnp.zeros(y_ref.shape, F32)


def _moe(tile_e, src_tok, n_used, hn, w_sorted, wg, wu, wd, tm):
    n_tiles = tile_e.shape[0]
    d = wg.shape[1]
    ff = wg.shape[2]
    parts = d // LANES
    src3 = src_tok.reshape(n_tiles, 1, tm)
    grid_spec = pltpu.PrefetchScalarGridSpec(
        num_scalar_prefetch=2,
        grid=(n_tiles,),
        in_specs=[pl.BlockSpec((1, 1, tm), lambda i, te, nt: (0, 0, 0), memory_space=pltpu.SMEM),
                  pl.BlockSpec((1, 1, tm), lambda i, te, nt: (jnp.minimum(i + 1, n_tiles - 1), 0, 0),
                               memory_space=pltpu.SMEM),
                  pl.BlockSpec(memory_space=pl.ANY),
                  pl.BlockSpec((tm, 1), lambda i, te, nt: (i, 0)),
                  pl.BlockSpec((1, d, ff), lambda i, te, nt: (te[i], 0, 0)),
                  pl.BlockSpec((1, d, ff), lambda i, te, nt: (te[i], 0, 0)),
                  pl.BlockSpec((1, ff, d), lambda i, te, nt: (te[i], 0, 0))],
        out_specs=pl.BlockSpec((tm * parts, LANES), lambda i, te, nt: (i, 0)),
        scratch_shapes=[pltpu.VMEM((2 * tm * GATHER_PITCH, LANES), F32),
                        pltpu.SemaphoreType.DMA((2,)),
                        pltpu.VMEM((d, ff), BF16), pltpu.VMEM((d, ff), BF16), pltpu.VMEM((ff, d), BF16)],
    )
    return pl.pallas_call(
        _moe_kernel,
        grid_spec=grid_spec,
        out_shape=jax.ShapeDtypeStruct((n_tiles * tm * parts, LANES), F32),
        compiler_params=_cparams(1),
        name="moe",
    )(tile_e, n_used, src3, src3, hn, w_sorted, wg, wu, wd)


def _combine_kernel(pos0_ref, pos1_ref, y_hbm, h_ref, fw_ref, o_ref, ybuf, sem):
    tm, d = h_ref.shape
    parts = d // LANES
    i = pl.program_id(0)
    n = pl.num_programs(0)
    slot = i % 2

    def gather(pos_ref, slot_):
        def row(r, carry):
            for k in range(2):
                src = pl.multiple_of(pos_ref[0, 0, 2 * r + k] * parts, parts)
                dst = ((slot_ * 2 + k) * tm + r) * GATHER_PITCH
                pltpu.make_async_copy(y_hbm.at[pl.ds(src, parts)], ybuf.at[pl.ds(dst, parts)], sem.at[slot_]).start()
            return carry
        lax.fori_loop(0, tm, row, 0, unroll=8)

    @pl.when(i == 0)
    def _():
        gather(pos0_ref, 0)

    @pl.when(i + 1 < n)
    def _():
        gather(pos1_ref, 1 - slot)

    rows = tm * GATHER_PITCH
    base = slot * (2 * rows)
    pltpu.make_async_copy(y_hbm.at[pl.ds(0, 2 * tm * parts)], ybuf.at[pl.ds(base, 2 * tm * parts)], sem.at[slot]).wait()
    out = h_ref[...] + (_from_token_rows(ybuf, base, tm, d, GATHER_PITCH)
                        + _from_token_rows(ybuf, base + rows, tm, d, GATHER_PITCH))
    o_ref[...] = _rms(out, fw_ref[...])


def _combine(pos, y, h, fw, tm=256):
    n, d = h.shape
    steps = n // tm
    pos3 = pos.reshape(steps, 1, 2 * tm)
    return pl.pallas_call(
        _combine_kernel,
        grid=(steps,),
        in_specs=[pl.BlockSpec((1, 1, 2 * tm), lambda i: (0, 0, 0), memory_space=pltpu.SMEM),
                  pl.BlockSpec((1, 1, 2 * tm), lambda i: (jnp.minimum(i + 1, steps - 1), 0, 0),
                               memory_space=pltpu.SMEM),
                  pl.BlockSpec(memory_space=pl.ANY),
                  pl.BlockSpec((tm, d), lambda i: (i, 0)),
                  pl.BlockSpec((1, d), lambda i: (0, 0))],
        out_specs=pl.BlockSpec((tm, d), lambda i: (i, 0)),
        out_shape=jax.ShapeDtypeStruct((n, d), F32),
        scratch_shapes=[pltpu.VMEM((2 * 2 * tm * GATHER_PITCH, LANES), F32), pltpu.SemaphoreType.DMA((2,))],
        compiler_params=_cparams(1),
        name="combine",
    )(pos3, pos3, y, h, fw)


def _biasgen_kernel(tab_ref, bm_ref, bn_ref, bw_ref, om_ref, on_ref, ow_ref):
    h = pl.program_id(0)
    far = tab_ref[REL_BUCKETS - 1, h]

    def build(b_ref, shift):
        idx = b_ref[...]
        out = jnp.full(idx.shape, NEG_INF, F32)
        for bucket in range(REL_BUCKETS):
            out = jnp.where(idx == bucket, (tab_ref[bucket, h] - shift) * LOG2E, out)
        return out

    om_ref[0] = build(bm_ref, 0.0)
    on_ref[0] = build(bn_ref, far)
    ow_ref[0] = build(bw_ref, 0.0)


def _biasgen(rel_table, bm, bn, bw):
    heads = rel_table.shape[1]
    full = lambda a: pl.BlockSpec(a.shape, lambda h: (0, 0))
    out = lambda a: pl.BlockSpec((1,) + a.shape, lambda h: (h, 0, 0))
    return pl.pallas_call(
        _biasgen_kernel,
        grid=(heads,),
        in_specs=[pl.BlockSpec(memory_space=pltpu.SMEM), full(bm), full(bn), full(bw)],
        out_specs=[out(bm), out(bn), out(bw)],
        out_shape=[jax.ShapeDtypeStruct((heads,) + a.shape, F32) for a in (bm, bn, bw)],
        compiler_params=_cparams(1),
        name="biasgen",
    )(rel_table, bm, bn, bw)


def _bias_tables(rel_table, t, ncp):
    tq = AT_TQ
    far = REL_MAX_DIST
    buckets = _bucket_table(far + 1)
    i = np.arange(tq)[:, None]

    def bucket_map(dist, ok):
        return jnp.asarray(np.where(ok, buckets[np.clip(dist, 0, far)], -1).astype(np.int32))

    step = tq // CMP_STRIDE
    u = np.arange(2 * ncp)[None, :]
    dist_m = i - CMP_STRIDE * (u - ncp) - (CMP_BLOCK - 1)
    j = np.arange(SLC_CHUNK)[None, :]
    dist_n = (SLC_CHUNK - tq) + i - j
    j = np.arange(WINDOW + tq)[None, :]
    dist_w = WINDOW + i - j
    master, nbias, wbias = _biasgen(rel_table,
                                    bucket_map(dist_m, np.ones_like(dist_m, bool)),
                                    bucket_map(dist_n, dist_n >= 0),
                                    bucket_map(dist_w, (dist_w >= 0) & (dist_w < WINDOW)))
    master = master.reshape(N_NSA_KV, NSA_GQA, tq, 2 * ncp)
    bias_c = jnp.stack([master[..., ncp - step * qt:2 * ncp - step * qt] for qt in range(t // tq)], axis=2)
    bias_c = bias_c.reshape(N_NSA_KV, NSA_GQA, t, ncp)
    nbias = nbias.reshape(N_NSA_KV, NSA_GQA * tq, SLC_CHUNK)
    wbias = wbias.reshape(N_NSA_KV, NSA_GQA * tq, WINDOW + tq)
    return bias_c, nbias, wbias


def _compress_weights(pe, w1, w2):
    half = CMP_STRIDE
    eye = jnp.eye(N_NSA_KV, dtype=F32)

    def expand_w1(w):
        w = w.reshape(half, HEAD_DIM, CMP_HIDDEN)
        return jnp.einsum("idn,gh->igdhn", w, eye).reshape(half * N_NSA_KV * HEAD_DIM, N_NSA_KV * CMP_HIDDEN)

    def expand_pe(p):
        return jnp.broadcast_to(p[:, None, :], (half, N_NSA_KV, HEAD_DIM)).reshape(1, -1)

    w1a = expand_w1(w1[:half * HEAD_DIM]).astype(BF16)
    w1b = expand_w1(w1[half * HEAD_DIM:]).astype(BF16)
    w2x = jnp.einsum("nd,gh->gnhd", w2, eye).reshape(N_NSA_KV * CMP_HIDDEN, N_NSA_KV * HEAD_DIM).astype(BF16)
    return expand_pe(pe[:half]), expand_pe(pe[half:]), w1a, w1b, w2x


def _routing_tables(eid, wts, tm):
    n = eid.shape[0]
    e_flat = eid.reshape(-1)
    onehot = (e_flat[:, None] == jnp.arange(N_EXPERTS, dtype=jnp.int32)[None, :]).astype(jnp.int32)
    csum = jnp.cumsum(onehot, axis=0)
    rank = jnp.take_along_axis(csum, e_flat[:, None], axis=1)[:, 0] - 1
    counts = csum[-1]
    padded = ((counts + tm - 1) // tm) * tm
    ends = jnp.cumsum(padded)
    starts = ends - padded
    pos = (starts[e_flat] + rank).astype(jnp.int32)
    n_tiles = (2 * n) // tm + N_EXPERTS
    rows = n_tiles * tm
    src_pair = jnp.zeros((rows,), jnp.int32).at[pos].set(jnp.arange(2 * n, dtype=jnp.int32))
    src_tok = src_pair // 2
    w_sorted = wts.reshape(-1)[src_pair]
    tile_start = jnp.arange(n_tiles, dtype=jnp.int32) * tm
    tile_e = jnp.sum((ends[None, :] <= tile_start[:, None]).astype(jnp.int32), axis=1)
    tile_e = jnp.minimum(tile_e, N_EXPERTS - 1)
    n_used = (ends[-1] // tm).astype(jnp.int32).reshape(1)
    last_e = tile_e[jnp.maximum(n_used[0] - 1, 0)]
    tile_e = jnp.where(tile_start < ends[-1], tile_e, last_e)
    return tile_e, src_tok, n_used, w_sorted.reshape(rows, 1), pos


def kernel(x, attn_norm_w, w_in, cmp_pe_k, cmp_pe_v, cmp_k_w1, cmp_k_w2, cmp_v_w1, cmp_v_w2,
           rel_bias_table, fox_forget_b, nsa_out_norm_w, fox_out_norm_w, w_out, ffn_norm_w,
           router_group_w, router_group_b, router_expert_w, router_expert_b,
           expert_w_gate, expert_w_up, expert_w_down, final_norm_w):
    b, t, d = x.shape
    n = b * t
    depth = w_in.shape[0]
    assert t % 512 == 0 and t >= WINDOW + AT_TQ and d == 2048 and t // SLC_BLOCK <= MASK_BLOCK_LANES
    rows = t // CMP_STRIDE
    bias_c, nbias, wbias = _bias_tables(rel_bias_table, t, rows)
    ns = t // SLC_BLOCK
    ratio = SLC_BLOCK // CMP_STRIDE
    span = CMP_BLOCK // CMP_STRIDE
    nc = (t - CMP_BLOCK) // CMP_STRIDE + 1
    impm = np.zeros((ns, rows), np.float32)
    for blk in range(ns):
        for a in range(ratio):
            for s in range(span):
                c = blk * ratio + a - s
                if 0 <= c < nc:
                    impm[blk, c] += 1.0
    impm = jnp.asarray(impm, BF16)
    eye_q = jnp.eye(AT_TQ, dtype=BF16)
    eye_h = jnp.eye(N_FOX_HEADS, dtype=BF16)
    tri = jnp.asarray(np.tril(np.ones((t, t), np.float32)), BF16)
    moe_tm = 256

    h = x.reshape(n, d)
    for layer in range(depth):
        w_main, w_misc = _project_weights(w_in[layer])
        main, misc = _proj(h, attn_norm_w[layer][None, :], w_main, w_misc)
        main3 = main.reshape(b, t, MAIN_COLS)
        misc3 = misc.reshape(b, t, LANES)

        xkv = jnp.stack([main3[:, :, COL_KCMP:COL_KCMP + 256], main3[:, :, COL_VCMP:COL_VCMP + 256]])
        xkv = xkv.reshape(2, b, rows, CMP_STRIDE * 256)
        pk = _compress_weights(cmp_pe_k[layer], cmp_k_w1[layer], cmp_k_w2[layer])
        pv = _compress_weights(cmp_pe_v[layer], cmp_v_w1[layer], cmp_v_w2[layer])
        kvc = _compress(xkv, *[jnp.stack([a, c]) for a, c in zip(pk, pv)])

        gates = misc3[:, :, MISC_GATE:MISC_GATE + 48].reshape(b, t, N_NSA_KV, 12).transpose(0, 2, 1, 3)
        o_nsa = _nsa(main3, kvc, bias_c, gates, impm, eye_q, nbias, wbias)

        c, ct = _foxprep(misc3, fox_forget_b[layer][None, :], tri, eye_h)
        o_fox = _fox(main3, c, ct)

        wr = jnp.concatenate([router_expert_w[layer], router_group_w[layer]], axis=1)
        wr = jnp.pad(wr, ((0, 0), (0, LANES - wr.shape[1])))
        wr_hi = wr.astype(BF16)
        wr_lo = (wr - wr_hi.astype(F32)).astype(BF16)
        br = jnp.concatenate([router_expert_b[layer], router_group_b[layer]])
        br = jnp.pad(br, (0, LANES - br.shape[0]))[None, :]
        half = N_NSA_HEADS * HEAD_DIM
        h, hn, eid, wts = _outproj(
            o_nsa.reshape(n, half), o_fox.reshape(n, half),
            h, nsa_out_norm_w[layer][None, :], fox_out_norm_w[layer][None, :],
            w_out[layer].astype(BF16), ffn_norm_w[layer][None, :], jnp.concatenate([wr_hi, wr_lo], axis=1), br)

        tile_e, src_tok, n_used, w_sorted, pos = _routing_tables(eid[:, :2], wts[:, :2], moe_tm)
        y = _moe(tile_e, src_tok, n_used, hn, w_sorted,
                 expert_w_gate[layer], expert_w_up[layer], expert_w_down[layer], moe_tm)
        last = layer == depth - 1
        fw = final_norm_w if last else jnp.ones((d,), F32)
        assert last, "the fused final norm assumes a single layer"
        h = _combine(pos, y, h, fw[None, :])
    return h.reshape(b, t, d)
```

```python
import functools
import math

import numpy as np
import jax
import jax.numpy as jnp
from jax import lax
from jax.experimental import pallas as pl
from jax.experimental.pallas import tpu as pltpu

F32 = jnp.float32
BF16 = jnp.bfloat16

HEAD_DIM = 64
N_NSA_HEADS = 16
N_FOX_HEADS = 16
NSA_GQA = 4
N_NSA_KV = 4
CMP_BLOCK = 32
CMP_STRIDE = 16
CMP_HIDDEN = 128
SLC_BLOCK = 64
SLC_TOP_N = 16
WINDOW = 512
REL_BUCKETS = 32
REL_MAX_DIST = 128
N_GROUPS = 4
EXPERTS_PER_GROUP = 8
N_EXPERTS = 32
EXPERT_FF = 512
NORM_EPS = 1e-6
NEG_INF = -1e30
FORCE_BONUS = 1e4
SCALE = HEAD_DIM ** -0.5
LOG2E = math.log2(math.e)
Q_SCALE = SCALE * LOG2E

LANES = 128
VMEM_LIMIT = 56 * 1024 * 1024

COL_NQ = 0
COL_KCMP = 1024
COL_VCMP = 1280
COL_SLC = 1536
COL_WIN = 2048
COL_FQ = 2560
COL_FK = 3584
COL_FV = 4608
MAIN_COLS = 5632
MISC_GATE = 0
MISC_FF = 48

AT_TQ = 128
NSA_GROUPS_PER_STEP = 4
FOX_PAIRS_PER_STEP = 4
FOX_CUMSUM_BLOCK = 256
SLC_CHUNK = 512
SLC_SHIFT = 6
assert 1 << SLC_SHIFT == SLC_BLOCK
MASK_BLOCK_LANES = 32
PAD_LANE = HEAD_DIM + MASK_BLOCK_LANES
KEY_PAD = 512


def _nt(a, b):
    return lax.dot_general(a, b, (((1,), (1,)), ((), ())), preferred_element_type=F32)


def _dot(a, b):
    return jnp.dot(a, b, preferred_element_type=F32)


def _split3(x):
    hi = x.astype(BF16)
    r = x - hi.astype(F32)
    mid = r.astype(BF16)
    r = r - mid.astype(F32)
    return hi, mid, r.astype(BF16)


def _cparams(grid_rank):
    return pltpu.CompilerParams(dimension_semantics=("arbitrary",) * grid_rank, vmem_limit_bytes=VMEM_LIMIT)


def _bucket_table(n):
    d = np.arange(n, dtype=np.int64)
    max_exact = REL_BUCKETS // 2
    rel = np.log(np.maximum(d, 1).astype(np.float64) / max_exact) / math.log(REL_MAX_DIST / max_exact)
    scaled = rel * (REL_BUCKETS - max_exact)
    frac = scaled - np.floor(scaled)
    inner = (d > max_exact) & (d < REL_MAX_DIST)
    assert np.all((frac[inner] > 1e-3) & (frac[inner] < 1 - 1e-3))
    large = np.minimum(max_exact + np.floor(scaled + 1e-6).astype(np.int64), REL_BUCKETS - 1)
    return np.where(d < max_exact, d, large).astype(np.int32)


def _project_weights(w):
    d = w.shape[0]
    sizes = [1024] + [256] * 6 + [48, 1024, 1024, 1024, 16]
    offs = np.concatenate([[0], np.cumsum(sizes)])
    nq, kcmp, vcmp, kslc, vslc, kwin, vwin, ngate, fq, fk, fv, ff = [
        w[:, int(offs[i]):int(offs[i + 1])] for i in range(12)]

    def interleave(k, v):
        k = k.reshape(d, N_NSA_KV, HEAD_DIM)
        v = v.reshape(d, N_NSA_KV, HEAD_DIM)
        return jnp.stack([k, v], axis=2).reshape(d, N_NSA_KV * 2 * HEAD_DIM)

    main = jnp.concatenate([nq * Q_SCALE, kcmp, vcmp, interleave(kslc, vslc), interleave(kwin, vwin),
                            fq * Q_SCALE, fk, fv], axis=1)
    assert main.shape[1] == MAIN_COLS
    misc = jnp.concatenate([ngate, ff, jnp.zeros((d, LANES - 64), w.dtype)], axis=1)
    return main.astype(BF16), misc.astype(BF16)


def _proj_kernel(x_ref, nw_ref, w_ref, wm_ref, o_ref, om_ref, *, tn):
    x = x_ref[...]
    y = x * lax.rsqrt(jnp.mean(x * x, axis=-1, keepdims=True) + NORM_EPS) * nw_ref[...]
    xn = y.astype(BF16)
    om_ref[...] = _dot(xn, wm_ref[...])
    for c in range(o_ref.shape[1] // tn):
        o_ref[:, c * tn:(c + 1) * tn] = _dot(xn, w_ref[:, c * tn:(c + 1) * tn]).astype(BF16)


def _proj(x2, norm_w, w_main, w_misc, tm=512, tn=512):
    n, d = x2.shape
    once = pl.Buffered(1)
    return pl.pallas_call(
        functools.partial(_proj_kernel, tn=tn),
        grid=(n // tm,),
        in_specs=[pl.BlockSpec((tm, d), lambda i: (i, 0)),
                  pl.BlockSpec((1, d), lambda i: (0, 0)),
                  pl.BlockSpec((d, MAIN_COLS), lambda i: (0, 0), pipeline_mode=once),
                  pl.BlockSpec((d, LANES), lambda i: (0, 0), pipeline_mode=once)],
        out_specs=[pl.BlockSpec((tm, MAIN_COLS), lambda i: (i, 0)),
                   pl.BlockSpec((tm, LANES), lambda i: (i, 0))],
        out_shape=[jax.ShapeDtypeStruct((n, MAIN_COLS), BF16),
                   jax.ShapeDtypeStruct((n, LANES), F32)],
        compiler_params=_cparams(1),
        name="proj",
    )(x2, norm_w, w_main, w_misc)


def _compress_kernel(x_ref, pea_ref, peb_ref, w1a_ref, w1b_ref, w2_ref, o_ref):
    x = x_ref[0, 0].astype(F32)
    xa = (x + pea_ref[0]).astype(BF16)
    xb = (x + peb_ref[0]).astype(BF16)
    a = _dot(xa, w1a_ref[0])
    b = _dot(xb, w1b_ref[0])
    rows = a.shape[0]
    pre = a + pltpu.roll(b, rows - 1, 0)
    hid = pre * jax.nn.sigmoid(pre)
    out = _dot(hid.astype(BF16), w2_ref[0])
    for g in range(N_NSA_KV):
        blk = out[:, g * HEAD_DIM:(g + 1) * HEAD_DIM]
        o_ref[0, 0, g] = jnp.concatenate([blk, blk], axis=1)


def _compress(xkv, pea, peb, w1a, w1b, w2):
    _, b, rows, width = xkv.shape
    hid = N_NSA_KV * CMP_HIDDEN
    return pl.pallas_call(
        _compress_kernel,
        grid=(2, b),
        in_specs=[pl.BlockSpec((1, 1, rows, width), lambda s, i: (s, i, 0, 0)),
                  pl.BlockSpec((1, 1, width), lambda s, i: (s, 0, 0)),
                  pl.BlockSpec((1, 1, width), lambda s, i: (s, 0, 0)),
                  pl.BlockSpec((1, width, hid), lambda s, i: (s, 0, 0)),
                  pl.BlockSpec((1, width, hid), lambda s, i: (s, 0, 0)),
                  pl.BlockSpec((1, hid, N_NSA_KV * HEAD_DIM), lambda s, i: (s, 0, 0))],
        out_specs=pl.BlockSpec((1, 1, N_NSA_KV, rows, LANES), lambda s, i: (s, i, 0, 0, 0)),
        out_shape=jax.ShapeDtypeStruct((2, b, N_NSA_KV, rows, LANES), F32),
        compiler_params=_cparams(2),
        name="compress",
    )(xkv, pea, peb, w1a, w1b, w2)


def _foxprep_kernel(misc_ref, fb_ref, tri_ref, eye_ref, c_ref, ct_ref):
    z = misc_ref[0][:, MISC_FF:MISC_FF + N_FOX_HEADS] + fb_ref[...]
    logf = (jnp.minimum(z, 0.0) - jnp.log(1.0 + jnp.exp(-jnp.abs(z)))) * LOG2E
    tri = tri_ref[...]
    blk = tri.shape[0]
    parts = _split3(logf)
    total = jnp.zeros((1, N_FOX_HEADS), F32)
    pieces = []
    for i in range(logf.shape[0] // blk):
        piece = total
        for part in parts:
            piece = piece + _dot(tri, part[i * blk:(i + 1) * blk])
        total = piece[blk - 1:blk]
        pieces.append(piece)
    c = jnp.concatenate(pieces, axis=0)
    ct = None
    for part in _split3(c):
        term = _nt(eye_ref[...], part)
        ct = term if ct is None else ct + term
    for p in range(N_FOX_HEADS // 2):
        c_ref[0, p] = c[:, 2 * p:2 * p + 2]
        ct_ref[0, p] = ct[2 * p:2 * p + 2, :]


def _foxprep(misc3, fb, tri, eye):
    b, t, _ = misc3.shape
    hp = N_FOX_HEADS // 2
    return pl.pallas_call(
        _foxprep_kernel,
        grid=(b,),
        in_specs=[pl.BlockSpec((1, t, LANES), lambda i: (i, 0, 0)),
                  pl.BlockSpec((1, N_FOX_HEADS), lambda i: (0, 0)),
                  pl.BlockSpec(tri.shape, lambda i: (0, 0)),
                  pl.BlockSpec((N_FOX_HEADS, N_FOX_HEADS), lambda i: (0, 0))],
        out_specs=[pl.BlockSpec((1, hp, t, 2), lambda i: (i, 0, 0, 0)),
                   pl.BlockSpec((1, hp, 2, t), lambda i: (i, 0, 0, 0))],
        out_shape=[jax.ShapeDtypeStruct((b, hp, t, 2), F32),
                   jax.ShapeDtypeStruct((b, hp, 2, t), F32)],
        compiler_params=_cparams(1),
        name="foxprep",
    )(misc3, fb, tri, eye)


def _softmax_first(s, v):
    m = jnp.max(s, axis=-1, keepdims=True)
    p = jnp.exp2(s - m)
    return m, jnp.sum(p, axis=-1, keepdims=True), _dot(p.astype(BF16), v)


def _softmax_next(s, v, m, l, acc):
    m_new = jnp.maximum(m, jnp.max(s, axis=-1, keepdims=True))
    alpha = jnp.exp2(m - m_new)
    p = jnp.exp2(s - m_new)
    return m_new, alpha * l + jnp.sum(p, axis=-1, keepdims=True), alpha * acc + _dot(p.astype(BF16), v)


def _stack_heads(q, tail, n_heads):
    return jnp.concatenate(
        [jnp.concatenate([q[:, r * HEAD_DIM:(r + 1) * HEAD_DIM], tail], axis=1) for r in range(n_heads)], axis=0)


def _fill_key_scratch(kp_sc, kvp_sc, kv, with_blocks):
    t = kv.shape[0]
    lane = lax.broadcasted_iota(jnp.int32, (t, LANES), 1)
    if with_blocks:
        blk = lax.shift_right_logical(lax.broadcasted_iota(jnp.int32, (t, LANES), 0), SLC_SHIFT)
        aug = jnp.where(lane - HEAD_DIM == blk, 1.0, 0.0).astype(BF16)
    else:
        aug = jnp.zeros((t, LANES), BF16)
    kp_sc[KEY_PAD:, :] = jnp.where(lane < HEAD_DIM, kv, aug)
    lane_p = lax.broadcasted_iota(jnp.int32, (KEY_PAD, LANES), 1)
    kp_sc[0:KEY_PAD, :] = jnp.where(lane_p == PAD_LANE, 1.0, 0.0).astype(BF16)
    kvp_sc[KEY_PAD:, :] = jnp.where(lane < HEAD_DIM, jnp.where(lane == 0, 1.0, 0.0).astype(BF16), kv)
    kvp_sc[0:KEY_PAD, :] = jnp.zeros((KEY_PAD, LANES), BF16)


def _nsa_first(s, v):
    m = jnp.max(s, axis=-1, keepdims=True)
    return m, _dot(jnp.exp2(s - m).astype(BF16), v)


def _nsa_next(s, v, m, acc):
    m_new = jnp.maximum(m, jnp.max(s, axis=-1, keepdims=True))
    return m_new, jnp.exp2(m - m_new) * acc + _dot(jnp.exp2(s - m_new).astype(BF16), v)


def _nsa_kernel(q_ref, kvc_ref, cbias_ref, gate_ref, impm_ref, eye_ref, skv_ref, nbias_ref, wkv_ref, wbias_ref,
                o_ref, skp_sc, skvp_sc, wkp_sc, wkvp_sc):
    tq = q_ref.shape[1]
    ncp = kvc_ref.shape[3]
    ns = impm_ref.shape[0]
    chunk = SLC_CHUNK
    span = wbias_ref.shape[2]
    qw = NSA_GQA * HEAD_DIM
    rows = NSA_GQA * tq
    qt = pl.program_id(2)
    t0 = qt * tq
    groups = range(skp_sc.shape[0])

    @pl.when(qt == 0)
    def _():
        for c in groups:
            _fill_key_scratch(skp_sc.at[c], skvp_sc.at[c], skv_ref[0, :, c * LANES:(c + 1) * LANES], True)
            _fill_key_scratch(wkp_sc.at[c], wkvp_sc.at[c], wkv_ref[0, :, c * LANES:(c + 1) * LANES], False)

    t_col = t0 + (lax.broadcasted_iota(jnp.int32, (rows, ncp), 0) & (tq - 1))
    c_row = lax.broadcasted_iota(jnp.int32, (rows, ncp), 1)
    valid = t_col >= c_row * CMP_STRIDE + (CMP_BLOCK - 1)
    blk = lax.broadcasted_iota(jnp.int32, (ns, tq), 0)
    t_row = t0 + lax.broadcasted_iota(jnp.int32, (ns, tq), 1)
    cur = t_row // SLC_BLOCK
    bonus = jnp.where((blk == 0) | (blk == cur) | (blk == cur - 1), FORCE_BONUS, 0.0)
    blk_valid = blk * SLC_BLOCK <= t_row
    pad_row = jnp.where(lax.broadcasted_iota(jnp.int32, (HEAD_DIM - MASK_BLOCK_LANES, tq), 0) == 0, 1.0, 0.0)
    lane = lax.broadcasted_iota(jnp.int32, (rows, LANES), 1)
    upper = lane >= HEAD_DIM
    win_tail = jnp.where(lane == PAD_LANE, NEG_INF, 0.0).astype(BF16)
    zero_tail = jnp.zeros((tq, HEAD_DIM), BF16)
    q_all = q_ref[0]

    qst, sel_tail, o_cmp = [], [], []
    for c in groups:
        qs = _stack_heads(q_all[:, c * qw:(c + 1) * qw], zero_tail, NSA_GQA)
        kc = kvc_ref[0, 0, c].astype(BF16)
        vc = kvc_ref[1, 0, c].astype(BF16)
        s = jnp.where(valid, _nt(qs, kc) + cbias_ref[c].reshape(rows, ncp), NEG_INF)
        m = jnp.max(s, axis=-1, keepdims=True)
        e = jnp.where(valid, jnp.exp2(s - m), 0.0)
        p = e / jnp.maximum(jnp.sum(e, axis=-1, keepdims=True), 1e-30)
        o_cmp.append(_dot(p.astype(BF16), vc))
        p_grp = p[0:tq]
        for r in range(1, NSA_GQA):
            p_grp = p_grp + p[r * tq:(r + 1) * tq]

        imp = None
        for part in _split3(p_grp):
            term = _nt(impm_ref[...], part)
            imp = term if imp is None else imp + term
        score = jnp.where(blk_valid, imp + bonus, NEG_INF)
        rank = jnp.zeros((ns, tq), F32)
        for m_blk in range(ns):
            other = score[m_blk:m_blk + 1, :]
            ahead = (other > score) | ((other == score) & (blk > m_blk))
            rank = rank + jnp.where(ahead, 1.0, 0.0)
        parts = [jnp.where(rank < float(min(SLC_TOP_N, ns)), 0.0, 1.0), pad_row]
        if ns < MASK_BLOCK_LANES:
            parts.insert(1, jnp.zeros((MASK_BLOCK_LANES - ns, tq), F32))
        flags = _nt(eye_ref[...], jnp.concatenate(parts, axis=0).astype(BF16))
        tail = jnp.concatenate([zero_tail, (flags * NEG_INF).astype(BF16)], axis=1)
        qst.append(qs)
        sel_tail.append(jnp.concatenate([tail] * NSA_GQA, axis=0))

    qsel = [jnp.where(upper, sel_tail[c], qst[c]) for c in groups]
    near = pl.multiple_of(t0 + tq - chunk + KEY_PAD, tq)
    state = tuple(_nsa_first(_nt(qsel[c], skp_sc[c, pl.ds(near, chunk), :]) + nbias_ref[c],
                             skvp_sc[c, pl.ds(near, chunk), :]) for c in groups)

    def far(j, carry):
        first = pl.multiple_of(near - (j + 1) * chunk, tq)
        return tuple(_nsa_next(_nt(qsel[c], skp_sc[c, pl.ds(first, chunk), :]),
                               skvp_sc[c, pl.ds(first, chunk), :], *carry[c]) for c in groups)

    state = lax.fori_loop(0, (t0 + tq - 1) // chunk, far, state)

    wfirst = pl.multiple_of(t0, tq)
    for c in groups:
        qwin = jnp.where(upper, win_tail, qst[c])
        _, acc_w = _nsa_first(_nt(qwin, wkp_sc[c, pl.ds(wfirst, span), :]) + wbias_ref[c],
                              wkvp_sc[c, pl.ds(wfirst, span), :])
        acc_s = state[c][1]
        gate = jax.nn.sigmoid(gate_ref[0, c])
        g = [jnp.concatenate([gate[:, 3 * r + j:3 * r + j + 1] for r in range(NSA_GQA)], axis=0) for j in range(3)]
        tot = o_cmp[c] * g[0] + acc_s * (g[1] / acc_s[:, 0:1]) + acc_w * (g[2] / acc_w[:, 0:1])
        out = tot[:, HEAD_DIM:].astype(o_ref.dtype)
        for r in range(NSA_GQA):
            col = (c * NSA_GQA + r) * HEAD_DIM
            o_ref[0, :, col:col + HEAD_DIM] = out[r * tq:(r + 1) * tq]


def _nsa(main3, kvc, bias_c, gates, impm, eye, nbias, wbias):
    b, t, _ = main3.shape
    ncp = kvc.shape[3]
    ns = impm.shape[0]
    tq = AT_TQ
    qw = NSA_GQA * HEAD_DIM
    gs = NSA_GROUPS_PER_STEP
    kvw = gs * LANES
    span = wbias.shape[2]
    assert KEY_PAD >= SLC_CHUNK and KEY_PAD % tq == 0 and span - tq == KEY_PAD
    assert COL_SLC % kvw == 0 and COL_WIN % kvw == 0
    key_scratch = pltpu.VMEM((gs, t + KEY_PAD, LANES), BF16)
    return pl.pallas_call(
        _nsa_kernel,
        grid=(b, N_NSA_KV // gs, t // tq),
        in_specs=[pl.BlockSpec((1, tq, gs * qw), lambda i, g, j: (i, j, g)),
                  pl.BlockSpec((2, 1, gs, ncp, LANES), lambda i, g, j: (0, i, g, 0, 0)),
                  pl.BlockSpec((gs, NSA_GQA, tq, ncp), lambda i, g, j: (g, 0, j, 0)),
                  pl.BlockSpec((1, gs, tq, 3 * NSA_GQA), lambda i, g, j: (i, g, j, 0)),
                  pl.BlockSpec((ns, ncp), lambda i, g, j: (0, 0)),
                  pl.BlockSpec((tq, tq), lambda i, g, j: (0, 0)),
                  pl.BlockSpec((1, t, kvw), lambda i, g, j: (i, 0, COL_SLC // kvw + g)),
                  pl.BlockSpec((gs, NSA_GQA * tq, SLC_CHUNK), lambda i, g, j: (g, 0, 0)),
                  pl.BlockSpec((1, t, kvw), lambda i, g, j: (i, 0, COL_WIN // kvw + g)),
                  pl.BlockSpec((gs, NSA_GQA * tq, span), lambda i, g, j: (g, 0, 0))],
        out_specs=pl.BlockSpec((1, tq, gs * qw), lambda i, g, j: (i, j, g)),
        out_shape=jax.ShapeDtypeStruct((b, t, N_NSA_HEADS * HEAD_DIM), BF16),
        scratch_shapes=[key_scratch, key_scratch, key_scratch, key_scratch],
        compiler_params=_cparams(3),
        name="nsa",
    )(main3, kvc, bias_c, gates, impm, eye, main3, nbias, main3, wbias)


def _fox_kernel(q_ref, k_ref, v_ref, c_ref, ct_ref, o_ref, *, chunk):
    tq = q_ref.shape[1]
    t0 = pl.program_id(2) * tq
    pairs = range(c_ref.shape[1])
    lane = lax.broadcasted_iota(jnp.int32, (tq, LANES), 1)
    low = lane < HEAD_DIM
    q_all = q_ref[0]
    zero = jnp.zeros((tq, LANES), BF16)
    qst, c_col = [], []
    for p in pairs:
        q = q_all[:, p * LANES:(p + 1) * LANES]
        qst.append(jnp.concatenate([jnp.where(low, q, zero), jnp.where(low, zero, q)], axis=0))
        c_col.append(c_ref[0, p])

    def logits(p, start):
        s = _nt(qst[p], k_ref[0, pl.ds(start, chunk), p * LANES:(p + 1) * LANES])
        c_row = ct_ref[0, p, :, pl.ds(start, chunk)]
        return jnp.concatenate([s[:tq] + c_col[p][:, 0:1] - c_row[0:1],
                                s[tq:] + c_col[p][:, 1:2] - c_row[1:2]], axis=0)

    def values(p, start):
        return v_ref[0, pl.ds(start, chunk), p * LANES:(p + 1) * LANES]

    d0 = pl.multiple_of((t0 // chunk) * chunk, chunk)
    row = lax.broadcasted_iota(jnp.int32, (2 * tq, chunk), 0) & (tq - 1)
    causal = row - lax.broadcasted_iota(jnp.int32, (2 * tq, chunk), 1) + (t0 - d0) >= 0
    state = tuple(_softmax_first(jnp.where(causal, logits(p, d0), NEG_INF), values(p, d0)) for p in pairs)

    def below(j, carry):
        start = pl.multiple_of(j * chunk, chunk)
        return tuple(_softmax_next(logits(p, start), values(p, start), *carry[p]) for p in pairs)

    state = lax.fori_loop(0, t0 // chunk, below, state)
    for p in pairs:
        _, l, acc = state[p]
        o = acc / l
        o_ref[0, :, p * LANES:(p + 1) * LANES] = jnp.where(low, o[:tq], o[tq:]).astype(o_ref.dtype)


def _fox(main3, c, ct, tq=256, chunk=512):
    b, t, _ = main3.shape
    ps = FOX_PAIRS_PER_STEP
    hp = N_FOX_HEADS // 2
    w = ps * LANES
    assert chunk % tq == 0 and t % chunk == 0
    assert COL_FQ % w == 0 and COL_FK % w == 0 and COL_FV % w == 0
    return pl.pallas_call(
        functools.partial(_fox_kernel, chunk=chunk),
        grid=(b, hp // ps, t // tq),
        in_specs=[pl.BlockSpec((1, tq, w), lambda i, p, j: (i, j, COL_FQ // w + p)),
                  pl.BlockSpec((1, t, w), lambda i, p, j: (i, 0, COL_FK // w + p)),
                  pl.BlockSpec((1, t, w), lambda i, p, j: (i, 0, COL_FV // w + p)),
                  pl.BlockSpec((1, ps, tq, 2), lambda i, p, j: (i, p, j, 0)),
                  pl.BlockSpec((1, ps, 2, t), lambda i, p, j: (i, p, 0, 0))],
        out_specs=pl.BlockSpec((1, tq, w), lambda i, p, j: (i, j, p)),
        out_shape=jax.ShapeDtypeStruct((b, t, N_FOX_HEADS * HEAD_DIM), BF16),
        compiler_params=_cparams(3),
        name="fox",
    )(main3, main3, main3, c, ct)


def _rms(x, w):
    return x * lax.rsqrt(jnp.mean(x * x, axis=-1, keepdims=True) + NORM_EPS) * w


def _to_token_rows(ref, x):
    tm, d = x.shape
    parts = d // LANES
    for a in range(parts):
        ref[pl.ds(a, tm, stride=parts), :] = x[:, a * LANES:(a + 1) * LANES]


def _from_token_rows(ref, base, tm, d, pitch):
    return jnp.concatenate([ref[pl.ds(base + a, tm, stride=pitch), :] for a in range(d // LANES)], axis=1)


GATHER_PITCH = 20


def _outproj_kernel(on_ref, of_ref, x_ref, nnw_ref, fnw_ref, wo_ref, ffw_ref,
                    wr_ref, br_ref, h_ref, hn_ref, eid_ref, wt_ref):
    mixed = jnp.concatenate([_rms(on_ref[...].astype(F32), nnw_ref[...]),
                             _rms(of_ref[...].astype(F32), fnw_ref[...])], axis=-1)
    h = x_ref[...] + _dot(mixed.astype(BF16), wo_ref[...])
    h_ref[...] = h
    hn = _rms(h, ffw_ref[...])
    _to_token_rows(hn_ref, hn)

    h_hi, h_mid, _ = _split3(hn)
    both = _dot(h_hi, wr_ref[...])
    logits = (both[:, :LANES] + _dot(h_mid, wr_ref[:, :LANES]) + both[:, LANES:]) + br_ref[...]
    tm = logits.shape[0]
    lane = lax.broadcasted_iota(jnp.int32, (tm, LANES), 1)
    big = jnp.int32(LANES)
    is_grp = (lane >= N_EXPERTS) & (lane < N_EXPERTS + N_GROUPS)
    glog = jnp.where(is_grp, logits, NEG_INF)
    gmax = jnp.max(glog, axis=-1, keepdims=True)
    gsel = jnp.min(jnp.where(glog == gmax, lane, big), axis=-1, keepdims=True) - N_EXPERTS
    p_gsel = 1.0 / jnp.sum(jnp.where(is_grp, jnp.exp(glog - gmax), 0.0), axis=-1, keepdims=True)
    in_grp = (lane < N_EXPERTS) & (lane // EXPERTS_PER_GROUP == gsel)
    e1 = jnp.where(in_grp, logits, NEG_INF)
    v1 = jnp.max(e1, axis=-1, keepdims=True)
    i1 = jnp.min(jnp.where(e1 == v1, lane, big), axis=-1, keepdims=True)
    e2 = jnp.where(lane == i1, NEG_INF, e1)
    v2 = jnp.max(e2, axis=-1, keepdims=True)
    i2 = jnp.min(jnp.where(e2 == v2, lane, big), axis=-1, keepdims=True)
    ex = jnp.exp(v2 - v1)
    w1 = p_gsel / (1.0 + ex)
    w2 = p_gsel * ex / (1.0 + ex)
    eid_ref[...] = jnp.where(lane == 0, i1, jnp.where(lane == 1, i2, 0))
    wt_ref[...] = jnp.where(lane == 0, w1, jnp.where(lane == 1, w2, 0.0))


def _outproj(on, of, x2, nnw, fnw, wo, ffw, wr, br, tm=512):
    n, d = x2.shape
    half = on.shape[1]
    row = lambda i: (i, 0)
    fixed = lambda i: (0, 0)
    return pl.pallas_call(
        _outproj_kernel,
        grid=(n // tm,),
        in_specs=[pl.BlockSpec((tm, half), row), pl.BlockSpec((tm, half), row),
                  pl.BlockSpec((tm, d), row),
                  pl.BlockSpec((1, half), fixed), pl.BlockSpec((1, half), fixed),
                  pl.BlockSpec((d, d), fixed), pl.BlockSpec((1, d), fixed),
                  pl.BlockSpec((d, 2 * LANES), fixed), pl.BlockSpec((1, LANES), fixed)],
        out_specs=[pl.BlockSpec((tm, d), row), pl.BlockSpec((tm * (d // LANES), LANES), row),
                   pl.BlockSpec((tm, LANES), row), pl.BlockSpec((tm, LANES), row)],
        out_shape=[jax.ShapeDtypeStruct((n, d), F32), jax.ShapeDtypeStruct((n * (d // LANES), LANES), F32),
                   jax.ShapeDtypeStruct((n, LANES), jnp.int32), jax.ShapeDtypeStruct((n, LANES), F32)],
        compiler_params=_cparams(1),
        name="outproj",
    )(on, of, x2, nnw, fnw, wo, ffw, wr, br)


def _moe_kernel(te_ref, nt_ref, src0_ref, src1_ref, hn_hbm, ws_ref, wg_ref, wu_ref, wd_ref, y_ref,
                xbuf, sem, wg_sc, wu_sc, wd_sc):
    tm = ws_ref.shape[0]
    d = wg_sc.shape[0]
    parts = d // LANES
    i = pl.program_id(0)
    n_used = nt_ref[0]
    slot = i % 2

    def gather(src_ref, slot_):
        def row(r, carry):
            src = pl.multiple_of(src_ref[0, 0, r] * parts, parts)
            dst = (slot_ * tm + r) * GATHER_PITCH
            pltpu.make_async_copy(hn_hbm.at[pl.ds(src, parts)], xbuf.at[pl.ds(dst, parts)], sem.at[slot_]).start()
            return carry
        lax.fori_loop(0, tm, row, 0, unroll=8)

    @pl.when(i == 0)
    def _():
        gather(src0_ref, 0)

    @pl.when(i + 1 < n_used)
    def _():
        gather(src1_ref, 1 - slot)

    prev = te_ref[jnp.maximum(i - 1, 0)]

    @pl.when((i == 0) | (te_ref[i] != prev))
    def _():
        wg_sc[...] = wg_ref[0].astype(BF16)
        wu_sc[...] = wu_ref[0].astype(BF16)
        wd_sc[...] = wd_ref[0].astype(BF16)

    @pl.when(i < n_used)
    def _():
        base = slot * (tm * GATHER_PITCH)
        pltpu.make_async_copy(hn_hbm.at[pl.ds(0, tm * parts)], xbuf.at[pl.ds(base, tm * parts)], sem.at[slot]).wait()
        x = _from_token_rows(xbuf, base, tm, d, GATHER_PITCH).astype(BF16)
        gate = _dot(x, wg_sc[...])
        up = _dot(x, wu_sc[...])
        hid = gate * jax.nn.sigmoid(gate) * up
        _to_token_rows(y_ref, ws_ref[...] * _dot(hid.astype(BF16), wd_sc[...]))

    @pl.when(i >= n_used)
    def _():
        y_ref[...] = jnp.zeros(y_ref.shape, F32)


def _moe(tile_e, src_tok, n_used, hn, w_sorted, wg, wu, wd, tm):
    n_tiles = tile_e.shape[0]
    d = wg.shape[1]
    ff = wg.shape[2]
    parts = d // LANES
    src3 = src_tok.reshape(n_tiles, 1, tm)
    grid_spec = pltpu.PrefetchScalarGridSpec(
        num_scalar_prefetch=2,
        grid=(n_tiles,),
        in_specs=[pl.BlockSpec((1, 1, tm), lambda i, te, nt: (0, 0, 0), memory_space=pltpu.SMEM),
                  pl.BlockSpec((1, 1, tm), lambda i, te, nt: (jnp.minimum(i + 1, n_tiles - 1), 0, 0),
                               memory_space=pltpu.SMEM),
                  pl.BlockSpec(memory_space=pl.ANY),
                  pl.BlockSpec((tm, 1), lambda i, te, nt: (i, 0)),
                  pl.BlockSpec((1, d, ff), lambda i, te, nt: (te[i], 0, 0)),
                  pl.BlockSpec((1, d, ff), lambda i, te, nt: (te[i], 0, 0)),
                  pl.BlockSpec((1, ff, d), lambda i, te, nt: (te[i], 0, 0))],
        out_specs=pl.BlockSpec((tm * parts, LANES), lambda i, te, nt: (i, 0)),
        scratch_shapes=[pltpu.VMEM((2 * tm * GATHER_PITCH, LANES), F32),
                        pltpu.SemaphoreType.DMA((2,)),
                        pltpu.VMEM((d, ff), BF16), pltpu.VMEM((d, ff), BF16), pltpu.VMEM((ff, d), BF16)],
    )
    return pl.pallas_call(
        _moe_kernel,
        grid_spec=grid_spec,
        out_shape=jax.ShapeDtypeStruct((n_tiles * tm * parts, LANES), F32),
        compiler_params=_cparams(1),
        name="moe",
    )(tile_e, n_used, src3, src3, hn, w_sorted, wg, wu, wd)


def _combine_kernel(pos0_ref, pos1_ref, y_hbm, h_ref, fw_ref, o_ref, ybuf, sem):
    tm, d = h_ref.shape
    parts = d // LANES
    i = pl.program_id(0)
    n = pl.num_programs(0)
    slot = i % 2

    def gather(pos_ref, slot_):
        def row(r, carry):
            for k in range(2):
                src = pl.multiple_of(pos_ref[0, 0, 2 * r + k] * parts, parts)
                dst = ((slot_ * 2 + k) * tm + r) * GATHER_PITCH
                pltpu.make_async_copy(y_hbm.at[pl.ds(src, parts)], ybuf.at[pl.ds(dst, parts)], sem.at[slot_]).start()
            return carry
        lax.fori_loop(0, tm, row, 0, unroll=8)

    @pl.when(i == 0)
    def _():
        gather(pos0_ref, 0)

    @pl.when(i + 1 < n)
    def _():
        gather(pos1_ref, 1 - slot)

    rows = tm * GATHER_PITCH
    base = slot * (2 * rows)
    pltpu.make_async_copy(y_hbm.at[pl.ds(0, 2 * tm * parts)], ybuf.at[pl.ds(base, 2 * tm * parts)], sem.at[slot]).wait()
    out = h_ref[...] + (_from_token_rows(ybuf, base, tm, d, GATHER_PITCH)
                        + _from_token_rows(ybuf, base + rows, tm, d, GATHER_PITCH))
    o_ref[...] = _rms(out, fw_ref[...])


def _combine(pos, y, h, fw, tm=256):
    n, d = h.shape
    steps = n // tm
    pos3 = pos.reshape(steps, 1, 2 * tm)
    return pl.pallas_call(
        _combine_kernel,
        grid=(steps,),
        in_specs=[pl.BlockSpec((1, 1, 2 * tm), lambda i: (0, 0, 0), memory_space=pltpu.SMEM),
                  pl.BlockSpec((1, 1, 2 * tm), lambda i: (jnp.minimum(i + 1, steps - 1), 0, 0),
                               memory_space=pltpu.SMEM),
                  pl.BlockSpec(memory_space=pl.ANY),
                  pl.BlockSpec((tm, d), lambda i: (i, 0)),
                  pl.BlockSpec((1, d), lambda i: (0, 0))],
        out_specs=pl.BlockSpec((tm, d), lambda i: (i, 0)),
        out_shape=jax.ShapeDtypeStruct((n, d), F32),
        scratch_shapes=[pltpu.VMEM((2 * 2 * tm * GATHER_PITCH, LANES), F32), pltpu.SemaphoreType.DMA((2,))],
        compiler_params=_cparams(1),
        name="combine",
    )(pos3, pos3, y, h, fw)


def _biasgen_kernel(tab_ref, bm_ref, bn_ref, bw_ref, om_ref, on_ref, ow_ref):
    h = pl.program_id(0)
    far = tab_ref[REL_BUCKETS - 1, h]

    def build(b_ref, shift):
        idx = b_ref[...]
        out = jnp.full(idx.shape, NEG_INF, F32)
        for bucket in range(REL_BUCKETS):
            out = jnp.where(idx == bucket, (tab_ref[bucket, h] - shift) * LOG2E, out)
        return out

    om_ref[0] = build(bm_ref, 0.0)
    on_ref[0] = build(bn_ref, far)
    ow_ref[0] = build(bw_ref, 0.0)


def _biasgen(rel_table, bm, bn, bw):
    heads = rel_table.shape[1]
    full = lambda a: pl.BlockSpec(a.shape, lambda h: (0, 0))
    out = lambda a: pl.BlockSpec((1,) + a.shape, lambda h: (h, 0, 0))
    return pl.pallas_call(
        _biasgen_kernel,
        grid=(heads,),
        in_specs=[pl.BlockSpec(memory_space=pltpu.SMEM), full(bm), full(bn), full(bw)],
        out_specs=[out(bm), out(bn), out(bw)],
        out_shape=[jax.ShapeDtypeStruct((heads,) + a.shape, F32) for a in (bm, bn, bw)],
        compiler_params=_cparams(1),
        name="biasgen",
    )(rel_table, bm, bn, bw)


def _bias_tables(rel_table, t, ncp):
    tq = AT_TQ
    far = REL_MAX_DIST
    buckets = _bucket_table(far + 1)
    i = np.arange(tq)[:, None]

    def bucket_map(dist, ok):
        return jnp.asarray(np.where(ok, buckets[np.clip(dist, 0, far)], -1).astype(np.int32))

    step = tq // CMP_STRIDE
    u = np.arange(2 * ncp)[None, :]
    dist_m = i - CMP_STRIDE * (u - ncp) - (CMP_BLOCK - 1)
    j = np.arange(SLC_CHUNK)[None, :]
    dist_n = (SLC_CHUNK - tq) + i - j
    j = np.arange(WINDOW + tq)[None, :]
    dist_w = WINDOW + i - j
    master, nbias, wbias = _biasgen(rel_table,
                                    bucket_map(dist_m, np.ones_like(dist_m, bool)),
                                    bucket_map(dist_n, dist_n >= 0),
                                    bucket_map(dist_w, (dist_w >= 0) & (dist_w < WINDOW)))
    master = master.reshape(N_NSA_KV, NSA_GQA, tq, 2 * ncp)
    bias_c = jnp.stack([master[..., ncp - step * qt:2 * ncp - step * qt] for qt in range(t // tq)], axis=2)
    bias_c = bias_c.reshape(N_NSA_KV, NSA_GQA, t, ncp)
    nbias = nbias.reshape(N_NSA_KV, NSA_GQA * tq, SLC_CHUNK)
    wbias = wbias.reshape(N_NSA_KV, NSA_GQA * tq, WINDOW + tq)
    return bias_c, nbias, wbias


def _compress_weights(pe, w1, w2):
    half = CMP_STRIDE
    eye = jnp.eye(N_NSA_KV, dtype=F32)

    def expand_w1(w):
        w = w.reshape(half, HEAD_DIM, CMP_HIDDEN)
        return jnp.einsum("idn,gh->igdhn", w, eye).reshape(half * N_NSA_KV * HEAD_DIM, N_NSA_KV * CMP_HIDDEN)

    def expand_pe(p):
        return jnp.broadcast_to(p[:, None, :], (half, N_NSA_KV, HEAD_DIM)).reshape(1, -1)

    w1a = expand_w1(w1[:half * HEAD_DIM]).astype(BF16)
    w1b = expand_w1(w1[half * HEAD_DIM:]).astype(BF16)
    w2x = jnp.einsum("nd,gh->gnhd", w2, eye).reshape(N_NSA_KV * CMP_HIDDEN, N_NSA_KV * HEAD_DIM).astype(BF16)
    return expand_pe(pe[:half]), expand_pe(pe[half:]), w1a, w1b, w2x


def _routing_tables(eid, wts, tm):
    n = eid.shape[0]
    e_flat = eid.reshape(-1)
    onehot = (e_flat[:, None] == jnp.arange(N_EXPERTS, dtype=jnp.int32)[None, :]).astype(jnp.int32)
    csum = jnp.cumsum(onehot, axis=0)
    rank = jnp.take_along_axis(csum, e_flat[:, None], axis=1)[:, 0] - 1
    counts = csum[-1]
    padded = ((counts + tm - 1) // tm) * tm
    ends = jnp.cumsum(padded)
    starts = ends - padded
    pos = (starts[e_flat] + rank).astype(jnp.int32)
    n_tiles = (2 * n) // tm + N_EXPERTS
    rows = n_tiles * tm
    src_pair = jnp.zeros((rows,), jnp.int32).at[pos].set(jnp.arange(2 * n, dtype=jnp.int32))
    src_tok = src_pair // 2
    w_sorted = wts.reshape(-1)[src_pair]
    tile_start = jnp.arange(n_tiles, dtype=jnp.int32) * tm
    tile_e = jnp.sum((ends[None, :] <= tile_start[:, None]).astype(jnp.int32), axis=1)
    tile_e = jnp.minimum(tile_e, N_EXPERTS - 1)
    n_used = (ends[-1] // tm).astype(jnp.int32).reshape(1)
    last_e = tile_e[jnp.maximum(n_used[0] - 1, 0)]
    tile_e = jnp.where(tile_start < ends[-1], tile_e, last_e)
    return tile_e, src_tok, n_used, w_sorted.reshape(rows, 1), pos


def kernel(x, attn_norm_w, w_in, cmp_pe_k, cmp_pe_v, cmp_k_w1, cmp_k_w2, cmp_v_w1, cmp_v_w2,
           rel_bias_table, fox_forget_b, nsa_out_norm_w, fox_out_norm_w, w_out, ffn_norm_w,
           router_group_w, router_group_b, router_expert_w, router_expert_b,
           expert_w_gate, expert_w_up, expert_w_down, final_norm_w):
    b, t, d = x.shape
    n = b * t
    depth = w_in.shape[0]
    assert t % 512 == 0 and t >= WINDOW + AT_TQ and d == 2048 and t // SLC_BLOCK <= MASK_BLOCK_LANES
    rows = t // CMP_STRIDE
    bias_c, nbias, wbias = _bias_tables(rel_bias_table, t, rows)
    ns = t // SLC_BLOCK
    ratio = SLC_BLOCK // CMP_STRIDE
    span = CMP_BLOCK // CMP_STRIDE
    nc = (t - CMP_BLOCK) // CMP_STRIDE + 1
    impm = np.zeros((ns, rows), np.float32)
    for blk in range(ns):
        for a in range(ratio):
            for s in range(span):
                c = blk * ratio + a - s
                if 0 <= c < nc:
                    impm[blk, c] += 1.0
    impm = jnp.asarray(impm, BF16)
    eye_q = jnp.eye(AT_TQ, dtype=BF16)
    eye_h = jnp.eye(N_FOX_HEADS, dtype=BF16)
    tri = jnp.asarray(np.tril(np.ones((FOX_CUMSUM_BLOCK, FOX_CUMSUM_BLOCK), np.float32)), BF16)
    moe_tm = 256

    h = x.reshape(n, d)
    for layer in range(depth):
        w_main, w_misc = _project_weights(w_in[layer])
        main, misc = _proj(h, attn_norm_w[layer][None, :], w_main, w_misc)
        main3 = main.reshape(b, t, MAIN_COLS)
        misc3 = misc.reshape(b, t, LANES)

        xkv = jnp.stack([main3[:, :, COL_KCMP:COL_KCMP + 256], main3[:, :, COL_VCMP:COL_VCMP + 256]])
        xkv = xkv.reshape(2, b, rows, CMP_STRIDE * 256)
        pk = _compress_weights(cmp_pe_k[layer], cmp_k_w1[layer], cmp_k_w2[layer])
        pv = _compress_weights(cmp_pe_v[layer], cmp_v_w1[layer], cmp_v_w2[layer])
        kvc = _compress(xkv, *[jnp.stack([a, c]) for a, c in zip(pk, pv)])

        gates = misc3[:, :, MISC_GATE:MISC_GATE + 48].reshape(b, t, N_NSA_KV, 12).transpose(0, 2, 1, 3)
        o_nsa = _nsa(main3, kvc, bias_c, gates, impm, eye_q, nbias, wbias)

        c, ct = _foxprep(misc3, fox_forget_b[layer][None, :], tri, eye_h)
        o_fox = _fox(main3, c, ct)

        wr = jnp.concatenate([router_expert_w[layer], router_group_w[layer]], axis=1)
        wr = jnp.pad(wr, ((0, 0), (0, LANES - wr.shape[1])))
        wr_hi = wr.astype(BF16)
        wr_lo = (wr - wr_hi.astype(F32)).astype(BF16)
        br = jnp.concatenate([router_expert_b[layer], router_group_b[layer]])
        br = jnp.pad(br, (0, LANES - br.shape[0]))[None, :]
        half = N_NSA_HEADS * HEAD_DIM
        h, hn, eid, wts = _outproj(
            o_nsa.reshape(n, half), o_fox.reshape(n, half),
            h, nsa_out_norm_w[layer][None, :], fox_out_norm_w[layer][None, :],
            w_out[layer].astype(BF16), ffn_norm_w[layer][None, :], jnp.concatenate([wr_hi, wr_lo], axis=1), br)

        tile_e, src_tok, n_used, w_sorted, pos = _routing_tables(eid[:, :2], wts[:, :2], moe_tm)
        y = _moe(tile_e, src_tok, n_used, hn, w_sorted,
                 expert_w_gate[layer], expert_w_up[layer], expert_w_down[layer], moe_tm)
        last = layer == depth - 1
        fw = final_norm_w if last else jnp.ones((d,), F32)
        assert last, "the fused final norm assumes a single layer"
        h = _combine(pos, y, h, fw[None, :])
    return h.reshape(b, t, d)
```

```python
import functools
import math

import numpy as np
import jax
import jax.numpy as jnp
from jax import lax
from jax.experimental import pallas as pl
from jax.experimental.pallas import tpu as pltpu

F32 = jnp.float32
BF16 = jnp.bfloat16

HEAD_DIM = 64
N_NSA_HEADS = 16
N_FOX_HEADS = 16
NSA_GQA = 4
N_NSA_KV = 4
CMP_BLOCK = 32
CMP_STRIDE = 16
CMP_HIDDEN = 128
SLC_BLOCK = 64
SLC_TOP_N = 16
WINDOW = 512
REL_BUCKETS = 32
REL_MAX_DIST = 128
N_GROUPS = 4
EXPERTS_PER_GROUP = 8
N_EXPERTS = 32
EXPERT_FF = 512
NORM_EPS = 1e-6
NEG_INF = -1e30
FORCE_BONUS = 1e4
SCALE = HEAD_DIM ** -0.5
LOG2E = math.log2(math.e)
Q_SCALE = SCALE * LOG2E

LANES = 128
VMEM_LIMIT = 56 * 1024 * 1024

COL_NQ = 0
COL_KCMP = 1024
COL_VCMP = 1280
COL_SLC = 1536
COL_WIN = 2048
COL_FQ = 2560
COL_FK = 3584
COL_FV = 4608
MAIN_COLS = 5632
MISC_GATE = 0
MISC_FF = 48

AT_TQ = 128
NSA_GROUPS_PER_STEP = 4
FOX_PAIRS_PER_STEP = 4
FOX_CUMSUM_BLOCK = 256
SLC_CHUNK = 512
SLC_SHIFT = 6
assert 1 << SLC_SHIFT == SLC_BLOCK
MASK_BLOCK_LANES = 32
PAD_LANE = HEAD_DIM + MASK_BLOCK_LANES
KEY_PAD = 512


def _nt(a, b):
    return lax.dot_general(a, b, (((1,), (1,)), ((), ())), preferred_element_type=F32)


def _dot(a, b):
    return jnp.dot(a, b, preferred_element_type=F32)


def _split3(x):
    hi = x.astype(BF16)
    r = x - hi.astype(F32)
    mid = r.astype(BF16)
    r = r - mid.astype(F32)
    return hi, mid, r.astype(BF16)


def _cparams(grid_rank):
    return pltpu.CompilerParams(dimension_semantics=("arbitrary",) * grid_rank, vmem_limit_bytes=VMEM_LIMIT)


def _bucket_table(n):
    d = np.arange(n, dtype=np.int64)
    max_exact = REL_BUCKETS // 2
    rel = np.log(np.maximum(d, 1).astype(np.float64) / max_exact) / math.log(REL_MAX_DIST / max_exact)
    scaled = rel * (REL_BUCKETS - max_exact)
    frac = scaled - np.floor(scaled)
    inner = (d > max_exact) & (d < REL_MAX_DIST)
    assert np.all((frac[inner] > 1e-3) & (frac[inner] < 1 - 1e-3))
    large = np.minimum(max_exact + np.floor(scaled + 1e-6).astype(np.int64), REL_BUCKETS - 1)
    return np.where(d < max_exact, d, large).astype(np.int32)


def _project_weights(w):
    d = w.shape[0]
    sizes = [1024] + [256] * 6 + [48, 1024, 1024, 1024, 16]
    offs = np.concatenate([[0], np.cumsum(sizes)])
    nq, kcmp, vcmp, kslc, vslc, kwin, vwin, ngate, fq, fk, fv, ff = [
        w[:, int(offs[i]):int(offs[i + 1])] for i in range(12)]

    def interleave(k, v):
        k = k.reshape(d, N_NSA_KV, HEAD_DIM)
        v = v.reshape(d, N_NSA_KV, HEAD_DIM)
        return jnp.stack([k, v], axis=2).reshape(d, N_NSA_KV * 2 * HEAD_DIM)

    main = jnp.concatenate([nq * Q_SCALE, kcmp, vcmp, interleave(kslc, vslc), interleave(kwin, vwin),
                            fq * Q_SCALE, fk, fv], axis=1)
    assert main.shape[1] == MAIN_COLS
    misc = jnp.concatenate([ngate, ff, jnp.zeros((d, LANES - 64), w.dtype)], axis=1)
    return main.astype(BF16), misc.astype(BF16)


def _proj_kernel(x_ref, nw_ref, w_ref, wm_ref, o_ref, om_ref, *, tn):
    x = x_ref[...]
    y = x * lax.rsqrt(jnp.mean(x * x, axis=-1, keepdims=True) + NORM_EPS) * nw_ref[...]
    xn = y.astype(BF16)
    om_ref[...] = _dot(xn, wm_ref[...])
    for c in range(o_ref.shape[1] // tn):
        o_ref[:, c * tn:(c + 1) * tn] = _dot(xn, w_ref[:, c * tn:(c + 1) * tn]).astype(BF16)


def _proj(x2, norm_w, w_main, w_misc, tm=512, tn=512):
    n, d = x2.shape
    once = pl.Buffered(1)
    return pl.pallas_call(
        functools.partial(_proj_kernel, tn=tn),
        grid=(n // tm,),
        in_specs=[pl.BlockSpec((tm, d), lambda i: (i, 0)),
                  pl.BlockSpec((1, d), lambda i: (0, 0)),
                  pl.BlockSpec((d, MAIN_COLS), lambda i: (0, 0), pipeline_mode=once),
                  pl.BlockSpec((d, LANES), lambda i: (0, 0), pipeline_mode=once)],
        out_specs=[pl.BlockSpec((tm, MAIN_COLS), lambda i: (i, 0)),
                   pl.BlockSpec((tm, LANES), lambda i: (i, 0))],
        out_shape=[jax.ShapeDtypeStruct((n, MAIN_COLS), BF16),
                   jax.ShapeDtypeStruct((n, LANES), F32)],
        compiler_params=_cparams(1),
        name="proj",
    )(x2, norm_w, w_main, w_misc)


def _compress_kernel(x_ref, pea_ref, peb_ref, w1a_ref, w1b_ref, w2_ref, o_ref):
    x = x_ref[0, 0].astype(F32)
    xa = (x + pea_ref[0]).astype(BF16)
    xb = (x + peb_ref[0]).astype(BF16)
    a = _dot(xa, w1a_ref[0])
    b = _dot(xb, w1b_ref[0])
    rows = a.shape[0]
    pre = a + pltpu.roll(b, rows - 1, 0)
    hid = pre * jax.nn.sigmoid(pre)
    out = _dot(hid.astype(BF16), w2_ref[0])
    for g in range(N_NSA_KV):
        blk = out[:, g * HEAD_DIM:(g + 1) * HEAD_DIM]
        o_ref[0, 0, g] = jnp.concatenate([blk, blk], axis=1)


def _compress(xkv, pea, peb, w1a, w1b, w2):
    _, b, rows, width = xkv.shape
    hid = N_NSA_KV * CMP_HIDDEN
    return pl.pallas_call(
        _compress_kernel,
        grid=(2, b),
        in_specs=[pl.BlockSpec((1, 1, rows, width), lambda s, i: (s, i, 0, 0)),
                  pl.BlockSpec((1, 1, width), lambda s, i: (s, 0, 0)),
                  pl.BlockSpec((1, 1, width), lambda s, i: (s, 0, 0)),
                  pl.BlockSpec((1, width, hid), lambda s, i: (s, 0, 0)),
                  pl.BlockSpec((1, width, hid), lambda s, i: (s, 0, 0)),
                  pl.BlockSpec((1, hid, N_NSA_KV * HEAD_DIM), lambda s, i: (s, 0, 0))],
        out_specs=pl.BlockSpec((1, 1, N_NSA_KV, rows, LANES), lambda s, i: (s, i, 0, 0, 0)),
        out_shape=jax.ShapeDtypeStruct((2, b, N_NSA_KV, rows, LANES), F32),
        compiler_params=_cparams(2),
        name="compress",
    )(xkv, pea, peb, w1a, w1b, w2)


def _foxprep_kernel(misc_ref, fb_ref, tri_ref, eye_ref, c_ref, ct_ref):
    z = misc_ref[0][:, MISC_FF:MISC_FF + N_FOX_HEADS] + fb_ref[...]
    logf = (jnp.minimum(z, 0.0) - jnp.log(1.0 + jnp.exp(-jnp.abs(z)))) * LOG2E
    tri = tri_ref[...]
    blk = tri.shape[0]
    parts = _split3(logf)
    total = jnp.zeros((1, N_FOX_HEADS), F32)
    pieces = []
    for i in range(logf.shape[0] // blk):
        piece = total
        for part in parts:
            piece = piece + _dot(tri, part[i * blk:(i + 1) * blk])
        total = piece[blk - 1:blk]
        pieces.append(piece)
    c = jnp.concatenate(pieces, axis=0)
    ct = None
    for part in _split3(c):
        term = _nt(eye_ref[...], part)
        ct = term if ct is None else ct + term
    for p in range(N_FOX_HEADS // 2):
        c_ref[0, p] = c[:, 2 * p:2 * p + 2]
        ct_ref[0, p] = ct[2 * p:2 * p + 2, :]


def _foxprep(misc3, fb, tri, eye):
    b, t, _ = misc3.shape
    hp = N_FOX_HEADS // 2
    return pl.pallas_call(
        _foxprep_kernel,
        grid=(b,),
        in_specs=[pl.BlockSpec((1, t, LANES), lambda i: (i, 0, 0)),
                  pl.BlockSpec((1, N_FOX_HEADS), lambda i: (0, 0)),
                  pl.BlockSpec(tri.shape, lambda i: (0, 0)),
                  pl.BlockSpec((N_FOX_HEADS, N_FOX_HEADS), lambda i: (0, 0))],
        out_specs=[pl.BlockSpec((1, hp, t, 2), lambda i: (i, 0, 0, 0)),
                   pl.BlockSpec((1, hp, 2, t), lambda i: (i, 0, 0, 0))],
        out_shape=[jax.ShapeDtypeStruct((b, hp, t, 2), F32),
                   jax.ShapeDtypeStruct((b, hp, 2, t), F32)],
        compiler_params=_cparams(1),
        name="foxprep",
    )(misc3, fb, tri, eye)


def _softmax_first(s, v):
    m = jnp.max(s, axis=-1, keepdims=True)
    p = jnp.exp2(s - m)
    return m, jnp.sum(p, axis=-1, keepdims=True), _dot(p.astype(BF16), v)


def _softmax_next(s, v, m, l, acc):
    m_new = jnp.maximum(m, jnp.max(s, axis=-1, keepdims=True))
    alpha = jnp.exp2(m - m_new)
    p = jnp.exp2(s - m_new)
    return m_new, alpha * l + jnp.sum(p, axis=-1, keepdims=True), alpha * acc + _dot(p.astype(BF16), v)


def _stack_heads(q, tail, n_heads):
    return jnp.concatenate(
        [jnp.concatenate([q[:, r * HEAD_DIM:(r + 1) * HEAD_DIM], tail], axis=1) for r in range(n_heads)], axis=0)


def _fill_key_scratch(kp_sc, kvp_sc, kv, with_blocks):
    t = kv.shape[0]
    lane = lax.broadcasted_iota(jnp.int32, (t, LANES), 1)
    if with_blocks:
        blk = lax.shift_right_logical(lax.broadcasted_iota(jnp.int32, (t, LANES), 0), SLC_SHIFT)
        aug = jnp.where(lane - HEAD_DIM == blk, 1.0, 0.0).astype(BF16)
    else:
        aug = jnp.zeros((t, LANES), BF16)
    kp_sc[KEY_PAD:, :] = jnp.where(lane < HEAD_DIM, kv, aug)
    lane_p = lax.broadcasted_iota(jnp.int32, (KEY_PAD, LANES), 1)
    kp_sc[0:KEY_PAD, :] = jnp.where(lane_p == PAD_LANE, 1.0, 0.0).astype(BF16)
    kvp_sc[KEY_PAD:, :] = jnp.where(lane < HEAD_DIM, jnp.where(lane == 0, 1.0, 0.0).astype(BF16), kv)
    kvp_sc[0:KEY_PAD, :] = jnp.zeros((KEY_PAD, LANES), BF16)


def _nsa_first(s, v):
    m = jnp.max(s, axis=-1, keepdims=True)
    return m, _dot(jnp.exp2(s - m).astype(BF16), v)


def _nsa_next(s, v, m, acc):
    m_new = jnp.maximum(m, jnp.max(s, axis=-1, keepdims=True))
    return m_new, jnp.exp2(m - m_new) * acc + _dot(jnp.exp2(s - m_new).astype(BF16), v)


def _nsa_kernel(q_ref, kvc_ref, cbias_ref, gate_ref, impm_ref, eye_ref, skv_ref, nbias_ref, wkv_ref, wbias_ref,
                o_ref, skp_sc, skvp_sc, wkp_sc, wkvp_sc):
    tq = q_ref.shape[1]
    ncp = kvc_ref.shape[3]
    ns = impm_ref.shape[0]
    chunk = SLC_CHUNK
    span = wbias_ref.shape[2]
    qw = NSA_GQA * HEAD_DIM
    rows = NSA_GQA * tq
    qt = pl.program_id(2)
    t0 = qt * tq
    groups = range(skp_sc.shape[0])

    @pl.when(qt == 0)
    def _():
        for c in groups:
            _fill_key_scratch(skp_sc.at[c], skvp_sc.at[c], skv_ref[0, :, c * LANES:(c + 1) * LANES], True)
            _fill_key_scratch(wkp_sc.at[c], wkvp_sc.at[c], wkv_ref[0, :, c * LANES:(c + 1) * LANES], False)

    t_col = t0 + (lax.broadcasted_iota(jnp.int32, (rows, ncp), 0) & (tq - 1))
    c_row = lax.broadcasted_iota(jnp.int32, (rows, ncp), 1)
    valid = t_col >= c_row * CMP_STRIDE + (CMP_BLOCK - 1)
    blk = lax.broadcasted_iota(jnp.int32, (ns, tq), 0)
    t_row = t0 + lax.broadcasted_iota(jnp.int32, (ns, tq), 1)
    cur = t_row // SLC_BLOCK
    bonus = jnp.where((blk == 0) | (blk == cur) | (blk == cur - 1), FORCE_BONUS, 0.0)
    blk_valid = blk * SLC_BLOCK <= t_row
    pad_row = jnp.where(lax.broadcasted_iota(jnp.int32, (HEAD_DIM - MASK_BLOCK_LANES, tq), 0) == 0, 1.0, 0.0)
    lane = lax.broadcasted_iota(jnp.int32, (rows, LANES), 1)
    upper = lane >= HEAD_DIM
    win_tail = jnp.where(lane == PAD_LANE, NEG_INF, 0.0).astype(BF16)
    zero_tail = jnp.zeros((tq, HEAD_DIM), BF16)
    q_all = q_ref[0]

    qst, sel_tail, o_cmp = [], [], []
    for c in groups:
        qs = _stack_heads(q_all[:, c * qw:(c + 1) * qw], zero_tail, NSA_GQA)
        kc = kvc_ref[0, 0, c].astype(BF16)
        vc = kvc_ref[1, 0, c].astype(BF16)
        s = jnp.where(valid, _nt(qs, kc) + cbias_ref[c].reshape(rows, ncp), NEG_INF)
        m = jnp.max(s, axis=-1, keepdims=True)
        e = jnp.where(valid, jnp.exp2(s - m), 0.0)
        p = e / jnp.maximum(jnp.sum(e, axis=-1, keepdims=True), 1e-30)
        o_cmp.append(_dot(p.astype(BF16), vc))
        p_grp = p[0:tq]
        for r in range(1, NSA_GQA):
            p_grp = p_grp + p[r * tq:(r + 1) * tq]

        imp = None
        for part in _split3(p_grp):
            term = _nt(impm_ref[...], part)
            imp = term if imp is None else imp + term
        score = jnp.where(blk_valid, imp + bonus, NEG_INF)
        rank = jnp.zeros((ns, tq), F32)
        for m_blk in range(ns):
            other = score[m_blk:m_blk + 1, :]
            ahead = (other > score) | ((other == score) & (blk > m_blk))
            rank = rank + jnp.where(ahead, 1.0, 0.0)
        parts = [jnp.where(rank < float(min(SLC_TOP_N, ns)), 0.0, 1.0), pad_row]
        if ns < MASK_BLOCK_LANES:
            parts.insert(1, jnp.zeros((MASK_BLOCK_LANES - ns, tq), F32))
        flags = _nt(eye_ref[...], jnp.concatenate(parts, axis=0).astype(BF16))
        tail = jnp.concatenate([zero_tail, (flags * NEG_INF).astype(BF16)], axis=1)
        qst.append(qs)
        sel_tail.append(jnp.concatenate([tail] * NSA_GQA, axis=0))

    qsel = [jnp.where(upper, sel_tail[c], qst[c]) for c in groups]
    near = pl.multiple_of(t0 + tq - chunk + KEY_PAD, tq)
    state = tuple(_nsa_first(_nt(qsel[c], skp_sc[c, pl.ds(near, chunk), :]) + nbias_ref[c],
                             skvp_sc[c, pl.ds(near, chunk), :]) for c in groups)

    def far(j, carry):
        first = pl.multiple_of(near - (j + 1) * chunk, tq)
        return tuple(_nsa_next(_nt(qsel[c], skp_sc[c, pl.ds(first, chunk), :]),
                               skvp_sc[c, pl.ds(first, chunk), :], *carry[c]) for c in groups)

    state = lax.fori_loop(0, (t0 + tq - 1) // chunk, far, state)

    wfirst = pl.multiple_of(t0, tq)
    for c in groups:
        qwin = jnp.where(upper, win_tail, qst[c])
        _, acc_w = _nsa_first(_nt(qwin, wkp_sc[c, pl.ds(wfirst, span), :]) + wbias_ref[c],
                              wkvp_sc[c, pl.ds(wfirst, span), :])
        acc_s = state[c][1]
        gate = jax.nn.sigmoid(gate_ref[0, c])
        g = [jnp.concatenate([gate[:, 3 * r + j:3 * r + j + 1] for r in range(NSA_GQA)], axis=0) for j in range(3)]
        tot = o_cmp[c] * g[0] + acc_s * (g[1] / acc_s[:, 0:1]) + acc_w * (g[2] / acc_w[:, 0:1])
        out = tot[:, HEAD_DIM:].astype(o_ref.dtype)
        for r in range(NSA_GQA):
            col = (c * NSA_GQA + r) * HEAD_DIM
            o_ref[0, :, col:col + HEAD_DIM] = out[r * tq:(r + 1) * tq]


def _nsa(main3, kvc, bias_c, gates, impm, eye, nbias, wbias):
    b, t, _ = main3.shape
    ncp = kvc.shape[3]
    ns = impm.shape[0]
    tq = AT_TQ
    qw = NSA_GQA * HEAD_DIM
    gs = NSA_GROUPS_PER_STEP
    kvw = gs * LANES
    span = wbias.shape[2]
    assert KEY_PAD >= SLC_CHUNK and KEY_PAD % tq == 0 and span - tq == KEY_PAD
    assert COL_SLC % kvw == 0 and COL_WIN % kvw == 0
    key_scratch = pltpu.VMEM((gs, t + KEY_PAD, LANES), BF16)
    return pl.pallas_call(
        _nsa_kernel,
        grid=(b, N_NSA_KV // gs, t // tq),
        in_specs=[pl.BlockSpec((1, tq, gs * qw), lambda i, g, j: (i, j, g)),
                  pl.BlockSpec((2, 1, gs, ncp, LANES), lambda i, g, j: (0, i, g, 0, 0)),
                  pl.BlockSpec((gs, NSA_GQA, tq, ncp), lambda i, g, j: (g, 0, j, 0)),
                  pl.BlockSpec((1, gs, tq, 3 * NSA_GQA), lambda i, g, j: (i, g, j, 0)),
                  pl.BlockSpec((ns, ncp), lambda i, g, j: (0, 0)),
                  pl.BlockSpec((tq, tq), lambda i, g, j: (0, 0)),
                  pl.BlockSpec((1, t, kvw), lambda i, g, j: (i, 0, COL_SLC // kvw + g)),
                  pl.BlockSpec((gs, NSA_GQA * tq, SLC_CHUNK), lambda i, g, j: (g, 0, 0)),
                  pl.BlockSpec((1, t, kvw), lambda i, g, j: (i, 0, COL_WIN // kvw + g)),
                  pl.BlockSpec((gs, NSA_GQA * tq, span), lambda i, g, j: (g, 0, 0))],
        out_specs=pl.BlockSpec((1, tq, gs * qw), lambda i, g, j: (i, j, g)),
        out_shape=jax.ShapeDtypeStruct((b, t, N_NSA_HEADS * HEAD_DIM), BF16),
        scratch_shapes=[key_scratch, key_scratch, key_scratch, key_scratch],
        compiler_params=_cparams(3),
        name="nsa",
    )(main3, kvc, bias_c, gates, impm, eye, main3, nbias, main3, wbias)


def _fox_kernel(q_ref, k_ref, v_ref, c_ref, ct_ref, o_ref, *, chunk):
    tq = q_ref.shape[1]
    t0 = pl.program_id(2) * tq
    pairs = range(c_ref.shape[1])
    lane = lax.broadcasted_iota(jnp.int32, (tq, LANES), 1)
    low = lane < HEAD_DIM
    q_all = q_ref[0]
    zero = jnp.zeros((tq, LANES), BF16)
    qst, c_col = [], []
    for p in pairs:
        q = q_all[:, p * LANES:(p + 1) * LANES]
        qst.append(jnp.concatenate([jnp.where(low, q, zero), jnp.where(low, zero, q)], axis=0))
        c_col.append(c_ref[0, p])

    def logits(p, start):
        s = _nt(qst[p], k_ref[0, pl.ds(start, chunk), p * LANES:(p + 1) * LANES])
        c_row = ct_ref[0, p, :, pl.ds(start, chunk)]
        return jnp.concatenate([s[:tq] + c_col[p][:, 0:1] - c_row[0:1],
                                s[tq:] + c_col[p][:, 1:2] - c_row[1:2]], axis=0)

    def values(p, start):
        return v_ref[0, pl.ds(start, chunk), p * LANES:(p + 1) * LANES]

    d0 = pl.multiple_of((t0 // chunk) * chunk, chunk)
    row = lax.broadcasted_iota(jnp.int32, (2 * tq, chunk), 0) & (tq - 1)
    causal = row - lax.broadcasted_iota(jnp.int32, (2 * tq, chunk), 1) + (t0 - d0) >= 0
    state = tuple(_softmax_first(jnp.where(causal, logits(p, d0), NEG_INF), values(p, d0)) for p in pairs)

    def below(j, carry):
        start = pl.multiple_of(j * chunk, chunk)
        return tuple(_softmax_next(logits(p, start), values(p, start), *carry[p]) for p in pairs)

    state = lax.fori_loop(0, t0 // chunk, below, state)
    for p in pairs:
        _, l, acc = state[p]
        o = acc / l
        o_ref[0, :, p * LANES:(p + 1) * LANES] = jnp.where(low, o[:tq], o[tq:]).astype(o_ref.dtype)


def _fox(main3, c, ct, tq=256, chunk=512):
    b, t, _ = main3.shape
    ps = FOX_PAIRS_PER_STEP
    hp = N_FOX_HEADS // 2
    w = ps * LANES
    assert chunk % tq == 0 and t % chunk == 0
    assert COL_FQ % w == 0 and COL_FK % w == 0 and COL_FV % w == 0
    return pl.pallas_call(
        functools.partial(_fox_kernel, chunk=chunk),
        grid=(b, hp // ps, t // tq),
        in_specs=[pl.BlockSpec((1, tq, w), lambda i, p, j: (i, j, COL_FQ // w + p)),
                  pl.BlockSpec((1, t, w), lambda i, p, j: (i, 0, COL_FK // w + p)),
                  pl.BlockSpec((1, t, w), lambda i, p, j: (i, 0, COL_FV // w + p)),
                  pl.BlockSpec((1, ps, tq, 2), lambda i, p, j: (i, p, j, 0)),
                  pl.BlockSpec((1, ps, 2, t), lambda i, p, j: (i, p, 0, 0))],
        out_specs=pl.BlockSpec((1, tq, w), lambda i, p, j: (i, j, p)),
        out_shape=jax.ShapeDtypeStruct((b, t, N_FOX_HEADS * HEAD_DIM), BF16),
        compiler_params=_cparams(3),
        name="fox",
    )(main3, main3, main3, c, ct)


def _rms(x, w):
    return x * lax.rsqrt(jnp.mean(x * x, axis=-1, keepdims=True) + NORM_EPS) * w


def _to_token_rows(ref, x):
    tm, d = x.shape
    parts = d // LANES
    for a in range(parts):
        ref[pl.ds(a, tm, stride=parts), :] = x[:, a * LANES:(a + 1) * LANES]


def _from_token_rows(ref, base, tm, d, pitch):
    return jnp.concatenate([ref[pl.ds(base + a, tm, stride=pitch), :] for a in range(d // LANES)], axis=1)


GATHER_PITCH = 20


def _outproj_kernel(on_ref, of_ref, x_ref, nnw_ref, fnw_ref, wo_ref, ffw_ref,
                    wr_ref, br_ref, h_ref, hn_ref, eid_ref, wt_ref):
    mixed = jnp.concatenate([_rms(on_ref[...].astype(F32), nnw_ref[...]),
                             _rms(of_ref[...].astype(F32), fnw_ref[...])], axis=-1)
    h = x_ref[...] + _dot(mixed.astype(BF16), wo_ref[...])
    h_ref[...] = h
    hn = _rms(h, ffw_ref[...])
    _to_token_rows(hn_ref, hn)

    h_hi, h_mid, _ = _split3(hn)
    both = _dot(h_hi, wr_ref[...])
    logits = (both[:, :LANES] + _dot(h_mid, wr_ref[:, :LANES]) + both[:, LANES:]) + br_ref[...]
    tm = logits.shape[0]
    lane = lax.broadcasted_iota(jnp.int32, (tm, LANES), 1)
    big = jnp.int32(LANES)
    is_grp = (lane >= N_EXPERTS) & (lane < N_EXPERTS + N_GROUPS)
    glog = jnp.where(is_grp, logits, NEG_INF)
    gmax = jnp.max(glog, axis=-1, keepdims=True)
    gsel = jnp.min(jnp.where(glog == gmax, lane, big), axis=-1, keepdims=True) - N_EXPERTS
    p_gsel = 1.0 / jnp.sum(jnp.where(is_grp, jnp.exp(glog - gmax), 0.0), axis=-1, keepdims=True)
    in_grp = (lane < N_EXPERTS) & (lane // EXPERTS_PER_GROUP == gsel)
    e1 = jnp.where(in_grp, logits, NEG_INF)
    v1 = jnp.max(e1, axis=-1, keepdims=True)
    i1 = jnp.min(jnp.where(e1 == v1, lane, big), axis=-1, keepdims=True)
    e2 = jnp.where(lane == i1, NEG_INF, e1)
    v2 = jnp.max(e2, axis=-1, keepdims=True)
    i2 = jnp.min(jnp.where(e2 == v2, lane, big), axis=-1, keepdims=True)
    ex = jnp.exp(v2 - v1)
    w1 = p_gsel / (1.0 + ex)
    w2 = p_gsel * ex / (1.0 + ex)
    eid_ref[...] = jnp.where(lane == 0, i1, jnp.where(lane == 1, i2, 0))
    wt_ref[...] = jnp.where(lane == 0, w1, jnp.where(lane == 1, w2, 0.0))


def _outproj(on, of, x2, nnw, fnw, wo, ffw, wr, br, tm=512):
    n, d = x2.shape
    half = on.shape[1]
    row = lambda i: (i, 0)
    fixed = lambda i: (0, 0)
    return pl.pallas_call(
        _outproj_kernel,
        grid=(n // tm,),
        in_specs=[pl.BlockSpec((tm, half), row), pl.BlockSpec((tm, half), row),
                  pl.BlockSpec((tm, d), row),
                  pl.BlockSpec((1, half), fixed), pl.BlockSpec((1, half), fixed),
                  pl.BlockSpec((d, d), fixed), pl.BlockSpec((1, d), fixed),
                  pl.BlockSpec((d, 2 * LANES), fixed), pl.BlockSpec((1, LANES), fixed)],
        out_specs=[pl.BlockSpec((tm, d), row), pl.BlockSpec((tm * (d // LANES), LANES), row),
                   pl.BlockSpec((tm, LANES), row), pl.BlockSpec((tm, LANES), row)],
        out_shape=[jax.ShapeDtypeStruct((n, d), F32), jax.ShapeDtypeStruct((n * (d // LANES), LANES), F32),
                   jax.ShapeDtypeStruct((n, LANES), jnp.int32), jax.ShapeDtypeStruct((n, LANES), F32)],
        compiler_params=_cparams(1),
        name="outproj",
    )(on, of, x2, nnw, fnw, wo, ffw, wr, br)


def _moe_kernel(te_ref, nt_ref, src0_ref, src1_ref, hn_hbm, wg_ref, wu_ref, wd_ref, y_ref,
                xbuf, sem, wg_sc, wu_sc, wd_sc):
    tm = src0_ref.shape[2]
    d = wg_sc.shape[0]
    parts = d // LANES
    i = pl.program_id(0)
    n_used = nt_ref[0]
    slot = i % 2

    def gather(src_ref, slot_):
        def row(r, carry):
            src = pl.multiple_of(src_ref[0, 0, r] * parts, parts)
            dst = (slot_ * tm + r) * GATHER_PITCH
            pltpu.make_async_copy(hn_hbm.at[pl.ds(src, parts)], xbuf.at[pl.ds(dst, parts)], sem.at[slot_]).start()
            return carry
        lax.fori_loop(0, tm, row, 0, unroll=8)

    @pl.when(i == 0)
    def _():
        gather(src0_ref, 0)

    @pl.when(i + 1 < n_used)
    def _():
        gather(src1_ref, 1 - slot)

    prev = te_ref[jnp.maximum(i - 1, 0)]

    @pl.when((i == 0) | (te_ref[i] != prev))
    def _():
        wg_sc[...] = wg_ref[0].astype(BF16)
        wu_sc[...] = wu_ref[0].astype(BF16)
        wd_sc[...] = wd_ref[0].astype(BF16)

    @pl.when(i < n_used)
    def _():
        base = slot * (tm * GATHER_PITCH)
        pltpu.make_async_copy(hn_hbm.at[pl.ds(0, tm * parts)], xbuf.at[pl.ds(base, tm * parts)], sem.at[slot]).wait()
        x = _from_token_rows(xbuf, base, tm, d, GATHER_PITCH).astype(BF16)
        gate = _dot(x, wg_sc[...])
        up = _dot(x, wu_sc[...])
        hid = gate * jax.nn.sigmoid(gate) * up
        _to_token_rows(y_ref, _dot(hid.astype(BF16), wd_sc[...]))

    @pl.when(i >= n_used)
    def _():
        y_ref[...] = jnp.zeros(y_ref.shape, F32)


def _moe(tile_e, src_tok, n_used, hn, wg, wu, wd, tm):
    n_tiles = tile_e.shape[0]
    d = wg.shape[1]
    ff = wg.shape[2]
    parts = d // LANES
    src3 = src_tok.reshape(n_tiles, 1, tm)
    grid_spec = pltpu.PrefetchScalarGridSpec(
        num_scalar_prefetch=2,
        grid=(n_tiles,),
        in_specs=[pl.BlockSpec((1, 1, tm), lambda i, te, nt: (0, 0, 0), memory_space=pltpu.SMEM),
                  pl.BlockSpec((1, 1, tm), lambda i, te, nt: (jnp.minimum(i + 1, n_tiles - 1), 0, 0),
                               memory_space=pltpu.SMEM),
                  pl.BlockSpec(memory_space=pl.ANY),
                  pl.BlockSpec((1, d, ff), lambda i, te, nt: (te[i], 0, 0)),
                  pl.BlockSpec((1, d, ff), lambda i, te, nt: (te[i], 0, 0)),
                  pl.BlockSpec((1, ff, d), lambda i, te, nt: (te[i], 0, 0))],
        out_specs=pl.BlockSpec((tm * parts, LANES), lambda i, te, nt: (i, 0)),
        scratch_shapes=[pltpu.VMEM((2 * tm * GATHER_PITCH, LANES), F32),
                        pltpu.SemaphoreType.DMA((2,)),
                        pltpu.VMEM((d, ff), BF16), pltpu.VMEM((d, ff), BF16), pltpu.VMEM((ff, d), BF16)],
    )
    return pl.pallas_call(
        _moe_kernel,
        grid_spec=grid_spec,
        out_shape=jax.ShapeDtypeStruct((n_tiles * tm * parts, LANES), F32),
        compiler_params=_cparams(1),
        name="moe",
    )(tile_e, n_used, src3, src3, hn, wg, wu, wd)


def _combine_kernel(pos0_ref, pos1_ref, y_hbm, wt_ref, h_ref, fw_ref, o_ref, ybuf, sem):
    tm, d = h_ref.shape
    parts = d // LANES
    i = pl.program_id(0)
    n = pl.num_programs(0)
    slot = i % 2

    def gather(pos_ref, slot_):
        def row(r, carry):
            for k in range(2):
                src = pl.multiple_of(pos_ref[0, 0, 2 * r + k] * parts, parts)
                dst = ((slot_ * 2 + k) * tm + r) * GATHER_PITCH
                pltpu.make_async_copy(y_hbm.at[pl.ds(src, parts)], ybuf.at[pl.ds(dst, parts)], sem.at[slot_]).start()
            return carry
        lax.fori_loop(0, tm, row, 0, unroll=8)

    @pl.when(i == 0)
    def _():
        gather(pos0_ref, 0)

    @pl.when(i + 1 < n)
    def _():
        gather(pos1_ref, 1 - slot)

    rows = tm * GATHER_PITCH
    base = slot * (2 * rows)
    pltpu.make_async_copy(y_hbm.at[pl.ds(0, 2 * tm * parts)], ybuf.at[pl.ds(base, 2 * tm * parts)], sem.at[slot]).wait()
    wt = wt_ref[...]
    out = h_ref[...] + (wt[:, 0:1] * _from_token_rows(ybuf, base, tm, d, GATHER_PITCH)
                        + wt[:, 1:2] * _from_token_rows(ybuf, base + rows, tm, d, GATHER_PITCH))
    o_ref[...] = _rms(out, fw_ref[...])


def _combine(pos, y, wts, h, fw, tm=256):
    n, d = h.shape
    steps = n // tm
    pos3 = pos.reshape(steps, 1, 2 * tm)
    return pl.pallas_call(
        _combine_kernel,
        grid=(steps,),
        in_specs=[pl.BlockSpec((1, 1, 2 * tm), lambda i: (0, 0, 0), memory_space=pltpu.SMEM),
                  pl.BlockSpec((1, 1, 2 * tm), lambda i: (jnp.minimum(i + 1, steps - 1), 0, 0),
                               memory_space=pltpu.SMEM),
                  pl.BlockSpec(memory_space=pl.ANY),
                  pl.BlockSpec((tm, LANES), lambda i: (i, 0)),
                  pl.BlockSpec((tm, d), lambda i: (i, 0)),
                  pl.BlockSpec((1, d), lambda i: (0, 0))],
        out_specs=pl.BlockSpec((tm, d), lambda i: (i, 0)),
        out_shape=jax.ShapeDtypeStruct((n, d), F32),
        scratch_shapes=[pltpu.VMEM((2 * 2 * tm * GATHER_PITCH, LANES), F32), pltpu.SemaphoreType.DMA((2,))],
        compiler_params=_cparams(1),
        name="combine",
    )(pos3, pos3, y, wts, h, fw)


def _biasgen_kernel(tab_ref, bm_ref, bn_ref, bw_ref, om_ref, on_ref, ow_ref):
    h = pl.program_id(0)
    far = tab_ref[REL_BUCKETS - 1, h]

    def build(b_ref, shift):
        idx = b_ref[...]
        out = jnp.full(idx.shape, NEG_INF, F32)
        for bucket in range(REL_BUCKETS):
            out = jnp.where(idx == bucket, (tab_ref[bucket, h] - shift) * LOG2E, out)
        return out

    om_ref[0] = build(bm_ref, 0.0)
    on_ref[0] = build(bn_ref, far)
    ow_ref[0] = build(bw_ref, 0.0)


def _biasgen(rel_table, bm, bn, bw):
    heads = rel_table.shape[1]
    full = lambda a: pl.BlockSpec(a.shape, lambda h: (0, 0))
    out = lambda a: pl.BlockSpec((1,) + a.shape, lambda h: (h, 0, 0))
    return pl.pallas_call(
        _biasgen_kernel,
        grid=(heads,),
        in_specs=[pl.BlockSpec(memory_space=pltpu.SMEM), full(bm), full(bn), full(bw)],
        out_specs=[out(bm), out(bn), out(bw)],
        out_shape=[jax.ShapeDtypeStruct((heads,) + a.shape, F32) for a in (bm, bn, bw)],
        compiler_params=_cparams(1),
        name="biasgen",
    )(rel_table, bm, bn, bw)


def _bias_tables(rel_table, t, ncp):
    tq = AT_TQ
    far = REL_MAX_DIST
    buckets = _bucket_table(far + 1)
    i = np.arange(tq)[:, None]

    def bucket_map(dist, ok):
        return jnp.asarray(np.where(ok, buckets[np.clip(dist, 0, far)], -1).astype(np.int32))

    step = tq // CMP_STRIDE
    u = np.arange(2 * ncp)[None, :]
    dist_m = i - CMP_STRIDE * (u - ncp) - (CMP_BLOCK - 1)
    j = np.arange(SLC_CHUNK)[None, :]
    dist_n = (SLC_CHUNK - tq) + i - j
    j = np.arange(WINDOW + tq)[None, :]
    dist_w = WINDOW + i - j
    master, nbias, wbias = _biasgen(rel_table,
                                    bucket_map(dist_m, np.ones_like(dist_m, bool)),
                                    bucket_map(dist_n, dist_n >= 0),
                                    bucket_map(dist_w, (dist_w >= 0) & (dist_w < WINDOW)))
    master = master.reshape(N_NSA_KV, NSA_GQA, tq, 2 * ncp)
    bias_c = jnp.stack([master[..., ncp - step * qt:2 * ncp - step * qt] for qt in range(t // tq)], axis=2)
    bias_c = bias_c.reshape(N_NSA_KV, NSA_GQA, t, ncp)
    nbias = nbias.reshape(N_NSA_KV, NSA_GQA * tq, SLC_CHUNK)
    wbias = wbias.reshape(N_NSA_KV, NSA_GQA * tq, WINDOW + tq)
    return bias_c, nbias, wbias


def _compress_weights(pe, w1, w2):
    half = CMP_STRIDE
    eye = jnp.eye(N_NSA_KV, dtype=F32)

    def expand_w1(w):
        w = w.reshape(half, HEAD_DIM, CMP_HIDDEN)
        return jnp.einsum("idn,gh->igdhn", w, eye).reshape(half * N_NSA_KV * HEAD_DIM, N_NSA_KV * CMP_HIDDEN)

    def expand_pe(p):
        return jnp.broadcast_to(p[:, None, :], (half, N_NSA_KV, HEAD_DIM)).reshape(1, -1)

    w1a = expand_w1(w1[:half * HEAD_DIM]).astype(BF16)
    w1b = expand_w1(w1[half * HEAD_DIM:]).astype(BF16)
    w2x = jnp.einsum("nd,gh->gnhd", w2, eye).reshape(N_NSA_KV * CMP_HIDDEN, N_NSA_KV * HEAD_DIM).astype(BF16)
    return expand_pe(pe[:half]), expand_pe(pe[half:]), w1a, w1b, w2x


def _routing_tables(eid, tm):
    n = eid.shape[0]
    e_flat = eid.reshape(-1)
    onehot = (e_flat[:, None] == jnp.arange(N_EXPERTS, dtype=jnp.int32)[None, :]).astype(jnp.int32)
    csum = jnp.cumsum(onehot, axis=0)
    counts = csum[-1]
    padded = ((counts + tm - 1) // tm) * tm
    ends = jnp.cumsum(padded)
    starts = ends - padded
    pos = (jnp.sum(onehot * (starts[None, :] + csum), axis=1) - 1).astype(jnp.int32)
    n_tiles = (2 * n) // tm + N_EXPERTS
    rows = n_tiles * tm
    src_pair = jnp.zeros((rows,), jnp.int32).at[pos].set(jnp.arange(2 * n, dtype=jnp.int32))
    src_tok = src_pair // 2
    tile_start = jnp.arange(n_tiles, dtype=jnp.int32) * tm
    tile_e = jnp.sum((ends[None, :] <= tile_start[:, None]).astype(jnp.int32), axis=1)
    tile_e = jnp.minimum(tile_e, N_EXPERTS - 1)
    n_used = (ends[-1] // tm).astype(jnp.int32).reshape(1)
    last_e = tile_e[jnp.maximum(n_used[0] - 1, 0)]
    tile_e = jnp.where(tile_start < ends[-1], tile_e, last_e)
    return tile_e, src_tok, n_used, pos


def kernel(x, attn_norm_w, w_in, cmp_pe_k, cmp_pe_v, cmp_k_w1, cmp_k_w2, cmp_v_w1, cmp_v_w2,
           rel_bias_table, fox_forget_b, nsa_out_norm_w, fox_out_norm_w, w_out, ffn_norm_w,
           router_group_w, router_group_b, router_expert_w, router_expert_b,
           expert_w_gate, expert_w_up, expert_w_down, final_norm_w):
    b, t, d = x.shape
    n = b * t
    depth = w_in.shape[0]
    assert t % 512 == 0 and t >= WINDOW + AT_TQ and d == 2048 and t // SLC_BLOCK <= MASK_BLOCK_LANES
    rows = t // CMP_STRIDE
    bias_c, nbias, wbias = _bias_tables(rel_bias_table, t, rows)
    ns = t // SLC_BLOCK
    ratio = SLC_BLOCK // CMP_STRIDE
    span = CMP_BLOCK // CMP_STRIDE
    nc = (t - CMP_BLOCK) // CMP_STRIDE + 1
    impm = np.zeros((ns, rows), np.float32)
    for blk in range(ns):
        for a in range(ratio):
            for s in range(span):
                c = blk * ratio + a - s
                if 0 <= c < nc:
                    impm[blk, c] += 1.0
    impm = jnp.asarray(impm, BF16)
    eye_q = jnp.eye(AT_TQ, dtype=BF16)
    eye_h = jnp.eye(N_FOX_HEADS, dtype=BF16)
    tri = jnp.asarray(np.tril(np.ones((FOX_CUMSUM_BLOCK, FOX_CUMSUM_BLOCK), np.float32)), BF16)
    moe_tm = 256

    h = x.reshape(n, d)
    for layer in range(depth):
        w_main, w_misc = _project_weights(w_in[layer])
        main, misc = _proj(h, attn_norm_w[layer][None, :], w_main, w_misc)
        main3 = main.reshape(b, t, MAIN_COLS)
        misc3 = misc.reshape(b, t, LANES)

        xkv = jnp.stack([main3[:, :, COL_KCMP:COL_KCMP + 256], main3[:, :, COL_VCMP:COL_VCMP + 256]])
        xkv = xkv.reshape(2, b, rows, CMP_STRIDE * 256)
        pk = _compress_weights(cmp_pe_k[layer], cmp_k_w1[layer], cmp_k_w2[layer])
        pv = _compress_weights(cmp_pe_v[layer], cmp_v_w1[layer], cmp_v_w2[layer])
        kvc = _compress(xkv, *[jnp.stack([a, c]) for a, c in zip(pk, pv)])

        gates = misc3[:, :, MISC_GATE:MISC_GATE + 48].reshape(b, t, N_NSA_KV, 12).transpose(0, 2, 1, 3)
        o_nsa = _nsa(main3, kvc, bias_c, gates, impm, eye_q, nbias, wbias)

        c, ct = _foxprep(misc3, fox_forget_b[layer][None, :], tri, eye_h)
        o_fox = _fox(main3, c, ct)

        wr = jnp.concatenate([router_expert_w[layer], router_group_w[layer]], axis=1)
        wr = jnp.pad(wr, ((0, 0), (0, LANES - wr.shape[1])))
        wr_hi = wr.astype(BF16)
        wr_lo = (wr - wr_hi.astype(F32)).astype(BF16)
        br = jnp.concatenate([router_expert_b[layer], router_group_b[layer]])
        br = jnp.pad(br, (0, LANES - br.shape[0]))[None, :]
        half = N_NSA_HEADS * HEAD_DIM
        h, hn, eid, wts = _outproj(
            o_nsa.reshape(n, half), o_fox.reshape(n, half),
            h, nsa_out_norm_w[layer][None, :], fox_out_norm_w[layer][None, :],
            w_out[layer].astype(BF16), ffn_norm_w[layer][None, :], jnp.concatenate([wr_hi, wr_lo], axis=1), br)

        tile_e, src_tok, n_used, pos = _routing_tables(eid[:, :2], moe_tm)
        y = _moe(tile_e, src_tok, n_used, hn, expert_w_gate[layer], expert_w_up[layer], expert_w_down[layer], moe_tm)
        last = layer == depth - 1
        fw = final_norm_w if last else jnp.ones((d,), F32)
        assert last, "the fused final norm assumes a single layer"
        h = _combine(pos, y, wts, h, fw[None, :])
    return h.reshape(b, t, d)
```

```python
import functools
import math

import numpy as np
import jax
import jax.numpy as jnp
from jax import lax
from jax.experimental import pallas as pl
from jax.experimental.pallas import tpu as pltpu

F32 = jnp.float32
BF16 = jnp.bfloat16

HEAD_DIM = 64
N_NSA_HEADS = 16
N_FOX_HEADS = 16
NSA_GQA = 4
N_NSA_KV = 4
CMP_BLOCK = 32
CMP_STRIDE = 16
CMP_HIDDEN = 128
SLC_BLOCK = 64
SLC_TOP_N = 16
WINDOW = 512
REL_BUCKETS = 32
REL_MAX_DIST = 128
N_GROUPS = 4
EXPERTS_PER_GROUP = 8
N_EXPERTS = 32
EXPERT_FF = 512
NORM_EPS = 1e-6
NEG_INF = -1e30
FORCE_BONUS = 1e4
SCALE = HEAD_DIM ** -0.5
LOG2E = math.log2(math.e)
Q_SCALE = SCALE * LOG2E

LANES = 128
VMEM_LIMIT = 56 * 1024 * 1024

COL_NQ = 0
COL_KCMP = 1024
COL_VCMP = 1280
COL_SLC = 1536
COL_WIN = 2048
COL_FQ = 2560
COL_FK = 3584
COL_FV = 4608
MAIN_COLS = 5632
MISC_GATE = 0
MISC_FF = 48

AT_TQ = 256
NSA_GROUPS_PER_STEP = 2
FOX_PAIRS_PER_STEP = 4
FOX_CUMSUM_BLOCK = 256
SLC_CHUNK = 512
SLC_SHIFT = 6
assert 1 << SLC_SHIFT == SLC_BLOCK
MASK_BLOCK_LANES = 32
PAD_LANE = HEAD_DIM + MASK_BLOCK_LANES
KEY_PAD = 512


def _nt(a, b):
    return lax.dot_general(a, b, (((1,), (1,)), ((), ())), preferred_element_type=F32)


def _dot(a, b):
    return jnp.dot(a, b, preferred_element_type=F32)


def _split3(x):
    hi = x.astype(BF16)
    r = x - hi.astype(F32)
    mid = r.astype(BF16)
    r = r - mid.astype(F32)
    return hi, mid, r.astype(BF16)


def _cparams(grid_rank):
    return pltpu.CompilerParams(dimension_semantics=("arbitrary",) * grid_rank, vmem_limit_bytes=VMEM_LIMIT)


def _bucket_table(n):
    d = np.arange(n, dtype=np.int64)
    max_exact = REL_BUCKETS // 2
    rel = np.log(np.maximum(d, 1).astype(np.float64) / max_exact) / math.log(REL_MAX_DIST / max_exact)
    scaled = rel * (REL_BUCKETS - max_exact)
    frac = scaled - np.floor(scaled)
    inner = (d > max_exact) & (d < REL_MAX_DIST)
    assert np.all((frac[inner] > 1e-3) & (frac[inner] < 1 - 1e-3))
    large = np.minimum(max_exact + np.floor(scaled + 1e-6).astype(np.int64), REL_BUCKETS - 1)
    return np.where(d < max_exact, d, large).astype(np.int32)


def _project_weights(w):
    d = w.shape[0]
    sizes = [1024] + [256] * 6 + [48, 1024, 1024, 1024, 16]
    offs = np.concatenate([[0], np.cumsum(sizes)])
    nq, kcmp, vcmp, kslc, vslc, kwin, vwin, ngate, fq, fk, fv, ff = [
        w[:, int(offs[i]):int(offs[i + 1])] for i in range(12)]

    def interleave(k, v):
        k = k.reshape(d, N_NSA_KV, HEAD_DIM)
        v = v.reshape(d, N_NSA_KV, HEAD_DIM)
        return jnp.stack([k, v], axis=2).reshape(d, N_NSA_KV * 2 * HEAD_DIM)

    main = jnp.concatenate([nq * Q_SCALE, kcmp, vcmp, interleave(kslc, vslc), interleave(kwin, vwin),
                            fq * Q_SCALE, fk, fv], axis=1)
    assert main.shape[1] == MAIN_COLS
    misc = jnp.concatenate([ngate, ff, jnp.zeros((d, LANES - 64), w.dtype)], axis=1)
    return main.astype(BF16), misc.astype(BF16)


def _proj_kernel(x_ref, nw_ref, w_ref, wm_ref, o_ref, om_ref, *, tn):
    x = x_ref[...]
    y = x * lax.rsqrt(jnp.mean(x * x, axis=-1, keepdims=True) + NORM_EPS) * nw_ref[...]
    xn = y.astype(BF16)
    om_ref[...] = _dot(xn, wm_ref[...])
    for c in range(o_ref.shape[1] // tn):
        o_ref[:, c * tn:(c + 1) * tn] = _dot(xn, w_ref[:, c * tn:(c + 1) * tn]).astype(BF16)


def _proj(x2, norm_w, w_main, w_misc, tm=512, tn=512):
    n, d = x2.shape
    once = pl.Buffered(1)
    return pl.pallas_call(
        functools.partial(_proj_kernel, tn=tn),
        grid=(n // tm,),
        in_specs=[pl.BlockSpec((tm, d), lambda i: (i, 0)),
                  pl.BlockSpec((1, d), lambda i: (0, 0)),
                  pl.BlockSpec((d, MAIN_COLS), lambda i: (0, 0), pipeline_mode=once),
                  pl.BlockSpec((d, LANES), lambda i: (0, 0), pipeline_mode=once)],
        out_specs=[pl.BlockSpec((tm, MAIN_COLS), lambda i: (i, 0)),
                   pl.BlockSpec((tm, LANES), lambda i: (i, 0))],
        out_shape=[jax.ShapeDtypeStruct((n, MAIN_COLS), BF16),
                   jax.ShapeDtypeStruct((n, LANES), F32)],
        compiler_params=_cparams(1),
        name="proj",
    )(x2, norm_w, w_main, w_misc)


def _compress_kernel(x_ref, pea_ref, peb_ref, w1a_ref, w1b_ref, w2_ref, o_ref):
    x = x_ref[0, 0].astype(F32)
    xa = (x + pea_ref[0]).astype(BF16)
    xb = (x + peb_ref[0]).astype(BF16)
    a = _dot(xa, w1a_ref[0])
    b = _dot(xb, w1b_ref[0])
    rows = a.shape[0]
    pre = a + pltpu.roll(b, rows - 1, 0)
    hid = pre * jax.nn.sigmoid(pre)
    out = _dot(hid.astype(BF16), w2_ref[0])
    for g in range(N_NSA_KV):
        blk = out[:, g * HEAD_DIM:(g + 1) * HEAD_DIM]
        o_ref[0, 0, g] = jnp.concatenate([blk, blk], axis=1)


def _compress(xkv, pea, peb, w1a, w1b, w2):
    _, b, rows, width = xkv.shape
    hid = N_NSA_KV * CMP_HIDDEN
    return pl.pallas_call(
        _compress_kernel,
        grid=(2, b),
        in_specs=[pl.BlockSpec((1, 1, rows, width), lambda s, i: (s, i, 0, 0)),
                  pl.BlockSpec((1, 1, width), lambda s, i: (s, 0, 0)),
                  pl.BlockSpec((1, 1, width), lambda s, i: (s, 0, 0)),
                  pl.BlockSpec((1, width, hid), lambda s, i: (s, 0, 0)),
                  pl.BlockSpec((1, width, hid), lambda s, i: (s, 0, 0)),
                  pl.BlockSpec((1, hid, N_NSA_KV * HEAD_DIM), lambda s, i: (s, 0, 0))],
        out_specs=pl.BlockSpec((1, 1, N_NSA_KV, rows, LANES), lambda s, i: (s, i, 0, 0, 0)),
        out_shape=jax.ShapeDtypeStruct((2, b, N_NSA_KV, rows, LANES), F32),
        compiler_params=_cparams(2),
        name="compress",
    )(xkv, pea, peb, w1a, w1b, w2)


def _foxprep_kernel(misc_ref, fb_ref, tri_ref, eye_ref, c_ref, ct_ref):
    z = misc_ref[0][:, MISC_FF:MISC_FF + N_FOX_HEADS] + fb_ref[...]
    logf = (jnp.minimum(z, 0.0) - jnp.log(1.0 + jnp.exp(-jnp.abs(z)))) * LOG2E
    tri = tri_ref[...]
    blk = tri.shape[0]
    parts = _split3(logf)
    total = jnp.zeros((1, N_FOX_HEADS), F32)
    pieces = []
    for i in range(logf.shape[0] // blk):
        piece = total
        for part in parts:
            piece = piece + _dot(tri, part[i * blk:(i + 1) * blk])
        total = piece[blk - 1:blk]
        pieces.append(piece)
    c = jnp.concatenate(pieces, axis=0)
    ct = None
    for part in _split3(c):
        term = _nt(eye_ref[...], part)
        ct = term if ct is None else ct + term
    for p in range(N_FOX_HEADS // 2):
        c_ref[0, p] = c[:, 2 * p:2 * p + 2]
        ct_ref[0, p] = ct[2 * p:2 * p + 2, :]


def _foxprep(misc3, fb, tri, eye):
    b, t, _ = misc3.shape
    hp = N_FOX_HEADS // 2
    return pl.pallas_call(
        _foxprep_kernel,
        grid=(b,),
        in_specs=[pl.BlockSpec((1, t, LANES), lambda i: (i, 0, 0)),
                  pl.BlockSpec((1, N_FOX_HEADS), lambda i: (0, 0)),
                  pl.BlockSpec(tri.shape, lambda i: (0, 0)),
                  pl.BlockSpec((N_FOX_HEADS, N_FOX_HEADS), lambda i: (0, 0))],
        out_specs=[pl.BlockSpec((1, hp, t, 2), lambda i: (i, 0, 0, 0)),
                   pl.BlockSpec((1, hp, 2, t), lambda i: (i, 0, 0, 0))],
        out_shape=[jax.ShapeDtypeStruct((b, hp, t, 2), F32),
                   jax.ShapeDtypeStruct((b, hp, 2, t), F32)],
        compiler_params=_cparams(1),
        name="foxprep",
    )(misc3, fb, tri, eye)


def _softmax_first(s, v):
    m = jnp.max(s, axis=-1, keepdims=True)
    p = jnp.exp2(s - m)
    return m, jnp.sum(p, axis=-1, keepdims=True), _dot(p.astype(BF16), v)


def _softmax_next(s, v, m, l, acc):
    m_new = jnp.maximum(m, jnp.max(s, axis=-1, keepdims=True))
    alpha = jnp.exp2(m - m_new)
    p = jnp.exp2(s - m_new)
    return m_new, alpha * l + jnp.sum(p, axis=-1, keepdims=True), alpha * acc + _dot(p.astype(BF16), v)


def _stack_heads(q, tail, n_heads):
    return jnp.concatenate(
        [jnp.concatenate([q[:, r * HEAD_DIM:(r + 1) * HEAD_DIM], tail], axis=1) for r in range(n_heads)], axis=0)


def _fill_key_scratch(kp_sc, kvp_sc, kv, with_blocks):
    t = kv.shape[0]
    lane = lax.broadcasted_iota(jnp.int32, (t, LANES), 1)
    if with_blocks:
        blk = lax.shift_right_logical(lax.broadcasted_iota(jnp.int32, (t, LANES), 0), SLC_SHIFT)
        aug = jnp.where(lane - HEAD_DIM == blk, 1.0, 0.0).astype(BF16)
    else:
        aug = jnp.zeros((t, LANES), BF16)
    kp_sc[KEY_PAD:, :] = jnp.where(lane < HEAD_DIM, kv, aug)
    lane_p = lax.broadcasted_iota(jnp.int32, (KEY_PAD, LANES), 1)
    kp_sc[0:KEY_PAD, :] = jnp.where(lane_p == PAD_LANE, 1.0, 0.0).astype(BF16)
    kvp_sc[KEY_PAD:, :] = jnp.where(lane < HEAD_DIM, jnp.where(lane == 0, 1.0, 0.0).astype(BF16), kv)
    kvp_sc[0:KEY_PAD, :] = jnp.zeros((KEY_PAD, LANES), BF16)


def _nsa_first(s, v):
    m = jnp.max(s, axis=-1, keepdims=True)
    return m, _dot(jnp.exp2(s - m).astype(BF16), v)


def _nsa_next(s, v, m, acc):
    m_new = jnp.maximum(m, jnp.max(s, axis=-1, keepdims=True))
    return m_new, jnp.exp2(m - m_new) * acc + _dot(jnp.exp2(s - m_new).astype(BF16), v)


def _nsa_kernel(q_ref, kvc_ref, cbias_ref, gate_ref, impm_ref, eye_ref, skv_ref, nbias_ref, wkv_ref, wbias_ref,
                o_ref, skp_sc, skvp_sc, wkp_sc, wkvp_sc):
    tq = q_ref.shape[1]
    ncp = kvc_ref.shape[3]
    ns = impm_ref.shape[0]
    chunk = SLC_CHUNK
    span = wbias_ref.shape[2]
    qw = NSA_GQA * HEAD_DIM
    rows = NSA_GQA * tq
    qt = pl.program_id(2)
    t0 = qt * tq
    groups = range(skp_sc.shape[0])

    @pl.when(qt == 0)
    def _():
        for c in groups:
            _fill_key_scratch(skp_sc.at[c], skvp_sc.at[c], skv_ref[0, :, c * LANES:(c + 1) * LANES], True)
            _fill_key_scratch(wkp_sc.at[c], wkvp_sc.at[c], wkv_ref[0, :, c * LANES:(c + 1) * LANES], False)

    t_col = t0 + (lax.broadcasted_iota(jnp.int32, (rows, ncp), 0) & (tq - 1))
    c_row = lax.broadcasted_iota(jnp.int32, (rows, ncp), 1)
    valid = t_col >= c_row * CMP_STRIDE + (CMP_BLOCK - 1)
    blk = lax.broadcasted_iota(jnp.int32, (ns, tq), 0)
    t_row = t0 + lax.broadcasted_iota(jnp.int32, (ns, tq), 1)
    cur = t_row // SLC_BLOCK
    bonus = jnp.where((blk == 0) | (blk == cur) | (blk == cur - 1), FORCE_BONUS, 0.0)
    blk_valid = blk * SLC_BLOCK <= t_row
    pad_row = jnp.where(lax.broadcasted_iota(jnp.int32, (HEAD_DIM - MASK_BLOCK_LANES, tq), 0) == 0, 1.0, 0.0)
    lane = lax.broadcasted_iota(jnp.int32, (rows, LANES), 1)
    upper = lane >= HEAD_DIM
    win_tail = jnp.where(lane == PAD_LANE, NEG_INF, 0.0).astype(BF16)
    zero_tail = jnp.zeros((tq, HEAD_DIM), BF16)
    q_all = q_ref[0]

    qst, sel_tail, o_cmp = [], [], []
    for c in groups:
        qs = _stack_heads(q_all[:, c * qw:(c + 1) * qw], zero_tail, NSA_GQA)
        kc = kvc_ref[0, 0, c].astype(BF16)
        vc = kvc_ref[1, 0, c].astype(BF16)
        s = jnp.where(valid, _nt(qs, kc) + cbias_ref[c].reshape(rows, ncp), NEG_INF)
        m = jnp.max(s, axis=-1, keepdims=True)
        e = jnp.where(valid, jnp.exp2(s - m), 0.0)
        p = e / jnp.maximum(jnp.sum(e, axis=-1, keepdims=True), 1e-30)
        o_cmp.append(_dot(p.astype(BF16), vc))
        p_grp = p[0:tq]
        for r in range(1, NSA_GQA):
            p_grp = p_grp + p[r * tq:(r + 1) * tq]

        imp = None
        for part in _split3(p_grp):
            term = _nt(impm_ref[...], part)
            imp = term if imp is None else imp + term
        score = jnp.where(blk_valid, imp + bonus, NEG_INF)
        rank = jnp.zeros((ns, tq), F32)
        for m_blk in range(ns):
            other = score[m_blk:m_blk + 1, :]
            ahead = (other > score) | ((other == score) & (blk > m_blk))
            rank = rank + jnp.where(ahead, 1.0, 0.0)
        parts = [jnp.where(rank < float(min(SLC_TOP_N, ns)), 0.0, 1.0), pad_row]
        if ns < MASK_BLOCK_LANES:
            parts.insert(1, jnp.zeros((MASK_BLOCK_LANES - ns, tq), F32))
        flags = _nt(eye_ref[...], jnp.concatenate(parts, axis=0).astype(BF16))
        tail = jnp.concatenate([zero_tail, (flags * NEG_INF).astype(BF16)], axis=1)
        qst.append(qs)
        sel_tail.append(jnp.concatenate([tail] * NSA_GQA, axis=0))

    qsel = [jnp.where(upper, sel_tail[c], qst[c]) for c in groups]
    near = pl.multiple_of(t0 + tq - chunk + KEY_PAD, tq)
    state = tuple(_nsa_first(_nt(qsel[c], skp_sc[c, pl.ds(near, chunk), :]) + nbias_ref[c],
                             skvp_sc[c, pl.ds(near, chunk), :]) for c in groups)

    def far(j, carry):
        first = pl.multiple_of(near - (j + 1) * chunk, tq)
        return tuple(_nsa_next(_nt(qsel[c], skp_sc[c, pl.ds(first, chunk), :]),
                               skvp_sc[c, pl.ds(first, chunk), :], *carry[c]) for c in groups)

    state = lax.fori_loop(0, (t0 + tq - 1) // chunk, far, state)

    wfirst = pl.multiple_of(t0, tq)
    for c in groups:
        qwin = jnp.where(upper, win_tail, qst[c])
        _, acc_w = _nsa_first(_nt(qwin, wkp_sc[c, pl.ds(wfirst, span), :]) + wbias_ref[c],
                              wkvp_sc[c, pl.ds(wfirst, span), :])
        acc_s = state[c][1]
        gate = jax.nn.sigmoid(gate_ref[0, c])
        g = [jnp.concatenate([gate[:, 3 * r + j:3 * r + j + 1] for r in range(NSA_GQA)], axis=0) for j in range(3)]
        tot = o_cmp[c] * g[0] + acc_s * (g[1] / acc_s[:, 0:1]) + acc_w * (g[2] / acc_w[:, 0:1])
        out = tot[:, HEAD_DIM:].astype(o_ref.dtype)
        for r in range(NSA_GQA):
            col = (c * NSA_GQA + r) * HEAD_DIM
            o_ref[0, :, col:col + HEAD_DIM] = out[r * tq:(r + 1) * tq]


def _nsa(main3, kvc, bias_c, gates, impm, eye, nbias, wbias):
    b, t, _ = main3.shape
    ncp = kvc.shape[3]
    ns = impm.shape[0]
    tq = AT_TQ
    qw = NSA_GQA * HEAD_DIM
    gs = NSA_GROUPS_PER_STEP
    kvw = gs * LANES
    span = wbias.shape[2]
    assert KEY_PAD >= SLC_CHUNK and KEY_PAD % tq == 0 and span - tq == KEY_PAD
    assert COL_SLC % kvw == 0 and COL_WIN % kvw == 0
    key_scratch = pltpu.VMEM((gs, t + KEY_PAD, LANES), BF16)
    return pl.pallas_call(
        _nsa_kernel,
        grid=(b, N_NSA_KV // gs, t // tq),
        in_specs=[pl.BlockSpec((1, tq, gs * qw), lambda i, g, j: (i, j, g)),
                  pl.BlockSpec((2, 1, gs, ncp, LANES), lambda i, g, j: (0, i, g, 0, 0)),
                  pl.BlockSpec((gs, NSA_GQA, tq, ncp), lambda i, g, j: (g, 0, j, 0)),
                  pl.BlockSpec((1, gs, tq, 3 * NSA_GQA), lambda i, g, j: (i, g, j, 0)),
                  pl.BlockSpec((ns, ncp), lambda i, g, j: (0, 0)),
                  pl.BlockSpec((tq, tq), lambda i, g, j: (0, 0)),
                  pl.BlockSpec((1, t, kvw), lambda i, g, j: (i, 0, COL_SLC // kvw + g)),
                  pl.BlockSpec((gs, NSA_GQA * tq, SLC_CHUNK), lambda i, g, j: (g, 0, 0)),
                  pl.BlockSpec((1, t, kvw), lambda i, g, j: (i, 0, COL_WIN // kvw + g)),
                  pl.BlockSpec((gs, NSA_GQA * tq, span), lambda i, g, j: (g, 0, 0))],
        out_specs=pl.BlockSpec((1, tq, gs * qw), lambda i, g, j: (i, j, g)),
        out_shape=jax.ShapeDtypeStruct((b, t, N_NSA_HEADS * HEAD_DIM), BF16),
        scratch_shapes=[key_scratch, key_scratch, key_scratch, key_scratch],
        compiler_params=_cparams(3),
        name="nsa",
    )(main3, kvc, bias_c, gates, impm, eye, main3, nbias, main3, wbias)


def _fox_kernel(q_ref, k_ref, v_ref, c_ref, ct_ref, o_ref, *, chunk):
    tq = q_ref.shape[1]
    t0 = pl.program_id(2) * tq
    pairs = range(c_ref.shape[1])
    lane = lax.broadcasted_iota(jnp.int32, (tq, LANES), 1)
    low = lane < HEAD_DIM
    q_all = q_ref[0]
    zero = jnp.zeros((tq, LANES), BF16)
    qst, c_col = [], []
    for p in pairs:
        q = q_all[:, p * LANES:(p + 1) * LANES]
        qst.append(jnp.concatenate([jnp.where(low, q, zero), jnp.where(low, zero, q)], axis=0))
        c_col.append(c_ref[0, p])

    def logits(p, start):
        s = _nt(qst[p], k_ref[0, pl.ds(start, chunk), p * LANES:(p + 1) * LANES])
        c_row = ct_ref[0, p, :, pl.ds(start, chunk)]
        return jnp.concatenate([s[:tq] + c_col[p][:, 0:1] - c_row[0:1],
                                s[tq:] + c_col[p][:, 1:2] - c_row[1:2]], axis=0)

    def values(p, start):
        return v_ref[0, pl.ds(start, chunk), p * LANES:(p + 1) * LANES]

    d0 = pl.multiple_of((t0 // chunk) * chunk, chunk)
    row = lax.broadcasted_iota(jnp.int32, (2 * tq, chunk), 0) & (tq - 1)
    causal = row - lax.broadcasted_iota(jnp.int32, (2 * tq, chunk), 1) + (t0 - d0) >= 0
    state = tuple(_softmax_first(jnp.where(causal, logits(p, d0), NEG_INF), values(p, d0)) for p in pairs)

    def below(j, carry):
        start = pl.multiple_of(j * chunk, chunk)
        return tuple(_softmax_next(logits(p, start), values(p, start), *carry[p]) for p in pairs)

    state = lax.fori_loop(0, t0 // chunk, below, state)
    for p in pairs:
        _, l, acc = state[p]
        o = acc / l
        o_ref[0, :, p * LANES:(p + 1) * LANES] = jnp.where(low, o[:tq], o[tq:]).astype(o_ref.dtype)


def _fox(main3, c, ct, tq=256, chunk=512):
    b, t, _ = main3.shape
    ps = FOX_PAIRS_PER_STEP
    hp = N_FOX_HEADS // 2
    w = ps * LANES
    assert chunk % tq == 0 and t % chunk == 0
    assert COL_FQ % w == 0 and COL_FK % w == 0 and COL_FV % w == 0
    return pl.pallas_call(
        functools.partial(_fox_kernel, chunk=chunk),
        grid=(b, hp // ps, t // tq),
        in_specs=[pl.BlockSpec((1, tq, w), lambda i, p, j: (i, j, COL_FQ // w + p)),
                  pl.BlockSpec((1, t, w), lambda i, p, j: (i, 0, COL_FK // w + p)),
                  pl.BlockSpec((1, t, w), lambda i, p, j: (i, 0, COL_FV // w + p)),
                  pl.BlockSpec((1, ps, tq, 2), lambda i, p, j: (i, p, j, 0)),
                  pl.BlockSpec((1, ps, 2, t), lambda i, p, j: (i, p, 0, 0))],
        out_specs=pl.BlockSpec((1, tq, w), lambda i, p, j: (i, j, p)),
        out_shape=jax.ShapeDtypeStruct((b, t, N_FOX_HEADS * HEAD_DIM), BF16),
        compiler_params=_cparams(3),
        name="fox",
    )(main3, main3, main3, c, ct)


def _rms(x, w):
    return x * lax.rsqrt(jnp.mean(x * x, axis=-1, keepdims=True) + NORM_EPS) * w


def _to_token_rows(ref, x):
    tm, d = x.shape
    parts = d // LANES
    for a in range(parts):
        ref[pl.ds(a, tm, stride=parts), :] = x[:, a * LANES:(a + 1) * LANES]


def _from_token_rows(ref, base, tm, d, pitch):
    return jnp.concatenate([ref[pl.ds(base + a, tm, stride=pitch), :] for a in range(d // LANES)], axis=1)


GATHER_PITCH = 20


def _outproj_kernel(on_ref, of_ref, x_ref, nnw_ref, fnw_ref, wo_ref, ffw_ref,
                    wr_ref, br_ref, h_ref, hn_ref, eid_ref, wt_ref):
    mixed = jnp.concatenate([_rms(on_ref[...].astype(F32), nnw_ref[...]),
                             _rms(of_ref[...].astype(F32), fnw_ref[...])], axis=-1)
    h = x_ref[...] + _dot(mixed.astype(BF16), wo_ref[...])
    h_ref[...] = h
    hn = _rms(h, ffw_ref[...])
    _to_token_rows(hn_ref, hn)

    h_hi, h_mid, _ = _split3(hn)
    both = _dot(h_hi, wr_ref[...])
    logits = (both[:, :LANES] + _dot(h_mid, wr_ref[:, :LANES]) + both[:, LANES:]) + br_ref[...]
    tm = logits.shape[0]
    lane = lax.broadcasted_iota(jnp.int32, (tm, LANES), 1)
    big = jnp.int32(LANES)
    is_grp = (lane >= N_EXPERTS) & (lane < N_EXPERTS + N_GROUPS)
    glog = jnp.where(is_grp, logits, NEG_INF)
    gmax = jnp.max(glog, axis=-1, keepdims=True)
    gsel = jnp.min(jnp.where(glog == gmax, lane, big), axis=-1, keepdims=True) - N_EXPERTS
    p_gsel = 1.0 / jnp.sum(jnp.where(is_grp, jnp.exp(glog - gmax), 0.0), axis=-1, keepdims=True)
    in_grp = (lane < N_EXPERTS) & (lane // EXPERTS_PER_GROUP == gsel)
    e1 = jnp.where(in_grp, logits, NEG_INF)
    v1 = jnp.max(e1, axis=-1, keepdims=True)
    i1 = jnp.min(jnp.where(e1 == v1, lane, big), axis=-1, keepdims=True)
    e2 = jnp.where(lane == i1, NEG_INF, e1)
    v2 = jnp.max(e2, axis=-1, keepdims=True)
    i2 = jnp.min(jnp.where(e2 == v2, lane, big), axis=-1, keepdims=True)
    ex = jnp.exp(v2 - v1)
    w1 = p_gsel / (1.0 + ex)
    w2 = p_gsel * ex / (1.0 + ex)
    eid_ref[...] = jnp.where(lane == 0, i1, jnp.where(lane == 1, i2, 0))
    wt_ref[...] = jnp.where(lane == 0, w1, jnp.where(lane == 1, w2, 0.0))


def _outproj(on, of, x2, nnw, fnw, wo, ffw, wr, br, tm=512):
    n, d = x2.shape
    half = on.shape[1]
    row = lambda i: (i, 0)
    fixed = lambda i: (0, 0)
    return pl.pallas_call(
        _outproj_kernel,
        grid=(n // tm,),
        in_specs=[pl.BlockSpec((tm, half), row), pl.BlockSpec((tm, half), row),
                  pl.BlockSpec((tm, d), row),
                  pl.BlockSpec((1, half), fixed), pl.BlockSpec((1, half), fixed),
                  pl.BlockSpec((d, d), fixed), pl.BlockSpec((1, d), fixed),
                  pl.BlockSpec((d, 2 * LANES), fixed), pl.BlockSpec((1, LANES), fixed)],
        out_specs=[pl.BlockSpec((tm, d), row), pl.BlockSpec((tm * (d // LANES), LANES), row),
                   pl.BlockSpec((tm, LANES), row), pl.BlockSpec((tm, LANES), row)],
        out_shape=[jax.ShapeDtypeStruct((n, d), F32), jax.ShapeDtypeStruct((n * (d // LANES), LANES), F32),
                   jax.ShapeDtypeStruct((n, LANES), jnp.int32), jax.ShapeDtypeStruct((n, LANES), F32)],
        compiler_params=_cparams(1),
        name="outproj",
    )(on, of, x2, nnw, fnw, wo, ffw, wr, br)


def _moe_kernel(te_ref, nt_ref, src0_ref, src1_ref, hn_hbm, wg_ref, wu_ref, wd_ref, y_ref,
                xbuf, sem, wg_sc, wu_sc, wd_sc):
    tm = src0_ref.shape[2]
    d = wg_sc.shape[0]
    parts = d // LANES
    i = pl.program_id(0)
    n_used = nt_ref[0]
    slot = i % 2

    def gather(src_ref, slot_):
        def row(r, carry):
            src = pl.multiple_of(src_ref[0, 0, r] * parts, parts)
            dst = (slot_ * tm + r) * GATHER_PITCH
            pltpu.make_async_copy(hn_hbm.at[pl.ds(src, parts)], xbuf.at[pl.ds(dst, parts)], sem.at[slot_]).start()
            return carry
        lax.fori_loop(0, tm, row, 0, unroll=8)

    @pl.when(i == 0)
    def _():
        gather(src0_ref, 0)

    @pl.when(i + 1 < n_used)
    def _():
        gather(src1_ref, 1 - slot)

    prev = te_ref[jnp.maximum(i - 1, 0)]

    @pl.when((i == 0) | (te_ref[i] != prev))
    def _():
        wg_sc[...] = wg_ref[0].astype(BF16)
        wu_sc[...] = wu_ref[0].astype(BF16)
        wd_sc[...] = wd_ref[0].astype(BF16)

    @pl.when(i < n_used)
    def _():
        base = slot * (tm * GATHER_PITCH)
        pltpu.make_async_copy(hn_hbm.at[pl.ds(0, tm * parts)], xbuf.at[pl.ds(base, tm * parts)], sem.at[slot]).wait()
        x = _from_token_rows(xbuf, base, tm, d, GATHER_PITCH).astype(BF16)
        gate = _dot(x, wg_sc[...])
        up = _dot(x, wu_sc[...])
        hid = gate * jax.nn.sigmoid(gate) * up
        _to_token_rows(y_ref, _dot(hid.astype(BF16), wd_sc[...]))

    @pl.when(i >= n_used)
    def _():
        y_ref[...] = jnp.zeros(y_ref.shape, F32)


def _moe(tile_e, src_tok, n_used, hn, wg, wu, wd, tm):
    n_tiles = tile_e.shape[0]
    d = wg.shape[1]
    ff = wg.shape[2]
    parts = d // LANES
    src3 = src_tok.reshape(n_tiles, 1, tm)
    grid_spec = pltpu.PrefetchScalarGridSpec(
        num_scalar_prefetch=2,
        grid=(n_tiles,),
        in_specs=[pl.BlockSpec((1, 1, tm), lambda i, te, nt: (0, 0, 0), memory_space=pltpu.SMEM),
                  pl.BlockSpec((1, 1, tm), lambda i, te, nt: (jnp.minimum(i + 1, n_tiles - 1), 0, 0),
                               memory_space=pltpu.SMEM),
                  pl.BlockSpec(memory_space=pl.ANY),
                  pl.BlockSpec((1, d, ff), lambda i, te, nt: (te[i], 0, 0)),
                  pl.BlockSpec((1, d, ff), lambda i, te, nt: (te[i], 0, 0)),
                  pl.BlockSpec((1, ff, d), lambda i, te, nt: (te[i], 0, 0))],
        out_specs=pl.BlockSpec((tm * parts, LANES), lambda i, te, nt: (i, 0)),
        scratch_shapes=[pltpu.VMEM((2 * tm * GATHER_PITCH, LANES), F32),
                        pltpu.SemaphoreType.DMA((2,)),
                        pltpu.VMEM((d, ff), BF16), pltpu.VMEM((d, ff), BF16), pltpu.VMEM((ff, d), BF16)],
    )
    return pl.pallas_call(
        _moe_kernel,
        grid_spec=grid_spec,
        out_shape=jax.ShapeDtypeStruct((n_tiles * tm * parts, LANES), F32),
        compiler_params=_cparams(1),
        name="moe",
    )(tile_e, n_used, src3, src3, hn, wg, wu, wd)


def _combine_kernel(pos0_ref, pos1_ref, y_hbm, wt_ref, h_ref, fw_ref, o_ref, ybuf, sem):
    tm, d = h_ref.shape
    parts = d // LANES
    i = pl.program_id(0)
    n = pl.num_programs(0)
    slot = i % 2

    def gather(pos_ref, slot_):
        def row(r, carry):
            for k in range(2):
                src = pl.multiple_of(pos_ref[0, 0, 2 * r + k] * parts, parts)
                dst = ((slot_ * 2 + k) * tm + r) * GATHER_PITCH
                pltpu.make_async_copy(y_hbm.at[pl.ds(src, parts)], ybuf.at[pl.ds(dst, parts)], sem.at[slot_]).start()
            return carry
        lax.fori_loop(0, tm, row, 0, unroll=8)

    @pl.when(i == 0)
    def _():
        gather(pos0_ref, 0)

    @pl.when(i + 1 < n)
    def _():
        gather(pos1_ref, 1 - slot)

    rows = tm * GATHER_PITCH
    base = slot * (2 * rows)
    pltpu.make_async_copy(y_hbm.at[pl.ds(0, 2 * tm * parts)], ybuf.at[pl.ds(base, 2 * tm * parts)], sem.at[slot]).wait()
    wt = wt_ref[...]
    out = h_ref[...] + (wt[:, 0:1] * _from_token_rows(ybuf, base, tm, d, GATHER_PITCH)
                        + wt[:, 1:2] * _from_token_rows(ybuf, base + rows, tm, d, GATHER_PITCH))
    o_ref[...] = _rms(out, fw_ref[...])


def _combine(pos, y, wts, h, fw, tm=256):
    n, d = h.shape
    steps = n // tm
    pos3 = pos.reshape(steps, 1, 2 * tm)
    return pl.pallas_call(
        _combine_kernel,
        grid=(steps,),
        in_specs=[pl.BlockSpec((1, 1, 2 * tm), lambda i: (0, 0, 0), memory_space=pltpu.SMEM),
                  pl.BlockSpec((1, 1, 2 * tm), lambda i: (jnp.minimum(i + 1, steps - 1), 0, 0),
                               memory_space=pltpu.SMEM),
                  pl.BlockSpec(memory_space=pl.ANY),
                  pl.BlockSpec((tm, LANES), lambda i: (i, 0)),
                  pl.BlockSpec((tm, d), lambda i: (i, 0)),
                  pl.BlockSpec((1, d), lambda i: (0, 0))],
        out_specs=pl.BlockSpec((tm, d), lambda i: (i, 0)),
        out_shape=jax.ShapeDtypeStruct((n, d), F32),
        scratch_shapes=[pltpu.VMEM((2 * 2 * tm * GATHER_PITCH, LANES), F32), pltpu.SemaphoreType.DMA((2,))],
        compiler_params=_cparams(1),
        name="combine",
    )(pos3, pos3, y, wts, h, fw)


def _biasgen_kernel(tab_ref, bm_ref, bn_ref, bw_ref, om_ref, on_ref, ow_ref):
    h = pl.program_id(0)
    far = tab_ref[REL_BUCKETS - 1, h]

    def build(b_ref, shift):
        idx = b_ref[...]
        out = jnp.full(idx.shape, NEG_INF, F32)
        for bucket in range(REL_BUCKETS):
            out = jnp.where(idx == bucket, (tab_ref[bucket, h] - shift) * LOG2E, out)
        return out

    om_ref[0] = build(bm_ref, 0.0)
    on_ref[0] = build(bn_ref, far)
    ow_ref[0] = build(bw_ref, 0.0)


def _biasgen(rel_table, bm, bn, bw):
    heads = rel_table.shape[1]
    full = lambda a: pl.BlockSpec(a.shape, lambda h: (0, 0))
    out = lambda a: pl.BlockSpec((1,) + a.shape, lambda h: (h, 0, 0))
    return pl.pallas_call(
        _biasgen_kernel,
        grid=(heads,),
        in_specs=[pl.BlockSpec(memory_space=pltpu.SMEM), full(bm), full(bn), full(bw)],
        out_specs=[out(bm), out(bn), out(bw)],
        out_shape=[jax.ShapeDtypeStruct((heads,) + a.shape, F32) for a in (bm, bn, bw)],
        compiler_params=_cparams(1),
        name="biasgen",
    )(rel_table, bm, bn, bw)


def _bias_tables(rel_table, t, ncp):
    tq = AT_TQ
    far = REL_MAX_DIST
    buckets = _bucket_table(far + 1)
    i = np.arange(tq)[:, None]

    def bucket_map(dist, ok):
        return jnp.asarray(np.where(ok, buckets[np.clip(dist, 0, far)], -1).astype(np.int32))

    step = tq // CMP_STRIDE
    u = np.arange(2 * ncp)[None, :]
    dist_m = i - CMP_STRIDE * (u - ncp) - (CMP_BLOCK - 1)
    j = np.arange(SLC_CHUNK)[None, :]
    dist_n = (SLC_CHUNK - tq) + i - j
    j = np.arange(WINDOW + tq)[None, :]
    dist_w = WINDOW + i - j
    master, nbias, wbias = _biasgen(rel_table,
                                    bucket_map(dist_m, np.ones_like(dist_m, bool)),
                                    bucket_map(dist_n, dist_n >= 0),
                                    bucket_map(dist_w, (dist_w >= 0) & (dist_w < WINDOW)))
    master = master.reshape(N_NSA_KV, NSA_GQA, tq, 2 * ncp)
    bias_c = jnp.stack([master[..., ncp - step * qt:2 * ncp - step * qt] for qt in range(t // tq)], axis=2)
    bias_c = bias_c.reshape(N_NSA_KV, NSA_GQA, t, ncp)
    nbias = nbias.reshape(N_NSA_KV, NSA_GQA * tq, SLC_CHUNK)
    wbias = wbias.reshape(N_NSA_KV, NSA_GQA * tq, WINDOW + tq)
    return bias_c, nbias, wbias


def _compress_weights(pe, w1, w2):
    half = CMP_STRIDE
    eye = jnp.eye(N_NSA_KV, dtype=F32)

    def expand_w1(w):
        w = w.reshape(half, HEAD_DIM, CMP_HIDDEN)
        return jnp.einsum("idn,gh->igdhn", w, eye).reshape(half * N_NSA_KV * HEAD_DIM, N_NSA_KV * CMP_HIDDEN)

    def expand_pe(p):
        return jnp.broadcast_to(p[:, None, :], (half, N_NSA_KV, HEAD_DIM)).reshape(1, -1)

    w1a = expand_w1(w1[:half * HEAD_DIM]).astype(BF16)
    w1b = expand_w1(w1[half * HEAD_DIM:]).astype(BF16)
    w2x = jnp.einsum("nd,gh->gnhd", w2, eye).reshape(N_NSA_KV * CMP_HIDDEN, N_NSA_KV * HEAD_DIM).astype(BF16)
    return expand_pe(pe[:half]), expand_pe(pe[half:]), w1a, w1b, w2x


def _routing_tables(eid, tm):
    n = eid.shape[0]
    e_flat = eid.reshape(-1)
    onehot = (e_flat[:, None] == jnp.arange(N_EXPERTS, dtype=jnp.int32)[None, :]).astype(jnp.int32)
    csum = jnp.cumsum(onehot, axis=0)
    counts = csum[-1]
    padded = ((counts + tm - 1) // tm) * tm
    ends = jnp.cumsum(padded)
    starts = ends - padded
    pos = (jnp.sum(onehot * (starts[None, :] + csum), axis=1) - 1).astype(jnp.int32)
    n_tiles = (2 * n) // tm + N_EXPERTS
    rows = n_tiles * tm
    src_pair = jnp.zeros((rows,), jnp.int32).at[pos].set(jnp.arange(2 * n, dtype=jnp.int32))
    src_tok = src_pair // 2
    tile_start = jnp.arange(n_tiles, dtype=jnp.int32) * tm
    tile_e = jnp.sum((ends[None, :] <= tile_start[:, None]).astype(jnp.int32), axis=1)
    tile_e = jnp.minimum(tile_e, N_EXPERTS - 1)
    n_used = (ends[-1] // tm).astype(jnp.int32).reshape(1)
    last_e = tile_e[jnp.maximum(n_used[0] - 1, 0)]
    tile_e = jnp.where(tile_start < ends[-1], tile_e, last_e)
    return tile_e, src_tok, n_used, pos


def kernel(x, attn_norm_w, w_in, cmp_pe_k, cmp_pe_v, cmp_k_w1, cmp_k_w2, cmp_v_w1, cmp_v_w2,
           rel_bias_table, fox_forget_b, nsa_out_norm_w, fox_out_norm_w, w_out, ffn_norm_w,
           router_group_w, router_group_b, router_expert_w, router_expert_b,
           expert_w_gate, expert_w_up, expert_w_down, final_norm_w):
    b, t, d = x.shape
    n = b * t
    depth = w_in.shape[0]
    assert t % 512 == 0 and t >= WINDOW + AT_TQ and d == 2048 and t // SLC_BLOCK <= MASK_BLOCK_LANES
    rows = t // CMP_STRIDE
    bias_c, nbias, wbias = _bias_tables(rel_bias_table, t, rows)
    ns = t // SLC_BLOCK
    ratio = SLC_BLOCK // CMP_STRIDE
    span = CMP_BLOCK // CMP_STRIDE
    nc = (t - CMP_BLOCK) // CMP_STRIDE + 1
    impm = np.zeros((ns, rows), np.float32)
    for blk in range(ns):
        for a in range(ratio):
            for s in range(span):
                c = blk * ratio + a - s
                if 0 <= c < nc:
                    impm[blk, c] += 1.0
    impm = jnp.asarray(impm, BF16)
    eye_q = jnp.eye(AT_TQ, dtype=BF16)
    eye_h = jnp.eye(N_FOX_HEADS, dtype=BF16)
    tri = jnp.asarray(np.tril(np.ones((FOX_CUMSUM_BLOCK, FOX_CUMSUM_BLOCK), np.float32)), BF16)
    moe_tm = 256

    h = x.reshape(n, d)
    for layer in range(depth):
        w_main, w_misc = _project_weights(w_in[layer])
        main, misc = _proj(h, attn_norm_w[layer][None, :], w_main, w_misc)
        main3 = main.reshape(b, t, MAIN_COLS)
        misc3 = misc.reshape(b, t, LANES)

        xkv = jnp.stack([main3[:, :, COL_KCMP:COL_KCMP + 256], main3[:, :, COL_VCMP:COL_VCMP + 256]])
        xkv = xkv.reshape(2, b, rows, CMP_STRIDE * 256)
        pk = _compress_weights(cmp_pe_k[layer], cmp_k_w1[layer], cmp_k_w2[layer])
        pv = _compress_weights(cmp_pe_v[layer], cmp_v_w1[layer], cmp_v_w2[layer])
        kvc = _compress(xkv, *[jnp.stack([a, c]) for a, c in zip(pk, pv)])

        gates = misc3[:, :, MISC_GATE:MISC_GATE + 48].reshape(b, t, N_NSA_KV, 12).transpose(0, 2, 1, 3)
        o_nsa = _nsa(main3, kvc, bias_c, gates, impm, eye_q, nbias, wbias)

        c, ct = _foxprep(misc3, fox_forget_b[layer][None, :], tri, eye_h)
        o_fox = _fox(main3, c, ct)

        wr = jnp.concatenate([router_expert_w[layer], router_group_w[layer]], axis=1)
        wr = jnp.pad(wr, ((0, 0), (0, LANES - wr.shape[1])))
        wr_hi = wr.astype(BF16)
        wr_lo = (wr - wr_hi.astype(F32)).astype(BF16)
        br = jnp.concatenate([router_expert_b[layer], router_group_b[layer]])
        br = jnp.pad(br, (0, LANES - br.shape[0]))[None, :]
        half = N_NSA_HEADS * HEAD_DIM
        h, hn, eid, wts = _outproj(
            o_nsa.reshape(n, half), o_fox.reshape(n, half),
            h, nsa_out_norm_w[layer][None, :], fox_out_norm_w[layer][None, :],
            w_out[layer].astype(BF16), ffn_norm_w[layer][None, :], jnp.concatenate([wr_hi, wr_lo], axis=1), br)

        tile_e, src_tok, n_used, pos = _routing_tables(eid[:, :2], moe_tm)
        y = _moe(tile_e, src_tok, n_used, hn, expert_w_gate[layer], expert_w_up[layer], expert_w_down[layer], moe_tm)
        last = layer == depth - 1
        fw = final_norm_w if last else jnp.ones((d,), F32)
        assert last, "the fused final norm assumes a single layer"
        h = _combine(pos, y, wts, h, fw[None, :])
    return h.reshape(b, t, d)
```

```python
import functools
import math

import numpy as np
import jax
import jax.numpy as jnp
from jax import lax
from jax.experimental import pallas as pl
from jax.experimental.pallas import tpu as pltpu

F32 = jnp.float32
BF16 = jnp.bfloat16

HEAD_DIM = 64
N_NSA_HEADS = 16
N_FOX_HEADS = 16
NSA_GQA = 4
N_NSA_KV = 4
CMP_BLOCK = 32
CMP_STRIDE = 16
CMP_HIDDEN = 128
SLC_BLOCK = 64
SLC_TOP_N = 16
WINDOW = 512
REL_BUCKETS = 32
REL_MAX_DIST = 128
N_GROUPS = 4
EXPERTS_PER_GROUP = 8
N_EXPERTS = 32
EXPERT_FF = 512
NORM_EPS = 1e-6
NEG_INF = -1e30
FORCE_BONUS = 1e4
SCALE = HEAD_DIM ** -0.5
LOG2E = math.log2(math.e)
Q_SCALE = SCALE * LOG2E

LANES = 128
VMEM_LIMIT = 56 * 1024 * 1024

COL_NQ = 0
COL_KCMP = 1024
COL_VCMP = 1280
COL_SLC = 1536
COL_WIN = 2048
COL_FQ = 2560
COL_FK = 3584
COL_FV = 4608
MAIN_COLS = 5632
MISC_GATE = 0
MISC_FF = 48

AT_TQ = 256
NSA_GROUPS_PER_STEP = 2
FOX_PAIRS_PER_STEP = 4
FOX_CUMSUM_BLOCK = 256
SLC_CHUNK = 512
SLC_SHIFT = 6
assert 1 << SLC_SHIFT == SLC_BLOCK
MASK_BLOCK_LANES = 32
PAD_LANE = HEAD_DIM + MASK_BLOCK_LANES
KEY_PAD = 512


def _nt(a, b):
    return lax.dot_general(a, b, (((1,), (1,)), ((), ())), preferred_element_type=F32)


def _dot(a, b):
    return jnp.dot(a, b, preferred_element_type=F32)


def _split3(x):
    hi = x.astype(BF16)
    r = x - hi.astype(F32)
    mid = r.astype(BF16)
    r = r - mid.astype(F32)
    return hi, mid, r.astype(BF16)


def _cparams(grid_rank):
    return pltpu.CompilerParams(dimension_semantics=("arbitrary",) * grid_rank, vmem_limit_bytes=VMEM_LIMIT)


def _bucket_table(n):
    d = np.arange(n, dtype=np.int64)
    max_exact = REL_BUCKETS // 2
    rel = np.log(np.maximum(d, 1).astype(np.float64) / max_exact) / math.log(REL_MAX_DIST / max_exact)
    scaled = rel * (REL_BUCKETS - max_exact)
    frac = scaled - np.floor(scaled)
    inner = (d > max_exact) & (d < REL_MAX_DIST)
    assert np.all((frac[inner] > 1e-3) & (frac[inner] < 1 - 1e-3))
    large = np.minimum(max_exact + np.floor(scaled + 1e-6).astype(np.int64), REL_BUCKETS - 1)
    return np.where(d < max_exact, d, large).astype(np.int32)


def _project_weights(w):
    d = w.shape[0]
    sizes = [1024] + [256] * 6 + [48, 1024, 1024, 1024, 16]
    offs = np.concatenate([[0], np.cumsum(sizes)])
    nq, kcmp, vcmp, kslc, vslc, kwin, vwin, ngate, fq, fk, fv, ff = [
        w[:, int(offs[i]):int(offs[i + 1])] for i in range(12)]

    def interleave(k, v):
        k = k.reshape(d, N_NSA_KV, HEAD_DIM)
        v = v.reshape(d, N_NSA_KV, HEAD_DIM)
        return jnp.stack([k, v], axis=2).reshape(d, N_NSA_KV * 2 * HEAD_DIM)

    main = jnp.concatenate([nq * Q_SCALE, kcmp, vcmp, interleave(kslc, vslc), interleave(kwin, vwin),
                            fq * Q_SCALE, fk, fv], axis=1)
    assert main.shape[1] == MAIN_COLS
    misc = jnp.concatenate([ngate, ff, jnp.zeros((d, LANES - 64), w.dtype)], axis=1)
    return main.astype(BF16), misc.astype(BF16)


def _proj_kernel(x_ref, nw_ref, w_ref, wm_ref, o_ref, om_ref, *, tn):
    x = x_ref[...]
    y = x * lax.rsqrt(jnp.mean(x * x, axis=-1, keepdims=True) + NORM_EPS) * nw_ref[...]
    xn = y.astype(BF16)
    om_ref[...] = _dot(xn, wm_ref[...])
    for c in range(o_ref.shape[1] // tn):
        o_ref[:, c * tn:(c + 1) * tn] = _dot(xn, w_ref[:, c * tn:(c + 1) * tn]).astype(BF16)


def _proj(x2, norm_w, w_main, w_misc, tm=512, tn=512):
    n, d = x2.shape
    once = pl.Buffered(1)
    return pl.pallas_call(
        functools.partial(_proj_kernel, tn=tn),
        grid=(n // tm,),
        in_specs=[pl.BlockSpec((tm, d), lambda i: (i, 0)),
                  pl.BlockSpec((1, d), lambda i: (0, 0)),
                  pl.BlockSpec((d, MAIN_COLS), lambda i: (0, 0), pipeline_mode=once),
                  pl.BlockSpec((d, LANES), lambda i: (0, 0), pipeline_mode=once)],
        out_specs=[pl.BlockSpec((tm, MAIN_COLS), lambda i: (i, 0)),
                   pl.BlockSpec((tm, LANES), lambda i: (i, 0))],
        out_shape=[jax.ShapeDtypeStruct((n, MAIN_COLS), BF16),
                   jax.ShapeDtypeStruct((n, LANES), F32)],
        compiler_params=_cparams(1),
        name="proj",
    )(x2, norm_w, w_main, w_misc)


def _compress_kernel(x_ref, pea_ref, peb_ref, w1a_ref, w1b_ref, w2_ref, o_ref):
    x = x_ref[0, 0].astype(F32)
    xa = (x + pea_ref[0]).astype(BF16)
    xb = (x + peb_ref[0]).astype(BF16)
    a = _dot(xa, w1a_ref[0])
    b = _dot(xb, w1b_ref[0])
    rows = a.shape[0]
    pre = a + pltpu.roll(b, rows - 1, 0)
    hid = pre * jax.nn.sigmoid(pre)
    out = _dot(hid.astype(BF16), w2_ref[0])
    for g in range(N_NSA_KV):
        blk = out[:, g * HEAD_DIM:(g + 1) * HEAD_DIM]
        o_ref[0, 0, g] = jnp.concatenate([blk, blk], axis=1)


def _compress(xkv, pea, peb, w1a, w1b, w2):
    _, b, rows, width = xkv.shape
    hid = N_NSA_KV * CMP_HIDDEN
    return pl.pallas_call(
        _compress_kernel,
        grid=(2, b),
        in_specs=[pl.BlockSpec((1, 1, rows, width), lambda s, i: (s, i, 0, 0)),
                  pl.BlockSpec((1, 1, width), lambda s, i: (s, 0, 0)),
                  pl.BlockSpec((1, 1, width), lambda s, i: (s, 0, 0)),
                  pl.BlockSpec((1, width, hid), lambda s, i: (s, 0, 0)),
                  pl.BlockSpec((1, width, hid), lambda s, i: (s, 0, 0)),
                  pl.BlockSpec((1, hid, N_NSA_KV * HEAD_DIM), lambda s, i: (s, 0, 0))],
        out_specs=pl.BlockSpec((1, 1, N_NSA_KV, rows, LANES), lambda s, i: (s, i, 0, 0, 0)),
        out_shape=jax.ShapeDtypeStruct((2, b, N_NSA_KV, rows, LANES), F32),
        compiler_params=_cparams(2),
        name="compress",
    )(xkv, pea, peb, w1a, w1b, w2)


def _foxprep_kernel(misc_ref, fb_ref, tri_ref, eye_ref, c_ref, ct_ref):
    z = misc_ref[0][:, MISC_FF:MISC_FF + N_FOX_HEADS] + fb_ref[...]
    logf = (jnp.minimum(z, 0.0) - jnp.log(1.0 + jnp.exp(-jnp.abs(z)))) * LOG2E
    tri = tri_ref[...]
    blk = tri.shape[0]
    parts = _split3(logf)
    total = jnp.zeros((1, N_FOX_HEADS), F32)
    pieces = []
    for i in range(logf.shape[0] // blk):
        piece = total
        for part in parts:
            piece = piece + _dot(tri, part[i * blk:(i + 1) * blk])
        total = piece[blk - 1:blk]
        pieces.append(piece)
    c = jnp.concatenate(pieces, axis=0)
    ct = None
    for part in _split3(c):
        term = _nt(eye_ref[...], part)
        ct = term if ct is None else ct + term
    for p in range(N_FOX_HEADS // 2):
        c_ref[0, p] = c[:, 2 * p:2 * p + 2]
        ct_ref[0, p] = ct[2 * p:2 * p + 2, :]


def _foxprep(misc3, fb, tri, eye):
    b, t, _ = misc3.shape
    hp = N_FOX_HEADS // 2
    return pl.pallas_call(
        _foxprep_kernel,
        grid=(b,),
        in_specs=[pl.BlockSpec((1, t, LANES), lambda i: (i, 0, 0)),
                  pl.BlockSpec((1, N_FOX_HEADS), lambda i: (0, 0)),
                  pl.BlockSpec(tri.shape, lambda i: (0, 0)),
                  pl.BlockSpec((N_FOX_HEADS, N_FOX_HEADS), lambda i: (0, 0))],
        out_specs=[pl.BlockSpec((1, hp, t, 2), lambda i: (i, 0, 0, 0)),
                   pl.BlockSpec((1, hp, 2, t), lambda i: (i, 0, 0, 0))],
        out_shape=[jax.ShapeDtypeStruct((b, hp, t, 2), F32),
                   jax.ShapeDtypeStruct((b, hp, 2, t), F32)],
        compiler_params=_cparams(1),
        name="foxprep",
    )(misc3, fb, tri, eye)


def _softmax_first(s, v):
    m = jnp.max(s, axis=-1, keepdims=True)
    p = jnp.exp2(s - m)
    return m, jnp.sum(p, axis=-1, keepdims=True), _dot(p.astype(BF16), v)


def _softmax_next(s, v, m, l, acc):
    m_new = jnp.maximum(m, jnp.max(s, axis=-1, keepdims=True))
    alpha = jnp.exp2(m - m_new)
    p = jnp.exp2(s - m_new)
    return m_new, alpha * l + jnp.sum(p, axis=-1, keepdims=True), alpha * acc + _dot(p.astype(BF16), v)


def _stack_heads(q, tail, n_heads):
    return jnp.concatenate(
        [jnp.concatenate([q[:, r * HEAD_DIM:(r + 1) * HEAD_DIM], tail], axis=1) for r in range(n_heads)], axis=0)


def _fill_key_scratch(kp_sc, kvp_sc, kv, with_blocks):
    t = kv.shape[0]
    lane = lax.broadcasted_iota(jnp.int32, (t, LANES), 1)
    if with_blocks:
        blk = lax.shift_right_logical(lax.broadcasted_iota(jnp.int32, (t, LANES), 0), SLC_SHIFT)
        aug = jnp.where(lane - HEAD_DIM == blk, 1.0, 0.0).astype(BF16)
    else:
        aug = jnp.zeros((t, LANES), BF16)
    kp_sc[KEY_PAD:, :] = jnp.where(lane < HEAD_DIM, kv, aug)
    lane_p = lax.broadcasted_iota(jnp.int32, (KEY_PAD, LANES), 1)
    kp_sc[0:KEY_PAD, :] = jnp.where(lane_p == PAD_LANE, 1.0, 0.0).astype(BF16)
    kvp_sc[KEY_PAD:, :] = jnp.where(lane < HEAD_DIM, jnp.where(lane == 0, 1.0, 0.0).astype(BF16), kv)
    kvp_sc[0:KEY_PAD, :] = jnp.zeros((KEY_PAD, LANES), BF16)


def _nsa_first(s, v):
    m = jnp.max(s, axis=-1, keepdims=True)
    return m, _dot(jnp.exp2(s - m).astype(BF16), v)


def _nsa_next(s, v, m, acc):
    m_new = jnp.maximum(m, jnp.max(s, axis=-1, keepdims=True))
    return m_new, jnp.exp2(m - m_new) * acc + _dot(jnp.exp2(s - m_new).astype(BF16), v)


def _nsa_kernel(q_ref, kvc_ref, cbias_ref, gate_ref, impm_ref, eye_ref, skv_ref, nbias_ref, wkv_ref, wbias_ref,
                o_ref, skp_sc, skvp_sc, wkp_sc, wkvp_sc):
    tq = q_ref.shape[1]
    ncp = kvc_ref.shape[3]
    ns = impm_ref.shape[0]
    chunk = SLC_CHUNK
    span = wbias_ref.shape[2]
    qw = NSA_GQA * HEAD_DIM
    rows = NSA_GQA * tq
    qt = pl.program_id(2)
    t0 = qt * tq
    groups = range(skp_sc.shape[0])

    @pl.when(qt == 0)
    def _():
        for c in groups:
            _fill_key_scratch(skp_sc.at[c], skvp_sc.at[c], skv_ref[0, :, c * LANES:(c + 1) * LANES], True)
            _fill_key_scratch(wkp_sc.at[c], wkvp_sc.at[c], wkv_ref[0, :, c * LANES:(c + 1) * LANES], False)

    t_col = t0 + (lax.broadcasted_iota(jnp.int32, (rows, ncp), 0) & (tq - 1))
    c_row = lax.broadcasted_iota(jnp.int32, (rows, ncp), 1)
    valid = t_col >= c_row * CMP_STRIDE + (CMP_BLOCK - 1)
    blk = lax.broadcasted_iota(jnp.int32, (ns, tq), 0)
    t_row = t0 + lax.broadcasted_iota(jnp.int32, (ns, tq), 1)
    cur = t_row // SLC_BLOCK
    bonus = jnp.where((blk == 0) | (blk == cur) | (blk == cur - 1), FORCE_BONUS, 0.0)
    blk_valid = blk * SLC_BLOCK <= t_row
    pad_row = jnp.where(lax.broadcasted_iota(jnp.int32, (HEAD_DIM - MASK_BLOCK_LANES, tq), 0) == 0, 1.0, 0.0)
    lane = lax.broadcasted_iota(jnp.int32, (rows, LANES), 1)
    upper = lane >= HEAD_DIM
    win_tail = jnp.where(lane == PAD_LANE, NEG_INF, 0.0).astype(BF16)
    zero_tail = jnp.zeros((tq, HEAD_DIM), BF16)
    q_all = q_ref[0]

    qst, sel_tail, o_cmp = [], [], []
    for c in groups:
        qs = _stack_heads(q_all[:, c * qw:(c + 1) * qw], zero_tail, NSA_GQA)
        kc = kvc_ref[0, 0, c].astype(BF16)
        vc = kvc_ref[1, 0, c].astype(BF16)
        s = jnp.where(valid, _nt(qs, kc) + cbias_ref[c].reshape(rows, ncp), NEG_INF)
        m = jnp.max(s, axis=-1, keepdims=True)
        e = jnp.where(valid, jnp.exp2(s - m), 0.0)
        p = e / jnp.maximum(jnp.sum(e, axis=-1, keepdims=True), 1e-30)
        o_cmp.append(_dot(p.astype(BF16), vc))
        p_grp = p[0:tq]
        for r in range(1, NSA_GQA):
            p_grp = p_grp + p[r * tq:(r + 1) * tq]

        imp = None
        for part in _split3(p_grp):
            term = _nt(impm_ref[...], part)
            imp = term if imp is None else imp + term
        score = jnp.where(blk_valid, imp + bonus, NEG_INF)
        rank = jnp.zeros((ns, tq), F32)
        for m_blk in range(ns):
            other = score[m_blk:m_blk + 1, :]
            ahead = (other > score) | ((other == score) & (blk > m_blk))
            rank = rank + jnp.where(ahead, 1.0, 0.0)
        parts = [jnp.where(rank < float(min(SLC_TOP_N, ns)), 0.0, 1.0), pad_row]
        if ns < MASK_BLOCK_LANES:
            parts.insert(1, jnp.zeros((MASK_BLOCK_LANES - ns, tq), F32))
        flags = _nt(eye_ref[...], jnp.concatenate(parts, axis=0).astype(BF16))
        tail = jnp.concatenate([zero_tail, (flags * NEG_INF).astype(BF16)], axis=1)
        qst.append(qs)
        sel_tail.append(jnp.concatenate([tail] * NSA_GQA, axis=0))

    qsel = [jnp.where(upper, sel_tail[c], qst[c]) for c in groups]
    near = pl.multiple_of(t0 + tq - chunk + KEY_PAD, tq)
    state = tuple(_nsa_first(_nt(qsel[c], skp_sc[c, pl.ds(near, chunk), :]) + nbias_ref[c],
                             skvp_sc[c, pl.ds(near, chunk), :]) for c in groups)

    def far(j, carry):
        first = pl.multiple_of(near - (j + 1) * chunk, tq)
        return tuple(_nsa_next(_nt(qsel[c], skp_sc[c, pl.ds(first, chunk), :]),
                               skvp_sc[c, pl.ds(first, chunk), :], *carry[c]) for c in groups)

    state = lax.fori_loop(0, (t0 + tq - 1) // chunk, far, state)

    wfirst = pl.multiple_of(t0, tq)
    for c in groups:
        qwin = jnp.where(upper, win_tail, qst[c])
        _, acc_w = _nsa_first(_nt(qwin, wkp_sc[c, pl.ds(wfirst, span), :]) + wbias_ref[c],
                              wkvp_sc[c, pl.ds(wfirst, span), :])
        acc_s = state[c][1]
        gate = jax.nn.sigmoid(gate_ref[0, c])
        g = [jnp.concatenate([gate[:, 3 * r + j:3 * r + j + 1] for r in range(NSA_GQA)], axis=0) for j in range(3)]
        tot = o_cmp[c] * g[0] + acc_s * (g[1] / acc_s[:, 0:1]) + acc_w * (g[2] / acc_w[:, 0:1])
        out = tot[:, HEAD_DIM:].astype(o_ref.dtype)
        for r in range(NSA_GQA):
            col = (c * NSA_GQA + r) * HEAD_DIM
            o_ref[0, :, col:col + HEAD_DIM] = out[r * tq:(r + 1) * tq]


def _nsa(main3, kvc, bias_c, gates, impm, eye, nbias, wbias):
    b, t, _ = main3.shape
    ncp = kvc.shape[3]
    ns = impm.shape[0]
    tq = AT_TQ
    qw = NSA_GQA * HEAD_DIM
    gs = NSA_GROUPS_PER_STEP
    kvw = gs * LANES
    span = wbias.shape[2]
    assert KEY_PAD >= SLC_CHUNK and KEY_PAD % tq == 0 and span - tq == KEY_PAD
    assert COL_SLC % kvw == 0 and COL_WIN % kvw == 0
    key_scratch = pltpu.VMEM((gs, t + KEY_PAD, LANES), BF16)
    return pl.pallas_call(
        _nsa_kernel,
        grid=(b, N_NSA_KV // gs, t // tq),
        in_specs=[pl.BlockSpec((1, tq, gs * qw), lambda i, g, j: (i, j, g)),
                  pl.BlockSpec((2, 1, gs, ncp, LANES), lambda i, g, j: (0, i, g, 0, 0)),
                  pl.BlockSpec((gs, NSA_GQA, tq, ncp), lambda i, g, j: (g, 0, j, 0)),
                  pl.BlockSpec((1, gs, tq, 3 * NSA_GQA), lambda i, g, j: (i, g, j, 0)),
                  pl.BlockSpec((ns, ncp), lambda i, g, j: (0, 0)),
                  pl.BlockSpec((tq, tq), lambda i, g, j: (0, 0)),
                  pl.BlockSpec((1, t, kvw), lambda i, g, j: (i, 0, COL_SLC // kvw + g)),
                  pl.BlockSpec((gs, NSA_GQA * tq, SLC_CHUNK), lambda i, g, j: (g, 0, 0)),
                  pl.BlockSpec((1, t, kvw), lambda i, g, j: (i, 0, COL_WIN // kvw + g)),
                  pl.BlockSpec((gs, NSA_GQA * tq, span), lambda i, g, j: (g, 0, 0))],
        out_specs=pl.BlockSpec((1, tq, gs * qw), lambda i, g, j: (i, j, g)),
        out_shape=jax.ShapeDtypeStruct((b, t, N_NSA_HEADS * HEAD_DIM), BF16),
        scratch_shapes=[key_scratch, key_scratch, key_scratch, key_scratch],
        compiler_params=_cparams(3),
        name="nsa",
    )(main3, kvc, bias_c, gates, impm, eye, main3, nbias, main3, wbias)


def _fox_kernel(q_ref, k_ref, v_ref, c_ref, ct_ref, o_ref, *, chunk):
    tq = q_ref.shape[1]
    t0 = pl.program_id(2) * tq
    pairs = range(c_ref.shape[1])
    lane = lax.broadcasted_iota(jnp.int32, (tq, LANES), 1)
    low = lane < HEAD_DIM
    q_all = q_ref[0]
    zero = jnp.zeros((tq, LANES), BF16)
    qst, c_col = [], []
    for p in pairs:
        q = q_all[:, p * LANES:(p + 1) * LANES]
        qst.append(jnp.concatenate([jnp.where(low, q, zero), jnp.where(low, zero, q)], axis=0))
        c_col.append(c_ref[0, p])

    def logits(p, start):
        s = _nt(qst[p], k_ref[0, pl.ds(start, chunk), p * LANES:(p + 1) * LANES])
        c_row = ct_ref[0, p, :, pl.ds(start, chunk)]
        return jnp.concatenate([s[:tq] + c_col[p][:, 0:1] - c_row[0:1],
                                s[tq:] + c_col[p][:, 1:2] - c_row[1:2]], axis=0)

    def values(p, start):
        return v_ref[0, pl.ds(start, chunk), p * LANES:(p + 1) * LANES]

    d0 = pl.multiple_of((t0 // chunk) * chunk, chunk)
    row = lax.broadcasted_iota(jnp.int32, (2 * tq, chunk), 0) & (tq - 1)
    causal = row - lax.broadcasted_iota(jnp.int32, (2 * tq, chunk), 1) + (t0 - d0) >= 0
    state = tuple(_softmax_first(jnp.where(causal, logits(p, d0), NEG_INF), values(p, d0)) for p in pairs)

    def below(j, carry):
        start = pl.multiple_of(j * chunk, chunk)
        return tuple(_softmax_next(logits(p, start), values(p, start), *carry[p]) for p in pairs)

    state = lax.fori_loop(0, t0 // chunk, below, state)
    for p in pairs:
        _, l, acc = state[p]
        o = acc / l
        o_ref[0, :, p * LANES:(p + 1) * LANES] = jnp.where(low, o[:tq], o[tq:]).astype(o_ref.dtype)


def _fox(main3, c, ct, tq=256, chunk=512):
    b, t, _ = main3.shape
    ps = FOX_PAIRS_PER_STEP
    hp = N_FOX_HEADS // 2
    w = ps * LANES
    assert chunk % tq == 0 and t % chunk == 0
    assert COL_FQ % w == 0 and COL_FK % w == 0 and COL_FV % w == 0
    return pl.pallas_call(
        functools.partial(_fox_kernel, chunk=chunk),
        grid=(b, hp // ps, t // tq),
        in_specs=[pl.BlockSpec((1, tq, w), lambda i, p, j: (i, j, COL_FQ // w + p)),
                  pl.BlockSpec((1, t, w), lambda i, p, j: (i, 0, COL_FK // w + p)),
                  pl.BlockSpec((1, t, w), lambda i, p, j: (i, 0, COL_FV // w + p)),
                  pl.BlockSpec((1, ps, tq, 2), lambda i, p, j: (i, p, j, 0)),
                  pl.BlockSpec((1, ps, 2, t), lambda i, p, j: (i, p, 0, 0))],
        out_specs=pl.BlockSpec((1, tq, w), lambda i, p, j: (i, j, p)),
        out_shape=jax.ShapeDtypeStruct((b, t, N_FOX_HEADS * HEAD_DIM), BF16),
        compiler_params=_cparams(3),
        name="fox",
    )(main3, main3, main3, c, ct)


def _rms(x, w):
    return x * lax.rsqrt(jnp.mean(x * x, axis=-1, keepdims=True) + NORM_EPS) * w


def _to_token_rows(ref, x):
    tm, d = x.shape
    parts = d // LANES
    for a in range(parts):
        ref[pl.ds(a, tm, stride=parts), :] = x[:, a * LANES:(a + 1) * LANES]


def _from_token_rows(ref, base, tm, d, pitch):
    return jnp.concatenate([ref[pl.ds(base + a, tm, stride=pitch), :] for a in range(d // LANES)], axis=1)


GATHER_PITCH = 20


def _outproj_kernel(on_ref, of_ref, x_ref, nnw_ref, fnw_ref, wo_ref, ffw_ref,
                    wr_ref, br_ref, h_ref, hn_ref, eid_ref, wt_ref):
    mixed = jnp.concatenate([_rms(on_ref[...].astype(F32), nnw_ref[...]),
                             _rms(of_ref[...].astype(F32), fnw_ref[...])], axis=-1)
    h = x_ref[...] + _dot(mixed.astype(BF16), wo_ref[...])
    h_ref[...] = h
    hn = _rms(h, ffw_ref[...])
    _to_token_rows(hn_ref, hn)

    h_hi, h_mid, _ = _split3(hn)
    both = _dot(h_hi, wr_ref[...])
    logits = (both[:, :LANES] + _dot(h_mid, wr_ref[:, :LANES]) + both[:, LANES:]) + br_ref[...]
    tm = logits.shape[0]
    lane = lax.broadcasted_iota(jnp.int32, (tm, LANES), 1)
    big = jnp.int32(LANES)
    is_grp = (lane >= N_EXPERTS) & (lane < N_EXPERTS + N_GROUPS)
    glog = jnp.where(is_grp, logits, NEG_INF)
    gmax = jnp.max(glog, axis=-1, keepdims=True)
    gsel = jnp.min(jnp.where(glog == gmax, lane, big), axis=-1, keepdims=True) - N_EXPERTS
    p_gsel = 1.0 / jnp.sum(jnp.where(is_grp, jnp.exp(glog - gmax), 0.0), axis=-1, keepdims=True)
    in_grp = (lane < N_EXPERTS) & (lane // EXPERTS_PER_GROUP == gsel)
    e1 = jnp.where(in_grp, logits, NEG_INF)
    v1 = jnp.max(e1, axis=-1, keepdims=True)
    i1 = jnp.min(jnp.where(e1 == v1, lane, big), axis=-1, keepdims=True)
    e2 = jnp.where(lane == i1, NEG_INF, e1)
    v2 = jnp.max(e2, axis=-1, keepdims=True)
    i2 = jnp.min(jnp.where(e2 == v2, lane, big), axis=-1, keepdims=True)
    ex = jnp.exp(v2 - v1)
    w1 = p_gsel / (1.0 + ex)
    w2 = p_gsel * ex / (1.0 + ex)
    eid_ref[...] = jnp.where(lane == 0, i1, jnp.where(lane == 1, i2, 0))
    wt_ref[...] = jnp.where(lane == 0, w1, jnp.where(lane == 1, w2, 0.0))


def _outproj(on, of, x2, nnw, fnw, wo, ffw, wr, br, tm=512):
    n, d = x2.shape
    half = on.shape[1]
    row = lambda i: (i, 0)
    fixed = lambda i: (0, 0)
    return pl.pallas_call(
        _outproj_kernel,
        grid=(n // tm,),
        in_specs=[pl.BlockSpec((tm, half), row), pl.BlockSpec((tm, half), row),
                  pl.BlockSpec((tm, d), row),
                  pl.BlockSpec((1, half), fixed), pl.BlockSpec((1, half), fixed),
                  pl.BlockSpec((d, d), fixed), pl.BlockSpec((1, d), fixed),
                  pl.BlockSpec((d, 2 * LANES), fixed), pl.BlockSpec((1, LANES), fixed)],
        out_specs=[pl.BlockSpec((tm, d), row), pl.BlockSpec((tm * (d // LANES), LANES), row),
                   pl.BlockSpec((tm, LANES), row), pl.BlockSpec((tm, LANES), row)],
        out_shape=[jax.ShapeDtypeStruct((n, d), F32), jax.ShapeDtypeStruct((n * (d // LANES), LANES), F32),
                   jax.ShapeDtypeStruct((n, LANES), jnp.int32), jax.ShapeDtypeStruct((n, LANES), F32)],
        compiler_params=_cparams(1),
        name="outproj",
    )(on, of, x2, nnw, fnw, wo, ffw, wr, br)


def _moe_kernel(te_ref, nt_ref, nxt_ref, wslot_ref, src0_ref, src1_ref, hn_hbm, wg_hbm, wu_hbm, wd_hbm, y_ref,
                xbuf, sem, wgb, wub, wdb, wsem, wg_sc, wu_sc, wd_sc):
    tm = src0_ref.shape[2]
    d = wg_sc.shape[0]
    parts = d // LANES
    i = pl.program_id(0)
    n_used = nt_ref[0]
    slot = i % 2

    def gather(src_ref, slot_):
        def row(r, carry):
            src = pl.multiple_of(src_ref[0, 0, r] * parts, parts)
            dst = (slot_ * tm + r) * GATHER_PITCH
            pltpu.make_async_copy(hn_hbm.at[pl.ds(src, parts)], xbuf.at[pl.ds(dst, parts)], sem.at[slot_]).start()
            return carry
        lax.fori_loop(0, tm, row, 0, unroll=8)

    @pl.when(i == 0)
    def _():
        gather(src0_ref, 0)

    @pl.when(i + 1 < n_used)
    def _():
        gather(src1_ref, 1 - slot)

    prev = te_ref[jnp.maximum(i - 1, 0)]
    e_cur = te_ref[i]
    wslot = wslot_ref[i]

    def fetch(e, s):
        return (pltpu.make_async_copy(wg_hbm.at[e], wgb.at[s], wsem.at[s, 0]),
                pltpu.make_async_copy(wu_hbm.at[e], wub.at[s], wsem.at[s, 1]),
                pltpu.make_async_copy(wd_hbm.at[e], wdb.at[s], wsem.at[s, 2]))

    @pl.when(i == 0)
    def _():
        for copy in fetch(e_cur, wslot):
            copy.start()

    @pl.when((i == 0) | (e_cur != prev))
    def _():
        for copy in fetch(e_cur, wslot):
            copy.wait()
        wg_sc[...] = wgb[wslot].astype(BF16)
        wu_sc[...] = wub[wslot].astype(BF16)
        wd_sc[...] = wdb[wslot].astype(BF16)
        e_next = nxt_ref[i]

        @pl.when(e_next != e_cur)
        def _():
            for copy in fetch(e_next, 1 - wslot):
                copy.start()

    @pl.when(i < n_used)
    def _():
        base = slot * (tm * GATHER_PITCH)
        pltpu.make_async_copy(hn_hbm.at[pl.ds(0, tm * parts)], xbuf.at[pl.ds(base, tm * parts)], sem.at[slot]).wait()
        x = _from_token_rows(xbuf, base, tm, d, GATHER_PITCH).astype(BF16)
        gate = _dot(x, wg_sc[...])
        up = _dot(x, wu_sc[...])
        hid = gate * jax.nn.sigmoid(gate) * up
        _to_token_rows(y_ref, _dot(hid.astype(BF16), wd_sc[...]))

    @pl.when(i >= n_used)
    def _():
        y_ref[...] = jnp.zeros(y_ref.shape, F32)


def _moe(tile_e, src_tok, n_used, next_e, wslot, hn, wg, wu, wd, tm):
    n_tiles = tile_e.shape[0]
    d = wg.shape[1]
    ff = wg.shape[2]
    parts = d // LANES
    src3 = src_tok.reshape(n_tiles, 1, tm)
    grid_spec = pltpu.PrefetchScalarGridSpec(
        num_scalar_prefetch=4,
        grid=(n_tiles,),
        in_specs=[pl.BlockSpec((1, 1, tm), lambda i, te, nt, nx, ws: (0, 0, 0), memory_space=pltpu.SMEM),
                  pl.BlockSpec((1, 1, tm), lambda i, te, nt, nx, ws: (jnp.minimum(i + 1, n_tiles - 1), 0, 0),
                               memory_space=pltpu.SMEM),
                  pl.BlockSpec(memory_space=pl.ANY), pl.BlockSpec(memory_space=pl.ANY),
                  pl.BlockSpec(memory_space=pl.ANY), pl.BlockSpec(memory_space=pl.ANY)],
        out_specs=pl.BlockSpec((tm * parts, LANES), lambda i, te, nt, nx, ws: (i, 0)),
        scratch_shapes=[pltpu.VMEM((2 * tm * GATHER_PITCH, LANES), F32),
                        pltpu.SemaphoreType.DMA((2,)),
                        pltpu.VMEM((2, d, ff), F32), pltpu.VMEM((2, d, ff), F32), pltpu.VMEM((2, ff, d), F32),
                        pltpu.SemaphoreType.DMA((2, 3)),
                        pltpu.VMEM((d, ff), BF16), pltpu.VMEM((d, ff), BF16), pltpu.VMEM((ff, d), BF16)],
    )
    return pl.pallas_call(
        _moe_kernel,
        grid_spec=grid_spec,
        out_shape=jax.ShapeDtypeStruct((n_tiles * tm * parts, LANES), F32),
        compiler_params=_cparams(1),
        name="moe",
    )(tile_e, n_used, next_e, wslot, src3, src3, hn, wg, wu, wd)


def _combine_kernel(pos0_ref, pos1_ref, y_hbm, wt_ref, h_ref, fw_ref, o_ref, ybuf, sem):
    tm, d = h_ref.shape
    parts = d // LANES
    i = pl.program_id(0)
    n = pl.num_programs(0)
    slot = i % 2

    def gather(pos_ref, slot_):
        def row(r, carry):
            for k in range(2):
                src = pl.multiple_of(pos_ref[0, 0, 2 * r + k] * parts, parts)
                dst = ((slot_ * 2 + k) * tm + r) * GATHER_PITCH
                pltpu.make_async_copy(y_hbm.at[pl.ds(src, parts)], ybuf.at[pl.ds(dst, parts)], sem.at[slot_]).start()
            return carry
        lax.fori_loop(0, tm, row, 0, unroll=8)

    @pl.when(i == 0)
    def _():
        gather(pos0_ref, 0)

    @pl.when(i + 1 < n)
    def _():
        gather(pos1_ref, 1 - slot)

    rows = tm * GATHER_PITCH
    base = slot * (2 * rows)
    pltpu.make_async_copy(y_hbm.at[pl.ds(0, 2 * tm * parts)], ybuf.at[pl.ds(base, 2 * tm * parts)], sem.at[slot]).wait()
    wt = wt_ref[...]
    out = h_ref[...] + (wt[:, 0:1] * _from_token_rows(ybuf, base, tm, d, GATHER_PITCH)
                        + wt[:, 1:2] * _from_token_rows(ybuf, base + rows, tm, d, GATHER_PITCH))
    o_ref[...] = _rms(out, fw_ref[...])


def _combine(pos, y, wts, h, fw, tm=256):
    n, d = h.shape
    steps = n // tm
    pos3 = pos.reshape(steps, 1, 2 * tm)
    return pl.pallas_call(
        _combine_kernel,
        grid=(steps,),
        in_specs=[pl.BlockSpec((1, 1, 2 * tm), lambda i: (0, 0, 0), memory_space=pltpu.SMEM),
                  pl.BlockSpec((1, 1, 2 * tm), lambda i: (jnp.minimum(i + 1, steps - 1), 0, 0),
                               memory_space=pltpu.SMEM),
                  pl.BlockSpec(memory_space=pl.ANY),
                  pl.BlockSpec((tm, LANES), lambda i: (i, 0)),
                  pl.BlockSpec((tm, d), lambda i: (i, 0)),
                  pl.BlockSpec((1, d), lambda i: (0, 0))],
        out_specs=pl.BlockSpec((tm, d), lambda i: (i, 0)),
        out_shape=jax.ShapeDtypeStruct((n, d), F32),
        scratch_shapes=[pltpu.VMEM((2 * 2 * tm * GATHER_PITCH, LANES), F32), pltpu.SemaphoreType.DMA((2,))],
        compiler_params=_cparams(1),
        name="combine",
    )(pos3, pos3, y, wts, h, fw)


def _biasgen_kernel(tab_ref, bm_ref, bn_ref, bw_ref, om_ref, on_ref, ow_ref):
    h = pl.program_id(0)
    far = tab_ref[REL_BUCKETS - 1, h]

    def build(b_ref, shift):
        idx = b_ref[...]
        out = jnp.full(idx.shape, NEG_INF, F32)
        for bucket in range(REL_BUCKETS):
            out = jnp.where(idx == bucket, (tab_ref[bucket, h] - shift) * LOG2E, out)
        return out

    om_ref[0] = build(bm_ref, 0.0)
    on_ref[0] = build(bn_ref, far)
    ow_ref[0] = build(bw_ref, 0.0)


def _biasgen(rel_table, bm, bn, bw):
    heads = rel_table.shape[1]
    full = lambda a: pl.BlockSpec(a.shape, lambda h: (0, 0))
    out = lambda a: pl.BlockSpec((1,) + a.shape, lambda h: (h, 0, 0))
    return pl.pallas_call(
        _biasgen_kernel,
        grid=(heads,),
        in_specs=[pl.BlockSpec(memory_space=pltpu.SMEM), full(bm), full(bn), full(bw)],
        out_specs=[out(bm), out(bn), out(bw)],
        out_shape=[jax.ShapeDtypeStruct((heads,) + a.shape, F32) for a in (bm, bn, bw)],
        compiler_params=_cparams(1),
        name="biasgen",
    )(rel_table, bm, bn, bw)


def _bias_tables(rel_table, t, ncp):
    tq = AT_TQ
    far = REL_MAX_DIST
    buckets = _bucket_table(far + 1)
    i = np.arange(tq)[:, None]

    def bucket_map(dist, ok):
        return jnp.asarray(np.where(ok, buckets[np.clip(dist, 0, far)], -1).astype(np.int32))

    step = tq // CMP_STRIDE
    u = np.arange(2 * ncp)[None, :]
    dist_m = i - CMP_STRIDE * (u - ncp) - (CMP_BLOCK - 1)
    j = np.arange(SLC_CHUNK)[None, :]
    dist_n = (SLC_CHUNK - tq) + i - j
    j = np.arange(WINDOW + tq)[None, :]
    dist_w = WINDOW + i - j
    master, nbias, wbias = _biasgen(rel_table,
                                    bucket_map(dist_m, np.ones_like(dist_m, bool)),
                                    bucket_map(dist_n, dist_n >= 0),
                                    bucket_map(dist_w, (dist_w >= 0) & (dist_w < WINDOW)))
    master = master.reshape(N_NSA_KV, NSA_GQA, tq, 2 * ncp)
    bias_c = jnp.stack([master[..., ncp - step * qt:2 * ncp - step * qt] for qt in range(t // tq)], axis=2)
    bias_c = bias_c.reshape(N_NSA_KV, NSA_GQA, t, ncp)
    nbias = nbias.reshape(N_NSA_KV, NSA_GQA * tq, SLC_CHUNK)
    wbias = wbias.reshape(N_NSA_KV, NSA_GQA * tq, WINDOW + tq)
    return bias_c, nbias, wbias


def _compress_weights(pe, w1, w2):
    half = CMP_STRIDE
    eye = jnp.eye(N_NSA_KV, dtype=F32)

    def expand_w1(w):
        w = w.reshape(half, HEAD_DIM, CMP_HIDDEN)
        return jnp.einsum("idn,gh->igdhn", w, eye).reshape(half * N_NSA_KV * HEAD_DIM, N_NSA_KV * CMP_HIDDEN)

    def expand_pe(p):
        return jnp.broadcast_to(p[:, None, :], (half, N_NSA_KV, HEAD_DIM)).reshape(1, -1)

    w1a = expand_w1(w1[:half * HEAD_DIM]).astype(BF16)
    w1b = expand_w1(w1[half * HEAD_DIM:]).astype(BF16)
    w2x = jnp.einsum("nd,gh->gnhd", w2, eye).reshape(N_NSA_KV * CMP_HIDDEN, N_NSA_KV * HEAD_DIM).astype(BF16)
    return expand_pe(pe[:half]), expand_pe(pe[half:]), w1a, w1b, w2x


def _routing_tables(eid, tm):
    n = eid.shape[0]
    e_flat = eid.reshape(-1)
    onehot = (e_flat[:, None] == jnp.arange(N_EXPERTS, dtype=jnp.int32)[None, :]).astype(jnp.int32)
    csum = jnp.cumsum(onehot, axis=0)
    counts = csum[-1]
    padded = ((counts + tm - 1) // tm) * tm
    ends = jnp.cumsum(padded)
    starts = ends - padded
    pos = (jnp.sum(onehot * (starts[None, :] + csum), axis=1) - 1).astype(jnp.int32)
    n_tiles = (2 * n) // tm + N_EXPERTS
    rows = n_tiles * tm
    src_pair = jnp.zeros((rows,), jnp.int32).at[pos].set(jnp.arange(2 * n, dtype=jnp.int32))
    src_tok = src_pair // 2
    tile_start = jnp.arange(n_tiles, dtype=jnp.int32) * tm
    tile_e = jnp.sum((ends[None, :] <= tile_start[:, None]).astype(jnp.int32), axis=1)
    tile_e = jnp.minimum(tile_e, N_EXPERTS - 1)
    n_used = (ends[-1] // tm).astype(jnp.int32).reshape(1)
    last_e = tile_e[jnp.maximum(n_used[0] - 1, 0)]
    tile_e = jnp.where(tile_start < ends[-1], tile_e, last_e)
    experts = jnp.arange(N_EXPERTS, dtype=jnp.int32)
    later = (experts[None, :] > experts[:, None]) & (padded[None, :] > 0)
    nxt = jnp.min(jnp.where(later, experts[None, :], N_EXPERTS), axis=1)
    nxt = jnp.where(nxt == N_EXPERTS, experts, nxt).astype(jnp.int32)
    next_e = jnp.sum((tile_e[:, None] == experts[None, :]).astype(jnp.int32) * nxt[None, :], axis=1)
    changed = jnp.concatenate([jnp.ones((1,), jnp.int32), (tile_e[1:] != tile_e[:-1]).astype(jnp.int32)])
    wslot = ((jnp.cumsum(changed) - 1) % 2).astype(jnp.int32)
    return tile_e, src_tok, n_used, pos, next_e, wslot


def kernel(x, attn_norm_w, w_in, cmp_pe_k, cmp_pe_v, cmp_k_w1, cmp_k_w2, cmp_v_w1, cmp_v_w2,
           rel_bias_table, fox_forget_b, nsa_out_norm_w, fox_out_norm_w, w_out, ffn_norm_w,
           router_group_w, router_group_b, router_expert_w, router_expert_b,
           expert_w_gate, expert_w_up, expert_w_down, final_norm_w):
    b, t, d = x.shape
    n = b * t
    depth = w_in.shape[0]
    assert t % 512 == 0 and t >= WINDOW + AT_TQ and d == 2048 and t // SLC_BLOCK <= MASK_BLOCK_LANES
    rows = t // CMP_STRIDE
    bias_c, nbias, wbias = _bias_tables(rel_bias_table, t, rows)
    ns = t // SLC_BLOCK
    ratio = SLC_BLOCK // CMP_STRIDE
    span = CMP_BLOCK // CMP_STRIDE
    nc = (t - CMP_BLOCK) // CMP_STRIDE + 1
    impm = np.zeros((ns, rows), np.float32)
    for blk in range(ns):
        for a in range(ratio):
            for s in range(span):
                c = blk * ratio + a - s
                if 0 <= c < nc:
                    impm[blk, c] += 1.0
    impm = jnp.asarray(impm, BF16)
    eye_q = jnp.eye(AT_TQ, dtype=BF16)
    eye_h = jnp.eye(N_FOX_HEADS, dtype=BF16)
    tri = jnp.asarray(np.tril(np.ones((FOX_CUMSUM_BLOCK, FOX_CUMSUM_BLOCK), np.float32)), BF16)
    moe_tm = 256

    h = x.reshape(n, d)
    for layer in range(depth):
        w_main, w_misc = _project_weights(w_in[layer])
        main, misc = _proj(h, attn_norm_w[layer][None, :], w_main, w_misc)
        main3 = main.reshape(b, t, MAIN_COLS)
        misc3 = misc.reshape(b, t, LANES)

        xkv = jnp.stack([main3[:, :, COL_KCMP:COL_KCMP + 256], main3[:, :, COL_VCMP:COL_VCMP + 256]])
        xkv = xkv.reshape(2, b, rows, CMP_STRIDE * 256)
        pk = _compress_weights(cmp_pe_k[layer], cmp_k_w1[layer], cmp_k_w2[layer])
        pv = _compress_weights(cmp_pe_v[layer], cmp_v_w1[layer], cmp_v_w2[layer])
        kvc = _compress(xkv, *[jnp.stack([a, c]) for a, c in zip(pk, pv)])

        gates = misc3[:, :, MISC_GATE:MISC_GATE + 48].reshape(b, t, N_NSA_KV, 12).transpose(0, 2, 1, 3)
        o_nsa = _nsa(main3, kvc, bias_c, gates, impm, eye_q, nbias, wbias)

        c, ct = _foxprep(misc3, fox_forget_b[layer][None, :], tri, eye_h)
        o_fox = _fox(main3, c, ct)

        wr = jnp.concatenate([router_expert_w[layer], router_group_w[layer]], axis=1)
        wr = jnp.pad(wr, ((0, 0), (0, LANES - wr.shape[1])))
        wr_hi = wr.astype(BF16)
        wr_lo = (wr - wr_hi.astype(F32)).astype(BF16)
        br = jnp.concatenate([router_expert_b[layer], router_group_b[layer]])
        br = jnp.pad(br, (0, LANES - br.shape[0]))[None, :]
        half = N_NSA_HEADS * HEAD_DIM
        h, hn, eid, wts = _outproj(
            o_nsa.reshape(n, half), o_fox.reshape(n, half),
            h, nsa_out_norm_w[layer][None, :], fox_out_norm_w[layer][None, :],
            w_out[layer].astype(BF16), ffn_norm_w[layer][None, :], jnp.concatenate([wr_hi, wr_lo], axis=1), br)

        tile_e, src_tok, n_used, pos, next_e, wslot = _routing_tables(eid[:, :2], moe_tm)
        y = _moe(tile_e, src_tok, n_used, next_e, wslot, hn,
                 expert_w_gate[layer], expert_w_up[layer], expert_w_down[layer], moe_tm)
        last = layer == depth - 1
        fw = final_norm_w if last else jnp.ones((d,), F32)
        assert last, "the fused final norm assumes a single layer"
        h = _combine(pos, y, wts, h, fw[None, :])
    return h.reshape(b, t, d)
```
